```python
import jax, jax.numpy as jnp
from jax import lax
import numpy as np

D_MODEL = 1024
BATCH = 8
SEQ = 8192
DEPTH = 2

CTX_LEN = 256
GRID_W = 64
MIX_WIDTH = D_MODEL
RET_V_WIDTH = MIX_WIDTH // 2
N_RET_HEADS = 8
RET_V_DIM = RET_V_WIDTH // N_RET_HEADS
RET_QK_DIM = RET_V_DIM // 2
RET_QK_WIDTH = N_RET_HEADS * RET_QK_DIM
POOL_WIDTH = MIX_WIDTH - RET_V_WIDTH
POOL_WINDOWS = (2, 4, 8, 16)
POOL_GROUPS = len(POOL_WINDOWS)
POOL_GROUP_DIM = POOL_WIDTH // POOL_GROUPS
Q_END = RET_QK_WIDTH
K_END = 2 * RET_QK_WIDTH
V_END = K_END + RET_V_WIDTH
G_END = V_END + RET_V_WIDTH
IN_WIDTH = G_END + POOL_WIDTH
CHUNK = 128
ROPE_BASE = 10000.0
N_EXPERTS = 16
EC_FACTOR = 2
D_EXPERT = 64 * int(round(8.0 * D_MODEL / 3.0 / 64.0))
EPS = 1e-6

kernel_name = 'hybrid_retention_pool_ecmoe_dit'


def rms_norm(x, g):
    xf = x.astype(jnp.float32)
    y = xf * lax.rsqrt(jnp.mean(xf * xf, axis=-1, keepdims=True) + EPS)
    return (y * g.astype(jnp.float32)).astype(x.dtype)


def modulate(h, shift, scale):
    return h * (1 + scale) + shift


def split_heads(a, dh):
    b, n, w = a.shape
    return a.reshape(b, n, w // dh, dh).transpose(0, 2, 1, 3)


def merge_heads(a):
    b, h, n, dh = a.shape
    return a.transpose(0, 2, 1, 3).reshape(b, n, h * dh)


def axial_rope(n):
    t = jnp.arange(n)
    row = (t // GRID_W).astype(jnp.float32)
    col = (t % GRID_W).astype(jnp.float32)
    n_freq = RET_QK_DIM // 4
    inv = ROPE_BASE ** (-jnp.arange(n_freq, dtype=jnp.float32) / n_freq)
    ang = jnp.concatenate([row[:, None] * inv, col[:, None] * inv], axis=-1)
    return jnp.cos(ang), jnp.sin(ang)


def apply_rope(a, cos, sin):
    half = a.shape[-1] // 2
    a1, a2 = a[..., :half], a[..., half:]
    return jnp.concatenate([a1 * cos - a2 * sin, a1 * sin + a2 * cos], axis=-1)


def retention_qkv(z, rope):
    q = split_heads(z[..., :Q_END], RET_QK_DIM).astype(jnp.float32)
    k = split_heads(z[..., Q_END:K_END], RET_QK_DIM).astype(jnp.float32) * (RET_QK_DIM ** -0.5)
    v = split_heads(z[..., K_END:V_END], RET_V_DIM).astype(jnp.float32)
    if rope is not None:
        cos, sin = rope
        q = apply_rope(q, cos, sin)
        k = apply_rope(k, cos, sin)
    return q, k, v


def retention_chunked(q, k, v, log_gamma, state0, inclusive):
    b, h, n, _ = q.shape
    dv = v.shape[-1]
    nc = n // CHUNK

    def to_chunks(a):
        return a.reshape(b, h, nc, CHUNK, a.shape[-1]).transpose(2, 0, 1, 3, 4)

    pos = jnp.arange(CHUNK, dtype=jnp.float32)
    diff = pos[:, None] - pos[None, :]
    mask = (diff >= 0) if inclusive else (diff > 0)
    lg = log_gamma[:, None, None]
    decay_intra = jnp.where(mask, jnp.exp(lg * jnp.where(mask, diff, 0.0)), 0.0)
    q_decay = jnp.exp(log_gamma[:, None] * (pos + 1.0))[..., None]
    k_decay = jnp.exp(log_gamma[:, None] * (CHUNK - 1.0 - pos))[..., None]
    chunk_decay = jnp.exp(log_gamma * CHUNK)[:, None, None]

    def step(state, inp):
        qi, ki, vi = inp
        scores = jnp.einsum('bhid,bhjd->bhij', qi, ki) * decay_intra
        inner = jnp.einsum('bhij,bhjv->bhiv', scores, vi)
        cross = jnp.einsum('bhid,bhdv->bhiv', qi * q_decay, state)
        new_state = state * chunk_decay + jnp.einsum('bhjd,bhjv->bhdv', ki * k_decay, vi)
        return new_state, inner + cross

    final, out = lax.scan(step, state0, (to_chunks(q), to_chunks(k), to_chunks(v)))
    out = out.transpose(1, 2, 0, 3, 4).reshape(b, h, n, dv)
    return out, final


def bidir_retention(q, k, v, lg_f, lg_b, s_f0, s_b0):
    out_f, s_f = retention_chunked(q, k, v, lg_f, s_f0, True)
    flip = lambda a: jnp.flip(a, axis=2)
    out_b, s_b = retention_chunked(flip(q), flip(k), flip(v), lg_b, s_b0, False)
    return out_f + flip(out_b), s_f, s_b


def head_group_norm(o, g):
    mu = jnp.mean(o, axis=-1, keepdims=True)
    var = jnp.mean(jnp.square(o - mu), axis=-1, keepdims=True)
    y = (o - mu) * lax.rsqrt(var + EPS)
    return merge_heads(y) * g.astype(jnp.float32)


def window_bounds(n, w):
    t = jnp.arange(n)
    return jnp.clip(t - w // 2, 0, n), jnp.clip(t + w // 2, 0, n)


def pool_latent(p, pool_w_l, pool_scale_l):
    b, n, _ = p.shape
    rows = n // GRID_W
    pg = p.reshape(b, rows, GRID_W, POOL_GROUPS, POOL_GROUP_DIM)
    outs = []
    for gi, w in enumerate(POOL_WINDOWS):
        xg = pg[..., gi, :].astype(jnp.float32)
        sat = jnp.pad(jnp.cumsum(jnp.cumsum(xg, axis=1), axis=2), ((0, 0), (1, 0), (1, 0), (0, 0)))
        r0, r1 = window_bounds(rows, w)
        c0, c1 = window_bounds(GRID_W, w)
        s = (sat[:, r1][:, :, c1] - sat[:, r0][:, :, c1]
             - sat[:, r1][:, :, c0] + sat[:, r0][:, :, c0])
        cnt = ((r1 - r0)[:, None] * (c1 - c0)[None, :]).astype(jnp.float32)
        mean = s / cnt[None, :, :, None]
        outs.append((mean - xg).astype(p.dtype) @ pool_w_l[gi])
    y = jnp.concatenate(outs, axis=-1).reshape(b, n, POOL_WIDTH)
    return y * pool_scale_l


def pool_context(p, pool_w_l, pool_scale_l):
    b, n, _ = p.shape
    outs = []
    for gi, w in enumerate(POOL_WINDOWS):
        xg = p[..., gi * POOL_GROUP_DIM:(gi + 1) * POOL_GROUP_DIM].astype(jnp.float32)
        cs = jnp.pad(jnp.cumsum(xg, axis=1), ((0, 0), (1, 0), (0, 0)))
        lo, hi = window_bounds(n, w)
        mean = (cs[:, hi] - cs[:, lo]) / (hi - lo).astype(jnp.float32)[None, :, None]
        outs.append((mean - xg).astype(p.dtype) @ pool_w_l[gi])
    return jnp.concatenate(outs, axis=-1) * pool_scale_l


def mixer_output(z, ret, pool_y, gn, w_out_l):
    g = z[..., V_END:G_END].astype(jnp.float32)
    r = (jax.nn.silu(g) * head_group_norm(ret, gn)).astype(z.dtype)
    return jnp.concatenate([r, pool_y], axis=-1) @ w_out_l


def expert_choice_ffn(h, w_router_l, w_gate_l, w_up_l, w_down_l):
    b, n, d = h.shape
    cap = EC_FACTOR * n // N_EXPERTS
    aff = jax.nn.softmax(jnp.einsum('bnd,de->bne', h, w_router_l).astype(jnp.float32), axis=-1)
    gate, idx = lax.top_k(aff.transpose(0, 2, 1), cap)
    xs = jax.vmap(lambda hb, ib: hb[ib])(h, idx)
    a = jnp.einsum('becd,edf->becf', xs, w_gate_l)
    u = jnp.einsum('becd,edf->becf', xs, w_up_l)
    y = jnp.einsum('becf,efd->becd', jax.nn.silu(a) * u, w_down_l)
    y = y * gate[..., None].astype(y.dtype)
    return jax.vmap(lambda ib, yb: jnp.zeros((n, d), yb.dtype).at[ib.reshape(-1)].add(yb.reshape(-1, d)))(idx, y)


def setup_inputs(seed: int = 0) -> dict:
    key = jax.random.key(seed)
    ks = jax.random.split(key, 24)
    f32 = jnp.float32
    nrm = lambda k, shape, s: jax.random.normal(k, shape, f32) * s
    gain = lambda k, shape: 1.0 + 0.05 * jax.random.normal(k, shape, f32)
    base_logit = jnp.log(2.0 ** (5.0 + jnp.arange(N_RET_HEADS, dtype=f32)) - 1.0)
    return {
        'x': nrm(ks[0], (BATCH, SEQ, D_MODEL), 1.0),
        'c': nrm(ks[1], (BATCH, D_MODEL), 1.0),
        'ctx': nrm(ks[2], (BATCH, CTX_LEN, D_MODEL), 1.0),
        'c_ctx': nrm(ks[3], (D_MODEL,), 1.0),
        'w_ada': nrm(ks[4], (DEPTH, D_MODEL, 6 * D_MODEL), 0.5 * D_MODEL ** -0.5),
        'b_ada': nrm(ks[5], (DEPTH, 6 * D_MODEL), 0.02),
        'norm_pre_mix': gain(ks[6], (DEPTH, D_MODEL)),
        'norm_post_mix': gain(ks[7], (DEPTH, D_MODEL)),
        'norm_pre_ffn': gain(ks[8], (DEPTH, D_MODEL)),
        'norm_post_ffn': gain(ks[9], (DEPTH, D_MODEL)),
        'w_in': nrm(ks[10], (DEPTH, D_MODEL, IN_WIDTH), D_MODEL ** -0.5),
        'ret_decay_fwd': base_logit[None] + nrm(ks[11], (DEPTH, N_RET_HEADS), 0.1),
        'ret_decay_bwd': base_logit[None] + nrm(ks[12], (DEPTH, N_RET_HEADS), 0.1),
        'ret_gn': gain(ks[13], (DEPTH, RET_V_WIDTH)),
        'pool_w': nrm(ks[14], (DEPTH, POOL_GROUPS, POOL_GROUP_DIM, POOL_GROUP_DIM), POOL_GROUP_DIM ** -0.5),
        'pool_scale': 0.5 + 0.1 * jax.random.normal(ks[15], (DEPTH, POOL_WIDTH), f32),
        'w_out': nrm(ks[16], (DEPTH, MIX_WIDTH, D_MODEL), MIX_WIDTH ** -0.5),
        'w_router': nrm(ks[17], (DEPTH, D_MODEL, N_EXPERTS), D_MODEL ** -0.5),
        'w_gate': nrm(ks[18], (DEPTH, N_EXPERTS, D_MODEL, D_EXPERT), D_MODEL ** -0.5),
        'w_up': nrm(ks[19], (DEPTH, N_EXPERTS, D_MODEL, D_EXPERT), D_MODEL ** -0.5),
        'w_down': nrm(ks[20], (DEPTH, N_EXPERTS, D_EXPERT, D_MODEL), D_EXPERT ** -0.5),
    }


def reference(x, c, ctx, c_ctx, w_ada, b_ada, norm_pre_mix, norm_post_mix, norm_pre_ffn,
              norm_post_ffn, w_in, ret_decay_fwd, ret_decay_bwd, ret_gn, pool_w, pool_scale,
              w_out, w_router, w_gate, w_up, w_down):
    n = x.shape[1]
    rope = axial_rope(n)
    for l in range(DEPTH):
        last = l == DEPTH - 1
        mod_x = (jax.nn.silu(c) @ w_ada[l] + b_ada[l])[:, None, :]
        mod_c = (jax.nn.silu(c_ctx) @ w_ada[l] + b_ada[l])[None, None, :]
        sh1, sc1, g1, sh2, sc2, g2 = jnp.split(mod_x, 6, axis=-1)
        csh1, csc1, cg1, csh2, csc2, cg2 = jnp.split(mod_c, 6, axis=-1)
        lg_f = jax.nn.log_sigmoid(ret_decay_fwd[l].astype(jnp.float32))
        lg_b = jax.nn.log_sigmoid(ret_decay_bwd[l].astype(jnp.float32))

        hc = modulate(rms_norm(ctx, norm_pre_mix[l]), csh1, csc1)
        hx = modulate(rms_norm(x, norm_pre_mix[l]), sh1, sc1)
        zc = hc @ (w_in[l][:, :V_END] if last else w_in[l])
        zx = hx @ w_in[l]
        qc, kc, vc = retention_qkv(zc, None)
        zero = jnp.zeros((ctx.shape[0], N_RET_HEADS, RET_QK_DIM, RET_V_DIM), jnp.float32)
        ret_c, s_f, s_b = bidir_retention(qc, kc, vc, lg_f, lg_b, zero, zero)
        qx, kx, vx = retention_qkv(zx, rope)
        ret_x, _, _ = bidir_retention(qx, kx, vx, lg_f, lg_b, s_f, s_b)
        pool_x = pool_latent(zx[..., G_END:], pool_w[l], pool_scale[l])
        mix_x = mixer_output(zx, ret_x, pool_x, ret_gn[l], w_out[l])
        x = x + g1 * rms_norm(mix_x, norm_post_mix[l])
        if not last:
            pool_c = pool_context(zc[..., G_END:], pool_w[l], pool_scale[l])
            mix_c = mixer_output(zc, ret_c, pool_c, ret_gn[l], w_out[l])
            ctx = ctx + cg1 * rms_norm(mix_c, norm_post_mix[l])

        hx2 = modulate(rms_norm(x, norm_pre_ffn[l]), sh2, sc2)
        ffn_x = expert_choice_ffn(hx2, w_router[l], w_gate[l], w_up[l], w_down[l])
        x = x + g2 * rms_norm(ffn_x, norm_post_ffn[l])
        if not last:
            hc2 = modulate(rms_norm(ctx, norm_pre_ffn[l]), csh2, csc2)
            ffn_c = expert_choice_ffn(hc2, w_router[l], w_gate[l], w_up[l], w_down[l])
            ctx = ctx + cg2 * rms_norm(ffn_c, norm_post_ffn[l])
    return x
```

```python
import functools

import jax
import jax.numpy as jnp
import numpy as np
from jax import lax
from jax.experimental import pallas as pl
from jax.experimental.pallas import tpu as pltpu

F32 = jnp.float32
BF16 = jnp.bfloat16

D_MODEL = 1024
GRID_W = 64
N_HEADS = 8
DV = 64
DK = 32
QK_W = N_HEADS * DK
V_W = N_HEADS * DV
POOL_W = 512
POOL_WINDOWS = (2, 4, 8, 16)
POOL_GROUP_DIM = 128
CHUNK = 128
ROPE_BASE = 10000.0
N_EXPERTS = 16
EC_FACTOR = 2
EPS = 1e-6
MAX_HALF_WINDOW = max(POOL_WINDOWS) // 2

VMEM_LIMIT = 56 * 1024 * 1024


def _cparams(*sem):
    return pltpu.CompilerParams(dimension_semantics=sem, vmem_limit_bytes=VMEM_LIMIT)


def _dot(a, b):
    return jnp.dot(a, b, preferred_element_type=F32)


def _dot_nt(a, b, precision=None):
    return lax.dot_general(a, b, (((1,), (1,)), ((), ())), precision=precision,
                           preferred_element_type=F32)


def _ada_kernel(cc_ref, w_ref, b_ref, o_ref):
    s = cc_ref[...]
    s = s * jax.nn.sigmoid(s)
    o_ref[0] = _dot(s.astype(BF16), w_ref[0].astype(BF16)) + b_ref[0]


def ada_modulation(cc, w_ada, b_ada):
    depth, d, d6 = w_ada.shape
    rows = cc.shape[0]
    tn = 1536
    return pl.pallas_call(
        _ada_kernel,
        out_shape=jax.ShapeDtypeStruct((depth, rows, d6), F32),
        grid=(depth, d6 // tn),
        in_specs=[
            pl.BlockSpec((rows, d), lambda l, j: (0, 0)),
            pl.BlockSpec((1, d, tn), lambda l, j: (l, 0, j)),
            pl.BlockSpec((1, 1, tn), lambda l, j: (l, 0, j)),
        ],
        out_specs=pl.BlockSpec((1, rows, tn), lambda l, j: (l, 0, j)),
        compiler_params=_cparams("parallel", "parallel"),
        name="ada_modulation",
    )(cc, w_ada, b_ada.reshape(depth, 1, d6))


def _norm_mod(xf, g, sh, sc):
    y = xf * lax.rsqrt(jnp.mean(xf * xf, axis=-1, keepdims=True) + EPS)
    return (y * g) * (1.0 + sc) + sh


def _premix_kernel(x_ref, g_ref, sh_ref, sc_ref, w_ref, *rest, rope):
    if rope:
        cos_ref, sin_ref, qk_ref, v_ref, gate_ref, p_ref = rest
    else:
        qk_ref, v_ref, gate_ref, p_ref = rest
    hb = _norm_mod(x_ref[0], g_ref[...], sh_ref[0], sc_ref[0]).astype(BF16)
    zqk = _dot(hb, w_ref[:, 0:2 * QK_W])
    half = QK_W // 2
    q1, q2 = zqk[:, 0:half], zqk[:, half:2 * half]
    k1 = zqk[:, 2 * half:3 * half] * (DK ** -0.5)
    k2 = zqk[:, 3 * half:4 * half] * (DK ** -0.5)
    if rope:
        cos, sin = cos_ref[...], sin_ref[...]
        q1, q2 = q1 * cos - q2 * sin, q1 * sin + q2 * cos
        k1, k2 = k1 * cos - k2 * sin, k1 * sin + k2 * cos
    qk_ref[0] = jnp.concatenate([q1, q2, k1, k2], axis=1).astype(BF16)
    o = 2 * QK_W
    v_ref[0] = _dot(hb, w_ref[:, o:o + V_W]).astype(BF16)
    gate_ref[0] = _dot(hb, w_ref[:, o + V_W:o + 2 * V_W])
    p_ref[0] = _dot(hb, w_ref[:, o + 2 * V_W:o + 2 * V_W + POOL_W])


def premix(x, norm_w, shift, scale, w_in_p, rope, tm):
    b, n, d = x.shape
    in_w = w_in_p.shape[1]
    row = lambda bi, i: (bi, i, 0)
    in_specs = [
        pl.BlockSpec((1, tm, d), row),
        pl.BlockSpec((1, d), lambda bi, i: (0, 0)),
        pl.BlockSpec((1, 1, d), lambda bi, i: (bi, 0, 0)),
        pl.BlockSpec((1, 1, d), lambda bi, i: (bi, 0, 0)),
        pl.BlockSpec((d, in_w), lambda bi, i: (0, 0)),
    ]
    args = [x, norm_w.reshape(1, d), shift, scale, w_in_p]
    if rope is not None:
        half = QK_W // 2
        in_specs += [pl.BlockSpec((tm, half), lambda bi, i: (i, 0))] * 2
        args += list(rope)
    return pl.pallas_call(
        functools.partial(_premix_kernel, rope=rope is not None),
        out_shape=(
            jax.ShapeDtypeStruct((b, n, 2 * QK_W), BF16),
            jax.ShapeDtypeStruct((b, n, V_W), BF16),
            jax.ShapeDtypeStruct((b, n, V_W), F32),
            jax.ShapeDtypeStruct((b, n, POOL_W), F32),
        ),
        grid=(b, n // tm),
        in_specs=in_specs,
        out_specs=(
            pl.BlockSpec((1, tm, 2 * QK_W), row),
            pl.BlockSpec((1, tm, V_W), row),
            pl.BlockSpec((1, tm, V_W), row),
            pl.BlockSpec((1, tm, POOL_W), row),
        ),
        compiler_params=_cparams("parallel", "parallel"),
        name="premix",
    )(*args)


def _fold_state(s):
    g = 2 * N_HEADS
    top = s[0:g]
    bot = s[QK_W // 2:QK_W // 2 + g]
    for h in range(1, N_HEADS):
        top = top + s[h * g:(h + 1) * g]
        bot = bot + s[QK_W // 2 + h * g:QK_W // 2 + (h + 1) * g]
    return jnp.concatenate([top, bot], axis=0)


def _expand_state(c):
    g = 2 * N_HEADS
    return jnp.concatenate([c[0:g]] * N_HEADS + [c[g:2 * g]] * N_HEADS, axis=0)


def _state_kernel(kf_ref, vf_ref, kb_ref, vb_ref, s0f_ref, s0b_ref, kdf_ref, kdb_ref,
                  cdf_ref, cdb_ref, bd_ref, sf_out, sb_out, ff_out, fb_out, sf_acc, sb_acc):
    c = pl.program_id(1)
    nc = pl.num_programs(1)
    bd = bd_ref[...]

    @pl.when(c == 0)
    def _():
        sf_acc[...] = _expand_state(s0f_ref[0]) * bd
        sb_acc[...] = _expand_state(s0b_ref[0]) * bd

    sf = sf_acc[...]
    sb = sb_acc[...]
    sf_out[0, 0] = _fold_state(sf).astype(BF16)
    sb_out[0, 0] = _fold_state(sb).astype(BF16)

    def update(s, k_ref, v_ref, kd_ref, cd_ref):
        kd = k_ref[0].astype(F32) * kd_ref[...]
        f = _dot(kd.T.astype(BF16), v_ref[0])
        return s * cd_ref[...] + f * bd

    sf_new = update(sf, kf_ref, vf_ref, kdf_ref, cdf_ref)
    sb_new = update(sb, kb_ref, vb_ref, kdb_ref, cdb_ref)
    sf_acc[...] = sf_new
    sb_acc[...] = sb_new

    @pl.when(c == nc - 1)
    def _():
        ff_out[0] = _fold_state(sf_new)
        fb_out[0] = _fold_state(sb_new)


def state_scan(qk, v, s0f, s0b, tabs):
    b, n, _ = qk.shape
    nc = n // CHUNK
    const = lambda shape: pl.BlockSpec(shape, lambda bi, c: (0,) * len(shape))
    return pl.pallas_call(
        _state_kernel,
        out_shape=(
            jax.ShapeDtypeStruct((b, nc, DK, V_W), BF16),
            jax.ShapeDtypeStruct((b, nc, DK, V_W), BF16),
            jax.ShapeDtypeStruct((b, DK, V_W), F32),
            jax.ShapeDtypeStruct((b, DK, V_W), F32),
        ),
        grid=(b, nc),
        in_specs=[
            pl.BlockSpec((1, CHUNK, QK_W), lambda bi, c: (bi, c, 1)),
            pl.BlockSpec((1, CHUNK, V_W), lambda bi, c: (bi, c, 0)),
            pl.BlockSpec((1, CHUNK, QK_W), lambda bi, c: (bi, nc - 1 - c, 1)),
            pl.BlockSpec((1, CHUNK, V_W), lambda bi, c: (bi, nc - 1 - c, 0)),
            pl.BlockSpec((1, DK, V_W), lambda bi, c: (bi, 0, 0)),
            pl.BlockSpec((1, DK, V_W), lambda bi, c: (bi, 0, 0)),
            const((CHUNK, QK_W)), const((CHUNK, QK_W)),
            const((1, V_W)), const((1, V_W)),
            const((QK_W, V_W)),
        ],
        out_specs=(
            pl.BlockSpec((1, 1, DK, V_W), lambda bi, c: (bi, c, 0, 0)),
            pl.BlockSpec((1, 1, DK, V_W), lambda bi, c: (bi, nc - 1 - c, 0, 0)),
            pl.BlockSpec((1, DK, V_W), lambda bi, c: (bi, 0, 0)),
            pl.BlockSpec((1, DK, V_W), lambda bi, c: (bi, 0, 0)),
        ),
        scratch_shapes=[pltpu.VMEM((QK_W, V_W), F32), pltpu.VMEM((QK_W, V_W), F32)],
        compiler_params=_cparams("parallel", "arbitrary"),
        name="state_scan",
    )(qk, v, qk, v, s0f, s0b, tabs["kdec_f"], tabs["kdec_b"], tabs["cdec_f"], tabs["cdec_b"],
      tabs["bd_f32"])


def _pool_kernel(*refs, gw, tile, halo, rows_total):
    if halo:
        prev_ref, cur_ref, next_ref, pw_ref, ps_ref, o_ref, buf = refs
    else:
        cur_ref, pw_ref, ps_ref, o_ref, buf = refs
    i = pl.program_id(1)
    last = pl.num_programs(1) - 1
    m = MAX_HALF_WINDOW
    span = tile + 2 * halo
    zeros_m = jnp.zeros((m, POOL_W), F32)
    buf[0:m] = zeros_m
    buf[m + span:2 * m + span] = zeros_m
    if halo:
        buf[m:m + halo] = jnp.where(i > 0, prev_ref[0], 0.0)
        buf[m + halo + tile:m + span] = jnp.where(i < last, next_ref[0], 0.0)
    buf[m + halo:m + halo + tile] = cur_ref[0]

    pos = lax.broadcasted_iota(jnp.int32, (span, 1), 0)
    col = pos % gw
    tpos = lax.broadcasted_iota(jnp.int32, (tile, 1), 0)
    tcol = tpos % gw
    trow = i * (tile // gw) + tpos // gw
    for gi, w in enumerate(POOL_WINDOWS):
        cs = slice(gi * POOL_GROUP_DIM, (gi + 1) * POOL_GROUP_DIM)
        hw = w // 2
        s = None
        for d in range(-hw, hw):
            valid = (col + d >= 0) & (col + d < gw)
            term = jnp.where(valid, buf[m + d:m + d + span, cs], 0.0)
            s = term if s is None else s + term
        if halo:
            acc = None
            for d in range(-hw, hw):
                start = halo + d * gw
                term = s[start:start + tile]
                acc = term if acc is None else acc + term
            cnt_r = jnp.minimum(trow + hw, rows_total) - jnp.maximum(trow - hw, 0)
        else:
            acc = s
            cnt_r = 1
        cnt_c = jnp.minimum(tcol + hw, gw) - jnp.maximum(tcol - hw, 0)
        cnt = (cnt_r * cnt_c).astype(F32)
        xg = buf[m + halo:m + halo + tile, cs]
        diff = (acc / cnt - xg).astype(BF16)
        y = _dot(diff, pw_ref[gi]) * ps_ref[:, cs]
        o_ref[0, :, cs] = y.astype(BF16)


def pool_mixer(p, pool_w_bf, pool_scale, gw, tile):
    b, n, pw = p.shape
    rows_total = n // gw
    halo = MAX_HALF_WINDOW * gw if rows_total > 1 else 0
    in_specs, args = [], []
    if halo:
        r = tile // halo
        nh = n // halo
        in_specs = [
            pl.BlockSpec((1, halo, pw), lambda bi, i: (bi, jnp.maximum(i * r - 1, 0), 0)),
            pl.BlockSpec((1, tile, pw), lambda bi, i: (bi, i, 0)),
            pl.BlockSpec((1, halo, pw), lambda bi, i: (bi, jnp.minimum((i + 1) * r, nh - 1), 0)),
        ]
        args = [p, p, p]
    else:
        in_specs = [pl.BlockSpec((1, tile, pw), lambda bi, i: (bi, i, 0))]
        args = [p]
    ng = len(POOL_WINDOWS)
    in_specs += [
        pl.BlockSpec((ng, POOL_GROUP_DIM, POOL_GROUP_DIM), lambda bi, i: (0, 0, 0)),
        pl.BlockSpec((1, pw), lambda bi, i: (0, 0)),
    ]
    args += [pool_w_bf, pool_scale.reshape(1, pw)]
    return pl.pallas_call(
        functools.partial(_pool_kernel, gw=gw, tile=tile, halo=halo, rows_total=rows_total),
        out_shape=jax.ShapeDtypeStruct((b, n, pw), BF16),
        grid=(b, n // tile),
        in_specs=in_specs,
        out_specs=pl.BlockSpec((1, tile, pw), lambda bi, i: (bi, i, 0)),
        scratch_shapes=[pltpu.VMEM((tile + 2 * halo + 2 * MAX_HALF_WINDOW, pw), F32)],
        compiler_params=_cparams("parallel", "parallel"),
        name="pool_mixer",
    )(*args)


def _ret_kernel(qk_ref, v_ref, gate_ref, py_ref, x_ref, sf_ref, sb_ref, mp_ref, qdf_ref, qdb_ref,
                hm_ref, bd_ref, avg_ref, gn_ref, wo_ref, nw_ref, g1_ref, o_ref, r_buf, *, nch):
    bd = bd_ref[...]
    avg = avg_ref[...]
    lane = lax.broadcasted_iota(jnp.int32, (1, 2 * DV), 1)
    lo_mask = (lane < DV).astype(BF16)
    hi_mask = (lane >= DV).astype(BF16)
    for ci in range(nch):
        rs = slice(ci * CHUNK, (ci + 1) * CHUNK)
        q = qk_ref[0, rs, 0:QK_W]
        k = qk_ref[0, rs, QK_W:2 * QK_W]
        vv = v_ref[0, rs, :]
        inner = []
        for j in range(N_HEADS // 2):
            kp = jnp.concatenate([k * hm_ref[2 * j], k * hm_ref[2 * j + 1]], axis=0)
            sc = _dot_nt(q, kp) * mp_ref[j]
            vpair = vv[:, 2 * DV * j:2 * DV * (j + 1)]
            vp = jnp.concatenate([vpair * lo_mask, vpair * hi_mask], axis=0)
            inner.append(_dot(sc.astype(BF16), vp))
        o = jnp.concatenate(inner, axis=1)
        sf_bd = _expand_state(sf_ref[0, ci]) * bd
        sb_bd = _expand_state(sb_ref[0, ci]) * bd
        o = o + _dot(q, sf_bd) * qdf_ref[...] + _dot(q, sb_bd) * qdb_ref[...]

        def head_mean(a):
            hi = a.astype(BF16)
            lo = (a - hi.astype(F32)).astype(BF16)
            return _dot(hi, avg) + _dot(lo, avg)

        dlt = o - head_mean(o)
        var = head_mean(dlt * dlt)
        yn = dlt * lax.rsqrt(var + EPS) * gn_ref[...]
        g = gate_ref[0, rs, :]
        r_buf[rs, :] = (g * jax.nn.sigmoid(g) * yn).astype(BF16)
    mix = _dot(r_buf[...], wo_ref[0:V_W, :]) + _dot(py_ref[0], wo_ref[V_W:V_W + POOL_W, :])
    y = mix * lax.rsqrt(jnp.mean(mix * mix, axis=-1, keepdims=True) + EPS) * nw_ref[...]
    o_ref[0] = x_ref[0] + g1_ref[0] * y


def retention_mixer(qk, v, gate, pool_y, x, sf, sb, tabs, gn, w_out_bf, norm_w, g1, tm):
    b, n, d = x.shape
    nch = tm // CHUNK
    row = lambda bi, i: (bi, i, 0)
    const = lambda shape: pl.BlockSpec(shape, lambda bi, i: (0,) * len(shape))
    return pl.pallas_call(
        functools.partial(_ret_kernel, nch=nch),
        out_shape=jax.ShapeDtypeStruct((b, n, d), F32),
        grid=(b, n // tm),
        in_specs=[
            pl.BlockSpec((1, tm, 2 * QK_W), row),
            pl.BlockSpec((1, tm, V_W), row),
            pl.BlockSpec((1, tm, V_W), row),
            pl.BlockSpec((1, tm, POOL_W), row),
            pl.BlockSpec((1, tm, d), row),
            pl.BlockSpec((1, nch, DK, V_W), lambda bi, i: (bi, i, 0, 0)),
            pl.BlockSpec((1, nch, DK, V_W), lambda bi, i: (bi, i, 0, 0)),
            const((N_HEADS // 2, CHUNK, 2 * CHUNK)),
            const((CHUNK, V_W)), const((CHUNK, V_W)),
            const((N_HEADS, 1, QK_W)),
            const((QK_W, V_W)),
            const((V_W, V_W)),
            const((1, V_W)),
            const((V_W + POOL_W, d)),
            const((1, d)),
            pl.BlockSpec((1, 1, d), lambda bi, i: (bi, 0, 0)),
        ],
        out_specs=pl.BlockSpec((1, tm, d), row),
        scratch_shapes=[pltpu.VMEM((tm, V_W), BF16)],
        compiler_params=_cparams("parallel", "parallel"),
        name="retention_mixer",
    )(qk, v, gate, pool_y, x, sf, sb, tabs["mpair"], tabs["qdec_f"], tabs["qdec_b"], tabs["hmask"],
      tabs["bd_bf16"], tabs["avg"], gn.reshape(1, V_W), w_out_bf, norm_w.reshape(1, d), g1)


def _router_kernel(x_ref, g_ref, sh_ref, sc_ref, wr_ref, h_ref, aff_ref):
    h = _norm_mod(x_ref[0], g_ref[...], sh_ref[0], sc_ref[0])
    h_ref[0] = h.astype(BF16)
    logits = _dot_nt(wr_ref[...], h, precision=lax.Precision.HIGHEST)
    mx = jnp.max(logits, axis=0, keepdims=True)
    e = jnp.exp(logits - mx)
    aff_ref[0] = e / jnp.sum(e, axis=0, keepdims=True)


def router(x, norm_w, shift, scale, w_router_t, tm):
    b, n, d = x.shape
    ne = w_router_t.shape[0]
    return pl.pallas_call(
        _router_kernel,
        out_shape=(jax.ShapeDtypeStruct((b, n, d), BF16), jax.ShapeDtypeStruct((b, ne, n), F32)),
        grid=(b, n // tm),
        in_specs=[
            pl.BlockSpec((1, tm, d), lambda bi, i: (bi, i, 0)),
            pl.BlockSpec((1, d), lambda bi, i: (0, 0)),
            pl.BlockSpec((1, 1, d), lambda bi, i: (bi, 0, 0)),
            pl.BlockSpec((1, 1, d), lambda bi, i: (bi, 0, 0)),
            pl.BlockSpec((ne, d), lambda bi, i: (0, 0)),
        ],
        out_specs=(
            pl.BlockSpec((1, tm, d), lambda bi, i: (bi, i, 0)),
            pl.BlockSpec((1, ne, tm), lambda bi, i: (bi, 0, i)),
        ),
        compiler_params=_cparams("parallel", "parallel"),
        name="router",
    )(x, norm_w.reshape(1, d), shift, scale, w_router_t)


def _ffn_kernel(x_ref, gate_ref, wg_ref, wu_ref, wd_ref, o_ref, *, rt):
    rows = x_ref.shape[2]
    for r0 in range(0, rows, rt):
        rs = slice(r0, r0 + rt)
        xt = x_ref[0, 0, rs, :]
        a = _dot(xt, wg_ref[0])
        u = _dot(xt, wu_ref[0])
        hh = (a * jax.nn.sigmoid(a) * u).astype(BF16)
        o_ref[0, 0, rs, :] = _dot(hh, wd_ref[0]) * gate_ref[0, 0, rs, :]


def expert_ffn(xs, gate, wg, wu, wd):
    g, ne, rows, d = xs.shape
    f = wg.shape[2]
    rt = min(rows, 256)
    one = pl.Buffered(1)
    return pl.pallas_call(
        functools.partial(_ffn_kernel, rt=rt),
        out_shape=jax.ShapeDtypeStruct((g, ne, rows, d), F32),
        grid=(ne, g),
        in_specs=[
            pl.BlockSpec((1, 1, rows, d), lambda e, gi: (gi, e, 0, 0)),
            pl.BlockSpec((1, 1, rows, 1), lambda e, gi: (gi, e, 0, 0)),
            pl.BlockSpec((1, d, f), lambda e, gi: (e, 0, 0), pipeline_mode=one),
            pl.BlockSpec((1, d, f), lambda e, gi: (e, 0, 0), pipeline_mode=one),
            pl.BlockSpec((1, f, d), lambda e, gi: (e, 0, 0), pipeline_mode=one),
        ],
        out_specs=pl.BlockSpec((1, 1, rows, d), lambda e, gi: (gi, e, 0, 0)),
        compiler_params=_cparams("arbitrary", "arbitrary"),
        name="expert_ffn",
    )(xs, gate, wg, wu, wd)


def _post_kernel(x_ref, y_ref, nw_ref, g_ref, o_ref):
    y = y_ref[0]
    yn = y * lax.rsqrt(jnp.mean(y * y, axis=-1, keepdims=True) + EPS) * nw_ref[...]
    o_ref[0] = x_ref[0] + g_ref[0] * yn


def post_residual(x, y, norm_w, g, tm):
    b, n, d = x.shape
    row = lambda bi, i: (bi, i, 0)
    return pl.pallas_call(
        _post_kernel,
        out_shape=jax.ShapeDtypeStruct((b, n, d), F32),
        grid=(b, n // tm),
        in_specs=[
            pl.BlockSpec((1, tm, d), row),
            pl.BlockSpec((1, tm, d), row),
            pl.BlockSpec((1, d), lambda bi, i: (0, 0)),
            pl.BlockSpec((1, 1, d), lambda bi, i: (bi, 0, 0)),
        ],
        out_specs=pl.BlockSpec((1, tm, d), row),
        compiler_params=_cparams("parallel", "parallel"),
        name="post_residual",
    )(x, y, norm_w.reshape(1, d), g)


def _qk_head_of_lane():
    half = QK_W // 2
    return (np.arange(QK_W) % half) // (DK // 2)


def _in_proj_perm():
    half = DK // 2
    first = [h * DK + i for h in range(N_HEADS) for i in range(half)]
    second = [h * DK + half + i for h in range(N_HEADS) for i in range(half)]
    qperm = np.array(first + second)
    rest = np.arange(2 * QK_W, 2 * QK_W + 2 * V_W + POOL_W)
    return np.concatenate([qperm, QK_W + qperm, rest])


def _static_tables():
    qk_head = _qk_head_of_lane()
    v_head = np.arange(V_W) // DV
    hmask = (qk_head[None, :] == np.arange(N_HEADS)[:, None])
    bd = (qk_head[:, None] == v_head[None, :])
    avg = (v_head[:, None] == v_head[None, :]).astype(np.float32) / DV
    return {
        "hmask": jnp.asarray(hmask[:, None, :], BF16),
        "bd_f32": jnp.asarray(bd, F32),
        "bd_bf16": jnp.asarray(bd, BF16),
        "avg": jnp.asarray(avg, BF16),
    }


def _decay_tables(lg_f, lg_b):
    pos = jnp.arange(CHUNK, dtype=F32)
    diff = pos[:, None] - pos[None, :]
    low = diff >= 0
    up = diff < 0
    m_f = jnp.where(low, jnp.exp(lg_f[:, None, None] * jnp.where(low, diff, 0.0)), 0.0)
    m_b = jnp.where(up, jnp.exp(lg_b[:, None, None] * jnp.where(up, -diff, 0.0)), 0.0)
    m = m_f + m_b
    mpair = m.reshape(N_HEADS // 2, 2, CHUNK, CHUNK).transpose(0, 2, 1, 3).reshape(
        N_HEADS // 2, CHUNK, 2 * CHUNK)
    qk_head = _qk_head_of_lane()
    v_head = np.arange(V_W) // DV
    qdec_f = jnp.exp(lg_f[None, :] * (pos[:, None] + 1.0))[:, v_head]
    qdec_b = jnp.exp(lg_b[None, :] * (CHUNK - pos[:, None]))[:, v_head]
    kdec_f = jnp.exp(lg_f[None, :] * (CHUNK - 1.0 - pos[:, None]))[:, qk_head]
    kdec_b = jnp.exp(lg_b[None, :] * pos[:, None])[:, qk_head]
    cdec_f = jnp.exp(lg_f * CHUNK)[None, v_head]
    cdec_b = jnp.exp(lg_b * CHUNK)[None, v_head]
    return {"mpair": mpair, "qdec_f": qdec_f, "qdec_b": qdec_b, "kdec_f": kdec_f,
            "kdec_b": kdec_b, "cdec_f": cdec_f, "cdec_b": cdec_b}


def _rope_tables(n):
    t = jnp.arange(n)
    row = (t // GRID_W).astype(F32)
    col = (t % GRID_W).astype(F32)
    n_freq = DK // 4
    inv = ROPE_BASE ** (-jnp.arange(n_freq, dtype=F32) / n_freq)
    ang = jnp.concatenate([row[:, None] * inv, col[:, None] * inv], axis=-1)
    return jnp.tile(jnp.cos(ang), (1, N_HEADS)), jnp.tile(jnp.sin(ang), (1, N_HEADS))


def _moe(h_bf, aff_t, wg, wu, wd, merge_batches):
    b, n, d = h_bf.shape
    cap = EC_FACTOR * n // N_EXPERTS
    gate, idx = lax.top_k(aff_t, cap)
    xs = jax.vmap(lambda hb, ib: hb[ib])(h_bf, idx)
    if merge_batches:
        xs = xs.transpose(1, 0, 2, 3).reshape(1, N_EXPERTS, b * cap, d)
        gt = gate.transpose(1, 0, 2).reshape(1, N_EXPERTS, b * cap, 1)
        y = expert_ffn(xs, gt, wg, wu, wd)
        y = y.reshape(N_EXPERTS, b, cap, d).transpose(1, 0, 2, 3)
    else:
        y = expert_ffn(xs, gate[..., None], wg, wu, wd)
    return jax.vmap(lambda ib, yb: jnp.zeros((n, d), yb.dtype).at[ib.reshape(-1)].add(yb.reshape(-1, d)))(idx, y)


def kernel(x, c, ctx, c_ctx, w_ada, b_ada, norm_pre_mix, norm_post_mix, norm_pre_ffn, norm_post_ffn, w_in, ret_decay_fwd, ret_decay_bwd, ret_gn, pool_w, pool_scale, w_out, w_router, w_gate, w_up, w_down):
    b, n, d = x.shape
    lc = ctx.shape[1]
    depth = w_ada.shape[0]
    rope = _rope_tables(n)
    static = _static_tables()
    perm = _in_proj_perm()

    cc = jnp.concatenate([c, c_ctx[None, :], jnp.zeros((7, d), F32)], axis=0)
    mods = ada_modulation(cc, w_ada, b_ada)

    tm = 512
    for l in range(depth):
        last = l == depth - 1
        mx = mods[l, :b].reshape(b, 1, 6, d)
        sh1, sc1, g1, sh2, sc2, g2 = [mx[:, :, i] for i in range(6)]
        mc = jnp.broadcast_to(mods[l, b].reshape(1, 1, 6, d), (b, 1, 6, d))
        csh1, csc1, cg1, csh2, csc2, cg2 = [mc[:, :, i] for i in range(6)]
        lg_f = jax.nn.log_sigmoid(ret_decay_fwd[l].astype(F32))
        lg_b = jax.nn.log_sigmoid(ret_decay_bwd[l].astype(F32))
        tabs = dict(static, **_decay_tables(lg_f, lg_b))
        w_in_p = w_in[l][:, perm].astype(BF16)
        w_out_bf = w_out[l].astype(BF16)
        pool_w_bf = pool_w[l].astype(BF16)

        qk_c, v_c, gate_c, p_c = premix(ctx, norm_pre_mix[l], csh1, csc1, w_in_p, None, lc)
        zero = jnp.zeros((b, DK, V_W), F32)
        sf_c, sb_c, s_f, s_b = state_scan(qk_c, v_c, zero, zero, tabs)
        qk_x, v_x, gate_x, p_x = premix(x, norm_pre_mix[l], sh1, sc1, w_in_p, rope, tm)
        sf_x, sb_x, _, _ = state_scan(qk_x, v_x, s_f, s_b, tabs)
        pool_x = pool_mixer(p_x, pool_w_bf, pool_scale[l], GRID_W, min(n, 2048))
        x = retention_mixer(qk_x, v_x, gate_x, pool_x, x, sf_x, sb_x, tabs, ret_gn[l], w_out_bf,
                            norm_post_mix[l], g1, tm)
        if not last:
            pool_c = pool_mixer(p_c, pool_w_bf, pool_scale[l], lc, lc)
            ctx = retention_mixer(qk_c, v_c, gate_c, pool_c, ctx, sf_c, sb_c, tabs, ret_gn[l],
                                  w_out_bf, norm_post_mix[l], cg1, lc)

        wg = w_gate[l].astype(BF16)
        wu = w_up[l].astype(BF16)
        wd = w_down[l].astype(BF16)
        wr_t = w_router[l].T
        h_x, aff_x = router(x, norm_pre_ffn[l], sh2, sc2, wr_t, tm)
        ffn_x = _moe(h_x, aff_x, wg, wu, wd, False)
        x = post_residual(x, ffn_x, norm_post_ffn[l], g2, tm)
        if not last:
            h_c, aff_c = router(ctx, norm_pre_ffn[l], csh2, csc2, wr_t, lc)
            ffn_c = _moe(h_c, aff_c, wg, wu, wd, True)
            ctx = post_residual(ctx, ffn_c, norm_post_ffn[l], cg2, lc)
    return x
```

```python
import functools

import jax
import jax.numpy as jnp
import numpy as np
from jax import lax
from jax.experimental import pallas as pl
from jax.experimental.pallas import tpu as pltpu

F32 = jnp.float32
BF16 = jnp.bfloat16

D_MODEL = 1024
GRID_W = 64
N_HEADS = 8
DV = 64
DK = 32
QK_W = N_HEADS * DK
V_W = N_HEADS * DV
POOL_W = 512
POOL_WINDOWS = (2, 4, 8, 16)
POOL_GROUP_DIM = 128
CHUNK = 128
ROPE_BASE = 10000.0
N_EXPERTS = 16
EC_FACTOR = 2
EPS = 1e-6
MAX_HALF_WINDOW = max(POOL_WINDOWS) // 2

VMEM_LIMIT = 56 * 1024 * 1024


def _cparams(*sem):
    return pltpu.CompilerParams(dimension_semantics=sem, vmem_limit_bytes=VMEM_LIMIT)


def _dot(a, b):
    return jnp.dot(a, b, preferred_element_type=F32)


def _dot_nt(a, b, precision=None):
    return lax.dot_general(a, b, (((1,), (1,)), ((), ())), precision=precision,
                           preferred_element_type=F32)


def _ada_kernel(cc_ref, w_ref, b_ref, o_ref):
    s = cc_ref[...]
    s = s * jax.nn.sigmoid(s)
    o_ref[0] = _dot(s.astype(BF16), w_ref[0].astype(BF16)) + b_ref[0]


def ada_modulation(cc, w_ada, b_ada):
    depth, d, d6 = w_ada.shape
    rows = cc.shape[0]
    tn = 1536
    return pl.pallas_call(
        _ada_kernel,
        out_shape=jax.ShapeDtypeStruct((depth, rows, d6), F32),
        grid=(depth, d6 // tn),
        in_specs=[
            pl.BlockSpec((rows, d), lambda l, j: (0, 0)),
            pl.BlockSpec((1, d, tn), lambda l, j: (l, 0, j)),
            pl.BlockSpec((1, 1, tn), lambda l, j: (l, 0, j)),
        ],
        out_specs=pl.BlockSpec((1, rows, tn), lambda l, j: (l, 0, j)),
        compiler_params=_cparams("parallel", "parallel"),
        name="ada_modulation",
    )(cc, w_ada, b_ada.reshape(depth, 1, d6))


def _norm_mod(xf, g, sh, sc):
    y = xf * lax.rsqrt(jnp.mean(xf * xf, axis=-1, keepdims=True) + EPS)
    return (y * g) * (1.0 + sc) + sh


def _premix_kernel(x_ref, g_ref, sh_ref, sc_ref, w_ref, *rest, rope):
    if rope:
        cos_ref, sin_ref, qk_ref, v_ref, gate_ref, p_ref = rest
    else:
        qk_ref, v_ref, gate_ref, p_ref = rest
    hb = _norm_mod(x_ref[0], g_ref[...], sh_ref[0], sc_ref[0]).astype(BF16)
    zqk = _dot(hb, w_ref[:, 0:2 * QK_W])
    half = QK_W // 2
    q1, q2 = zqk[:, 0:half], zqk[:, half:2 * half]
    k1 = zqk[:, 2 * half:3 * half] * (DK ** -0.5)
    k2 = zqk[:, 3 * half:4 * half] * (DK ** -0.5)
    if rope:
        cos, sin = cos_ref[...], sin_ref[...]
        q1, q2 = q1 * cos - q2 * sin, q1 * sin + q2 * cos
        k1, k2 = k1 * cos - k2 * sin, k1 * sin + k2 * cos
    qk_ref[0] = jnp.concatenate([q1, q2, k1, k2], axis=1).astype(BF16)
    o = 2 * QK_W
    v_ref[0] = _dot(hb, w_ref[:, o:o + V_W]).astype(BF16)
    gate_ref[0] = _dot(hb, w_ref[:, o + V_W:o + 2 * V_W])
    p_ref[0] = _dot(hb, w_ref[:, o + 2 * V_W:o + 2 * V_W + POOL_W])


def premix(x, norm_w, shift, scale, w_in_p, rope, tm):
    b, n, d = x.shape
    in_w = w_in_p.shape[1]
    row = lambda bi, i: (bi, i, 0)
    in_specs = [
        pl.BlockSpec((1, tm, d), row),
        pl.BlockSpec((1, d), lambda bi, i: (0, 0)),
        pl.BlockSpec((1, 1, d), lambda bi, i: (bi, 0, 0)),
        pl.BlockSpec((1, 1, d), lambda bi, i: (bi, 0, 0)),
        pl.BlockSpec((d, in_w), lambda bi, i: (0, 0)),
    ]
    args = [x, norm_w.reshape(1, d), shift, scale, w_in_p]
    if rope is not None:
        half = QK_W // 2
        in_specs += [pl.BlockSpec((tm, half), lambda bi, i: (i, 0))] * 2
        args += list(rope)
    return pl.pallas_call(
        functools.partial(_premix_kernel, rope=rope is not None),
        out_shape=(
            jax.ShapeDtypeStruct((b, n, 2 * QK_W), BF16),
            jax.ShapeDtypeStruct((b, n, V_W), BF16),
            jax.ShapeDtypeStruct((b, n, V_W), F32),
            jax.ShapeDtypeStruct((b, n, POOL_W), F32),
        ),
        grid=(b, n // tm),
        in_specs=in_specs,
        out_specs=(
            pl.BlockSpec((1, tm, 2 * QK_W), row),
            pl.BlockSpec((1, tm, V_W), row),
            pl.BlockSpec((1, tm, V_W), row),
            pl.BlockSpec((1, tm, POOL_W), row),
        ),
        compiler_params=_cparams("parallel", "parallel"),
        name="premix",
    )(*args)


def _fold_state(s):
    g = 2 * N_HEADS
    top = s[0:g]
    bot = s[QK_W // 2:QK_W // 2 + g]
    for h in range(1, N_HEADS):
        top = top + s[h * g:(h + 1) * g]
        bot = bot + s[QK_W // 2 + h * g:QK_W // 2 + (h + 1) * g]
    return jnp.concatenate([top, bot], axis=0)


def _expand_state(c):
    g = 2 * N_HEADS
    return jnp.concatenate([c[0:g]] * N_HEADS + [c[g:2 * g]] * N_HEADS, axis=0)


def _state_kernel(kf_ref, vf_ref, kb_ref, vb_ref, s0f_ref, s0b_ref, kdf_ref, kdb_ref,
                  cdf_ref, cdb_ref, bd_ref, sf_out, sb_out, ff_out, fb_out, sf_acc, sb_acc):
    c = pl.program_id(1)
    nc = pl.num_programs(1)
    bd = bd_ref[...]

    @pl.when(c == 0)
    def _():
        sf_acc[...] = _expand_state(s0f_ref[0]) * bd
        sb_acc[...] = _expand_state(s0b_ref[0]) * bd

    sf = sf_acc[...]
    sb = sb_acc[...]
    sf_out[0, 0] = _fold_state(sf).astype(BF16)
    sb_out[0, 0] = _fold_state(sb).astype(BF16)

    def update(s, k_ref, v_ref, kd_ref, cd_ref):
        kd = k_ref[0].astype(F32) * kd_ref[...]
        f = _dot(kd.T.astype(BF16), v_ref[0])
        return s * cd_ref[...] + f * bd

    sf_new = update(sf, kf_ref, vf_ref, kdf_ref, cdf_ref)
    sb_new = update(sb, kb_ref, vb_ref, kdb_ref, cdb_ref)
    sf_acc[...] = sf_new
    sb_acc[...] = sb_new

    @pl.when(c == nc - 1)
    def _():
        ff_out[0] = _fold_state(sf_new)
        fb_out[0] = _fold_state(sb_new)


def state_scan(qk, v, s0f, s0b, tabs):
    b, n, _ = qk.shape
    nc = n // CHUNK
    const = lambda shape: pl.BlockSpec(shape, lambda bi, c: (0,) * len(shape))
    return pl.pallas_call(
        _state_kernel,
        out_shape=(
            jax.ShapeDtypeStruct((b, nc, DK, V_W), BF16),
            jax.ShapeDtypeStruct((b, nc, DK, V_W), BF16),
            jax.ShapeDtypeStruct((b, DK, V_W), F32),
            jax.ShapeDtypeStruct((b, DK, V_W), F32),
        ),
        grid=(b, nc),
        in_specs=[
            pl.BlockSpec((1, CHUNK, QK_W), lambda bi, c: (bi, c, 1)),
            pl.BlockSpec((1, CHUNK, V_W), lambda bi, c: (bi, c, 0)),
            pl.BlockSpec((1, CHUNK, QK_W), lambda bi, c: (bi, nc - 1 - c, 1)),
            pl.BlockSpec((1, CHUNK, V_W), lambda bi, c: (bi, nc - 1 - c, 0)),
            pl.BlockSpec((1, DK, V_W), lambda bi, c: (bi, 0, 0)),
            pl.BlockSpec((1, DK, V_W), lambda bi, c: (bi, 0, 0)),
            const((CHUNK, QK_W)), const((CHUNK, QK_W)),
            const((1, V_W)), const((1, V_W)),
            const((QK_W, V_W)),
        ],
        out_specs=(
            pl.BlockSpec((1, 1, DK, V_W), lambda bi, c: (bi, c, 0, 0)),
            pl.BlockSpec((1, 1, DK, V_W), lambda bi, c: (bi, nc - 1 - c, 0, 0)),
            pl.BlockSpec((1, DK, V_W), lambda bi, c: (bi, 0, 0)),
            pl.BlockSpec((1, DK, V_W), lambda bi, c: (bi, 0, 0)),
        ),
        scratch_shapes=[pltpu.VMEM((QK_W, V_W), F32), pltpu.VMEM((QK_W, V_W), F32)],
        compiler_params=_cparams("parallel", "arbitrary"),
        name="state_scan",
    )(qk, v, qk, v, s0f, s0b, tabs["kdec_f"], tabs["kdec_b"], tabs["cdec_f"], tabs["cdec_b"],
      tabs["bd_f32"])


def _pool_kernel(*refs, gw, tile, halo, rows_total):
    if halo:
        prev_ref, cur_ref, next_ref, pw_ref, ps_ref, o_ref, buf = refs
    else:
        cur_ref, pw_ref, ps_ref, o_ref, buf = refs
    i = pl.program_id(1)
    last = pl.num_programs(1) - 1
    m = MAX_HALF_WINDOW
    span = tile + 2 * halo
    zeros_m = jnp.zeros((m, POOL_W), F32)
    buf[0:m] = zeros_m
    buf[m + span:2 * m + span] = zeros_m
    if halo:
        buf[m:m + halo] = jnp.where(i > 0, prev_ref[0], 0.0)
        buf[m + halo + tile:m + span] = jnp.where(i < last, next_ref[0], 0.0)
    buf[m + halo:m + halo + tile] = cur_ref[0]

    pos = lax.broadcasted_iota(jnp.int32, (span, 1), 0)
    col = pos % gw
    tpos = lax.broadcasted_iota(jnp.int32, (tile, 1), 0)
    tcol = tpos % gw
    trow = i * (tile // gw) + tpos // gw
    for gi, w in enumerate(POOL_WINDOWS):
        cs = slice(gi * POOL_GROUP_DIM, (gi + 1) * POOL_GROUP_DIM)
        hw = w // 2
        s = None
        for d in range(-hw, hw):
            valid = (col + d >= 0) & (col + d < gw)
            term = jnp.where(valid, buf[m + d:m + d + span, cs], 0.0)
            s = term if s is None else s + term
        if halo:
            acc = None
            for d in range(-hw, hw):
                start = halo + d * gw
                term = s[start:start + tile]
                acc = term if acc is None else acc + term
            cnt_r = jnp.minimum(trow + hw, rows_total) - jnp.maximum(trow - hw, 0)
        else:
            acc = s
            cnt_r = 1
        cnt_c = jnp.minimum(tcol + hw, gw) - jnp.maximum(tcol - hw, 0)
        cnt = (cnt_r * cnt_c).astype(F32)
        xg = buf[m + halo:m + halo + tile, cs]
        diff = (acc / cnt - xg).astype(BF16)
        y = _dot(diff, pw_ref[gi]) * ps_ref[:, cs]
        o_ref[0, :, cs] = y.astype(BF16)


def pool_mixer(p, pool_w_bf, pool_scale, gw, tile):
    b, n, pw = p.shape
    rows_total = n // gw
    halo = MAX_HALF_WINDOW * gw if rows_total > 1 else 0
    in_specs, args = [], []
    if halo:
        r = tile // halo
        nh = n // halo
        in_specs = [
            pl.BlockSpec((1, halo, pw), lambda bi, i: (bi, jnp.maximum(i * r - 1, 0), 0)),
            pl.BlockSpec((1, tile, pw), lambda bi, i: (bi, i, 0)),
            pl.BlockSpec((1, halo, pw), lambda bi, i: (bi, jnp.minimum((i + 1) * r, nh - 1), 0)),
        ]
        args = [p, p, p]
    else:
        in_specs = [pl.BlockSpec((1, tile, pw), lambda bi, i: (bi, i, 0))]
        args = [p]
    ng = len(POOL_WINDOWS)
    in_specs += [
        pl.BlockSpec((ng, POOL_GROUP_DIM, POOL_GROUP_DIM), lambda bi, i: (0, 0, 0)),
        pl.BlockSpec((1, pw), lambda bi, i: (0, 0)),
    ]
    args += [pool_w_bf, pool_scale.reshape(1, pw)]
    return pl.pallas_call(
        functools.partial(_pool_kernel, gw=gw, tile=tile, halo=halo, rows_total=rows_total),
        out_shape=jax.ShapeDtypeStruct((b, n, pw), BF16),
        grid=(b, n // tile),
        in_specs=in_specs,
        out_specs=pl.BlockSpec((1, tile, pw), lambda bi, i: (bi, i, 0)),
        scratch_shapes=[pltpu.VMEM((tile + 2 * halo + 2 * MAX_HALF_WINDOW, pw), F32)],
        compiler_params=_cparams("parallel", "parallel"),
        name="pool_mixer",
    )(*args)


def _ret_kernel(qk_ref, v_ref, gate_ref, py_ref, x_ref, sf_ref, sb_ref, mp_ref, qdf_ref, qdb_ref,
                hm_ref, bd_ref, avg_ref, gn_ref, wo_ref, nw_ref, g1_ref, o_ref, r_buf, *, nch):
    bd = bd_ref[...]
    avg = avg_ref[...]
    lane = lax.broadcasted_iota(jnp.int32, (1, 2 * DV), 1)
    lo_mask = (lane < DV).astype(BF16)
    hi_mask = (lane >= DV).astype(BF16)
    for ci in range(nch):
        rs = slice(ci * CHUNK, (ci + 1) * CHUNK)
        q = qk_ref[0, rs, 0:QK_W]
        k = qk_ref[0, rs, QK_W:2 * QK_W]
        vv = v_ref[0, rs, :]
        inner = []
        for j in range(N_HEADS // 2):
            kp = jnp.concatenate([k * hm_ref[2 * j], k * hm_ref[2 * j + 1]], axis=0)
            sc = _dot_nt(q, kp) * mp_ref[j]
            vpair = vv[:, 2 * DV * j:2 * DV * (j + 1)]
            vp = jnp.concatenate([vpair * lo_mask, vpair * hi_mask], axis=0)
            inner.append(_dot(sc.astype(BF16), vp))
        o = jnp.concatenate(inner, axis=1)
        sf_bd = _expand_state(sf_ref[0, ci]) * bd
        sb_bd = _expand_state(sb_ref[0, ci]) * bd
        o = o + _dot(q, sf_bd) * qdf_ref[...] + _dot(q, sb_bd) * qdb_ref[...]

        def head_mean(a):
            hi = a.astype(BF16)
            lo = (a - hi.astype(F32)).astype(BF16)
            return _dot(hi, avg) + _dot(lo, avg)

        dlt = o - head_mean(o)
        var = head_mean(dlt * dlt)
        yn = dlt * lax.rsqrt(var + EPS) * gn_ref[...]
        g = gate_ref[0, rs, :]
        r_buf[rs, :] = (g * jax.nn.sigmoid(g) * yn).astype(BF16)
    mix = _dot(r_buf[...], wo_ref[0:V_W, :]) + _dot(py_ref[0], wo_ref[V_W:V_W + POOL_W, :])
    y = mix * lax.rsqrt(jnp.mean(mix * mix, axis=-1, keepdims=True) + EPS) * nw_ref[...]
    o_ref[0] = x_ref[0] + g1_ref[0] * y


def retention_mixer(qk, v, gate, pool_y, x, sf, sb, tabs, gn, w_out_bf, norm_w, g1, tm):
    b, n, d = x.shape
    nch = tm // CHUNK
    row = lambda bi, i: (bi, i, 0)
    const = lambda shape: pl.BlockSpec(shape, lambda bi, i: (0,) * len(shape))
    return pl.pallas_call(
        functools.partial(_ret_kernel, nch=nch),
        out_shape=jax.ShapeDtypeStruct((b, n, d), F32),
        grid=(b, n // tm),
        in_specs=[
            pl.BlockSpec((1, tm, 2 * QK_W), row),
            pl.BlockSpec((1, tm, V_W), row),
            pl.BlockSpec((1, tm, V_W), row),
            pl.BlockSpec((1, tm, POOL_W), row),
            pl.BlockSpec((1, tm, d), row),
            pl.BlockSpec((1, nch, DK, V_W), lambda bi, i: (bi, i, 0, 0)),
            pl.BlockSpec((1, nch, DK, V_W), lambda bi, i: (bi, i, 0, 0)),
            const((N_HEADS // 2, CHUNK, 2 * CHUNK)),
            const((CHUNK, V_W)), const((CHUNK, V_W)),
            const((N_HEADS, 1, QK_W)),
            const((QK_W, V_W)),
            const((V_W, V_W)),
            const((1, V_W)),
            const((V_W + POOL_W, d)),
            const((1, d)),
            pl.BlockSpec((1, 1, d), lambda bi, i: (bi, 0, 0)),
        ],
        out_specs=pl.BlockSpec((1, tm, d), row),
        scratch_shapes=[pltpu.VMEM((tm, V_W), BF16)],
        compiler_params=_cparams("parallel", "parallel"),
        name="retention_mixer",
    )(qk, v, gate, pool_y, x, sf, sb, tabs["mpair"], tabs["qdec_f"], tabs["qdec_b"], tabs["hmask"],
      tabs["bd_bf16"], tabs["avg"], gn.reshape(1, V_W), w_out_bf, norm_w.reshape(1, d), g1)


def _router_kernel(x_ref, g_ref, sh_ref, sc_ref, wr_ref, h_ref, aff_ref):
    h = _norm_mod(x_ref[0], g_ref[...], sh_ref[0], sc_ref[0])
    h_ref[0] = h.astype(BF16)
    logits = _dot_nt(wr_ref[...], h, precision=lax.Precision.HIGHEST)
    mx = jnp.max(logits, axis=0, keepdims=True)
    e = jnp.exp(logits - mx)
    aff_ref[0] = e / jnp.sum(e, axis=0, keepdims=True)


def router(x, norm_w, shift, scale, w_router_t, tm):
    b, n, d = x.shape
    ne = w_router_t.shape[0]
    return pl.pallas_call(
        _router_kernel,
        out_shape=(jax.ShapeDtypeStruct((b, n, d), BF16), jax.ShapeDtypeStruct((b, ne, n), F32)),
        grid=(b, n // tm),
        in_specs=[
            pl.BlockSpec((1, tm, d), lambda bi, i: (bi, i, 0)),
            pl.BlockSpec((1, d), lambda bi, i: (0, 0)),
            pl.BlockSpec((1, 1, d), lambda bi, i: (bi, 0, 0)),
            pl.BlockSpec((1, 1, d), lambda bi, i: (bi, 0, 0)),
            pl.BlockSpec((ne, d), lambda bi, i: (0, 0)),
        ],
        out_specs=(
            pl.BlockSpec((1, tm, d), lambda bi, i: (bi, i, 0)),
            pl.BlockSpec((1, ne, tm), lambda bi, i: (bi, 0, i)),
        ),
        compiler_params=_cparams("parallel", "parallel"),
        name="router",
    )(x, norm_w.reshape(1, d), shift, scale, w_router_t)


def _ffn_kernel(x_ref, gate_ref, wg_ref, wu_ref, wd_ref, o_ref, *, rt):
    rows = x_ref.shape[2]
    for r0 in range(0, rows, rt):
        rs = slice(r0, r0 + rt)
        xt = x_ref[0, 0, rs, :]
        a = _dot(xt, wg_ref[0])
        u = _dot(xt, wu_ref[0])
        hh = (a * jax.nn.sigmoid(a) * u).astype(BF16)
        o_ref[0, 0, rs, :] = (_dot(hh, wd_ref[0]) * gate_ref[0, 0, rs, :]).astype(BF16)


def expert_ffn(xs, gate, wg, wu, wd):
    g, ne, rows, d = xs.shape
    f = wg.shape[2]
    rt = min(rows, 256)
    one = pl.Buffered(1)
    return pl.pallas_call(
        functools.partial(_ffn_kernel, rt=rt),
        out_shape=jax.ShapeDtypeStruct((g, ne, rows, d), BF16),
        grid=(ne, g),
        in_specs=[
            pl.BlockSpec((1, 1, rows, d), lambda e, gi: (gi, e, 0, 0)),
            pl.BlockSpec((1, 1, rows, 1), lambda e, gi: (gi, e, 0, 0)),
            pl.BlockSpec((1, d, f), lambda e, gi: (e, 0, 0), pipeline_mode=one),
            pl.BlockSpec((1, d, f), lambda e, gi: (e, 0, 0), pipeline_mode=one),
            pl.BlockSpec((1, f, d), lambda e, gi: (e, 0, 0), pipeline_mode=one),
        ],
        out_specs=pl.BlockSpec((1, 1, rows, d), lambda e, gi: (gi, e, 0, 0)),
        compiler_params=_cparams("arbitrary", "arbitrary"),
        name="expert_ffn",
    )(xs, gate, wg, wu, wd)


ROW_BLOCK = 128
COMBINE_WIN = 2 * ROW_BLOCK


def _combine_kernel(st_ref, idx_ref, y_hbm, x_ref, nw_ref, g_ref, o_ref, buf, xbuf, sem, xsem,
                    acc_ref, *, tt, ne, cap, merged):
    b = pl.program_id(0)
    k = pl.program_id(1)
    nk = pl.num_programs(1)
    nblk = y_hbm.shape[2] // ROW_BLOCK
    g = 0 if merged else b
    base = b * cap if merged else 0
    tok = k * tt + lax.broadcasted_iota(jnp.int32, (tt, 1), 0)
    lane = lax.broadcasted_iota(jnp.int32, (1, COMBINE_WIN), 1)

    def bounds(e):
        lo = base + st_ref[(b * (nk + 1) + k) * ne + e]
        hi = base + st_ref[(b * (nk + 1) + k + 1) * ne + e]
        w0 = jnp.minimum(lo // ROW_BLOCK, nblk - COMBINE_WIN // ROW_BLOCK)
        return lo, hi, w0

    def win_copy(e, w0):
        r0 = pl.multiple_of(w0 * ROW_BLOCK, ROW_BLOCK)
        return pltpu.make_async_copy(y_hbm.at[g, e, pl.ds(r0, COMBINE_WIN), :], buf.at[e], sem.at[e])

    def one_hot(e, r0, width, lo, hi):
        idx_w = idx_ref[0, e, :, pl.ds(pl.multiple_of(r0, ROW_BLOCK), width)]
        row = r0 + lane[:, 0:width]
        return ((idx_w == tok) & (row >= lo) & (row < hi)).astype(BF16)

    for e in range(ne):
        win_copy(e, bounds(e)[2]).start()
    acc_ref[...] = jnp.zeros_like(acc_ref)
    for e in range(ne):
        lo, hi, w0 = bounds(e)
        win_copy(e, w0).wait()
        acc_ref[...] += _dot(one_hot(e, w0 * ROW_BLOCK, COMBINE_WIN, lo, hi), buf[e])

        def extra(w, carry, e=e, lo=lo, hi=hi):
            r0 = pl.multiple_of(w * ROW_BLOCK, ROW_BLOCK)
            cp = pltpu.make_async_copy(y_hbm.at[g, e, pl.ds(r0, ROW_BLOCK), :], xbuf, xsem)
            cp.start()
            cp.wait()
            acc_ref[...] += _dot(one_hot(e, r0, ROW_BLOCK, lo, hi), xbuf[...])
            return carry

        lax.fori_loop(w0 + COMBINE_WIN // ROW_BLOCK, (hi + ROW_BLOCK - 1) // ROW_BLOCK, extra, 0)

    y = acc_ref[...]
    yn = y * lax.rsqrt(jnp.mean(y * y, axis=-1, keepdims=True) + EPS) * nw_ref[...]
    o_ref[0] = x_ref[0] + g_ref[0] * yn


def combine_post(starts, idx_rows, y, x, norm_w, g, tt, cap, merged):
    b, n, d = x.shape
    _, ne, rows, _ = y.shape
    gsel = (lambda bi: 0) if merged else (lambda bi: bi)
    grid_spec = pltpu.PrefetchScalarGridSpec(
        num_scalar_prefetch=1,
        grid=(b, n // tt),
        in_specs=[
            pl.BlockSpec((1, ne, 1, rows), lambda bi, k, st: (gsel(bi), 0, 0, 0)),
            pl.BlockSpec(memory_space=pl.ANY),
            pl.BlockSpec((1, tt, d), lambda bi, k, st: (bi, k, 0)),
            pl.BlockSpec((1, d), lambda bi, k, st: (0, 0)),
            pl.BlockSpec((1, 1, d), lambda bi, k, st: (bi, 0, 0)),
        ],
        out_specs=pl.BlockSpec((1, tt, d), lambda bi, k, st: (bi, k, 0)),
        scratch_shapes=[
            pltpu.VMEM((ne, COMBINE_WIN, d), BF16),
            pltpu.VMEM((ROW_BLOCK, d), BF16),
            pltpu.SemaphoreType.DMA((ne,)),
            pltpu.SemaphoreType.DMA(()),
            pltpu.VMEM((tt, d), F32),
        ],
    )
    return pl.pallas_call(
        functools.partial(_combine_kernel, tt=tt, ne=ne, cap=cap, merged=merged),
        out_shape=jax.ShapeDtypeStruct((b, n, d), F32),
        grid_spec=grid_spec,
        compiler_params=_cparams("arbitrary", "arbitrary"),
        name="combine_post",
    )(starts, idx_rows, y, x, norm_w.reshape(1, d), g)


def _qk_head_of_lane():
    half = QK_W // 2
    return (np.arange(QK_W) % half) // (DK // 2)


def _in_proj_perm():
    half = DK // 2
    first = [h * DK + i for h in range(N_HEADS) for i in range(half)]
    second = [h * DK + half + i for h in range(N_HEADS) for i in range(half)]
    qperm = np.array(first + second)
    rest = np.arange(2 * QK_W, 2 * QK_W + 2 * V_W + POOL_W)
    return np.concatenate([qperm, QK_W + qperm, rest])


def _static_tables():
    qk_head = _qk_head_of_lane()
    v_head = np.arange(V_W) // DV
    hmask = (qk_head[None, :] == np.arange(N_HEADS)[:, None])
    bd = (qk_head[:, None] == v_head[None, :])
    avg = (v_head[:, None] == v_head[None, :]).astype(np.float32) / DV
    return {
        "hmask": jnp.asarray(hmask[:, None, :], BF16),
        "bd_f32": jnp.asarray(bd, F32),
        "bd_bf16": jnp.asarray(bd, BF16),
        "avg": jnp.asarray(avg, BF16),
    }


def _decay_tables(lg_f, lg_b):
    pos = jnp.arange(CHUNK, dtype=F32)
    diff = pos[:, None] - pos[None, :]
    low = diff >= 0
    up = diff < 0
    m_f = jnp.where(low, jnp.exp(lg_f[:, None, None] * jnp.where(low, diff, 0.0)), 0.0)
    m_b = jnp.where(up, jnp.exp(lg_b[:, None, None] * jnp.where(up, -diff, 0.0)), 0.0)
    m = m_f + m_b
    mpair = m.reshape(N_HEADS // 2, 2, CHUNK, CHUNK).transpose(0, 2, 1, 3).reshape(
        N_HEADS // 2, CHUNK, 2 * CHUNK)
    qk_head = _qk_head_of_lane()
    v_head = np.arange(V_W) // DV
    qdec_f = jnp.exp(lg_f[None, :] * (pos[:, None] + 1.0))[:, v_head]
    qdec_b = jnp.exp(lg_b[None, :] * (CHUNK - pos[:, None]))[:, v_head]
    kdec_f = jnp.exp(lg_f[None, :] * (CHUNK - 1.0 - pos[:, None]))[:, qk_head]
    kdec_b = jnp.exp(lg_b[None, :] * pos[:, None])[:, qk_head]
    cdec_f = jnp.exp(lg_f * CHUNK)[None, v_head]
    cdec_b = jnp.exp(lg_b * CHUNK)[None, v_head]
    return {"mpair": mpair, "qdec_f": qdec_f, "qdec_b": qdec_b, "kdec_f": kdec_f,
            "kdec_b": kdec_b, "cdec_f": cdec_f, "cdec_b": cdec_b}


def _rope_tables(n):
    t = jnp.arange(n)
    row = (t // GRID_W).astype(F32)
    col = (t % GRID_W).astype(F32)
    n_freq = DK // 4
    inv = ROPE_BASE ** (-jnp.arange(n_freq, dtype=F32) / n_freq)
    ang = jnp.concatenate([row[:, None] * inv, col[:, None] * inv], axis=-1)
    return jnp.tile(jnp.cos(ang), (1, N_HEADS)), jnp.tile(jnp.sin(ang), (1, N_HEADS))


def _moe_residual(x, h_bf, aff_t, wg, wu, wd, norm_w, g, merged):
    b, n, d = h_bf.shape
    cap = EC_FACTOR * n // N_EXPERTS
    tt = min(n, 256)
    gate, idx = lax.top_k(aff_t, cap)
    idx, gate = lax.sort((idx, gate), dimension=2, num_keys=1)
    xs = jax.vmap(lambda hb, ib: hb[ib])(h_bf, idx)
    edges = jnp.arange(n // tt + 1, dtype=jnp.int32) * tt
    starts = jnp.sum(idx[:, :, None, :] < edges[None, None, :, None], axis=-1, dtype=jnp.int32)
    starts = starts.transpose(0, 2, 1).reshape(-1)
    if merged:
        xs = xs.transpose(1, 0, 2, 3).reshape(1, N_EXPERTS, b * cap, d)
        gate = gate.transpose(1, 0, 2).reshape(1, N_EXPERTS, b * cap)
        idx = idx.transpose(1, 0, 2).reshape(1, N_EXPERTS, b * cap)
    y = expert_ffn(xs, gate[..., None], wg, wu, wd)
    return combine_post(starts, idx[:, :, None, :], y, x, norm_w, g, tt, cap, merged)


def kernel(x, c, ctx, c_ctx, w_ada, b_ada, norm_pre_mix, norm_post_mix, norm_pre_ffn, norm_post_ffn, w_in, ret_decay_fwd, ret_decay_bwd, ret_gn, pool_w, pool_scale, w_out, w_router, w_gate, w_up, w_down):
    b, n, d = x.shape
    lc = ctx.shape[1]
    depth = w_ada.shape[0]
    rope = _rope_tables(n)
    static = _static_tables()
    perm = _in_proj_perm()

    cc = jnp.concatenate([c, c_ctx[None, :], jnp.zeros((7, d), F32)], axis=0)
    mods = ada_modulation(cc, w_ada, b_ada)

    tm = 512
    for l in range(depth):
        last = l == depth - 1
        mx = mods[l, :b].reshape(b, 1, 6, d)
        sh1, sc1, g1, sh2, sc2, g2 = [mx[:, :, i] for i in range(6)]
        mc = jnp.broadcast_to(mods[l, b].reshape(1, 1, 6, d), (b, 1, 6, d))
        csh1, csc1, cg1, csh2, csc2, cg2 = [mc[:, :, i] for i in range(6)]
        lg_f = jax.nn.log_sigmoid(ret_decay_fwd[l].astype(F32))
        lg_b = jax.nn.log_sigmoid(ret_decay_bwd[l].astype(F32))
        tabs = dict(static, **_decay_tables(lg_f, lg_b))
        w_in_p = w_in[l][:, perm].astype(BF16)
        w_out_bf = w_out[l].astype(BF16)
        pool_w_bf = pool_w[l].astype(BF16)

        qk_c, v_c, gate_c, p_c = premix(ctx, norm_pre_mix[l], csh1, csc1, w_in_p, None, lc)
        zero = jnp.zeros((b, DK, V_W), F32)
        sf_c, sb_c, s_f, s_b = state_scan(qk_c, v_c, zero, zero, tabs)
        qk_x, v_x, gate_x, p_x = premix(x, norm_pre_mix[l], sh1, sc1, w_in_p, rope, tm)
        sf_x, sb_x, _, _ = state_scan(qk_x, v_x, s_f, s_b, tabs)
        pool_x = pool_mixer(p_x, pool_w_bf, pool_scale[l], GRID_W, min(n, 2048))
        x = retention_mixer(qk_x, v_x, gate_x, pool_x, x, sf_x, sb_x, tabs, ret_gn[l], w_out_bf,
                            norm_post_mix[l], g1, tm)
        if not last:
            pool_c = pool_mixer(p_c, pool_w_bf, pool_scale[l], lc, lc)
            ctx = retention_mixer(qk_c, v_c, gate_c, pool_c, ctx, sf_c, sb_c, tabs, ret_gn[l],
                                  w_out_bf, norm_post_mix[l], cg1, lc)

        wg = w_gate[l].astype(BF16)
        wu = w_up[l].astype(BF16)
        wd = w_down[l].astype(BF16)
        wr_t = w_router[l].T
        h_x, aff_x = router(x, norm_pre_ffn[l], sh2, sc2, wr_t, tm)
        x = _moe_residual(x, h_x, aff_x, wg, wu, wd, norm_post_ffn[l], g2, False)
        if not last:
            h_c, aff_c = router(ctx, norm_pre_ffn[l], csh2, csc2, wr_t, lc)
            ctx = _moe_residual(ctx, h_c, aff_c, wg, wu, wd, norm_post_ffn[l], cg2, True)
    return x
```

```python
import functools

import jax
import jax.numpy as jnp
import numpy as np
from jax import lax
from jax.experimental import pallas as pl
from jax.experimental.pallas import tpu as pltpu

F32 = jnp.float32
BF16 = jnp.bfloat16

D_MODEL = 1024
GRID_W = 64
N_HEADS = 8
DV = 64
DK = 32
QK_W = N_HEADS * DK
V_W = N_HEADS * DV
POOL_W = 512
POOL_WINDOWS = (2, 4, 8, 16)
POOL_GROUP_DIM = 128
CHUNK = 128
ROPE_BASE = 10000.0
N_EXPERTS = 16
EC_FACTOR = 2
EPS = 1e-6
MAX_HALF_WINDOW = max(POOL_WINDOWS) // 2

VMEM_LIMIT = 56 * 1024 * 1024


def _cparams(*sem):
    return pltpu.CompilerParams(dimension_semantics=sem, vmem_limit_bytes=VMEM_LIMIT)


def _dot(a, b):
    return jnp.dot(a, b, preferred_element_type=F32)


def _dot_nt(a, b, precision=None):
    return lax.dot_general(a, b, (((1,), (1,)), ((), ())), precision=precision,
                           preferred_element_type=F32)


def _ada_kernel(cc_ref, w_ref, b_ref, o_ref):
    s = cc_ref[...]
    s = s * jax.nn.sigmoid(s)
    o_ref[0] = _dot(s.astype(BF16), w_ref[0].astype(BF16)) + b_ref[0]


def ada_modulation(cc, w_ada, b_ada):
    depth, d, d6 = w_ada.shape
    rows = cc.shape[0]
    tn = 1536
    return pl.pallas_call(
        _ada_kernel,
        out_shape=jax.ShapeDtypeStruct((depth, rows, d6), F32),
        grid=(depth, d6 // tn),
        in_specs=[
            pl.BlockSpec((rows, d), lambda l, j: (0, 0)),
            pl.BlockSpec((1, d, tn), lambda l, j: (l, 0, j)),
            pl.BlockSpec((1, 1, tn), lambda l, j: (l, 0, j)),
        ],
        out_specs=pl.BlockSpec((1, rows, tn), lambda l, j: (l, 0, j)),
        compiler_params=_cparams("parallel", "parallel"),
        name="ada_modulation",
    )(cc, w_ada, b_ada.reshape(depth, 1, d6))


def _norm_mod(xf, g, sh, sc):
    y = xf * lax.rsqrt(jnp.mean(xf * xf, axis=-1, keepdims=True) + EPS)
    return (y * g) * (1.0 + sc) + sh


def _premix_kernel(x_ref, g_ref, sh_ref, sc_ref, w_ref, *rest, rope):
    if rope:
        cos_ref, sin_ref, qk_ref, v_ref, gate_ref, p_ref = rest
    else:
        qk_ref, v_ref, gate_ref, p_ref = rest
    hb = _norm_mod(x_ref[0], g_ref[...], sh_ref[0], sc_ref[0]).astype(BF16)
    zqk = _dot(hb, w_ref[:, 0:2 * QK_W])
    half = QK_W // 2
    q1, q2 = zqk[:, 0:half], zqk[:, half:2 * half]
    k1 = zqk[:, 2 * half:3 * half] * (DK ** -0.5)
    k2 = zqk[:, 3 * half:4 * half] * (DK ** -0.5)
    if rope:
        cos, sin = cos_ref[...], sin_ref[...]
        q1, q2 = q1 * cos - q2 * sin, q1 * sin + q2 * cos
        k1, k2 = k1 * cos - k2 * sin, k1 * sin + k2 * cos
    qk_ref[0] = jnp.concatenate([q1, q2, k1, k2], axis=1).astype(BF16)
    o = 2 * QK_W
    v_ref[0] = _dot(hb, w_ref[:, o:o + V_W]).astype(BF16)
    gate_ref[0] = _dot(hb, w_ref[:, o + V_W:o + 2 * V_W])
    p_ref[0] = _dot(hb, w_ref[:, o + 2 * V_W:o + 2 * V_W + POOL_W])


def premix(x, norm_w, shift, scale, w_in_p, rope, tm):
    b, n, d = x.shape
    in_w = w_in_p.shape[1]
    row = lambda bi, i: (bi, i, 0)
    in_specs = [
        pl.BlockSpec((1, tm, d), row),
        pl.BlockSpec((1, d), lambda bi, i: (0, 0)),
        pl.BlockSpec((1, 1, d), lambda bi, i: (bi, 0, 0)),
        pl.BlockSpec((1, 1, d), lambda bi, i: (bi, 0, 0)),
        pl.BlockSpec((d, in_w), lambda bi, i: (0, 0)),
    ]
    args = [x, norm_w.reshape(1, d), shift, scale, w_in_p]
    if rope is not None:
        half = QK_W // 2
        in_specs += [pl.BlockSpec((tm, half), lambda bi, i: (i, 0))] * 2
        args += list(rope)
    return pl.pallas_call(
        functools.partial(_premix_kernel, rope=rope is not None),
        out_shape=(
            jax.ShapeDtypeStruct((b, n, 2 * QK_W), BF16),
            jax.ShapeDtypeStruct((b, n, V_W), BF16),
            jax.ShapeDtypeStruct((b, n, V_W), F32),
            jax.ShapeDtypeStruct((b, n, POOL_W), F32),
        ),
        grid=(b, n // tm),
        in_specs=in_specs,
        out_specs=(
            pl.BlockSpec((1, tm, 2 * QK_W), row),
            pl.BlockSpec((1, tm, V_W), row),
            pl.BlockSpec((1, tm, V_W), row),
            pl.BlockSpec((1, tm, POOL_W), row),
        ),
        compiler_params=_cparams("parallel", "parallel"),
        name="premix",
    )(*args)


def _fold_state(s):
    g = 2 * N_HEADS
    top = s[0:g]
    bot = s[QK_W // 2:QK_W // 2 + g]
    for h in range(1, N_HEADS):
        top = top + s[h * g:(h + 1) * g]
        bot = bot + s[QK_W // 2 + h * g:QK_W // 2 + (h + 1) * g]
    return jnp.concatenate([top, bot], axis=0)


def _expand_state(c):
    g = 2 * N_HEADS
    return jnp.concatenate([c[0:g]] * N_HEADS + [c[g:2 * g]] * N_HEADS, axis=0)


def _state_kernel(kf_ref, vf_ref, kb_ref, vb_ref, s0f_ref, s0b_ref, kdf_ref, kdb_ref,
                  cdf_ref, cdb_ref, bd_ref, sf_out, sb_out, ff_out, fb_out, sf_acc, sb_acc):
    c = pl.program_id(1)
    nc = pl.num_programs(1)
    bd = bd_ref[...]

    @pl.when(c == 0)
    def _():
        sf_acc[...] = _expand_state(s0f_ref[0]) * bd
        sb_acc[...] = _expand_state(s0b_ref[0]) * bd

    sf = sf_acc[...]
    sb = sb_acc[...]
    sf_out[0, 0] = _fold_state(sf).astype(BF16)
    sb_out[0, 0] = _fold_state(sb).astype(BF16)

    def update(s, k_ref, v_ref, kd_ref, cd_ref):
        kd = k_ref[0].astype(F32) * kd_ref[...]
        f = _dot(kd.T.astype(BF16), v_ref[0])
        return s * cd_ref[...] + f * bd

    sf_new = update(sf, kf_ref, vf_ref, kdf_ref, cdf_ref)
    sb_new = update(sb, kb_ref, vb_ref, kdb_ref, cdb_ref)
    sf_acc[...] = sf_new
    sb_acc[...] = sb_new

    @pl.when(c == nc - 1)
    def _():
        ff_out[0] = _fold_state(sf_new)
        fb_out[0] = _fold_state(sb_new)


def state_scan(qk, v, s0f, s0b, tabs):
    b, n, _ = qk.shape
    nc = n // CHUNK
    const = lambda shape: pl.BlockSpec(shape, lambda bi, c: (0,) * len(shape))
    return pl.pallas_call(
        _state_kernel,
        out_shape=(
            jax.ShapeDtypeStruct((b, nc, DK, V_W), BF16),
            jax.ShapeDtypeStruct((b, nc, DK, V_W), BF16),
            jax.ShapeDtypeStruct((b, DK, V_W), F32),
            jax.ShapeDtypeStruct((b, DK, V_W), F32),
        ),
        grid=(b, nc),
        in_specs=[
            pl.BlockSpec((1, CHUNK, QK_W), lambda bi, c: (bi, c, 1)),
            pl.BlockSpec((1, CHUNK, V_W), lambda bi, c: (bi, c, 0)),
            pl.BlockSpec((1, CHUNK, QK_W), lambda bi, c: (bi, nc - 1 - c, 1)),
            pl.BlockSpec((1, CHUNK, V_W), lambda bi, c: (bi, nc - 1 - c, 0)),
            pl.BlockSpec((1, DK, V_W), lambda bi, c: (bi, 0, 0)),
            pl.BlockSpec((1, DK, V_W), lambda bi, c: (bi, 0, 0)),
            const((CHUNK, QK_W)), const((CHUNK, QK_W)),
            const((1, V_W)), const((1, V_W)),
            const((QK_W, V_W)),
        ],
        out_specs=(
            pl.BlockSpec((1, 1, DK, V_W), lambda bi, c: (bi, c, 0, 0)),
            pl.BlockSpec((1, 1, DK, V_W), lambda bi, c: (bi, nc - 1 - c, 0, 0)),
            pl.BlockSpec((1, DK, V_W), lambda bi, c: (bi, 0, 0)),
            pl.BlockSpec((1, DK, V_W), lambda bi, c: (bi, 0, 0)),
        ),
        scratch_shapes=[pltpu.VMEM((QK_W, V_W), F32), pltpu.VMEM((QK_W, V_W), F32)],
        compiler_params=_cparams("parallel", "arbitrary"),
        name="state_scan",
    )(qk, v, qk, v, s0f, s0b, tabs["kdec_f"], tabs["kdec_b"], tabs["cdec_f"], tabs["cdec_b"],
      tabs["bd_f32"])


def _pool_kernel(*refs, gw, tile, halo, rows_total):
    if halo:
        prev_ref, cur_ref, next_ref, pw_ref, ps_ref, o_ref, buf = refs
    else:
        cur_ref, pw_ref, ps_ref, o_ref, buf = refs
    i = pl.program_id(1)
    last = pl.num_programs(1) - 1
    m = MAX_HALF_WINDOW
    span = tile + 2 * halo
    zeros_m = jnp.zeros((m, POOL_W), F32)
    buf[0:m] = zeros_m
    buf[m + span:2 * m + span] = zeros_m
    if halo:
        buf[m:m + halo] = jnp.where(i > 0, prev_ref[0], 0.0)
        buf[m + halo + tile:m + span] = jnp.where(i < last, next_ref[0], 0.0)
    buf[m + halo:m + halo + tile] = cur_ref[0]

    pos = lax.broadcasted_iota(jnp.int32, (span, 1), 0)
    col = pos % gw
    tpos = lax.broadcasted_iota(jnp.int32, (tile, 1), 0)
    tcol = tpos % gw
    trow = i * (tile // gw) + tpos // gw
    for gi, w in enumerate(POOL_WINDOWS):
        cs = slice(gi * POOL_GROUP_DIM, (gi + 1) * POOL_GROUP_DIM)
        hw = w // 2
        s = None
        for d in range(-hw, hw):
            valid = (col + d >= 0) & (col + d < gw)
            term = jnp.where(valid, buf[m + d:m + d + span, cs], 0.0)
            s = term if s is None else s + term
        if halo:
            acc = None
            for d in range(-hw, hw):
                start = halo + d * gw
                term = s[start:start + tile]
                acc = term if acc is None else acc + term
            cnt_r = jnp.minimum(trow + hw, rows_total) - jnp.maximum(trow - hw, 0)
        else:
            acc = s
            cnt_r = 1
        cnt_c = jnp.minimum(tcol + hw, gw) - jnp.maximum(tcol - hw, 0)
        cnt = (cnt_r * cnt_c).astype(F32)
        xg = buf[m + halo:m + halo + tile, cs]
        diff = (acc / cnt - xg).astype(BF16)
        y = _dot(diff, pw_ref[gi]) * ps_ref[:, cs]
        o_ref[0, :, cs] = y.astype(BF16)


def pool_mixer(p, pool_w_bf, pool_scale, gw, tile):
    b, n, pw = p.shape
    rows_total = n // gw
    halo = MAX_HALF_WINDOW * gw if rows_total > 1 else 0
    in_specs, args = [], []
    if halo:
        r = tile // halo
        nh = n // halo
        in_specs = [
            pl.BlockSpec((1, halo, pw), lambda bi, i: (bi, jnp.maximum(i * r - 1, 0), 0)),
            pl.BlockSpec((1, tile, pw), lambda bi, i: (bi, i, 0)),
            pl.BlockSpec((1, halo, pw), lambda bi, i: (bi, jnp.minimum((i + 1) * r, nh - 1), 0)),
        ]
        args = [p, p, p]
    else:
        in_specs = [pl.BlockSpec((1, tile, pw), lambda bi, i: (bi, i, 0))]
        args = [p]
    ng = len(POOL_WINDOWS)
    in_specs += [
        pl.BlockSpec((ng, POOL_GROUP_DIM, POOL_GROUP_DIM), lambda bi, i: (0, 0, 0)),
        pl.BlockSpec((1, pw), lambda bi, i: (0, 0)),
    ]
    args += [pool_w_bf, pool_scale.reshape(1, pw)]
    return pl.pallas_call(
        functools.partial(_pool_kernel, gw=gw, tile=tile, halo=halo, rows_total=rows_total),
        out_shape=jax.ShapeDtypeStruct((b, n, pw), BF16),
        grid=(b, n // tile),
        in_specs=in_specs,
        out_specs=pl.BlockSpec((1, tile, pw), lambda bi, i: (bi, i, 0)),
        scratch_shapes=[pltpu.VMEM((tile + 2 * halo + 2 * MAX_HALF_WINDOW, pw), F32)],
        compiler_params=_cparams("parallel", "parallel"),
        name="pool_mixer",
    )(*args)


def _ret_kernel(qk_ref, v_ref, gate_ref, py_ref, x_ref, sf_ref, sb_ref, mp_ref, qdf_ref, qdb_ref,
                hm_ref, bd_ref, avg_ref, gn_ref, wo_ref, nw_ref, g1_ref, o_ref, r_buf, *, nch):
    bd = bd_ref[...]
    avg = avg_ref[...]
    lane = lax.broadcasted_iota(jnp.int32, (1, 2 * DV), 1)
    lo_mask = (lane < DV).astype(BF16)
    hi_mask = (lane >= DV).astype(BF16)
    for ci in range(nch):
        rs = slice(ci * CHUNK, (ci + 1) * CHUNK)
        q = qk_ref[0, rs, 0:QK_W]
        k = qk_ref[0, rs, QK_W:2 * QK_W]
        vv = v_ref[0, rs, :]
        inner = []
        for j in range(N_HEADS // 2):
            kp = jnp.concatenate([k * hm_ref[2 * j], k * hm_ref[2 * j + 1]], axis=0)
            sc = _dot_nt(q, kp) * mp_ref[j]
            vpair = vv[:, 2 * DV * j:2 * DV * (j + 1)]
            vp = jnp.concatenate([vpair * lo_mask, vpair * hi_mask], axis=0)
            inner.append(_dot(sc.astype(BF16), vp))
        o = jnp.concatenate(inner, axis=1)
        sf_bd = _expand_state(sf_ref[0, ci]) * bd
        sb_bd = _expand_state(sb_ref[0, ci]) * bd
        o = o + _dot(q, sf_bd) * qdf_ref[...] + _dot(q, sb_bd) * qdb_ref[...]

        def head_mean(a):
            hi = a.astype(BF16)
            lo = (a - hi.astype(F32)).astype(BF16)
            return _dot(hi, avg) + _dot(lo, avg)

        dlt = o - head_mean(o)
        var = head_mean(dlt * dlt)
        yn = dlt * lax.rsqrt(var + EPS) * gn_ref[...]
        g = gate_ref[0, rs, :]
        r_buf[rs, :] = (g * jax.nn.sigmoid(g) * yn).astype(BF16)
    mix = _dot(r_buf[...], wo_ref[0:V_W, :]) + _dot(py_ref[0], wo_ref[V_W:V_W + POOL_W, :])
    y = mix * lax.rsqrt(jnp.mean(mix * mix, axis=-1, keepdims=True) + EPS) * nw_ref[...]
    o_ref[0] = x_ref[0] + g1_ref[0] * y


def retention_mixer(qk, v, gate, pool_y, x, sf, sb, tabs, gn, w_out_bf, norm_w, g1, tm):
    b, n, d = x.shape
    nch = tm // CHUNK
    row = lambda bi, i: (bi, i, 0)
    const = lambda shape: pl.BlockSpec(shape, lambda bi, i: (0,) * len(shape))
    return pl.pallas_call(
        functools.partial(_ret_kernel, nch=nch),
        out_shape=jax.ShapeDtypeStruct((b, n, d), F32),
        grid=(b, n // tm),
        in_specs=[
            pl.BlockSpec((1, tm, 2 * QK_W), row),
            pl.BlockSpec((1, tm, V_W), row),
            pl.BlockSpec((1, tm, V_W), row),
            pl.BlockSpec((1, tm, POOL_W), row),
            pl.BlockSpec((1, tm, d), row),
            pl.BlockSpec((1, nch, DK, V_W), lambda bi, i: (bi, i, 0, 0)),
            pl.BlockSpec((1, nch, DK, V_W), lambda bi, i: (bi, i, 0, 0)),
            const((N_HEADS // 2, CHUNK, 2 * CHUNK)),
            const((CHUNK, V_W)), const((CHUNK, V_W)),
            const((N_HEADS, 1, QK_W)),
            const((QK_W, V_W)),
            const((V_W, V_W)),
            const((1, V_W)),
            const((V_W + POOL_W, d)),
            const((1, d)),
            pl.BlockSpec((1, 1, d), lambda bi, i: (bi, 0, 0)),
        ],
        out_specs=pl.BlockSpec((1, tm, d), row),
        scratch_shapes=[pltpu.VMEM((tm, V_W), BF16)],
        compiler_params=_cparams("parallel", "parallel"),
        name="retention_mixer",
    )(qk, v, gate, pool_y, x, sf, sb, tabs["mpair"], tabs["qdec_f"], tabs["qdec_b"], tabs["hmask"],
      tabs["bd_bf16"], tabs["avg"], gn.reshape(1, V_W), w_out_bf, norm_w.reshape(1, d), g1)


def _router_kernel(x_ref, g_ref, sh_ref, sc_ref, wr_ref, h_ref, aff_ref):
    h = _norm_mod(x_ref[0], g_ref[...], sh_ref[0], sc_ref[0])
    h_ref[0] = h.astype(BF16)
    logits = _dot_nt(wr_ref[...], h, precision=lax.Precision.HIGHEST)
    mx = jnp.max(logits, axis=0, keepdims=True)
    e = jnp.exp(logits - mx)
    aff_ref[0] = e / jnp.sum(e, axis=0, keepdims=True)


def router(x, norm_w, shift, scale, w_router_t, tm):
    b, n, d = x.shape
    ne = w_router_t.shape[0]
    return pl.pallas_call(
        _router_kernel,
        out_shape=(jax.ShapeDtypeStruct((b, n, d), BF16), jax.ShapeDtypeStruct((b, ne, n), F32)),
        grid=(b, n // tm),
        in_specs=[
            pl.BlockSpec((1, tm, d), lambda bi, i: (bi, i, 0)),
            pl.BlockSpec((1, d), lambda bi, i: (0, 0)),
            pl.BlockSpec((1, 1, d), lambda bi, i: (bi, 0, 0)),
            pl.BlockSpec((1, 1, d), lambda bi, i: (bi, 0, 0)),
            pl.BlockSpec((ne, d), lambda bi, i: (0, 0)),
        ],
        out_specs=(
            pl.BlockSpec((1, tm, d), lambda bi, i: (bi, i, 0)),
            pl.BlockSpec((1, ne, tm), lambda bi, i: (bi, 0, i)),
        ),
        compiler_params=_cparams("parallel", "parallel"),
        name="router",
    )(x, norm_w.reshape(1, d), shift, scale, w_router_t)


def _ffn_kernel(x_ref, gate_ref, wg_ref, wu_ref, wd_ref, o_ref, *, rt):
    rows = x_ref.shape[2]
    for r0 in range(0, rows, rt):
        rs = slice(r0, r0 + rt)
        xt = x_ref[0, 0, rs, :]
        a = _dot(xt, wg_ref[0])
        u = _dot(xt, wu_ref[0])
        hh = (a * jax.nn.sigmoid(a) * u).astype(BF16)
        o_ref[0, 0, rs, :] = (_dot(hh, wd_ref[0]) * gate_ref[0, 0, rs, :]).astype(BF16)


def expert_ffn(xs, gate, wg, wu, wd):
    g, ne, rows, d = xs.shape
    f = wg.shape[2]
    rt = min(rows, 256)
    one = pl.Buffered(1)
    return pl.pallas_call(
        functools.partial(_ffn_kernel, rt=rt),
        out_shape=jax.ShapeDtypeStruct((g, ne, rows, d), BF16),
        grid=(ne, g),
        in_specs=[
            pl.BlockSpec((1, 1, rows, d), lambda e, gi: (gi, e, 0, 0)),
            pl.BlockSpec((1, 1, rows, 1), lambda e, gi: (gi, e, 0, 0)),
            pl.BlockSpec((1, d, f), lambda e, gi: (e, 0, 0), pipeline_mode=one),
            pl.BlockSpec((1, d, f), lambda e, gi: (e, 0, 0), pipeline_mode=one),
            pl.BlockSpec((1, f, d), lambda e, gi: (e, 0, 0), pipeline_mode=one),
        ],
        out_specs=pl.BlockSpec((1, 1, rows, d), lambda e, gi: (gi, e, 0, 0)),
        compiler_params=_cparams("arbitrary", "arbitrary"),
        name="expert_ffn",
    )(xs, gate, wg, wu, wd)


ROUTE_WIN = 64
LANES = 128
VAL_ROWS = 8


def _combine_kernel(st_ref, rank_ref, y_hbm, x_ref, nw_ref, g_ref, o_ref, buf, xbuf, sem, xsem,
                    acc_ref, *, tt, ne, cap, merged):
    b = pl.program_id(0)
    k = pl.program_id(1)
    nk = pl.num_programs(1)
    rows_total = y_hbm.shape[2]
    win = ROUTE_WIN
    g = 0 if merged else b
    base = b * cap if merged else 0
    lane = lax.broadcasted_iota(jnp.int32, (1, LANES), 1)

    def bounds(e):
        lo = base + st_ref[(b * (nk + 1) + k) * ne + e]
        hi = base + st_ref[(b * (nk + 1) + k + 1) * ne + e]
        return lo, hi, jnp.minimum((lo // 16) * 16, rows_total - win)

    def win_copy(e, a0):
        src = y_hbm.at[g, e, pl.ds(pl.multiple_of(a0, 16), win), :]
        return pltpu.make_async_copy(src, buf.at[pl.ds(e * win, win), :], sem.at[e])

    for e in range(ne):
        win_copy(e, bounds(e)[2]).start()

    rkf = rank_ref[0].astype(F32)
    rk_t = jnp.concatenate([rkf, jnp.full((LANES - ne, tt), -1.0, F32)], axis=0).T
    rk_t = rk_t.astype(jnp.int32)

    def target(e, origin):
        col = rk_t[:, e:e + 1]
        return jnp.where(col >= 0, col + (base - origin), -1)

    pieces = []
    for j in range(ne // 2):
        t_even = target(2 * j, bounds(2 * j)[2])
        t_odd = target(2 * j + 1, bounds(2 * j + 1)[2])
        t_odd = jnp.where((t_odd >= 0) & (t_odd < win), t_odd + win, -1)
        pieces.append((jnp.where(lane < win, t_even, t_odd) == lane).astype(BF16))
    p = jnp.concatenate(pieces, axis=1)
    for e in range(ne):
        win_copy(e, bounds(e)[2]).wait()
    acc_ref[...] = _dot(p, buf[...])

    for e in range(ne):
        lo, hi, a0 = bounds(e)

        def extra(w, carry, e=e, a0=a0):
            start = a0 + w * win
            aw = pl.multiple_of(jnp.minimum(start, rows_total - win), 16)
            cp = pltpu.make_async_copy(y_hbm.at[g, e, pl.ds(aw, win), :], xbuf, xsem)
            cp.start()
            cp.wait()
            col = rk_t[:, e:e + 1]
            ok = (col >= 0) & (col + base >= start)
            px = (jnp.where(ok, col + (base - aw), -1) == lane[:, 0:win]).astype(BF16)
            acc_ref[...] += _dot(px, xbuf[...])
            return carry

        lax.fori_loop(1, (hi - a0 + win - 1) // win, extra, 0)

    y = acc_ref[...]
    yn = y * lax.rsqrt(jnp.mean(y * y, axis=-1, keepdims=True) + EPS) * nw_ref[...]
    o_ref[0] = x_ref[0] + g_ref[0] * yn


def combine_post(starts, rank_t, y, x, norm_w, g, tt, cap, merged):
    b, n, d = x.shape
    _, ne, rows, _ = y.shape
    grid_spec = pltpu.PrefetchScalarGridSpec(
        num_scalar_prefetch=1,
        grid=(b, n // tt),
        in_specs=[
            pl.BlockSpec((1, ne, tt), lambda bi, k, st: (bi, 0, k)),
            pl.BlockSpec(memory_space=pl.ANY),
            pl.BlockSpec((1, tt, d), lambda bi, k, st: (bi, k, 0)),
            pl.BlockSpec((1, d), lambda bi, k, st: (0, 0)),
            pl.BlockSpec((1, 1, d), lambda bi, k, st: (bi, 0, 0)),
        ],
        out_specs=pl.BlockSpec((1, tt, d), lambda bi, k, st: (bi, k, 0)),
        scratch_shapes=[
            pltpu.VMEM((ne * ROUTE_WIN, d), BF16),
            pltpu.VMEM((ROUTE_WIN, d), BF16),
            pltpu.SemaphoreType.DMA((ne,)),
            pltpu.SemaphoreType.DMA(()),
            pltpu.VMEM((tt, d), F32),
        ],
    )
    return pl.pallas_call(
        functools.partial(_combine_kernel, tt=tt, ne=ne, cap=cap, merged=merged),
        out_shape=jax.ShapeDtypeStruct((b, n, d), F32),
        grid_spec=grid_spec,
        compiler_params=_cparams("arbitrary", "arbitrary"),
        name="combine_post",
    )(starts, rank_t, y, x, norm_w.reshape(1, d), g)


def _select_kernel(aff_ref, tri_ref, rank_ref, st_ref, *, cap, tt):
    a = aff_ref[0]
    ne, n = a.shape
    bits = lax.bitcast_convert_type(a, jnp.int32)

    def search(_, c):
        lo, hi = c
        mid = lo + ((hi - lo) >> 1)
        cnt = jnp.sum((bits >= mid).astype(F32), axis=1, keepdims=True)
        ge = cnt >= cap
        return jnp.where(ge, mid, lo), jnp.where(ge, hi, mid)

    lo0 = jnp.zeros((ne, 1), jnp.int32)
    hi0 = jnp.full((ne, 1), 0x7F800000, jnp.int32)
    thr, _ = lax.fori_loop(0, 31, search, (lo0, hi0))
    gt = bits > thr
    eq = bits == thr
    need = cap - jnp.sum(gt.astype(F32), axis=1, keepdims=True)
    m = jnp.concatenate([gt, eq], axis=0).astype(BF16)
    tri = tri_ref[...]
    lane = lax.broadcasted_iota(jnp.int32, (1, LANES), 1)
    off = jnp.zeros((2 * ne, 1), F32)
    st = jnp.zeros((ne, LANES), jnp.int32)
    for j in range(n // LANES):
        cs = slice(j * LANES, (j + 1) * LANES)
        if (j * LANES) % tt == 0:
            off_sel = off[:ne] + jnp.minimum(off[ne:], need)
            st = jnp.where(lane == (j * LANES) // tt, off_sel.astype(jnp.int32), st)
        mj = m[:, cs]
        pj = _dot(mj, tri) + off
        pe = pj[ne:]
        sel = gt[:, cs] | (eq[:, cs] & (pe < need))
        rank_ref[0, :, cs] = jnp.where(sel, pj[:ne] + jnp.minimum(pe, need), -1.0).astype(jnp.int32)
        off = off + jnp.sum(mj.astype(F32), axis=1, keepdims=True)
    st_ref[0] = jnp.where(lane == n // tt, cap, st)


def route_select(aff_t, cap, tt):
    b, ne, n = aff_t.shape
    tri = jnp.asarray(np.triu(np.ones((LANES, LANES), np.float32), 1), BF16)
    return pl.pallas_call(
        functools.partial(_select_kernel, cap=cap, tt=tt),
        out_shape=(jax.ShapeDtypeStruct((b, ne, n), jnp.int32),
                   jax.ShapeDtypeStruct((b, ne, LANES), jnp.int32)),
        grid=(b,),
        in_specs=[pl.BlockSpec((1, ne, n), lambda bi: (bi, 0, 0)),
                  pl.BlockSpec((LANES, LANES), lambda bi: (0, 0))],
        out_specs=(pl.BlockSpec((1, ne, n), lambda bi: (bi, 0, 0)),
                   pl.BlockSpec((1, ne, LANES), lambda bi: (bi, 0, 0))),
        compiler_params=_cparams("parallel"),
        name="route_select",
    )(aff_t, tri)


def _compact_kernel(st_ref, rank_ref, aff_ref, out_ref, *, tt, ne, cap):
    b = pl.program_id(0)
    k = pl.program_id(1)
    nk = pl.num_programs(1)
    win = ROUTE_WIN

    @pl.when(k == 0)
    def _():
        out_ref[...] = jnp.zeros_like(out_ref)

    rk = rank_ref[0]
    a = aff_ref[0]
    g1 = a.astype(BF16).astype(F32)
    r1 = a - g1
    g2 = r1.astype(BF16).astype(F32)
    g3 = r1 - g2
    tok = k * tt + lax.broadcasted_iota(jnp.int32, (1, tt), 1)
    ids = jnp.concatenate([(tok >> 6).astype(F32), (tok & 63).astype(F32),
                           jnp.zeros((VAL_ROWS - 2, tt), F32)], axis=0)
    pad = jnp.zeros((LANES - VAL_ROWS - 3 * ne, tt), F32)
    payload = jnp.concatenate([ids, g1, g2, g3, pad], axis=0).astype(BF16)
    sub = lax.broadcasted_iota(jnp.int32, (win, 1), 0)
    for e in range(ne):
        lo = st_ref[(b * (nk + 1) + k) * ne + e]
        hi = st_ref[(b * (nk + 1) + k + 1) * ne + e]
        a0 = (lo // 8) * 8
        row = rk[e:e + 1, :]

        def window(w, carry, e=e, a0=a0, row=row):
            aw = pl.multiple_of(a0 + w * win, 8)
            p = ((row - aw) == sub).astype(BF16)
            out_ref[0, e, pl.ds(aw, win), :] += _dot_nt(p, payload)
            return carry

        lax.fori_loop(0, (hi - a0 + win - 1) // win, window, 0)


def route_compact(starts, rank_t, aff_t, cap, tt):
    b, ne, n = aff_t.shape
    grid_spec = pltpu.PrefetchScalarGridSpec(
        num_scalar_prefetch=1,
        grid=(b, n // tt),
        in_specs=[pl.BlockSpec((1, ne, tt), lambda bi, k, st: (bi, 0, k)),
                  pl.BlockSpec((1, ne, tt), lambda bi, k, st: (bi, 0, k))],
        out_specs=pl.BlockSpec((1, ne, cap + ROUTE_WIN, LANES), lambda bi, k, st: (bi, 0, 0, 0)),
    )
    return pl.pallas_call(
        functools.partial(_compact_kernel, tt=tt, ne=ne, cap=cap),
        out_shape=jax.ShapeDtypeStruct((b, ne, cap + ROUTE_WIN, LANES), F32),
        grid_spec=grid_spec,
        compiler_params=_cparams("parallel", "arbitrary"),
        name="route_compact",
    )(starts, rank_t, aff_t)


def _qk_head_of_lane():
    half = QK_W // 2
    return (np.arange(QK_W) % half) // (DK // 2)


def _in_proj_perm():
    half = DK // 2
    first = [h * DK + i for h in range(N_HEADS) for i in range(half)]
    second = [h * DK + half + i for h in range(N_HEADS) for i in range(half)]
    qperm = np.array(first + second)
    rest = np.arange(2 * QK_W, 2 * QK_W + 2 * V_W + POOL_W)
    return np.concatenate([qperm, QK_W + qperm, rest])


def _static_tables():
    qk_head = _qk_head_of_lane()
    v_head = np.arange(V_W) // DV
    hmask = (qk_head[None, :] == np.arange(N_HEADS)[:, None])
    bd = (qk_head[:, None] == v_head[None, :])
    avg = (v_head[:, None] == v_head[None, :]).astype(np.float32) / DV
    return {
        "hmask": jnp.asarray(hmask[:, None, :], BF16),
        "bd_f32": jnp.asarray(bd, F32),
        "bd_bf16": jnp.asarray(bd, BF16),
        "avg": jnp.asarray(avg, BF16),
    }


def _decay_tables(lg_f, lg_b):
    pos = jnp.arange(CHUNK, dtype=F32)
    diff = pos[:, None] - pos[None, :]
    low = diff >= 0
    up = diff < 0
    m_f = jnp.where(low, jnp.exp(lg_f[:, None, None] * jnp.where(low, diff, 0.0)), 0.0)
    m_b = jnp.where(up, jnp.exp(lg_b[:, None, None] * jnp.where(up, -diff, 0.0)), 0.0)
    m = m_f + m_b
    mpair = m.reshape(N_HEADS // 2, 2, CHUNK, CHUNK).transpose(0, 2, 1, 3).reshape(
        N_HEADS // 2, CHUNK, 2 * CHUNK)
    qk_head = _qk_head_of_lane()
    v_head = np.arange(V_W) // DV
    qdec_f = jnp.exp(lg_f[None, :] * (pos[:, None] + 1.0))[:, v_head]
    qdec_b = jnp.exp(lg_b[None, :] * (CHUNK - pos[:, None]))[:, v_head]
    kdec_f = jnp.exp(lg_f[None, :] * (CHUNK - 1.0 - pos[:, None]))[:, qk_head]
    kdec_b = jnp.exp(lg_b[None, :] * pos[:, None])[:, qk_head]
    cdec_f = jnp.exp(lg_f * CHUNK)[None, v_head]
    cdec_b = jnp.exp(lg_b * CHUNK)[None, v_head]
    return {"mpair": mpair, "qdec_f": qdec_f, "qdec_b": qdec_b, "kdec_f": kdec_f,
            "kdec_b": kdec_b, "cdec_f": cdec_f, "cdec_b": cdec_b}


def _rope_tables(n):
    t = jnp.arange(n)
    row = (t // GRID_W).astype(F32)
    col = (t % GRID_W).astype(F32)
    n_freq = DK // 4
    inv = ROPE_BASE ** (-jnp.arange(n_freq, dtype=F32) / n_freq)
    ang = jnp.concatenate([row[:, None] * inv, col[:, None] * inv], axis=-1)
    return jnp.tile(jnp.cos(ang), (1, N_HEADS)), jnp.tile(jnp.sin(ang), (1, N_HEADS))


def _moe_residual(x, h_bf, aff_t, wg, wu, wd, norm_w, g, merged):
    b, n, d = h_bf.shape
    cap = EC_FACTOR * n // N_EXPERTS
    ne = N_EXPERTS
    tt = min(n, 256)
    nk = n // tt
    rank_t, st = route_select(aff_t, cap, tt)
    starts = st[:, :, :nk + 1].transpose(0, 2, 1).reshape(-1)
    rows = route_compact(starts, rank_t, aff_t, cap, tt)[:, :, :cap]
    idx = (rows[..., 0] * 64.0 + rows[..., 1]).astype(jnp.int32)
    pieces = rows[..., VAL_ROWS:VAL_ROWS + 3 * ne].reshape(b, ne, cap, 3, ne)
    own = pieces[:, jnp.arange(ne), :, :, jnp.arange(ne)]
    gate = ((own[..., 0] + own[..., 1]) + own[..., 2]).transpose(1, 0, 2)
    xs = jax.vmap(lambda hb, ib: hb[ib])(h_bf, idx)
    if merged:
        xs = xs.transpose(1, 0, 2, 3).reshape(1, ne, b * cap, d)
        gate = gate.transpose(1, 0, 2).reshape(1, ne, b * cap)
    y = expert_ffn(xs, gate[..., None], wg, wu, wd)
    return combine_post(starts, rank_t, y, x, norm_w, g, tt, cap, merged)


def kernel(x, c, ctx, c_ctx, w_ada, b_ada, norm_pre_mix, norm_post_mix, norm_pre_ffn, norm_post_ffn, w_in, ret_decay_fwd, ret_decay_bwd, ret_gn, pool_w, pool_scale, w_out, w_router, w_gate, w_up, w_down):
    b, n, d = x.shape
    lc = ctx.shape[1]
    depth = w_ada.shape[0]
    rope = _rope_tables(n)
    static = _static_tables()
    perm = _in_proj_perm()

    cc = jnp.concatenate([c, c_ctx[None, :], jnp.zeros((7, d), F32)], axis=0)
    mods = ada_modulation(cc, w_ada, b_ada)

    tm = 512
    for l in range(depth):
        last = l == depth - 1
        mx = mods[l, :b].reshape(b, 1, 6, d)
        sh1, sc1, g1, sh2, sc2, g2 = [mx[:, :, i] for i in range(6)]
        mc = jnp.broadcast_to(mods[l, b].reshape(1, 1, 6, d), (b, 1, 6, d))
        csh1, csc1, cg1, csh2, csc2, cg2 = [mc[:, :, i] for i in range(6)]
        lg_f = jax.nn.log_sigmoid(ret_decay_fwd[l].astype(F32))
        lg_b = jax.nn.log_sigmoid(ret_decay_bwd[l].astype(F32))
        tabs = dict(static, **_decay_tables(lg_f, lg_b))
        w_in_p = w_in[l][:, perm].astype(BF16)
        w_out_bf = w_out[l].astype(BF16)
        pool_w_bf = pool_w[l].astype(BF16)

        qk_c, v_c, gate_c, p_c = premix(ctx, norm_pre_mix[l], csh1, csc1, w_in_p, None, lc)
        zero = jnp.zeros((b, DK, V_W), F32)
        sf_c, sb_c, s_f, s_b = state_scan(qk_c, v_c, zero, zero, tabs)
        qk_x, v_x, gate_x, p_x = premix(x, norm_pre_mix[l], sh1, sc1, w_in_p, rope, tm)
        sf_x, sb_x, _, _ = state_scan(qk_x, v_x, s_f, s_b, tabs)
        pool_x = pool_mixer(p_x, pool_w_bf, pool_scale[l], GRID_W, min(n, 2048))
        x = retention_mixer(qk_x, v_x, gate_x, pool_x, x, sf_x, sb_x, tabs, ret_gn[l], w_out_bf,
                            norm_post_mix[l], g1, tm)
        if not last:
            pool_c = pool_mixer(p_c, pool_w_bf, pool_scale[l], lc, lc)
            ctx = retention_mixer(qk_c, v_c, gate_c, pool_c, ctx, sf_c, sb_c, tabs, ret_gn[l],
                                  w_out_bf, norm_post_mix[l], cg1, lc)

        wg = w_gate[l].astype(BF16)
        wu = w_up[l].astype(BF16)
        wd = w_down[l].astype(BF16)
        wr_t = w_router[l].T
        h_x, aff_x = router(x, norm_pre_ffn[l], sh2, sc2, wr_t, tm)
        x = _moe_residual(x, h_x, aff_x, wg, wu, wd, norm_post_ffn[l], g2, False)
        if not last:
            h_c, aff_c = router(ctx, norm_pre_ffn[l], csh2, csc2, wr_t, lc)
            ctx = _moe_residual(ctx, h_c, aff_c, wg, wu, wd, norm_post_ffn[l], cg2, True)
    return x
```

```python
import functools

import jax
import jax.numpy as jnp
import numpy as np
from jax import lax
from jax.experimental import pallas as pl
from jax.experimental.pallas import tpu as pltpu

F32 = jnp.float32
BF16 = jnp.bfloat16

D_MODEL = 1024
GRID_W = 64
N_HEADS = 8
DV = 64
DK = 32
QK_W = N_HEADS * DK
V_W = N_HEADS * DV
POOL_W = 512
POOL_WINDOWS = (2, 4, 8, 16)
POOL_GROUP_DIM = 128
CHUNK = 128
ROPE_BASE = 10000.0
N_EXPERTS = 16
EC_FACTOR = 2
EPS = 1e-6
MAX_HALF_WINDOW = max(POOL_WINDOWS) // 2

VMEM_LIMIT = 56 * 1024 * 1024


def _cparams(*sem):
    return pltpu.CompilerParams(dimension_semantics=sem, vmem_limit_bytes=VMEM_LIMIT)


def _dot(a, b):
    return jnp.dot(a, b, preferred_element_type=F32)


def _dot_nt(a, b, precision=None):
    return lax.dot_general(a, b, (((1,), (1,)), ((), ())), precision=precision,
                           preferred_element_type=F32)


def _ada_kernel(cc_ref, w_ref, b_ref, o_ref):
    s = cc_ref[...]
    s = s * jax.nn.sigmoid(s)
    o_ref[0] = _dot(s.astype(BF16), w_ref[0].astype(BF16)) + b_ref[0]


def ada_modulation(cc, w_ada, b_ada):
    depth, d, d6 = w_ada.shape
    rows = cc.shape[0]
    tn = 1536
    return pl.pallas_call(
        _ada_kernel,
        out_shape=jax.ShapeDtypeStruct((depth, rows, d6), F32),
        grid=(depth, d6 // tn),
        in_specs=[
            pl.BlockSpec((rows, d), lambda l, j: (0, 0)),
            pl.BlockSpec((1, d, tn), lambda l, j: (l, 0, j)),
            pl.BlockSpec((1, 1, tn), lambda l, j: (l, 0, j)),
        ],
        out_specs=pl.BlockSpec((1, rows, tn), lambda l, j: (l, 0, j)),
        compiler_params=_cparams("parallel", "parallel"),
        name="ada_modulation",
    )(cc, w_ada, b_ada.reshape(depth, 1, d6))


def _norm_mod(xf, g, sh, sc):
    y = xf * lax.rsqrt(jnp.mean(xf * xf, axis=-1, keepdims=True) + EPS)
    return (y * g) * (1.0 + sc) + sh


def _premix_kernel(x_ref, g_ref, sh_ref, sc_ref, w_ref, *rest, rope):
    if rope:
        cos_ref, sin_ref, qk_ref, v_ref, gate_ref, p_ref = rest
    else:
        qk_ref, v_ref, gate_ref, p_ref = rest
    hb = _norm_mod(x_ref[0], g_ref[...], sh_ref[0], sc_ref[0]).astype(BF16)
    zqk = _dot(hb, w_ref[:, 0:2 * QK_W])
    half = QK_W // 2
    q1, q2 = zqk[:, 0:half], zqk[:, half:2 * half]
    k1 = zqk[:, 2 * half:3 * half] * (DK ** -0.5)
    k2 = zqk[:, 3 * half:4 * half] * (DK ** -0.5)
    if rope:
        cos, sin = cos_ref[...], sin_ref[...]
        q1, q2 = q1 * cos - q2 * sin, q1 * sin + q2 * cos
        k1, k2 = k1 * cos - k2 * sin, k1 * sin + k2 * cos
    qk_ref[0] = jnp.concatenate([q1, q2, k1, k2], axis=1).astype(BF16)
    o = 2 * QK_W
    v_ref[0] = _dot(hb, w_ref[:, o:o + V_W]).astype(BF16)
    gate_ref[0] = _dot(hb, w_ref[:, o + V_W:o + 2 * V_W])
    p_ref[0] = _dot(hb, w_ref[:, o + 2 * V_W:o + 2 * V_W + POOL_W])


def premix(x, norm_w, shift, scale, w_in_p, rope, tm):
    b, n, d = x.shape
    in_w = w_in_p.shape[1]
    row = lambda bi, i: (bi, i, 0)
    in_specs = [
        pl.BlockSpec((1, tm, d), row),
        pl.BlockSpec((1, d), lambda bi, i: (0, 0)),
        pl.BlockSpec((1, 1, d), lambda bi, i: (bi, 0, 0)),
        pl.BlockSpec((1, 1, d), lambda bi, i: (bi, 0, 0)),
        pl.BlockSpec((d, in_w), lambda bi, i: (0, 0)),
    ]
    args = [x, norm_w.reshape(1, d), shift, scale, w_in_p]
    if rope is not None:
        half = QK_W // 2
        in_specs += [pl.BlockSpec((tm, half), lambda bi, i: (i, 0))] * 2
        args += list(rope)
    return pl.pallas_call(
        functools.partial(_premix_kernel, rope=rope is not None),
        out_shape=(
            jax.ShapeDtypeStruct((b, n, 2 * QK_W), BF16),
            jax.ShapeDtypeStruct((b, n, V_W), BF16),
            jax.ShapeDtypeStruct((b, n, V_W), F32),
            jax.ShapeDtypeStruct((b, n, POOL_W), F32),
        ),
        grid=(b, n // tm),
        in_specs=in_specs,
        out_specs=(
            pl.BlockSpec((1, tm, 2 * QK_W), row),
            pl.BlockSpec((1, tm, V_W), row),
            pl.BlockSpec((1, tm, V_W), row),
            pl.BlockSpec((1, tm, POOL_W), row),
        ),
        compiler_params=_cparams("parallel", "parallel"),
        name="premix",
    )(*args)


def _fold_state(s):
    g = 2 * N_HEADS
    top = s[0:g]
    bot = s[QK_W // 2:QK_W // 2 + g]
    for h in range(1, N_HEADS):
        top = top + s[h * g:(h + 1) * g]
        bot = bot + s[QK_W // 2 + h * g:QK_W // 2 + (h + 1) * g]
    return jnp.concatenate([top, bot], axis=0)


def _expand_state(c):
    g = 2 * N_HEADS
    return jnp.concatenate([c[0:g]] * N_HEADS + [c[g:2 * g]] * N_HEADS, axis=0)


def _state_kernel(kf_ref, vf_ref, kb_ref, vb_ref, s0f_ref, s0b_ref, kdf_ref, kdb_ref,
                  cdf_ref, cdb_ref, bd_ref, sf_out, sb_out, ff_out, fb_out, sf_acc, sb_acc, *, cps):
    c = pl.program_id(1)
    nc = pl.num_programs(1)
    bd = bd_ref[...]

    @pl.when(c == 0)
    def _():
        sf_acc[...] = _expand_state(s0f_ref[0]) * bd
        sb_acc[...] = _expand_state(s0b_ref[0]) * bd

    def update(s, k, v, kd_ref, cd_ref):
        kd = k.astype(F32) * kd_ref[...]
        f = _dot(kd.T.astype(BF16), v)
        return s * cd_ref[...] + f * bd

    sf = sf_acc[...]
    sb = sb_acc[...]
    for i in range(cps):
        j = cps - 1 - i
        fs = slice(i * CHUNK, (i + 1) * CHUNK)
        bs = slice(j * CHUNK, (j + 1) * CHUNK)
        sf_out[0, i] = _fold_state(sf).astype(BF16)
        sb_out[0, j] = _fold_state(sb).astype(BF16)
        sf = update(sf, kf_ref[0, fs, :], vf_ref[0, fs, :], kdf_ref, cdf_ref)
        sb = update(sb, kb_ref[0, bs, :], vb_ref[0, bs, :], kdb_ref, cdb_ref)
    sf_acc[...] = sf
    sb_acc[...] = sb

    @pl.when(c == nc - 1)
    def _():
        ff_out[0] = _fold_state(sf)
        fb_out[0] = _fold_state(sb)


def state_scan(qk, v, s0f, s0b, tabs):
    b, n, _ = qk.shape
    nc = n // CHUNK
    cps = min(nc, 4)
    ns = nc // cps
    rows = cps * CHUNK
    const = lambda shape: pl.BlockSpec(shape, lambda bi, c: (0,) * len(shape))
    return pl.pallas_call(
        functools.partial(_state_kernel, cps=cps),
        out_shape=(
            jax.ShapeDtypeStruct((b, nc, DK, V_W), BF16),
            jax.ShapeDtypeStruct((b, nc, DK, V_W), BF16),
            jax.ShapeDtypeStruct((b, DK, V_W), F32),
            jax.ShapeDtypeStruct((b, DK, V_W), F32),
        ),
        grid=(b, ns),
        in_specs=[
            pl.BlockSpec((1, rows, QK_W), lambda bi, c: (bi, c, 1)),
            pl.BlockSpec((1, rows, V_W), lambda bi, c: (bi, c, 0)),
            pl.BlockSpec((1, rows, QK_W), lambda bi, c: (bi, ns - 1 - c, 1)),
            pl.BlockSpec((1, rows, V_W), lambda bi, c: (bi, ns - 1 - c, 0)),
            pl.BlockSpec((1, DK, V_W), lambda bi, c: (bi, 0, 0)),
            pl.BlockSpec((1, DK, V_W), lambda bi, c: (bi, 0, 0)),
            const((CHUNK, QK_W)), const((CHUNK, QK_W)),
            const((1, V_W)), const((1, V_W)),
            const((QK_W, V_W)),
        ],
        out_specs=(
            pl.BlockSpec((1, cps, DK, V_W), lambda bi, c: (bi, c, 0, 0)),
            pl.BlockSpec((1, cps, DK, V_W), lambda bi, c: (bi, ns - 1 - c, 0, 0)),
            pl.BlockSpec((1, DK, V_W), lambda bi, c: (bi, 0, 0)),
            pl.BlockSpec((1, DK, V_W), lambda bi, c: (bi, 0, 0)),
        ),
        scratch_shapes=[pltpu.VMEM((QK_W, V_W), F32), pltpu.VMEM((QK_W, V_W), F32)],
        compiler_params=_cparams("parallel", "arbitrary"),
        name="state_scan",
    )(qk, v, qk, v, s0f, s0b, tabs["kdec_f"], tabs["kdec_b"], tabs["cdec_f"], tabs["cdec_b"],
      tabs["bd_f32"])


def _pool_kernel(*refs, gw, tile, halo, rows_total):
    if halo:
        prev_ref, cur_ref, next_ref, pw_ref, ps_ref, o_ref, buf = refs
    else:
        cur_ref, pw_ref, ps_ref, o_ref, buf = refs
    i = pl.program_id(1)
    last = pl.num_programs(1) - 1
    m = MAX_HALF_WINDOW
    span = tile + 2 * halo
    zeros_m = jnp.zeros((m, POOL_W), F32)
    buf[0:m] = zeros_m
    buf[m + span:2 * m + span] = zeros_m
    if halo:
        buf[m:m + halo] = jnp.where(i > 0, prev_ref[0], 0.0)
        buf[m + halo + tile:m + span] = jnp.where(i < last, next_ref[0], 0.0)
    buf[m + halo:m + halo + tile] = cur_ref[0]

    pos = lax.broadcasted_iota(jnp.int32, (span, 1), 0)
    col = pos % gw
    tpos = lax.broadcasted_iota(jnp.int32, (tile, 1), 0)
    tcol = tpos % gw
    trow = i * (tile // gw) + tpos // gw
    for gi, w in enumerate(POOL_WINDOWS):
        cs = slice(gi * POOL_GROUP_DIM, (gi + 1) * POOL_GROUP_DIM)
        hw = w // 2
        s = None
        for d in range(-hw, hw):
            valid = (col + d >= 0) & (col + d < gw)
            term = jnp.where(valid, buf[m + d:m + d + span, cs], 0.0)
            s = term if s is None else s + term
        if halo:
            acc = None
            for d in range(-hw, hw):
                start = halo + d * gw
                term = s[start:start + tile]
                acc = term if acc is None else acc + term
            cnt_r = jnp.minimum(trow + hw, rows_total) - jnp.maximum(trow - hw, 0)
        else:
            acc = s
            cnt_r = 1
        cnt_c = jnp.minimum(tcol + hw, gw) - jnp.maximum(tcol - hw, 0)
        cnt = (cnt_r * cnt_c).astype(F32)
        xg = buf[m + halo:m + halo + tile, cs]
        diff = (acc / cnt - xg).astype(BF16)
        y = _dot(diff, pw_ref[gi]) * ps_ref[:, cs]
        o_ref[0, :, cs] = y.astype(BF16)


def pool_mixer(p, pool_w_bf, pool_scale, gw, tile):
    b, n, pw = p.shape
    rows_total = n // gw
    halo = MAX_HALF_WINDOW * gw if rows_total > 1 else 0
    in_specs, args = [], []
    if halo:
        r = tile // halo
        nh = n // halo
        in_specs = [
            pl.BlockSpec((1, halo, pw), lambda bi, i: (bi, jnp.maximum(i * r - 1, 0), 0)),
            pl.BlockSpec((1, tile, pw), lambda bi, i: (bi, i, 0)),
            pl.BlockSpec((1, halo, pw), lambda bi, i: (bi, jnp.minimum((i + 1) * r, nh - 1), 0)),
        ]
        args = [p, p, p]
    else:
        in_specs = [pl.BlockSpec((1, tile, pw), lambda bi, i: (bi, i, 0))]
        args = [p]
    ng = len(POOL_WINDOWS)
    in_specs += [
        pl.BlockSpec((ng, POOL_GROUP_DIM, POOL_GROUP_DIM), lambda bi, i: (0, 0, 0)),
        pl.BlockSpec((1, pw), lambda bi, i: (0, 0)),
    ]
    args += [pool_w_bf, pool_scale.reshape(1, pw)]
    return pl.pallas_call(
        functools.partial(_pool_kernel, gw=gw, tile=tile, halo=halo, rows_total=rows_total),
        out_shape=jax.ShapeDtypeStruct((b, n, pw), BF16),
        grid=(b, n // tile),
        in_specs=in_specs,
        out_specs=pl.BlockSpec((1, tile, pw), lambda bi, i: (bi, i, 0)),
        scratch_shapes=[pltpu.VMEM((tile + 2 * halo + 2 * MAX_HALF_WINDOW, pw), F32)],
        compiler_params=_cparams("parallel", "parallel"),
        name="pool_mixer",
    )(*args)


def _ret_kernel(qk_ref, v_ref, gate_ref, py_ref, x_ref, sf_ref, sb_ref, mp_ref, qdf_ref, qdb_ref,
                hm_ref, bd_ref, avg_ref, gn_ref, wo_ref, nw_ref, g1_ref, o_ref, o_buf, *, nch):
    bd = bd_ref[...]
    avg = avg_ref[...]
    lane = lax.broadcasted_iota(jnp.int32, (1, 2 * DV), 1)
    lo_mask = (lane < DV).astype(BF16)
    hi_mask = (lane >= DV).astype(BF16)
    for ci in range(nch):
        rs = slice(ci * CHUNK, (ci + 1) * CHUNK)
        q = qk_ref[0, rs, 0:QK_W]
        k = qk_ref[0, rs, QK_W:2 * QK_W]
        vv = v_ref[0, rs, :]
        inner = []
        for j in range(N_HEADS // 2):
            kp = jnp.concatenate([k * hm_ref[2 * j], k * hm_ref[2 * j + 1]], axis=0)
            sc = _dot_nt(q, kp) * mp_ref[j]
            vpair = vv[:, 2 * DV * j:2 * DV * (j + 1)]
            vp = jnp.concatenate([vpair * lo_mask, vpair * hi_mask], axis=0)
            inner.append(_dot(sc.astype(BF16), vp))
        o = jnp.concatenate(inner, axis=1)
        sf_bd = _expand_state(sf_ref[0, ci]) * bd
        sb_bd = _expand_state(sb_ref[0, ci]) * bd
        o_buf[rs, :] = o + _dot(q, sf_bd) * qdf_ref[...] + _dot(q, sb_bd) * qdb_ref[...]

    def head_mean(a):
        hi = a.astype(BF16)
        lo = (a - hi.astype(F32)).astype(BF16)
        return _dot(hi, avg) + _dot(lo, avg)

    o = o_buf[...]
    dlt = o - head_mean(o)
    var = head_mean(dlt * dlt)
    yn = dlt * lax.rsqrt(var + EPS) * gn_ref[...]
    g = gate_ref[0]
    r = (g * jax.nn.sigmoid(g) * yn).astype(BF16)
    mix = _dot(r, wo_ref[0:V_W, :]) + _dot(py_ref[0], wo_ref[V_W:V_W + POOL_W, :])
    y = mix * lax.rsqrt(jnp.mean(mix * mix, axis=-1, keepdims=True) + EPS) * nw_ref[...]
    o_ref[0] = x_ref[0] + g1_ref[0] * y


def retention_mixer(qk, v, gate, pool_y, x, sf, sb, tabs, gn, w_out_bf, norm_w, g1, tm):
    b, n, d = x.shape
    nch = tm // CHUNK
    row = lambda bi, i: (bi, i, 0)
    const = lambda shape: pl.BlockSpec(shape, lambda bi, i: (0,) * len(shape))
    return pl.pallas_call(
        functools.partial(_ret_kernel, nch=nch),
        out_shape=jax.ShapeDtypeStruct((b, n, d), F32),
        grid=(b, n // tm),
        in_specs=[
            pl.BlockSpec((1, tm, 2 * QK_W), row),
            pl.BlockSpec((1, tm, V_W), row),
            pl.BlockSpec((1, tm, V_W), row),
            pl.BlockSpec((1, tm, POOL_W), row),
            pl.BlockSpec((1, tm, d), row),
            pl.BlockSpec((1, nch, DK, V_W), lambda bi, i: (bi, i, 0, 0)),
            pl.BlockSpec((1, nch, DK, V_W), lambda bi, i: (bi, i, 0, 0)),
            const((N_HEADS // 2, CHUNK, 2 * CHUNK)),
            const((CHUNK, V_W)), const((CHUNK, V_W)),
            const((N_HEADS, 1, QK_W)),
            const((QK_W, V_W)),
            const((V_W, V_W)),
            const((1, V_W)),
            const((V_W + POOL_W, d)),
            const((1, d)),
            pl.BlockSpec((1, 1, d), lambda bi, i: (bi, 0, 0)),
        ],
        out_specs=pl.BlockSpec((1, tm, d), row),
        scratch_shapes=[pltpu.VMEM((tm, V_W), F32)],
        compiler_params=_cparams("parallel", "parallel"),
        name="retention_mixer",
    )(qk, v, gate, pool_y, x, sf, sb, tabs["mpair"], tabs["qdec_f"], tabs["qdec_b"], tabs["hmask"],
      tabs["bd_bf16"], tabs["avg"], gn.reshape(1, V_W), w_out_bf, norm_w.reshape(1, d), g1)


def _router_kernel(x_ref, g_ref, sh_ref, sc_ref, wr_ref, h_ref, aff_ref):
    h = _norm_mod(x_ref[0], g_ref[...], sh_ref[0], sc_ref[0])
    h_ref[0] = h.astype(BF16)
    logits = _dot_nt(wr_ref[...], h, precision=lax.Precision.HIGHEST)
    mx = jnp.max(logits, axis=0, keepdims=True)
    e = jnp.exp(logits - mx)
    aff_ref[0] = e / jnp.sum(e, axis=0, keepdims=True)


def router(x, norm_w, shift, scale, w_router_t, tm):
    b, n, d = x.shape
    ne = w_router_t.shape[0]
    return pl.pallas_call(
        _router_kernel,
        out_shape=(jax.ShapeDtypeStruct((b, n, d), BF16), jax.ShapeDtypeStruct((b, ne, n), F32)),
        grid=(b, n // tm),
        in_specs=[
            pl.BlockSpec((1, tm, d), lambda bi, i: (bi, i, 0)),
            pl.BlockSpec((1, d), lambda bi, i: (0, 0)),
            pl.BlockSpec((1, 1, d), lambda bi, i: (bi, 0, 0)),
            pl.BlockSpec((1, 1, d), lambda bi, i: (bi, 0, 0)),
            pl.BlockSpec((ne, d), lambda bi, i: (0, 0)),
        ],
        out_specs=(
            pl.BlockSpec((1, tm, d), lambda bi, i: (bi, i, 0)),
            pl.BlockSpec((1, ne, tm), lambda bi, i: (bi, 0, i)),
        ),
        compiler_params=_cparams("parallel", "parallel"),
        name="router",
    )(x, norm_w.reshape(1, d), shift, scale, w_router_t)


def _ffn_kernel(x_ref, gate_ref, wg_ref, wu_ref, wd_ref, o_ref, *, rt):
    rows = x_ref.shape[2]
    for r0 in range(0, rows, rt):
        rs = slice(r0, r0 + rt)
        xt = x_ref[0, 0, rs, :]
        a = _dot(xt, wg_ref[0, 0])
        u = _dot(xt, wu_ref[0, 0])
        hh = (a * jax.nn.sigmoid(a) * u).astype(BF16)
        o_ref[0, 0, rs, :] = (_dot(hh, wd_ref[0, 0]) * gate_ref[0, 0, rs, :]).astype(BF16)


def expert_ffn(xs, gate, wg, wu, wd, layer):
    g, ne, rows, d = xs.shape
    f = wg.shape[3]
    rt = min(rows, 256)
    one = pl.Buffered(1)
    return pl.pallas_call(
        functools.partial(_ffn_kernel, rt=rt),
        out_shape=jax.ShapeDtypeStruct((g, ne, rows, d), BF16),
        grid=(ne, g),
        in_specs=[
            pl.BlockSpec((1, 1, rows, d), lambda e, gi: (gi, e, 0, 0)),
            pl.BlockSpec((1, 1, rows, 1), lambda e, gi: (gi, e, 0, 0)),
            pl.BlockSpec((1, 1, d, f), lambda e, gi: (layer, e, 0, 0), pipeline_mode=one),
            pl.BlockSpec((1, 1, d, f), lambda e, gi: (layer, e, 0, 0), pipeline_mode=one),
            pl.BlockSpec((1, 1, f, d), lambda e, gi: (layer, e, 0, 0), pipeline_mode=one),
        ],
        out_specs=pl.BlockSpec((1, 1, rows, d), lambda e, gi: (gi, e, 0, 0)),
        compiler_params=_cparams("arbitrary", "arbitrary"),
        name="expert_ffn",
    )(xs, gate, wg, wu, wd)


WEIGHT_CAST_CHUNKS = 4


def _cast_kernel(g_ref, u_ref, d_ref, go_ref, uo_ref, do_ref):
    go_ref[...] = g_ref[...].astype(BF16)
    uo_ref[...] = u_ref[...].astype(BF16)
    do_ref[...] = d_ref[...].astype(BF16)


def cast_expert_weights(w_gate, w_up, w_down):
    depth, ne, d, f = w_gate.shape
    nc = WEIGHT_CAST_CHUNKS
    up_spec = pl.BlockSpec((1, 1, d // nc, f), lambda l, e, c: (l, e, c, 0))
    down_spec = pl.BlockSpec((1, 1, f // nc, d), lambda l, e, c: (l, e, c, 0))
    return pl.pallas_call(
        _cast_kernel,
        out_shape=(jax.ShapeDtypeStruct(w_gate.shape, BF16), jax.ShapeDtypeStruct(w_up.shape, BF16),
                   jax.ShapeDtypeStruct(w_down.shape, BF16)),
        grid=(depth, ne, nc),
        in_specs=[up_spec, up_spec, down_spec],
        out_specs=(up_spec, up_spec, down_spec),
        compiler_params=_cparams("parallel", "parallel", "parallel"),
        name="cast_expert_weights",
    )(w_gate, w_up, w_down)


ROUTE_WIN = 64
LANES = 128
VAL_ROWS = 8


def _combine_kernel(st_ref, rank_ref, y_hbm, x_ref, nw_ref, g_ref, o_ref, buf, xbuf, sem, xsem,
                    acc_ref, *, tt, ne, cap, merged):
    b = pl.program_id(0)
    k = pl.program_id(1)
    nk = pl.num_programs(1)
    step = b * nk + k
    slot = step % 2
    rows_total = y_hbm.shape[2]
    win = ROUTE_WIN
    g = 0 if merged else b
    base = b * cap if merged else 0
    lane = lax.broadcasted_iota(jnp.int32, (1, LANES), 1)

    def tile_bounds(bb, kk, e):
        off = bb * cap if merged else 0
        lo = off + st_ref[(bb * (nk + 1) + kk) * ne + e]
        hi = off + st_ref[(bb * (nk + 1) + kk + 1) * ne + e]
        return lo, hi, jnp.minimum((lo // 16) * 16, rows_total - win)

    def bounds(e):
        return tile_bounds(b, k, e)

    def win_copy(bb, e, a0, sl):
        src = y_hbm.at[0 if merged else bb, e, pl.ds(pl.multiple_of(a0, 16), win), :]
        return pltpu.make_async_copy(src, buf.at[sl, pl.ds(e * win, win), :], sem.at[sl, e])

    def start_tile(bb, kk, sl):
        for e in range(ne):
            win_copy(bb, e, tile_bounds(bb, kk, e)[2], sl).start()

    @pl.when(step == 0)
    def _():
        start_tile(b, k, slot)

    nxt = step + 1

    @pl.when(nxt < pl.num_programs(0) * nk)
    def _():
        start_tile(nxt // nk, nxt % nk, 1 - slot)

    rkf = rank_ref[0].astype(F32)
    rk_t = jnp.concatenate([rkf, jnp.full((LANES - ne, tt), -1.0, F32)], axis=0).T
    rk_t = rk_t.astype(jnp.int32)

    def target(e, origin):
        col = rk_t[:, e:e + 1]
        return jnp.where(col >= 0, col + (base - origin), -1)

    pieces = []
    for j in range(ne // 2):
        t_even = target(2 * j, bounds(2 * j)[2])
        t_odd = target(2 * j + 1, bounds(2 * j + 1)[2])
        t_odd = jnp.where((t_odd >= 0) & (t_odd < win), t_odd + win, -1)
        pieces.append((jnp.where(lane < win, t_even, t_odd) == lane).astype(BF16))
    p = jnp.concatenate(pieces, axis=1)
    for e in range(ne):
        win_copy(b, e, bounds(e)[2], slot).wait()
    acc_ref[...] = _dot(p, buf[slot])

    for e in range(ne):
        lo, hi, a0 = bounds(e)

        def extra(w, carry, e=e, a0=a0):
            start = a0 + w * win
            aw = pl.multiple_of(jnp.minimum(start, rows_total - win), 16)
            cp = pltpu.make_async_copy(y_hbm.at[g, e, pl.ds(aw, win), :], xbuf, xsem)
            cp.start()
            cp.wait()
            col = rk_t[:, e:e + 1]
            ok = (col >= 0) & (col + base >= start)
            px = (jnp.where(ok, col + (base - aw), -1) == lane[:, 0:win]).astype(BF16)
            acc_ref[...] += _dot(px, xbuf[...])
            return carry

        lax.fori_loop(1, (hi - a0 + win - 1) // win, extra, 0)

    y = acc_ref[...]
    yn = y * lax.rsqrt(jnp.mean(y * y, axis=-1, keepdims=True) + EPS) * nw_ref[...]
    o_ref[0] = x_ref[0] + g_ref[0] * yn


def combine_post(starts, rank_t, y, x, norm_w, g, tt, cap, merged):
    b, n, d = x.shape
    _, ne, rows, _ = y.shape
    grid_spec = pltpu.PrefetchScalarGridSpec(
        num_scalar_prefetch=1,
        grid=(b, n // tt),
        in_specs=[
            pl.BlockSpec((1, ne, tt), lambda bi, k, st: (bi, 0, k)),
            pl.BlockSpec(memory_space=pl.ANY),
            pl.BlockSpec((1, tt, d), lambda bi, k, st: (bi, k, 0)),
            pl.BlockSpec((1, d), lambda bi, k, st: (0, 0)),
            pl.BlockSpec((1, 1, d), lambda bi, k, st: (bi, 0, 0)),
        ],
        out_specs=pl.BlockSpec((1, tt, d), lambda bi, k, st: (bi, k, 0)),
        scratch_shapes=[
            pltpu.VMEM((2, ne * ROUTE_WIN, d), BF16),
            pltpu.VMEM((ROUTE_WIN, d), BF16),
            pltpu.SemaphoreType.DMA((2, ne)),
            pltpu.SemaphoreType.DMA(()),
            pltpu.VMEM((tt, d), F32),
        ],
    )
    return pl.pallas_call(
        functools.partial(_combine_kernel, tt=tt, ne=ne, cap=cap, merged=merged),
        out_shape=jax.ShapeDtypeStruct((b, n, d), F32),
        grid_spec=grid_spec,
        compiler_params=_cparams("arbitrary", "arbitrary"),
        name="combine_post",
    )(starts, rank_t, y, x, norm_w.reshape(1, d), g)


def _select_kernel(aff_ref, tri_ref, rank_ref, st_ref, *, cap, tt):
    a = aff_ref[0]
    ne, n = a.shape
    bits = lax.bitcast_convert_type(a, jnp.int32)

    def search(_, c):
        lo, hi = c
        mid = lo + ((hi - lo) >> 1)
        cnt = jnp.sum((bits >= mid).astype(F32), axis=1, keepdims=True)
        ge = cnt >= cap
        return jnp.where(ge, mid, lo), jnp.where(ge, hi, mid)

    lo0 = jnp.zeros((ne, 1), jnp.int32)
    hi0 = jnp.full((ne, 1), 0x7F800000, jnp.int32)
    thr, _ = lax.fori_loop(0, 31, search, (lo0, hi0))
    gt = bits > thr
    eq = bits == thr
    need = cap - jnp.sum(gt.astype(F32), axis=1, keepdims=True)
    m = jnp.concatenate([gt, eq], axis=0).astype(BF16)
    tri = tri_ref[...]
    lane = lax.broadcasted_iota(jnp.int32, (1, LANES), 1)
    off = jnp.zeros((2 * ne, 1), F32)
    st = jnp.zeros((ne, LANES), jnp.int32)
    for j in range(n // LANES):
        cs = slice(j * LANES, (j + 1) * LANES)
        if (j * LANES) % tt == 0:
            off_sel = off[:ne] + jnp.minimum(off[ne:], need)
            st = jnp.where(lane == (j * LANES) // tt, off_sel.astype(jnp.int32), st)
        mj = m[:, cs]
        pj = _dot(mj, tri) + off
        pe = pj[ne:]
        sel = gt[:, cs] | (eq[:, cs] & (pe < need))
        rank_ref[0, :, cs] = jnp.where(sel, pj[:ne] + jnp.minimum(pe, need), -1.0).astype(jnp.int32)
        off = off + jnp.sum(mj.astype(F32), axis=1, keepdims=True)
    st_ref[0] = jnp.where(lane == n // tt, cap, st)


def route_select(aff_t, cap, tt):
    b, ne, n = aff_t.shape
    tri = jnp.asarray(np.triu(np.ones((LANES, LANES), np.float32), 1), BF16)
    return pl.pallas_call(
        functools.partial(_select_kernel, cap=cap, tt=tt),
        out_shape=(jax.ShapeDtypeStruct((b, ne, n), jnp.int32),
                   jax.ShapeDtypeStruct((b, ne, LANES), jnp.int32)),
        grid=(b,),
        in_specs=[pl.BlockSpec((1, ne, n), lambda bi: (bi, 0, 0)),
                  pl.BlockSpec((LANES, LANES), lambda bi: (0, 0))],
        out_specs=(pl.BlockSpec((1, ne, n), lambda bi: (bi, 0, 0)),
                   pl.BlockSpec((1, ne, LANES), lambda bi: (bi, 0, 0))),
        compiler_params=_cparams("parallel"),
        name="route_select",
    )(aff_t, tri)


def _compact_kernel(st_ref, rank_ref, aff_ref, out_ref, *, tt, ne, cap):
    b = pl.program_id(0)
    k = pl.program_id(1)
    nk = pl.num_programs(1)
    win = ROUTE_WIN

    @pl.when(k == 0)
    def _():
        out_ref[...] = jnp.zeros_like(out_ref)

    rk = rank_ref[0]
    a = aff_ref[0]
    g1 = a.astype(BF16).astype(F32)
    r1 = a - g1
    g2 = r1.astype(BF16).astype(F32)
    g3 = r1 - g2
    tok = k * tt + lax.broadcasted_iota(jnp.int32, (1, tt), 1)
    ids = jnp.concatenate([(tok >> 6).astype(F32), (tok & 63).astype(F32),
                           jnp.zeros((VAL_ROWS - 2, tt), F32)], axis=0)
    pad = jnp.zeros((LANES - VAL_ROWS - 3 * ne, tt), F32)
    payload = jnp.concatenate([ids, g1, g2, g3, pad], axis=0).astype(BF16)
    sub = lax.broadcasted_iota(jnp.int32, (win, 1), 0)
    for e in range(ne):
        lo = st_ref[(b * (nk + 1) + k) * ne + e]
        hi = st_ref[(b * (nk + 1) + k + 1) * ne + e]
        a0 = (lo // 8) * 8
        row = rk[e:e + 1, :]

        def window(w, carry, e=e, a0=a0, row=row):
            aw = pl.multiple_of(a0 + w * win, 8)
            p = ((row - aw) == sub).astype(BF16)
            out_ref[0, e, pl.ds(aw, win), :] += _dot_nt(p, payload)
            return carry

        window(0, 0)
        lax.fori_loop(1, (hi - a0 + win - 1) // win, window, 0)


def route_compact(starts, rank_t, aff_t, cap, tt):
    b, ne, n = aff_t.shape
    grid_spec = pltpu.PrefetchScalarGridSpec(
        num_scalar_prefetch=1,
        grid=(b, n // tt),
        in_specs=[pl.BlockSpec((1, ne, tt), lambda bi, k, st: (bi, 0, k)),
                  pl.BlockSpec((1, ne, tt), lambda bi, k, st: (bi, 0, k))],
        out_specs=pl.BlockSpec((1, ne, cap + ROUTE_WIN, LANES), lambda bi, k, st: (bi, 0, 0, 0)),
    )
    return pl.pallas_call(
        functools.partial(_compact_kernel, tt=tt, ne=ne, cap=cap),
        out_shape=jax.ShapeDtypeStruct((b, ne, cap + ROUTE_WIN, LANES), F32),
        grid_spec=grid_spec,
        compiler_params=_cparams("parallel", "arbitrary"),
        name="route_compact",
    )(starts, rank_t, aff_t)


def _qk_head_of_lane():
    half = QK_W // 2
    return (np.arange(QK_W) % half) // (DK // 2)


def _in_proj_perm():
    half = DK // 2
    first = [h * DK + i for h in range(N_HEADS) for i in range(half)]
    second = [h * DK + half + i for h in range(N_HEADS) for i in range(half)]
    qperm = np.array(first + second)
    rest = np.arange(2 * QK_W, 2 * QK_W + 2 * V_W + POOL_W)
    return np.concatenate([qperm, QK_W + qperm, rest])


def _static_tables():
    qk_head = _qk_head_of_lane()
    v_head = np.arange(V_W) // DV
    hmask = (qk_head[None, :] == np.arange(N_HEADS)[:, None])
    bd = (qk_head[:, None] == v_head[None, :])
    avg = (v_head[:, None] == v_head[None, :]).astype(np.float32) / DV
    return {
        "hmask": jnp.asarray(hmask[:, None, :], BF16),
        "bd_f32": jnp.asarray(bd, F32),
        "bd_bf16": jnp.asarray(bd, BF16),
        "avg": jnp.asarray(avg, BF16),
    }


def _decay_tables(lg_f, lg_b):
    pos = jnp.arange(CHUNK, dtype=F32)
    diff = pos[:, None] - pos[None, :]
    low = diff >= 0
    up = diff < 0
    m_f = jnp.where(low, jnp.exp(lg_f[:, None, None] * jnp.where(low, diff, 0.0)), 0.0)
    m_b = jnp.where(up, jnp.exp(lg_b[:, None, None] * jnp.where(up, -diff, 0.0)), 0.0)
    m = m_f + m_b
    mpair = m.reshape(N_HEADS // 2, 2, CHUNK, CHUNK).transpose(0, 2, 1, 3).reshape(
        N_HEADS // 2, CHUNK, 2 * CHUNK)
    qk_head = _qk_head_of_lane()
    v_head = np.arange(V_W) // DV
    qdec_f = jnp.exp(lg_f[None, :] * (pos[:, None] + 1.0))[:, v_head]
    qdec_b = jnp.exp(lg_b[None, :] * (CHUNK - pos[:, None]))[:, v_head]
    kdec_f = jnp.exp(lg_f[None, :] * (CHUNK - 1.0 - pos[:, None]))[:, qk_head]
    kdec_b = jnp.exp(lg_b[None, :] * pos[:, None])[:, qk_head]
    cdec_f = jnp.exp(lg_f * CHUNK)[None, v_head]
    cdec_b = jnp.exp(lg_b * CHUNK)[None, v_head]
    return {"mpair": mpair, "qdec_f": qdec_f, "qdec_b": qdec_b, "kdec_f": kdec_f,
            "kdec_b": kdec_b, "cdec_f": cdec_f, "cdec_b": cdec_b}


def _rope_tables(n):
    t = jnp.arange(n)
    row = (t // GRID_W).astype(F32)
    col = (t % GRID_W).astype(F32)
    n_freq = DK // 4
    inv = ROPE_BASE ** (-jnp.arange(n_freq, dtype=F32) / n_freq)
    ang = jnp.concatenate([row[:, None] * inv, col[:, None] * inv], axis=-1)
    return jnp.tile(jnp.cos(ang), (1, N_HEADS)), jnp.tile(jnp.sin(ang), (1, N_HEADS))


def _moe_residual(x, h_bf, aff_t, wg, wu, wd, layer, norm_w, g, merged):
    b, n, d = h_bf.shape
    cap = EC_FACTOR * n // N_EXPERTS
    ne = N_EXPERTS
    tt = min(n, 256)
    nk = n // tt
    rank_t, st = route_select(aff_t, cap, tt)
    starts = st[:, :, :nk + 1].transpose(0, 2, 1).reshape(-1)
    rows = route_compact(starts, rank_t, aff_t, cap, tt)[:, :, :cap]
    idx = (rows[..., 0] * 64.0 + rows[..., 1]).astype(jnp.int32)
    pieces = rows[..., VAL_ROWS:VAL_ROWS + 3 * ne].reshape(b, ne, cap, 3, ne)
    own = pieces[:, jnp.arange(ne), :, :, jnp.arange(ne)]
    gate = ((own[..., 0] + own[..., 1]) + own[..., 2]).transpose(1, 0, 2)
    xs = jax.vmap(lambda hb, ib: hb[ib])(h_bf, idx)
    if merged:
        xs = xs.transpose(1, 0, 2, 3).reshape(1, ne, b * cap, d)
        gate = gate.transpose(1, 0, 2).reshape(1, ne, b * cap)
    y = expert_ffn(xs, gate[..., None], wg, wu, wd, layer)
    return combine_post(starts, rank_t, y, x, norm_w, g, tt, cap, merged)


def kernel(x, c, ctx, c_ctx, w_ada, b_ada, norm_pre_mix, norm_post_mix, norm_pre_ffn, norm_post_ffn, w_in, ret_decay_fwd, ret_decay_bwd, ret_gn, pool_w, pool_scale, w_out, w_router, w_gate, w_up, w_down):
    b, n, d = x.shape
    lc = ctx.shape[1]
    depth = w_ada.shape[0]
    rope = _rope_tables(n)
    static = _static_tables()
    perm = _in_proj_perm()

    cc = jnp.concatenate([c, c_ctx[None, :], jnp.zeros((7, d), F32)], axis=0)
    mods = ada_modulation(cc, w_ada, b_ada)
    wg, wu, wd = cast_expert_weights(w_gate, w_up, w_down)

    tm = 512
    for l in range(depth):
        last = l == depth - 1
        mx = mods[l, :b].reshape(b, 1, 6, d)
        sh1, sc1, g1, sh2, sc2, g2 = [mx[:, :, i] for i in range(6)]
        mc = jnp.broadcast_to(mods[l, b].reshape(1, 1, 6, d), (b, 1, 6, d))
        csh1, csc1, cg1, csh2, csc2, cg2 = [mc[:, :, i] for i in range(6)]
        lg_f = jax.nn.log_sigmoid(ret_decay_fwd[l].astype(F32))
        lg_b = jax.nn.log_sigmoid(ret_decay_bwd[l].astype(F32))
        tabs = dict(static, **_decay_tables(lg_f, lg_b))
        w_in_p = w_in[l][:, perm].astype(BF16)
        w_out_bf = w_out[l].astype(BF16)
        pool_w_bf = pool_w[l].astype(BF16)

        qk_c, v_c, gate_c, p_c = premix(ctx, norm_pre_mix[l], csh1, csc1, w_in_p, None, lc)
        zero = jnp.zeros((b, DK, V_W), F32)
        sf_c, sb_c, s_f, s_b = state_scan(qk_c, v_c, zero, zero, tabs)
        qk_x, v_x, gate_x, p_x = premix(x, norm_pre_mix[l], sh1, sc1, w_in_p, rope, tm)
        sf_x, sb_x, _, _ = state_scan(qk_x, v_x, s_f, s_b, tabs)
        pool_x = pool_mixer(p_x, pool_w_bf, pool_scale[l], GRID_W, min(n, 2048))
        x = retention_mixer(qk_x, v_x, gate_x, pool_x, x, sf_x, sb_x, tabs, ret_gn[l], w_out_bf,
                            norm_post_mix[l], g1, tm)
        if not last:
            pool_c = pool_mixer(p_c, pool_w_bf, pool_scale[l], lc, lc)
            ctx = retention_mixer(qk_c, v_c, gate_c, pool_c, ctx, sf_c, sb_c, tabs, ret_gn[l],
                                  w_out_bf, norm_post_mix[l], cg1, lc)

        wr_t = w_router[l].T
        h_x, aff_x = router(x, norm_pre_ffn[l], sh2, sc2, wr_t, tm)
        x = _moe_residual(x, h_x, aff_x, wg, wu, wd, l, norm_post_ffn[l], g2, False)
        if not last:
            h_c, aff_c = router(ctx, norm_pre_ffn[l], csh2, csc2, wr_t, lc)
            ctx = _moe_residual(ctx, h_c, aff_c, wg, wu, wd, l, norm_post_ffn[l], cg2, True)
    return x
```

```python
import functools

import jax
import jax.numpy as jnp
import numpy as np
from jax import lax
from jax.experimental import pallas as pl
from jax.experimental.pallas import tpu as pltpu

F32 = jnp.float32
BF16 = jnp.bfloat16

D_MODEL = 1024
GRID_W = 64
N_HEADS = 8
DV = 64
DK = 32
QK_W = N_HEADS * DK
V_W = N_HEADS * DV
POOL_W = 512
POOL_WINDOWS = (2, 4, 8, 16)
POOL_GROUP_DIM = 128
CHUNK = 128
ROPE_BASE = 10000.0
N_EXPERTS = 16
EC_FACTOR = 2
EPS = 1e-6
MAX_HALF_WINDOW = max(POOL_WINDOWS) // 2

VMEM_LIMIT = 56 * 1024 * 1024


def _cparams(*sem):
    return pltpu.CompilerParams(dimension_semantics=sem, vmem_limit_bytes=VMEM_LIMIT)


def _dot(a, b):
    return jnp.dot(a, b, preferred_element_type=F32)


def _dot_nt(a, b, precision=None):
    return lax.dot_general(a, b, (((1,), (1,)), ((), ())), precision=precision,
                           preferred_element_type=F32)


def _ada_kernel(cc_ref, w_ref, b_ref, o_ref):
    s = cc_ref[...]
    s = s * jax.nn.sigmoid(s)
    o_ref[0] = _dot(s.astype(BF16), w_ref[0].astype(BF16)) + b_ref[0]


def ada_modulation(cc, w_ada, b_ada):
    depth, d, d6 = w_ada.shape
    rows = cc.shape[0]
    tn = 1536
    return pl.pallas_call(
        _ada_kernel,
        out_shape=jax.ShapeDtypeStruct((depth, rows, d6), F32),
        grid=(depth, d6 // tn),
        in_specs=[
            pl.BlockSpec((rows, d), lambda l, j: (0, 0)),
            pl.BlockSpec((1, d, tn), lambda l, j: (l, 0, j)),
            pl.BlockSpec((1, 1, tn), lambda l, j: (l, 0, j)),
        ],
        out_specs=pl.BlockSpec((1, rows, tn), lambda l, j: (l, 0, j)),
        compiler_params=_cparams("parallel", "parallel"),
        name="ada_modulation",
    )(cc, w_ada, b_ada.reshape(depth, 1, d6))


def _norm_mod(xf, g, sh, sc):
    y = xf * lax.rsqrt(jnp.mean(xf * xf, axis=-1, keepdims=True) + EPS)
    return (y * g) * (1.0 + sc) + sh


def _premix_kernel(x_ref, g_ref, sh_ref, sc_ref, w_ref, *rest, rope):
    if rope:
        cos_ref, sin_ref, qk_ref, v_ref, gate_ref, p_ref = rest
    else:
        qk_ref, v_ref, gate_ref, p_ref = rest
    hb = _norm_mod(x_ref[0], g_ref[...], sh_ref[0], sc_ref[0]).astype(BF16)
    zqk = _dot(hb, w_ref[:, 0:2 * QK_W])
    half = QK_W // 2
    q1, q2 = zqk[:, 0:half], zqk[:, half:2 * half]
    k1 = zqk[:, 2 * half:3 * half] * (DK ** -0.5)
    k2 = zqk[:, 3 * half:4 * half] * (DK ** -0.5)
    if rope:
        cos, sin = cos_ref[...], sin_ref[...]
        q1, q2 = q1 * cos - q2 * sin, q1 * sin + q2 * cos
        k1, k2 = k1 * cos - k2 * sin, k1 * sin + k2 * cos
    qk_ref[0] = jnp.concatenate([q1, q2, k1, k2], axis=1).astype(BF16)
    o = 2 * QK_W
    v_ref[0] = _dot(hb, w_ref[:, o:o + V_W]).astype(BF16)
    gate_ref[0] = _dot(hb, w_ref[:, o + V_W:o + 2 * V_W])
    p_ref[0] = _dot(hb, w_ref[:, o + 2 * V_W:o + 2 * V_W + POOL_W])


def premix(x, norm_w, shift, scale, w_in_p, rope, tm):
    b, n, d = x.shape
    in_w = w_in_p.shape[1]
    row = lambda bi, i: (bi, i, 0)
    in_specs = [
        pl.BlockSpec((1, tm, d), row),
        pl.BlockSpec((1, d), lambda bi, i: (0, 0)),
        pl.BlockSpec((1, 1, d), lambda bi, i: (bi, 0, 0)),
        pl.BlockSpec((1, 1, d), lambda bi, i: (bi, 0, 0)),
        pl.BlockSpec((d, in_w), lambda bi, i: (0, 0)),
    ]
    args = [x, norm_w.reshape(1, d), shift, scale, w_in_p]
    if rope is not None:
        half = QK_W // 2
        in_specs += [pl.BlockSpec((tm, half), lambda bi, i: (i, 0))] * 2
        args += list(rope)
    return pl.pallas_call(
        functools.partial(_premix_kernel, rope=rope is not None),
        out_shape=(
            jax.ShapeDtypeStruct((b, n, 2 * QK_W), BF16),
            jax.ShapeDtypeStruct((b, n, V_W), BF16),
            jax.ShapeDtypeStruct((b, n, V_W), F32),
            jax.ShapeDtypeStruct((b, n, POOL_W), F32),
        ),
        grid=(b, n // tm),
        in_specs=in_specs,
        out_specs=(
            pl.BlockSpec((1, tm, 2 * QK_W), row),
            pl.BlockSpec((1, tm, V_W), row),
            pl.BlockSpec((1, tm, V_W), row),
            pl.BlockSpec((1, tm, POOL_W), row),
        ),
        compiler_params=_cparams("parallel", "parallel"),
        name="premix",
    )(*args)


def _fold_state(s):
    g = 2 * N_HEADS
    top = s[0:g]
    bot = s[QK_W // 2:QK_W // 2 + g]
    for h in range(1, N_HEADS):
        top = top + s[h * g:(h + 1) * g]
        bot = bot + s[QK_W // 2 + h * g:QK_W // 2 + (h + 1) * g]
    return jnp.concatenate([top, bot], axis=0)


def _expand_state(c):
    g = 2 * N_HEADS
    return jnp.concatenate([c[0:g]] * N_HEADS + [c[g:2 * g]] * N_HEADS, axis=0)


def _state_kernel(kf_ref, vf_ref, kb_ref, vb_ref, s0f_ref, s0b_ref, kdf_ref, kdb_ref,
                  cdf_ref, cdb_ref, bd_ref, sf_out, sb_out, ff_out, fb_out, sf_acc, sb_acc, *, cps):
    c = pl.program_id(1)
    nc = pl.num_programs(1)
    bd = bd_ref[...]

    @pl.when(c == 0)
    def _():
        sf_acc[...] = _expand_state(s0f_ref[0]) * bd
        sb_acc[...] = _expand_state(s0b_ref[0]) * bd

    def update(s, k, v, kd_ref, cd_ref):
        kd = k.astype(F32) * kd_ref[...]
        f = _dot(kd.T.astype(BF16), v)
        return s * cd_ref[...] + f * bd

    sf = sf_acc[...]
    sb = sb_acc[...]
    for i in range(cps):
        j = cps - 1 - i
        fs = slice(i * CHUNK, (i + 1) * CHUNK)
        bs = slice(j * CHUNK, (j + 1) * CHUNK)
        sf_out[0, i] = _fold_state(sf).astype(BF16)
        sb_out[0, j] = _fold_state(sb).astype(BF16)
        sf = update(sf, kf_ref[0, fs, :], vf_ref[0, fs, :], kdf_ref, cdf_ref)
        sb = update(sb, kb_ref[0, bs, :], vb_ref[0, bs, :], kdb_ref, cdb_ref)
    sf_acc[...] = sf
    sb_acc[...] = sb

    @pl.when(c == nc - 1)
    def _():
        ff_out[0] = _fold_state(sf)
        fb_out[0] = _fold_state(sb)


def state_scan(qk, v, s0f, s0b, tabs):
    b, n, _ = qk.shape
    nc = n // CHUNK
    cps = min(nc, 4)
    ns = nc // cps
    rows = cps * CHUNK
    const = lambda shape: pl.BlockSpec(shape, lambda bi, c: (0,) * len(shape))
    return pl.pallas_call(
        functools.partial(_state_kernel, cps=cps),
        out_shape=(
            jax.ShapeDtypeStruct((b, nc, DK, V_W), BF16),
            jax.ShapeDtypeStruct((b, nc, DK, V_W), BF16),
            jax.ShapeDtypeStruct((b, DK, V_W), F32),
            jax.ShapeDtypeStruct((b, DK, V_W), F32),
        ),
        grid=(b, ns),
        in_specs=[
            pl.BlockSpec((1, rows, QK_W), lambda bi, c: (bi, c, 1)),
            pl.BlockSpec((1, rows, V_W), lambda bi, c: (bi, c, 0)),
            pl.BlockSpec((1, rows, QK_W), lambda bi, c: (bi, ns - 1 - c, 1)),
            pl.BlockSpec((1, rows, V_W), lambda bi, c: (bi, ns - 1 - c, 0)),
            pl.BlockSpec((1, DK, V_W), lambda bi, c: (bi, 0, 0)),
            pl.BlockSpec((1, DK, V_W), lambda bi, c: (bi, 0, 0)),
            const((CHUNK, QK_W)), const((CHUNK, QK_W)),
            const((1, V_W)), const((1, V_W)),
            const((QK_W, V_W)),
        ],
        out_specs=(
            pl.BlockSpec((1, cps, DK, V_W), lambda bi, c: (bi, c, 0, 0)),
            pl.BlockSpec((1, cps, DK, V_W), lambda bi, c: (bi, ns - 1 - c, 0, 0)),
            pl.BlockSpec((1, DK, V_W), lambda bi, c: (bi, 0, 0)),
            pl.BlockSpec((1, DK, V_W), lambda bi, c: (bi, 0, 0)),
        ),
        scratch_shapes=[pltpu.VMEM((QK_W, V_W), F32), pltpu.VMEM((QK_W, V_W), F32)],
        compiler_params=_cparams("parallel", "arbitrary"),
        name="state_scan",
    )(qk, v, qk, v, s0f, s0b, tabs["kdec_f"], tabs["kdec_b"], tabs["cdec_f"], tabs["cdec_b"],
      tabs["bd_f32"])


def _pool_kernel(*refs, gw, tile, halo, rows_total):
    if halo:
        prev_ref, cur_ref, next_ref, pw_ref, ps_ref, o_ref, buf = refs
    else:
        cur_ref, pw_ref, ps_ref, o_ref, buf = refs
    i = pl.program_id(1)
    last = pl.num_programs(1) - 1
    m = MAX_HALF_WINDOW
    span = tile + 2 * halo
    zeros_m = jnp.zeros((m, POOL_W), F32)
    buf[0:m] = zeros_m
    buf[m + span:2 * m + span] = zeros_m
    if halo:
        buf[m:m + halo] = jnp.where(i > 0, prev_ref[0], 0.0)
        buf[m + halo + tile:m + span] = jnp.where(i < last, next_ref[0], 0.0)
    buf[m + halo:m + halo + tile] = cur_ref[0]

    pos = lax.broadcasted_iota(jnp.int32, (span, 1), 0)
    col = pos % gw
    tpos = lax.broadcasted_iota(jnp.int32, (tile, 1), 0)
    tcol = tpos % gw
    trow = i * (tile // gw) + tpos // gw
    for gi, w in enumerate(POOL_WINDOWS):
        cs = slice(gi * POOL_GROUP_DIM, (gi + 1) * POOL_GROUP_DIM)
        hw = w // 2
        s = None
        for d in range(-hw, hw):
            valid = (col + d >= 0) & (col + d < gw)
            term = jnp.where(valid, buf[m + d:m + d + span, cs], 0.0)
            s = term if s is None else s + term
        if halo:
            acc = None
            for d in range(-hw, hw):
                start = halo + d * gw
                term = s[start:start + tile]
                acc = term if acc is None else acc + term
            cnt_r = jnp.minimum(trow + hw, rows_total) - jnp.maximum(trow - hw, 0)
        else:
            acc = s
            cnt_r = 1
        cnt_c = jnp.minimum(tcol + hw, gw) - jnp.maximum(tcol - hw, 0)
        cnt = (cnt_r * cnt_c).astype(F32)
        xg = buf[m + halo:m + halo + tile, cs]
        diff = (acc / cnt - xg).astype(BF16)
        y = _dot(diff, pw_ref[gi]) * ps_ref[:, cs]
        o_ref[0, :, cs] = y.astype(BF16)


def pool_mixer(p, pool_w_bf, pool_scale, gw, tile):
    b, n, pw = p.shape
    rows_total = n // gw
    halo = MAX_HALF_WINDOW * gw if rows_total > 1 else 0
    in_specs, args = [], []
    if halo:
        r = tile // halo
        nh = n // halo
        in_specs = [
            pl.BlockSpec((1, halo, pw), lambda bi, i: (bi, jnp.maximum(i * r - 1, 0), 0)),
            pl.BlockSpec((1, tile, pw), lambda bi, i: (bi, i, 0)),
            pl.BlockSpec((1, halo, pw), lambda bi, i: (bi, jnp.minimum((i + 1) * r, nh - 1), 0)),
        ]
        args = [p, p, p]
    else:
        in_specs = [pl.BlockSpec((1, tile, pw), lambda bi, i: (bi, i, 0))]
        args = [p]
    ng = len(POOL_WINDOWS)
    in_specs += [
        pl.BlockSpec((ng, POOL_GROUP_DIM, POOL_GROUP_DIM), lambda bi, i: (0, 0, 0)),
        pl.BlockSpec((1, pw), lambda bi, i: (0, 0)),
    ]
    args += [pool_w_bf, pool_scale.reshape(1, pw)]
    return pl.pallas_call(
        functools.partial(_pool_kernel, gw=gw, tile=tile, halo=halo, rows_total=rows_total),
        out_shape=jax.ShapeDtypeStruct((b, n, pw), BF16),
        grid=(b, n // tile),
        in_specs=in_specs,
        out_specs=pl.BlockSpec((1, tile, pw), lambda bi, i: (bi, i, 0)),
        scratch_shapes=[pltpu.VMEM((tile + 2 * halo + 2 * MAX_HALF_WINDOW, pw), F32)],
        compiler_params=_cparams("parallel", "parallel"),
        name="pool_mixer",
    )(*args)


def _ret_kernel(qk_ref, v_ref, gate_ref, py_ref, x_ref, sf_ref, sb_ref, mp_ref, qdf_ref, qdb_ref,
                hm_ref, bd_ref, avg_ref, gn_ref, wo_ref, nw_ref, g1_ref, o_ref, o_buf, *, nch):
    bd = bd_ref[...]
    avg = avg_ref[...]
    lane = lax.broadcasted_iota(jnp.int32, (1, 2 * DV), 1)
    lo_mask = (lane < DV).astype(BF16)
    hi_mask = (lane >= DV).astype(BF16)
    for ci in range(nch):
        rs = slice(ci * CHUNK, (ci + 1) * CHUNK)
        q = qk_ref[0, rs, 0:QK_W]
        k = qk_ref[0, rs, QK_W:2 * QK_W]
        vv = v_ref[0, rs, :]
        inner = []
        for j in range(N_HEADS // 2):
            kp = jnp.concatenate([k * hm_ref[2 * j], k * hm_ref[2 * j + 1]], axis=0)
            sc = _dot_nt(q, kp) * mp_ref[j]
            vpair = vv[:, 2 * DV * j:2 * DV * (j + 1)]
            vp = jnp.concatenate([vpair * lo_mask, vpair * hi_mask], axis=0)
            inner.append(_dot(sc.astype(BF16), vp))
        o = jnp.concatenate(inner, axis=1)
        sf_bd = _expand_state(sf_ref[0, ci]) * bd
        sb_bd = _expand_state(sb_ref[0, ci]) * bd
        o_buf[rs, :] = o + _dot(q, sf_bd) * qdf_ref[...] + _dot(q, sb_bd) * qdb_ref[...]

    def head_mean(a):
        hi = a.astype(BF16)
        lo = (a - hi.astype(F32)).astype(BF16)
        return _dot(hi, avg) + _dot(lo, avg)

    o = o_buf[...]
    dlt = o - head_mean(o)
    var = head_mean(dlt * dlt)
    yn = dlt * lax.rsqrt(var + EPS) * gn_ref[...]
    g = gate_ref[0]
    r = (g * jax.nn.sigmoid(g) * yn).astype(BF16)
    mix = _dot(r, wo_ref[0:V_W, :]) + _dot(py_ref[0], wo_ref[V_W:V_W + POOL_W, :])
    y = mix * lax.rsqrt(jnp.mean(mix * mix, axis=-1, keepdims=True) + EPS) * nw_ref[...]
    o_ref[0] = x_ref[0] + g1_ref[0] * y


def retention_mixer(qk, v, gate, pool_y, x, sf, sb, tabs, gn, w_out_bf, norm_w, g1, tm):
    b, n, d = x.shape
    nch = tm // CHUNK
    row = lambda bi, i: (bi, i, 0)
    const = lambda shape: pl.BlockSpec(shape, lambda bi, i: (0,) * len(shape))
    return pl.pallas_call(
        functools.partial(_ret_kernel, nch=nch),
        out_shape=jax.ShapeDtypeStruct((b, n, d), F32),
        grid=(b, n // tm),
        in_specs=[
            pl.BlockSpec((1, tm, 2 * QK_W), row),
            pl.BlockSpec((1, tm, V_W), row),
            pl.BlockSpec((1, tm, V_W), row),
            pl.BlockSpec((1, tm, POOL_W), row),
            pl.BlockSpec((1, tm, d), row),
            pl.BlockSpec((1, nch, DK, V_W), lambda bi, i: (bi, i, 0, 0)),
            pl.BlockSpec((1, nch, DK, V_W), lambda bi, i: (bi, i, 0, 0)),
            const((N_HEADS // 2, CHUNK, 2 * CHUNK)),
            const((CHUNK, V_W)), const((CHUNK, V_W)),
            const((N_HEADS, 1, QK_W)),
            const((QK_W, V_W)),
            const((V_W, V_W)),
            const((1, V_W)),
            const((V_W + POOL_W, d)),
            const((1, d)),
            pl.BlockSpec((1, 1, d), lambda bi, i: (bi, 0, 0)),
        ],
        out_specs=pl.BlockSpec((1, tm, d), row),
        scratch_shapes=[pltpu.VMEM((tm, V_W), F32)],
        compiler_params=_cparams("parallel", "parallel"),
        name="retention_mixer",
    )(qk, v, gate, pool_y, x, sf, sb, tabs["mpair"], tabs["qdec_f"], tabs["qdec_b"], tabs["hmask"],
      tabs["bd_bf16"], tabs["avg"], gn.reshape(1, V_W), w_out_bf, norm_w.reshape(1, d), g1)


def _router_kernel(x_ref, g_ref, sh_ref, sc_ref, wr_ref, h_ref, aff_ref):
    h = _norm_mod(x_ref[0], g_ref[...], sh_ref[0], sc_ref[0])
    h_hi = h.astype(BF16)
    h_ref[0] = h_hi
    ne = wr_ref.shape[0]
    h_lo = (h - h_hi.astype(F32)).astype(BF16)
    w = wr_ref[...]
    w_hi = w.astype(BF16)
    w_lo = (w - w_hi.astype(F32)).astype(BF16)
    both = _dot_nt(jnp.concatenate([w_hi, w_lo], axis=0), h_hi)
    logits = both[0:ne] + both[ne:2 * ne] + _dot_nt(w_hi, h_lo)
    mx = jnp.max(logits, axis=0, keepdims=True)
    e = jnp.exp(logits - mx)
    aff_ref[0] = e / jnp.sum(e, axis=0, keepdims=True)


def router(x, norm_w, shift, scale, w_router_t, tm):
    b, n, d = x.shape
    ne = w_router_t.shape[0]
    return pl.pallas_call(
        _router_kernel,
        out_shape=(jax.ShapeDtypeStruct((b, n, d), BF16), jax.ShapeDtypeStruct((b, ne, n), F32)),
        grid=(b, n // tm),
        in_specs=[
            pl.BlockSpec((1, tm, d), lambda bi, i: (bi, i, 0)),
            pl.BlockSpec((1, d), lambda bi, i: (0, 0)),
            pl.BlockSpec((1, 1, d), lambda bi, i: (bi, 0, 0)),
            pl.BlockSpec((1, 1, d), lambda bi, i: (bi, 0, 0)),
            pl.BlockSpec((ne, d), lambda bi, i: (0, 0)),
        ],
        out_specs=(
            pl.BlockSpec((1, tm, d), lambda bi, i: (bi, i, 0)),
            pl.BlockSpec((1, ne, tm), lambda bi, i: (bi, 0, i)),
        ),
        compiler_params=_cparams("parallel", "parallel"),
        name="router",
    )(x, norm_w.reshape(1, d), shift, scale, w_router_t)


def _ffn_kernel(x_ref, gate_ref, wg_ref, wu_ref, wd_ref, o_ref, *, rt):
    rows = x_ref.shape[2]
    for r0 in range(0, rows, rt):
        rs = slice(r0, r0 + rt)
        xt = x_ref[0, 0, rs, :]
        a = _dot_nt(xt, wg_ref[0, 0])
        u = _dot_nt(xt, wu_ref[0, 0])
        hh = (a * jax.nn.sigmoid(a) * u).astype(BF16)
        o_ref[0, 0, rs, :] = (_dot(hh, wd_ref[0, 0]) * gate_ref[0, 0, rs, :]).astype(BF16)


def expert_ffn(xs, gate, wg_t, wu_t, wd, layer):
    g, ne, rows, d = xs.shape
    f = wd.shape[2]
    rt = min(rows, 256)
    one = pl.Buffered(1)
    return pl.pallas_call(
        functools.partial(_ffn_kernel, rt=rt),
        out_shape=jax.ShapeDtypeStruct((g, ne, rows, d), BF16),
        grid=(ne, g),
        in_specs=[
            pl.BlockSpec((1, 1, rows, d), lambda e, gi: (gi, e, 0, 0)),
            pl.BlockSpec((1, 1, rows, 1), lambda e, gi: (gi, e, 0, 0)),
            pl.BlockSpec((1, 1, f, d), lambda e, gi: (layer, e, 0, 0), pipeline_mode=one),
            pl.BlockSpec((1, 1, f, d), lambda e, gi: (layer, e, 0, 0), pipeline_mode=one),
            pl.BlockSpec((1, 1, f, d), lambda e, gi: (layer, e, 0, 0), pipeline_mode=one),
        ],
        out_specs=pl.BlockSpec((1, 1, rows, d), lambda e, gi: (gi, e, 0, 0)),
        compiler_params=_cparams("arbitrary", "arbitrary"),
        name="expert_ffn",
    )(xs, gate, wg_t, wu_t, wd)


WEIGHT_CAST_CHUNKS = 4


def _cast_kernel(g_ref, u_ref, d_ref, go_ref, uo_ref, do_ref):
    go_ref[...] = g_ref[...].astype(BF16)
    uo_ref[...] = u_ref[...].astype(BF16)
    do_ref[...] = d_ref[...].astype(BF16)


def cast_expert_weights(w_gate_t, w_up_t, w_down):
    depth, ne, f, d = w_down.shape
    spec = pl.BlockSpec((1, 1, f // WEIGHT_CAST_CHUNKS, d), lambda l, e, c: (l, e, c, 0))
    out = jax.ShapeDtypeStruct(w_down.shape, BF16)
    return pl.pallas_call(
        _cast_kernel,
        out_shape=(out, out, out),
        grid=(depth, ne, WEIGHT_CAST_CHUNKS),
        in_specs=[spec, spec, spec],
        out_specs=(spec, spec, spec),
        compiler_params=_cparams("parallel", "parallel", "parallel"),
        name="cast_expert_weights",
    )(w_gate_t, w_up_t, w_down)


ROUTE_WIN = 64
LANES = 128
VAL_ROWS = 8


def _combine_kernel(st_ref, rank_ref, y_hbm, x_ref, nw_ref, g_ref, o_ref, buf, xbuf, sem, xsem,
                    acc_ref, *, tt, ne, cap, merged):
    b = pl.program_id(0)
    k = pl.program_id(1)
    nk = pl.num_programs(1)
    step = b * nk + k
    slot = step % 2
    rows_total = y_hbm.shape[2]
    win = ROUTE_WIN
    g = 0 if merged else b
    base = b * cap if merged else 0
    lane = lax.broadcasted_iota(jnp.int32, (1, LANES), 1)

    def tile_bounds(bb, kk, e):
        off = bb * cap if merged else 0
        lo = off + st_ref[(bb * (nk + 1) + kk) * ne + e]
        hi = off + st_ref[(bb * (nk + 1) + kk + 1) * ne + e]
        return lo, hi, jnp.minimum((lo // 16) * 16, rows_total - win)

    def bounds(e):
        return tile_bounds(b, k, e)

    def win_copy(bb, e, a0, sl):
        src = y_hbm.at[0 if merged else bb, e, pl.ds(pl.multiple_of(a0, 16), win), :]
        return pltpu.make_async_copy(src, buf.at[sl, pl.ds(e * win, win), :], sem.at[sl, e])

    def start_tile(bb, kk, sl):
        for e in range(ne):
            win_copy(bb, e, tile_bounds(bb, kk, e)[2], sl).start()

    @pl.when(step == 0)
    def _():
        start_tile(b, k, slot)

    nxt = step + 1

    @pl.when(nxt < pl.num_programs(0) * nk)
    def _():
        start_tile(nxt // nk, nxt % nk, 1 - slot)

    rkf = rank_ref[0].astype(F32)
    rk_t = jnp.concatenate([rkf, jnp.full((LANES - ne, tt), -1.0, F32)], axis=0).T
    rk_t = rk_t.astype(jnp.int32)

    def target(e, origin):
        col = rk_t[:, e:e + 1]
        return jnp.where(col >= 0, col + (base - origin), -1)

    pieces = []
    for j in range(ne // 2):
        t_even = target(2 * j, bounds(2 * j)[2])
        t_odd = target(2 * j + 1, bounds(2 * j + 1)[2])
        t_odd = jnp.where((t_odd >= 0) & (t_odd < win), t_odd + win, -1)
        pieces.append((jnp.where(lane < win, t_even, t_odd) == lane).astype(BF16))
    p = jnp.concatenate(pieces, axis=1)
    for e in range(ne):
        win_copy(b, e, bounds(e)[2], slot).wait()
    acc_ref[...] = _dot(p, buf[slot])

    for e in range(ne):
        lo, hi, a0 = bounds(e)

        def extra(w, carry, e=e, a0=a0):
            start = a0 + w * win
            aw = pl.multiple_of(jnp.minimum(start, rows_total - win), 16)
            cp = pltpu.make_async_copy(y_hbm.at[g, e, pl.ds(aw, win), :], xbuf, xsem)
            cp.start()
            cp.wait()
            col = rk_t[:, e:e + 1]
            ok = (col >= 0) & (col + base >= start)
            px = (jnp.where(ok, col + (base - aw), -1) == lane[:, 0:win]).astype(BF16)
            acc_ref[...] += _dot(px, xbuf[...])
            return carry

        lax.fori_loop(1, (hi - a0 + win - 1) // win, extra, 0)

    y = acc_ref[...]
    yn = y * lax.rsqrt(jnp.mean(y * y, axis=-1, keepdims=True) + EPS) * nw_ref[...]
    o_ref[0] = x_ref[0] + g_ref[0] * yn


def combine_post(starts, rank_t, y, x, norm_w, g, tt, cap, merged):
    b, n, d = x.shape
    _, ne, rows, _ = y.shape
    grid_spec = pltpu.PrefetchScalarGridSpec(
        num_scalar_prefetch=1,
        grid=(b, n // tt),
        in_specs=[
            pl.BlockSpec((1, ne, tt), lambda bi, k, st: (bi, 0, k)),
            pl.BlockSpec(memory_space=pl.ANY),
            pl.BlockSpec((1, tt, d), lambda bi, k, st: (bi, k, 0)),
            pl.BlockSpec((1, d), lambda bi, k, st: (0, 0)),
            pl.BlockSpec((1, 1, d), lambda bi, k, st: (bi, 0, 0)),
        ],
        out_specs=pl.BlockSpec((1, tt, d), lambda bi, k, st: (bi, k, 0)),
        scratch_shapes=[
            pltpu.VMEM((2, ne * ROUTE_WIN, d), BF16),
            pltpu.VMEM((ROUTE_WIN, d), BF16),
            pltpu.SemaphoreType.DMA((2, ne)),
            pltpu.SemaphoreType.DMA(()),
            pltpu.VMEM((tt, d), F32),
        ],
    )
    return pl.pallas_call(
        functools.partial(_combine_kernel, tt=tt, ne=ne, cap=cap, merged=merged),
        out_shape=jax.ShapeDtypeStruct((b, n, d), F32),
        grid_spec=grid_spec,
        compiler_params=_cparams("arbitrary", "arbitrary"),
        name="combine_post",
    )(starts, rank_t, y, x, norm_w.reshape(1, d), g)


def _select_kernel(aff_ref, tri_ref, rank_ref, st_ref, *, cap, tt):
    a = aff_ref[0]
    ne, n = a.shape
    bits = lax.bitcast_convert_type(a, jnp.int32)

    def search(_, c):
        lo, hi = c
        mid = lo + ((hi - lo) >> 1)
        cnt = jnp.sum((bits >= mid).astype(F32), axis=1, keepdims=True)
        ge = cnt >= cap
        return jnp.where(ge, mid, lo), jnp.where(ge, hi, mid)

    lo0 = jnp.zeros((ne, 1), jnp.int32)
    hi0 = jnp.full((ne, 1), 0x7F800000, jnp.int32)
    thr, _ = lax.fori_loop(0, 31, search, (lo0, hi0))
    gt = bits > thr
    eq = bits == thr
    need = cap - jnp.sum(gt.astype(F32), axis=1, keepdims=True)
    m = jnp.concatenate([gt, eq], axis=0).astype(BF16)
    tri = tri_ref[...]
    lane = lax.broadcasted_iota(jnp.int32, (1, LANES), 1)
    off = jnp.zeros((2 * ne, 1), F32)
    st = jnp.zeros((ne, LANES), jnp.int32)
    for j in range(n // LANES):
        cs = slice(j * LANES, (j + 1) * LANES)
        if (j * LANES) % tt == 0:
            off_sel = off[:ne] + jnp.minimum(off[ne:], need)
            st = jnp.where(lane == (j * LANES) // tt, off_sel.astype(jnp.int32), st)
        mj = m[:, cs]
        pj = _dot(mj, tri) + off
        pe = pj[ne:]
        sel = gt[:, cs] | (eq[:, cs] & (pe < need))
        rank_ref[0, :, cs] = jnp.where(sel, pj[:ne] + jnp.minimum(pe, need), -1.0).astype(jnp.int32)
        off = off + jnp.sum(mj.astype(F32), axis=1, keepdims=True)
    st_ref[0] = jnp.where(lane == n // tt, cap, st)


def route_select(aff_t, cap, tt):
    b, ne, n = aff_t.shape
    tri = jnp.asarray(np.triu(np.ones((LANES, LANES), np.float32), 1), BF16)
    return pl.pallas_call(
        functools.partial(_select_kernel, cap=cap, tt=tt),
        out_shape=(jax.ShapeDtypeStruct((b, ne, n), jnp.int32),
                   jax.ShapeDtypeStruct((b, ne, LANES), jnp.int32)),
        grid=(b,),
        in_specs=[pl.BlockSpec((1, ne, n), lambda bi: (bi, 0, 0)),
                  pl.BlockSpec((LANES, LANES), lambda bi: (0, 0))],
        out_specs=(pl.BlockSpec((1, ne, n), lambda bi: (bi, 0, 0)),
                   pl.BlockSpec((1, ne, LANES), lambda bi: (bi, 0, 0))),
        compiler_params=_cparams("parallel"),
        name="route_select",
    )(aff_t, tri)


def _compact_kernel(st_ref, rank_ref, aff_ref, out_ref, *, tt, ne, cap):
    b = pl.program_id(0)
    k = pl.program_id(1)
    nk = pl.num_programs(1)
    win = ROUTE_WIN

    @pl.when(k == 0)
    def _():
        out_ref[...] = jnp.zeros_like(out_ref)

    rk = rank_ref[0]
    a = aff_ref[0]
    g1 = a.astype(BF16).astype(F32)
    r1 = a - g1
    g2 = r1.astype(BF16).astype(F32)
    g3 = r1 - g2
    tok = k * tt + lax.broadcasted_iota(jnp.int32, (1, tt), 1)
    ids = jnp.concatenate([(tok >> 6).astype(F32), (tok & 63).astype(F32),
                           jnp.zeros((VAL_ROWS - 2, tt), F32)], axis=0)
    pad = jnp.zeros((LANES - VAL_ROWS - 3 * ne, tt), F32)
    payload = jnp.concatenate([ids, g1, g2, g3, pad], axis=0).astype(BF16)
    sub = lax.broadcasted_iota(jnp.int32, (win, 1), 0)

    def bounds(e):
        lo = st_ref[(b * (nk + 1) + k) * ne + e]
        hi = st_ref[(b * (nk + 1) + k + 1) * ne + e]
        return hi, (lo // 8) * 8

    def one_hot(e, aw):
        return ((rk[e:e + 1, :] - aw) == sub).astype(BF16)

    p_all = jnp.concatenate([one_hot(e, bounds(e)[1]) for e in range(ne)], axis=0)
    moved = _dot_nt(p_all, payload)
    for e in range(ne):
        a0 = pl.multiple_of(bounds(e)[1], 8)
        out_ref[0, e, pl.ds(a0, win), :] += moved[e * win:(e + 1) * win]

    for e in range(ne):
        hi, a0 = bounds(e)

        def window(w, carry, e=e, a0=a0):
            aw = pl.multiple_of(a0 + w * win, 8)
            out_ref[0, e, pl.ds(aw, win), :] += _dot_nt(one_hot(e, aw), payload)
            return carry

        lax.fori_loop(1, (hi - a0 + win - 1) // win, window, 0)


def route_compact(starts, rank_t, aff_t, cap, tt):
    b, ne, n = aff_t.shape
    grid_spec = pltpu.PrefetchScalarGridSpec(
        num_scalar_prefetch=1,
        grid=(b, n // tt),
        in_specs=[pl.BlockSpec((1, ne, tt), lambda bi, k, st: (bi, 0, k)),
                  pl.BlockSpec((1, ne, tt), lambda bi, k, st: (bi, 0, k))],
        out_specs=pl.BlockSpec((1, ne, cap + ROUTE_WIN, LANES), lambda bi, k, st: (bi, 0, 0, 0)),
    )
    return pl.pallas_call(
        functools.partial(_compact_kernel, tt=tt, ne=ne, cap=cap),
        out_shape=jax.ShapeDtypeStruct((b, ne, cap + ROUTE_WIN, LANES), F32),
        grid_spec=grid_spec,
        compiler_params=_cparams("parallel", "arbitrary"),
        name="route_compact",
    )(starts, rank_t, aff_t)


def _qk_head_of_lane():
    half = QK_W // 2
    return (np.arange(QK_W) % half) // (DK // 2)


def _in_proj_perm():
    half = DK // 2
    first = [h * DK + i for h in range(N_HEADS) for i in range(half)]
    second = [h * DK + half + i for h in range(N_HEADS) for i in range(half)]
    qperm = np.array(first + second)
    rest = np.arange(2 * QK_W, 2 * QK_W + 2 * V_W + POOL_W)
    return np.concatenate([qperm, QK_W + qperm, rest])


def _static_tables():
    qk_head = _qk_head_of_lane()
    v_head = np.arange(V_W) // DV
    hmask = (qk_head[None, :] == np.arange(N_HEADS)[:, None])
    bd = (qk_head[:, None] == v_head[None, :])
    avg = (v_head[:, None] == v_head[None, :]).astype(np.float32) / DV
    return {
        "hmask": jnp.asarray(hmask[:, None, :], BF16),
        "bd_f32": jnp.asarray(bd, F32),
        "bd_bf16": jnp.asarray(bd, BF16),
        "avg": jnp.asarray(avg, BF16),
    }


def _decay_tables(lg_f, lg_b):
    pos = jnp.arange(CHUNK, dtype=F32)
    diff = pos[:, None] - pos[None, :]
    low = diff >= 0
    up = diff < 0
    m_f = jnp.where(low, jnp.exp(lg_f[:, None, None] * jnp.where(low, diff, 0.0)), 0.0)
    m_b = jnp.where(up, jnp.exp(lg_b[:, None, None] * jnp.where(up, -diff, 0.0)), 0.0)
    m = m_f + m_b
    mpair = m.reshape(N_HEADS // 2, 2, CHUNK, CHUNK).transpose(0, 2, 1, 3).reshape(
        N_HEADS // 2, CHUNK, 2 * CHUNK)
    qk_head = _qk_head_of_lane()
    v_head = np.arange(V_W) // DV
    qdec_f = jnp.exp(lg_f[None, :] * (pos[:, None] + 1.0))[:, v_head]
    qdec_b = jnp.exp(lg_b[None, :] * (CHUNK - pos[:, None]))[:, v_head]
    kdec_f = jnp.exp(lg_f[None, :] * (CHUNK - 1.0 - pos[:, None]))[:, qk_head]
    kdec_b = jnp.exp(lg_b[None, :] * pos[:, None])[:, qk_head]
    cdec_f = jnp.exp(lg_f * CHUNK)[None, v_head]
    cdec_b = jnp.exp(lg_b * CHUNK)[None, v_head]
    return {"mpair": mpair, "qdec_f": qdec_f, "qdec_b": qdec_b, "kdec_f": kdec_f,
            "kdec_b": kdec_b, "cdec_f": cdec_f, "cdec_b": cdec_b}


def _rope_tables(n):
    t = jnp.arange(n)
    row = (t // GRID_W).astype(F32)
    col = (t % GRID_W).astype(F32)
    n_freq = DK // 4
    inv = ROPE_BASE ** (-jnp.arange(n_freq, dtype=F32) / n_freq)
    ang = jnp.concatenate([row[:, None] * inv, col[:, None] * inv], axis=-1)
    return jnp.tile(jnp.cos(ang), (1, N_HEADS)), jnp.tile(jnp.sin(ang), (1, N_HEADS))


def _moe_residual(x, h_bf, aff_t, wg, wu, wd, layer, norm_w, g, merged):
    b, n, d = h_bf.shape
    cap = EC_FACTOR * n // N_EXPERTS
    ne = N_EXPERTS
    tt = min(n, 256)
    nk = n // tt
    rank_t, st = route_select(aff_t, cap, tt)
    starts = st[:, :, :nk + 1].transpose(0, 2, 1).reshape(-1)
    rows = route_compact(starts, rank_t, aff_t, cap, tt)[:, :, :cap]
    idx = (rows[..., 0] * 64.0 + rows[..., 1]).astype(jnp.int32)
    pieces = rows[..., VAL_ROWS:VAL_ROWS + 3 * ne].reshape(b, ne, cap, 3, ne)
    own = pieces[:, jnp.arange(ne), :, :, jnp.arange(ne)]
    gate = ((own[..., 0] + own[..., 1]) + own[..., 2]).transpose(1, 0, 2)
    xs = jax.vmap(lambda hb, ib: hb[ib])(h_bf, idx)
    if merged:
        xs = xs.transpose(1, 0, 2, 3).reshape(1, ne, b * cap, d)
        gate = gate.transpose(1, 0, 2).reshape(1, ne, b * cap)
    y = expert_ffn(xs, gate[..., None], wg, wu, wd, layer)
    return combine_post(starts, rank_t, y, x, norm_w, g, tt, cap, merged)


def kernel(x, c, ctx, c_ctx, w_ada, b_ada, norm_pre_mix, norm_post_mix, norm_pre_ffn, norm_post_ffn, w_in, ret_decay_fwd, ret_decay_bwd, ret_gn, pool_w, pool_scale, w_out, w_router, w_gate, w_up, w_down):
    b, n, d = x.shape
    lc = ctx.shape[1]
    depth = w_ada.shape[0]
    rope = _rope_tables(n)
    static = _static_tables()
    perm = _in_proj_perm()

    cc = jnp.concatenate([c, c_ctx[None, :], jnp.zeros((7, d), F32)], axis=0)
    mods = ada_modulation(cc, w_ada, b_ada)
    wg, wu, wd = cast_expert_weights(jnp.swapaxes(w_gate, 2, 3), jnp.swapaxes(w_up, 2, 3), w_down)

    tm = 512
    for l in range(depth):
        last = l == depth - 1
        mx = mods[l, :b].reshape(b, 1, 6, d)
        sh1, sc1, g1, sh2, sc2, g2 = [mx[:, :, i] for i in range(6)]
        mc = jnp.broadcast_to(mods[l, b].reshape(1, 1, 6, d), (b, 1, 6, d))
        csh1, csc1, cg1, csh2, csc2, cg2 = [mc[:, :, i] for i in range(6)]
        lg_f = jax.nn.log_sigmoid(ret_decay_fwd[l].astype(F32))
        lg_b = jax.nn.log_sigmoid(ret_decay_bwd[l].astype(F32))
        tabs = dict(static, **_decay_tables(lg_f, lg_b))
        w_in_p = w_in[l][:, perm].astype(BF16)
        w_out_bf = w_out[l].astype(BF16)
        pool_w_bf = pool_w[l].astype(BF16)

        qk_c, v_c, gate_c, p_c = premix(ctx, norm_pre_mix[l], csh1, csc1, w_in_p, None, lc)
        zero = jnp.zeros((b, DK, V_W), F32)
        sf_c, sb_c, s_f, s_b = state_scan(qk_c, v_c, zero, zero, tabs)
        qk_x, v_x, gate_x, p_x = premix(x, norm_pre_mix[l], sh1, sc1, w_in_p, rope, tm)
        sf_x, sb_x, _, _ = state_scan(qk_x, v_x, s_f, s_b, tabs)
        pool_x = pool_mixer(p_x, pool_w_bf, pool_scale[l], GRID_W, min(n, 2048))
        x = retention_mixer(qk_x, v_x, gate_x, pool_x, x, sf_x, sb_x, tabs, ret_gn[l], w_out_bf,
                            norm_post_mix[l], g1, tm)
        if not last:
            pool_c = pool_mixer(p_c, pool_w_bf, pool_scale[l], lc, lc)
            ctx = retention_mixer(qk_c, v_c, gate_c, pool_c, ctx, sf_c, sb_c, tabs, ret_gn[l],
                                  w_out_bf, norm_post_mix[l], cg1, lc)

        wr_t = w_router[l].T
        h_x, aff_x = router(x, norm_pre_ffn[l], sh2, sc2, wr_t, tm)
        x = _moe_residual(x, h_x, aff_x, wg, wu, wd, l, norm_post_ffn[l], g2, False)
        if not last:
            h_c, aff_c = router(ctx, norm_pre_ffn[l], csh2, csc2, wr_t, lc)
            ctx = _moe_residual(ctx, h_c, aff_c, wg, wu, wd, l, norm_post_ffn[l], cg2, True)
    return x
```

```python
import functools

import jax
import jax.numpy as jnp
import numpy as np
from jax import lax
from jax.experimental import pallas as pl
from jax.experimental.pallas import tpu as pltpu
from jax.experimental.pallas import tpu_sc as plsc

F32 = jnp.float32
BF16 = jnp.bfloat16

D_MODEL = 1024
GRID_W = 64
N_HEADS = 8
DV = 64
DK = 32
QK_W = N_HEADS * DK
V_W = N_HEADS * DV
POOL_W = 512
POOL_WINDOWS = (2, 4, 8, 16)
POOL_GROUP_DIM = 128
CHUNK = 128
ROPE_BASE = 10000.0
N_EXPERTS = 16
EC_FACTOR = 2
EPS = 1e-6
MAX_HALF_WINDOW = max(POOL_WINDOWS) // 2

VMEM_LIMIT = 56 * 1024 * 1024


def _cparams(*sem):
    return pltpu.CompilerParams(dimension_semantics=sem, vmem_limit_bytes=VMEM_LIMIT)


def _dot(a, b):
    return jnp.dot(a, b, preferred_element_type=F32)


def _dot_nt(a, b, precision=None):
    return lax.dot_general(a, b, (((1,), (1,)), ((), ())), precision=precision,
                           preferred_element_type=F32)


def _ada_kernel(cc_ref, w_ref, b_ref, o_ref):
    s = cc_ref[...]
    s = s * jax.nn.sigmoid(s)
    o_ref[0] = _dot(s.astype(BF16), w_ref[0].astype(BF16)) + b_ref[0]


def ada_modulation(cc, w_ada, b_ada):
    depth, d, d6 = w_ada.shape
    rows = cc.shape[0]
    tn = 1536
    return pl.pallas_call(
        _ada_kernel,
        out_shape=jax.ShapeDtypeStruct((depth, rows, d6), F32),
        grid=(depth, d6 // tn),
        in_specs=[
            pl.BlockSpec((rows, d), lambda l, j: (0, 0)),
            pl.BlockSpec((1, d, tn), lambda l, j: (l, 0, j)),
            pl.BlockSpec((1, 1, tn), lambda l, j: (l, 0, j)),
        ],
        out_specs=pl.BlockSpec((1, rows, tn), lambda l, j: (l, 0, j)),
        compiler_params=_cparams("parallel", "parallel"),
        name="ada_modulation",
    )(cc, w_ada, b_ada.reshape(depth, 1, d6))


def _norm_mod(xf, g, sh, sc):
    y = xf * lax.rsqrt(jnp.mean(xf * xf, axis=-1, keepdims=True) + EPS)
    return (y * g) * (1.0 + sc) + sh


def _premix_kernel(x_ref, g_ref, sh_ref, sc_ref, w_ref, *rest, rope):
    if rope:
        cos_ref, sin_ref, qk_ref, v_ref, gate_ref, p_ref = rest
    else:
        qk_ref, v_ref, gate_ref, p_ref = rest
    hb = _norm_mod(x_ref[0], g_ref[...], sh_ref[0], sc_ref[0]).astype(BF16)
    zqk = _dot(hb, w_ref[:, 0:2 * QK_W])
    half = QK_W // 2
    q1, q2 = zqk[:, 0:half], zqk[:, half:2 * half]
    k1 = zqk[:, 2 * half:3 * half] * (DK ** -0.5)
    k2 = zqk[:, 3 * half:4 * half] * (DK ** -0.5)
    if rope:
        cos, sin = cos_ref[...], sin_ref[...]
        q1, q2 = q1 * cos - q2 * sin, q1 * sin + q2 * cos
        k1, k2 = k1 * cos - k2 * sin, k1 * sin + k2 * cos
    qk_ref[0] = jnp.concatenate([q1, q2, k1, k2], axis=1).astype(BF16)
    o = 2 * QK_W
    v_ref[0] = _dot(hb, w_ref[:, o:o + V_W]).astype(BF16)
    gate_ref[0] = _dot(hb, w_ref[:, o + V_W:o + 2 * V_W])
    p_ref[0] = _dot(hb, w_ref[:, o + 2 * V_W:o + 2 * V_W + POOL_W])


def premix(x, norm_w, shift, scale, w_in_p, rope, tm):
    b, n, d = x.shape
    in_w = w_in_p.shape[1]
    row = lambda bi, i: (bi, i, 0)
    in_specs = [
        pl.BlockSpec((1, tm, d), row),
        pl.BlockSpec((1, d), lambda bi, i: (0, 0)),
        pl.BlockSpec((1, 1, d), lambda bi, i: (bi, 0, 0)),
        pl.BlockSpec((1, 1, d), lambda bi, i: (bi, 0, 0)),
        pl.BlockSpec((d, in_w), lambda bi, i: (0, 0)),
    ]
    args = [x, norm_w.reshape(1, d), shift, scale, w_in_p]
    if rope is not None:
        half = QK_W // 2
        in_specs += [pl.BlockSpec((tm, half), lambda bi, i: (i, 0))] * 2
        args += list(rope)
    return pl.pallas_call(
        functools.partial(_premix_kernel, rope=rope is not None),
        out_shape=(
            jax.ShapeDtypeStruct((b, n, 2 * QK_W), BF16),
            jax.ShapeDtypeStruct((b, n, V_W), BF16),
            jax.ShapeDtypeStruct((b, n, V_W), F32),
            jax.ShapeDtypeStruct((b, n, POOL_W), F32),
        ),
        grid=(b, n // tm),
        in_specs=in_specs,
        out_specs=(
            pl.BlockSpec((1, tm, 2 * QK_W), row),
            pl.BlockSpec((1, tm, V_W), row),
            pl.BlockSpec((1, tm, V_W), row),
            pl.BlockSpec((1, tm, POOL_W), row),
        ),
        compiler_params=_cparams("parallel", "parallel"),
        name="premix",
    )(*args)


def _fold_state(s):
    g = 2 * N_HEADS
    top = s[0:g]
    bot = s[QK_W // 2:QK_W // 2 + g]
    for h in range(1, N_HEADS):
        top = top + s[h * g:(h + 1) * g]
        bot = bot + s[QK_W // 2 + h * g:QK_W // 2 + (h + 1) * g]
    return jnp.concatenate([top, bot], axis=0)


def _expand_state(c):
    g = 2 * N_HEADS
    return jnp.concatenate([c[0:g]] * N_HEADS + [c[g:2 * g]] * N_HEADS, axis=0)


def _state_kernel(kf_ref, vf_ref, kb_ref, vb_ref, s0f_ref, s0b_ref, kdf_ref, kdb_ref,
                  cdf_ref, cdb_ref, bd_ref, sf_out, sb_out, ff_out, fb_out, sf_acc, sb_acc, *, cps):
    c = pl.program_id(1)
    nc = pl.num_programs(1)
    bd = bd_ref[...]

    @pl.when(c == 0)
    def _():
        sf_acc[...] = _expand_state(s0f_ref[0]) * bd
        sb_acc[...] = _expand_state(s0b_ref[0]) * bd

    def update(s, k, v, kd_ref, cd_ref):
        kd = k.astype(F32) * kd_ref[...]
        f = _dot(kd.T.astype(BF16), v)
        return s * cd_ref[...] + f * bd

    sf = sf_acc[...]
    sb = sb_acc[...]
    for i in range(cps):
        j = cps - 1 - i
        fs = slice(i * CHUNK, (i + 1) * CHUNK)
        bs = slice(j * CHUNK, (j + 1) * CHUNK)
        sf_out[0, i] = _fold_state(sf).astype(BF16)
        sb_out[0, j] = _fold_state(sb).astype(BF16)
        sf = update(sf, kf_ref[0, fs, :], vf_ref[0, fs, :], kdf_ref, cdf_ref)
        sb = update(sb, kb_ref[0, bs, :], vb_ref[0, bs, :], kdb_ref, cdb_ref)
    sf_acc[...] = sf
    sb_acc[...] = sb

    @pl.when(c == nc - 1)
    def _():
        ff_out[0] = _fold_state(sf)
        fb_out[0] = _fold_state(sb)


def state_scan(qk, v, s0f, s0b, tabs):
    b, n, _ = qk.shape
    nc = n // CHUNK
    cps = min(nc, 4)
    ns = nc // cps
    rows = cps * CHUNK
    const = lambda shape: pl.BlockSpec(shape, lambda bi, c: (0,) * len(shape))
    return pl.pallas_call(
        functools.partial(_state_kernel, cps=cps),
        out_shape=(
            jax.ShapeDtypeStruct((b, nc, DK, V_W), BF16),
            jax.ShapeDtypeStruct((b, nc, DK, V_W), BF16),
            jax.ShapeDtypeStruct((b, DK, V_W), F32),
            jax.ShapeDtypeStruct((b, DK, V_W), F32),
        ),
        grid=(b, ns),
        in_specs=[
            pl.BlockSpec((1, rows, QK_W), lambda bi, c: (bi, c, 1)),
            pl.BlockSpec((1, rows, V_W), lambda bi, c: (bi, c, 0)),
            pl.BlockSpec((1, rows, QK_W), lambda bi, c: (bi, ns - 1 - c, 1)),
            pl.BlockSpec((1, rows, V_W), lambda bi, c: (bi, ns - 1 - c, 0)),
            pl.BlockSpec((1, DK, V_W), lambda bi, c: (bi, 0, 0)),
            pl.BlockSpec((1, DK, V_W), lambda bi, c: (bi, 0, 0)),
            const((CHUNK, QK_W)), const((CHUNK, QK_W)),
            const((1, V_W)), const((1, V_W)),
            const((QK_W, V_W)),
        ],
        out_specs=(
            pl.BlockSpec((1, cps, DK, V_W), lambda bi, c: (bi, c, 0, 0)),
            pl.BlockSpec((1, cps, DK, V_W), lambda bi, c: (bi, ns - 1 - c, 0, 0)),
            pl.BlockSpec((1, DK, V_W), lambda bi, c: (bi, 0, 0)),
            pl.BlockSpec((1, DK, V_W), lambda bi, c: (bi, 0, 0)),
        ),
        scratch_shapes=[pltpu.VMEM((QK_W, V_W), F32), pltpu.VMEM((QK_W, V_W), F32)],
        compiler_params=_cparams("parallel", "arbitrary"),
        name="state_scan",
    )(qk, v, qk, v, s0f, s0b, tabs["kdec_f"], tabs["kdec_b"], tabs["cdec_f"], tabs["cdec_b"],
      tabs["bd_f32"])


def _pool_kernel(*refs, gw, tile, halo, rows_total):
    if halo:
        prev_ref, cur_ref, next_ref, pw_ref, ps_ref, o_ref, buf = refs
    else:
        cur_ref, pw_ref, ps_ref, o_ref, buf = refs
    i = pl.program_id(1)
    last = pl.num_programs(1) - 1
    m = MAX_HALF_WINDOW
    span = tile + 2 * halo
    zeros_m = jnp.zeros((m, POOL_W), F32)
    buf[0:m] = zeros_m
    buf[m + span:2 * m + span] = zeros_m
    if halo:
        buf[m:m + halo] = jnp.where(i > 0, prev_ref[0], 0.0)
        buf[m + halo + tile:m + span] = jnp.where(i < last, next_ref[0], 0.0)
    buf[m + halo:m + halo + tile] = cur_ref[0]

    pos = lax.broadcasted_iota(jnp.int32, (span, 1), 0)
    col = pos % gw
    tpos = lax.broadcasted_iota(jnp.int32, (tile, 1), 0)
    tcol = tpos % gw
    trow = i * (tile // gw) + tpos // gw
    for gi, w in enumerate(POOL_WINDOWS):
        cs = slice(gi * POOL_GROUP_DIM, (gi + 1) * POOL_GROUP_DIM)
        hw = w // 2
        s = None
        for d in range(-hw, hw):
            valid = (col + d >= 0) & (col + d < gw)
            term = jnp.where(valid, buf[m + d:m + d + span, cs], 0.0)
            s = term if s is None else s + term
        if halo:
            acc = None
            for d in range(-hw, hw):
                start = halo + d * gw
                term = s[start:start + tile]
                acc = term if acc is None else acc + term
            cnt_r = jnp.minimum(trow + hw, rows_total) - jnp.maximum(trow - hw, 0)
        else:
            acc = s
            cnt_r = 1
        cnt_c = jnp.minimum(tcol + hw, gw) - jnp.maximum(tcol - hw, 0)
        cnt = (cnt_r * cnt_c).astype(F32)
        xg = buf[m + halo:m + halo + tile, cs]
        diff = (acc / cnt - xg).astype(BF16)
        y = _dot(diff, pw_ref[gi]) * ps_ref[:, cs]
        o_ref[0, :, cs] = y.astype(BF16)


def pool_mixer(p, pool_w_bf, pool_scale, gw, tile):
    b, n, pw = p.shape
    rows_total = n // gw
    halo = MAX_HALF_WINDOW * gw if rows_total > 1 else 0
    in_specs, args = [], []
    if halo:
        r = tile // halo
        nh = n // halo
        in_specs = [
            pl.BlockSpec((1, halo, pw), lambda bi, i: (bi, jnp.maximum(i * r - 1, 0), 0)),
            pl.BlockSpec((1, tile, pw), lambda bi, i: (bi, i, 0)),
            pl.BlockSpec((1, halo, pw), lambda bi, i: (bi, jnp.minimum((i + 1) * r, nh - 1), 0)),
        ]
        args = [p, p, p]
    else:
        in_specs = [pl.BlockSpec((1, tile, pw), lambda bi, i: (bi, i, 0))]
        args = [p]
    ng = len(POOL_WINDOWS)
    in_specs += [
        pl.BlockSpec((ng, POOL_GROUP_DIM, POOL_GROUP_DIM), lambda bi, i: (0, 0, 0)),
        pl.BlockSpec((1, pw), lambda bi, i: (0, 0)),
    ]
    args += [pool_w_bf, pool_scale.reshape(1, pw)]
    return pl.pallas_call(
        functools.partial(_pool_kernel, gw=gw, tile=tile, halo=halo, rows_total=rows_total),
        out_shape=jax.ShapeDtypeStruct((b, n, pw), BF16),
        grid=(b, n // tile),
        in_specs=in_specs,
        out_specs=pl.BlockSpec((1, tile, pw), lambda bi, i: (bi, i, 0)),
        scratch_shapes=[pltpu.VMEM((tile + 2 * halo + 2 * MAX_HALF_WINDOW, pw), F32)],
        compiler_params=_cparams("parallel", "parallel"),
        name="pool_mixer",
    )(*args)


def _ret_kernel(qk_ref, v_ref, gate_ref, py_ref, x_ref, sf_ref, sb_ref, mp_ref, qdf_ref, qdb_ref,
                hm_ref, bd_ref, avg_ref, gn_ref, wo_ref, nw_ref, g1_ref, o_ref, o_buf, *, nch):
    bd = bd_ref[...]
    avg = avg_ref[...]
    lane = lax.broadcasted_iota(jnp.int32, (1, 2 * DV), 1)
    lo_mask = (lane < DV).astype(BF16)
    hi_mask = (lane >= DV).astype(BF16)
    for ci in range(nch):
        rs = slice(ci * CHUNK, (ci + 1) * CHUNK)
        q = qk_ref[0, rs, 0:QK_W]
        k = qk_ref[0, rs, QK_W:2 * QK_W]
        vv = v_ref[0, rs, :]
        inner = []
        for j in range(N_HEADS // 2):
            kp = jnp.concatenate([k * hm_ref[2 * j], k * hm_ref[2 * j + 1]], axis=0)
            sc = _dot_nt(q, kp) * mp_ref[j]
            vpair = vv[:, 2 * DV * j:2 * DV * (j + 1)]
            vp = jnp.concatenate([vpair * lo_mask, vpair * hi_mask], axis=0)
            inner.append(_dot(sc.astype(BF16), vp))
        o = jnp.concatenate(inner, axis=1)
        sf_bd = _expand_state(sf_ref[0, ci]) * bd
        sb_bd = _expand_state(sb_ref[0, ci]) * bd
        o_buf[rs, :] = o + _dot(q, sf_bd) * qdf_ref[...] + _dot(q, sb_bd) * qdb_ref[...]

    def head_mean(a):
        hi = a.astype(BF16)
        lo = (a - hi.astype(F32)).astype(BF16)
        return _dot(hi, avg) + _dot(lo, avg)

    o = o_buf[...]
    dlt = o - head_mean(o)
    var = head_mean(dlt * dlt)
    yn = dlt * lax.rsqrt(var + EPS) * gn_ref[...]
    g = gate_ref[0]
    r = (g * jax.nn.sigmoid(g) * yn).astype(BF16)
    mix = _dot(r, wo_ref[0:V_W, :]) + _dot(py_ref[0], wo_ref[V_W:V_W + POOL_W, :])
    y = mix * lax.rsqrt(jnp.mean(mix * mix, axis=-1, keepdims=True) + EPS) * nw_ref[...]
    o_ref[0] = x_ref[0] + g1_ref[0] * y


def retention_mixer(qk, v, gate, pool_y, x, sf, sb, tabs, gn, w_out_bf, norm_w, g1, tm):
    b, n, d = x.shape
    nch = tm // CHUNK
    row = lambda bi, i: (bi, i, 0)
    const = lambda shape: pl.BlockSpec(shape, lambda bi, i: (0,) * len(shape))
    return pl.pallas_call(
        functools.partial(_ret_kernel, nch=nch),
        out_shape=jax.ShapeDtypeStruct((b, n, d), F32),
        grid=(b, n // tm),
        in_specs=[
            pl.BlockSpec((1, tm, 2 * QK_W), row),
            pl.BlockSpec((1, tm, V_W), row),
            pl.BlockSpec((1, tm, V_W), row),
            pl.BlockSpec((1, tm, POOL_W), row),
            pl.BlockSpec((1, tm, d), row),
            pl.BlockSpec((1, nch, DK, V_W), lambda bi, i: (bi, i, 0, 0)),
            pl.BlockSpec((1, nch, DK, V_W), lambda bi, i: (bi, i, 0, 0)),
            const((N_HEADS // 2, CHUNK, 2 * CHUNK)),
            const((CHUNK, V_W)), const((CHUNK, V_W)),
            const((N_HEADS, 1, QK_W)),
            const((QK_W, V_W)),
            const((V_W, V_W)),
            const((1, V_W)),
            const((V_W + POOL_W, d)),
            const((1, d)),
            pl.BlockSpec((1, 1, d), lambda bi, i: (bi, 0, 0)),
        ],
        out_specs=pl.BlockSpec((1, tm, d), row),
        scratch_shapes=[pltpu.VMEM((tm, V_W), F32)],
        compiler_params=_cparams("parallel", "parallel"),
        name="retention_mixer",
    )(qk, v, gate, pool_y, x, sf, sb, tabs["mpair"], tabs["qdec_f"], tabs["qdec_b"], tabs["hmask"],
      tabs["bd_bf16"], tabs["avg"], gn.reshape(1, V_W), w_out_bf, norm_w.reshape(1, d), g1)


def _packed_width(d, dtype):
    return d * jnp.dtype(dtype).itemsize // 4


def _pack_rows(h):
    bits = lax.bitcast_convert_type(h.astype(F32), jnp.uint32)
    if h.dtype.itemsize == 4:
        return bits
    half = h.shape[1] // 2
    return (bits[:, half:] & jnp.uint32(0xFFFF0000)) | (bits[:, :half] >> 16)


def _unpack_rows(w, dtype):
    if jnp.dtype(dtype).itemsize == 4:
        return lax.bitcast_convert_type(w, dtype)
    lo = lax.bitcast_convert_type(w << 16, F32)
    hi = lax.bitcast_convert_type(w & jnp.uint32(0xFFFF0000), F32)
    return jnp.concatenate([lo, hi], axis=1).astype(dtype)


def _router_kernel(x_ref, g_ref, sh_ref, sc_ref, wr_ref, h_ref, aff_ref):
    h = _norm_mod(x_ref[0], g_ref[...], sh_ref[0], sc_ref[0])
    h_hi = h.astype(BF16)
    h_ref[0] = _pack_rows(h_hi)
    ne = wr_ref.shape[0]
    h_lo = (h - h_hi.astype(F32)).astype(BF16)
    w = wr_ref[...]
    w_hi = w.astype(BF16)
    w_lo = (w - w_hi.astype(F32)).astype(BF16)
    both = _dot_nt(jnp.concatenate([w_hi, w_lo], axis=0), h_hi)
    logits = both[0:ne] + both[ne:2 * ne] + _dot_nt(w_hi, h_lo)
    mx = jnp.max(logits, axis=0, keepdims=True)
    e = jnp.exp(logits - mx)
    aff_ref[0] = e / jnp.sum(e, axis=0, keepdims=True)


def router(x, norm_w, shift, scale, w_router_t, tm):
    b, n, d = x.shape
    ne = w_router_t.shape[0]
    return pl.pallas_call(
        _router_kernel,
        out_shape=(jax.ShapeDtypeStruct((b, n, _packed_width(d, BF16)), jnp.uint32),
                   jax.ShapeDtypeStruct((b, ne, n), F32)),
        grid=(b, n // tm),
        in_specs=[
            pl.BlockSpec((1, tm, d), lambda bi, i: (bi, i, 0)),
            pl.BlockSpec((1, d), lambda bi, i: (0, 0)),
            pl.BlockSpec((1, 1, d), lambda bi, i: (bi, 0, 0)),
            pl.BlockSpec((1, 1, d), lambda bi, i: (bi, 0, 0)),
            pl.BlockSpec((ne, d), lambda bi, i: (0, 0)),
        ],
        out_specs=(
            pl.BlockSpec((1, tm, _packed_width(d, BF16)), lambda bi, i: (bi, i, 0)),
            pl.BlockSpec((1, ne, tm), lambda bi, i: (bi, 0, i)),
        ),
        compiler_params=_cparams("parallel", "parallel"),
        name="router",
    )(x, norm_w.reshape(1, d), shift, scale, w_router_t)


SC_CORES = 2
SC_SUBCORES = 16
GATHER_CHUNK = 64


def gather_rows(table, idx):
    rows, width = idx.shape[0], table.shape[1]
    workers = SC_CORES * SC_SUBCORES
    per_worker = rows // workers
    assert rows == per_worker * workers and per_worker % GATHER_CHUNK == 0
    mesh = plsc.VectorSubcoreMesh(core_axis_name="c", subcore_axis_name="s")

    def body(table_hbm, idx_hbm, out_hbm, idx_v, rows_v, sem):
        wid = lax.axis_index("s") * SC_CORES + lax.axis_index("c")
        base = wid * per_worker

        @pl.loop(0, per_worker // GATHER_CHUNK)
        def _(i):
            off = pl.multiple_of(base + i * GATHER_CHUNK, GATHER_CHUNK)
            pltpu.sync_copy(idx_hbm.at[pl.ds(off, GATHER_CHUNK)], idx_v)
            pltpu.async_copy(table_hbm.at[idx_v], rows_v, sem).wait()
            pltpu.sync_copy(rows_v, out_hbm.at[pl.ds(off, GATHER_CHUNK)])

    return pl.kernel(
        body,
        out_type=jax.ShapeDtypeStruct((rows, width), table.dtype),
        mesh=mesh,
        scratch_types=[
            pltpu.VMEM((GATHER_CHUNK,), jnp.int32),
            pltpu.VMEM((GATHER_CHUNK, width), table.dtype),
            pltpu.SemaphoreType.DMA,
        ],
        name="gather_rows",
    )(table, idx)


def _ffn_kernel(x_ref, gate_ref, wg_ref, wu_ref, wd_ref, o_ref, *, rt):
    rows = x_ref.shape[2]
    for r0 in range(0, rows, rt):
        rs = slice(r0, r0 + rt)
        xt = _unpack_rows(x_ref[0, 0, rs, :], wg_ref.dtype)
        a = _dot_nt(xt, wg_ref[0, 0])
        u = _dot_nt(xt, wu_ref[0, 0])
        hh = (a * jax.nn.sigmoid(a) * u).astype(BF16)
        o_ref[0, 0, rs, :] = (_dot(hh, wd_ref[0, 0]) * gate_ref[0, 0, rs, :]).astype(BF16)


def expert_ffn(xs, gate, wg_t, wu_t, wd, layer):
    g, ne, rows, dp = xs.shape
    f, d = wd.shape[2], wd.shape[3]
    rt = min(rows, 256)
    one = pl.Buffered(1)
    return pl.pallas_call(
        functools.partial(_ffn_kernel, rt=rt),
        out_shape=jax.ShapeDtypeStruct((g, ne, rows, d), BF16),
        grid=(ne, g),
        in_specs=[
            pl.BlockSpec((1, 1, rows, dp), lambda e, gi: (gi, e, 0, 0)),
            pl.BlockSpec((1, 1, rows, 1), lambda e, gi: (gi, e, 0, 0)),
            pl.BlockSpec((1, 1, f, d), lambda e, gi: (layer, e, 0, 0), pipeline_mode=one),
            pl.BlockSpec((1, 1, f, d), lambda e, gi: (layer, e, 0, 0), pipeline_mode=one),
            pl.BlockSpec((1, 1, f, d), lambda e, gi: (layer, e, 0, 0), pipeline_mode=one),
        ],
        out_specs=pl.BlockSpec((1, 1, rows, d), lambda e, gi: (gi, e, 0, 0)),
        compiler_params=_cparams("arbitrary", "arbitrary"),
        name="expert_ffn",
    )(xs, gate, wg_t, wu_t, wd)


WEIGHT_CAST_CHUNKS = 4


def _cast_kernel(g_ref, u_ref, d_ref, go_ref, uo_ref, do_ref):
    go_ref[...] = g_ref[...].astype(BF16)
    uo_ref[...] = u_ref[...].astype(BF16)
    do_ref[...] = d_ref[...].astype(BF16)


def cast_expert_weights(w_gate_t, w_up_t, w_down):
    depth, ne, f, d = w_down.shape
    spec = pl.BlockSpec((1, 1, f // WEIGHT_CAST_CHUNKS, d), lambda l, e, c: (l, e, c, 0))
    out = jax.ShapeDtypeStruct(w_down.shape, BF16)
    return pl.pallas_call(
        _cast_kernel,
        out_shape=(out, out, out),
        grid=(depth, ne, WEIGHT_CAST_CHUNKS),
        in_specs=[spec, spec, spec],
        out_specs=(spec, spec, spec),
        compiler_params=_cparams("parallel", "parallel", "parallel"),
        name="cast_expert_weights",
    )(w_gate_t, w_up_t, w_down)


ROUTE_WIN = 64
LANES = 128
VAL_ROWS = 8


def _combine_kernel(st_ref, rank_ref, y_hbm, x_ref, nw_ref, g_ref, o_ref, buf, xbuf, sem, xsem,
                    acc_ref, *, tt, ne, cap, merged):
    b = pl.program_id(0)
    k = pl.program_id(1)
    nk = pl.num_programs(1)
    step = b * nk + k
    slot = step % 2
    rows_total = y_hbm.shape[2]
    win = ROUTE_WIN
    g = 0 if merged else b
    base = b * cap if merged else 0
    lane = lax.broadcasted_iota(jnp.int32, (1, LANES), 1)

    def tile_bounds(bb, kk, e):
        off = bb * cap if merged else 0
        lo = off + st_ref[(bb * (nk + 1) + kk) * ne + e]
        hi = off + st_ref[(bb * (nk + 1) + kk + 1) * ne + e]
        return lo, hi, jnp.minimum((lo // 16) * 16, rows_total - win)

    def bounds(e):
        return tile_bounds(b, k, e)

    def win_copy(bb, e, a0, sl):
        src = y_hbm.at[0 if merged else bb, e, pl.ds(pl.multiple_of(a0, 16), win), :]
        return pltpu.make_async_copy(src, buf.at[sl, pl.ds(e * win, win), :], sem.at[sl, e])

    def start_tile(bb, kk, sl):
        for e in range(ne):
            win_copy(bb, e, tile_bounds(bb, kk, e)[2], sl).start()

    @pl.when(step == 0)
    def _():
        start_tile(b, k, slot)

    nxt = step + 1

    @pl.when(nxt < pl.num_programs(0) * nk)
    def _():
        start_tile(nxt // nk, nxt % nk, 1 - slot)

    rkf = rank_ref[0].astype(F32)
    rk_t = jnp.concatenate([rkf, jnp.full((LANES - ne, tt), -1.0, F32)], axis=0).T
    rk_t = rk_t.astype(jnp.int32)

    def target(e, origin):
        col = rk_t[:, e:e + 1]
        return jnp.where(col >= 0, col + (base - origin), -1)

    pieces = []
    for j in range(ne // 2):
        t_even = target(2 * j, bounds(2 * j)[2])
        t_odd = target(2 * j + 1, bounds(2 * j + 1)[2])
        t_odd = jnp.where((t_odd >= 0) & (t_odd < win), t_odd + win, -1)
        pieces.append((jnp.where(lane < win, t_even, t_odd) == lane).astype(BF16))
    p = jnp.concatenate(pieces, axis=1)
    for e in range(ne):
        win_copy(b, e, bounds(e)[2], slot).wait()
    acc_ref[...] = _dot(p, buf[slot])

    for e in range(ne):
        lo, hi, a0 = bounds(e)

        def extra(w, carry, e=e, a0=a0):
            start = a0 + w * win
            aw = pl.multiple_of(jnp.minimum(start, rows_total - win), 16)
            cp = pltpu.make_async_copy(y_hbm.at[g, e, pl.ds(aw, win), :], xbuf, xsem)
            cp.start()
            cp.wait()
            col = rk_t[:, e:e + 1]
            ok = (col >= 0) & (col + base >= start)
            px = (jnp.where(ok, col + (base - aw), -1) == lane[:, 0:win]).astype(BF16)
            acc_ref[...] += _dot(px, xbuf[...])
            return carry

        lax.fori_loop(1, (hi - a0 + win - 1) // win, extra, 0)

    y = acc_ref[...]
    yn = y * lax.rsqrt(jnp.mean(y * y, axis=-1, keepdims=True) + EPS) * nw_ref[...]
    o_ref[0] = x_ref[0] + g_ref[0] * yn


def combine_post(starts, rank_t, y, x, norm_w, g, tt, cap, merged):
    b, n, d = x.shape
    _, ne, rows, _ = y.shape
    grid_spec = pltpu.PrefetchScalarGridSpec(
        num_scalar_prefetch=1,
        grid=(b, n // tt),
        in_specs=[
            pl.BlockSpec((1, ne, tt), lambda bi, k, st: (bi, 0, k)),
            pl.BlockSpec(memory_space=pl.ANY),
            pl.BlockSpec((1, tt, d), lambda bi, k, st: (bi, k, 0)),
            pl.BlockSpec((1, d), lambda bi, k, st: (0, 0)),
            pl.BlockSpec((1, 1, d), lambda bi, k, st: (bi, 0, 0)),
        ],
        out_specs=pl.BlockSpec((1, tt, d), lambda bi, k, st: (bi, k, 0)),
        scratch_shapes=[
            pltpu.VMEM((2, ne * ROUTE_WIN, d), BF16),
            pltpu.VMEM((ROUTE_WIN, d), BF16),
            pltpu.SemaphoreType.DMA((2, ne)),
            pltpu.SemaphoreType.DMA(()),
            pltpu.VMEM((tt, d), F32),
        ],
    )
    return pl.pallas_call(
        functools.partial(_combine_kernel, tt=tt, ne=ne, cap=cap, merged=merged),
        out_shape=jax.ShapeDtypeStruct((b, n, d), F32),
        grid_spec=grid_spec,
        compiler_params=_cparams("arbitrary", "arbitrary"),
        name="combine_post",
    )(starts, rank_t, y, x, norm_w.reshape(1, d), g)


def _select_kernel(aff_ref, tri_ref, rank_ref, st_ref, *, cap, tt):
    a = aff_ref[0]
    ne, n = a.shape
    bits = lax.bitcast_convert_type(a, jnp.int32)

    def search(_, c):
        lo, hi = c
        mid = lo + ((hi - lo) >> 1)
        cnt = jnp.sum((bits >= mid).astype(F32), axis=1, keepdims=True)
        ge = cnt >= cap
        return jnp.where(ge, mid, lo), jnp.where(ge, hi, mid)

    lo0 = jnp.zeros((ne, 1), jnp.int32)
    hi0 = jnp.full((ne, 1), 0x7F800000, jnp.int32)
    thr, _ = lax.fori_loop(0, 31, search, (lo0, hi0))
    gt = bits > thr
    eq = bits == thr
    need = cap - jnp.sum(gt.astype(F32), axis=1, keepdims=True)
    m = jnp.concatenate([gt, eq], axis=0).astype(BF16)
    tri = tri_ref[...]
    lane = lax.broadcasted_iota(jnp.int32, (1, LANES), 1)
    off = jnp.zeros((2 * ne, 1), F32)
    st = jnp.zeros((ne, LANES), jnp.int32)
    for j in range(n // LANES):
        cs = slice(j * LANES, (j + 1) * LANES)
        if (j * LANES) % tt == 0:
            off_sel = off[:ne] + jnp.minimum(off[ne:], need)
            st = jnp.where(lane == (j * LANES) // tt, off_sel.astype(jnp.int32), st)
        mj = m[:, cs]
        pj = _dot(mj, tri) + off
        pe = pj[ne:]
        sel = gt[:, cs] | (eq[:, cs] & (pe < need))
        rank_ref[0, :, cs] = jnp.where(sel, pj[:ne] + jnp.minimum(pe, need), -1.0).astype(jnp.int32)
        off = off + jnp.sum(mj.astype(F32), axis=1, keepdims=True)
    st_ref[0] = jnp.where(lane == n // tt, cap, st)


def route_select(aff_t, cap, tt):
    b, ne, n = aff_t.shape
    tri = jnp.asarray(np.triu(np.ones((LANES, LANES), np.float32), 1), BF16)
    return pl.pallas_call(
        functools.partial(_select_kernel, cap=cap, tt=tt),
        out_shape=(jax.ShapeDtypeStruct((b, ne, n), jnp.int32),
                   jax.ShapeDtypeStruct((b, ne, LANES), jnp.int32)),
        grid=(b,),
        in_specs=[pl.BlockSpec((1, ne, n), lambda bi: (bi, 0, 0)),
                  pl.BlockSpec((LANES, LANES), lambda bi: (0, 0))],
        out_specs=(pl.BlockSpec((1, ne, n), lambda bi: (bi, 0, 0)),
                   pl.BlockSpec((1, ne, LANES), lambda bi: (bi, 0, 0))),
        compiler_params=_cparams("parallel"),
        name="route_select",
    )(aff_t, tri)


def _compact_kernel(st_ref, rank_ref, aff_ref, out_ref, *, tt, ne, cap):
    b = pl.program_id(0)
    k = pl.program_id(1)
    nk = pl.num_programs(1)
    win = ROUTE_WIN

    @pl.when(k == 0)
    def _():
        out_ref[...] = jnp.zeros_like(out_ref)

    rk = rank_ref[0]
    a = aff_ref[0]
    g1 = a.astype(BF16).astype(F32)
    r1 = a - g1
    g2 = r1.astype(BF16).astype(F32)
    g3 = r1 - g2
    tok = k * tt + lax.broadcasted_iota(jnp.int32, (1, tt), 1)
    ids = jnp.concatenate([(tok >> 6).astype(F32), (tok & 63).astype(F32),
                           jnp.zeros((VAL_ROWS - 2, tt), F32)], axis=0)
    pad = jnp.zeros((LANES - VAL_ROWS - 3 * ne, tt), F32)
    payload = jnp.concatenate([ids, g1, g2, g3, pad], axis=0).astype(BF16)
    sub = lax.broadcasted_iota(jnp.int32, (win, 1), 0)

    def bounds(e):
        lo = st_ref[(b * (nk + 1) + k) * ne + e]
        hi = st_ref[(b * (nk + 1) + k + 1) * ne + e]
        return hi, (lo // 8) * 8

    def one_hot(e, aw):
        return ((rk[e:e + 1, :] - aw) == sub).astype(BF16)

    p_all = jnp.concatenate([one_hot(e, bounds(e)[1]) for e in range(ne)], axis=0)
    moved = _dot_nt(p_all, payload)
    for e in range(ne):
        a0 = pl.multiple_of(bounds(e)[1], 8)
        out_ref[0, e, pl.ds(a0, win), :] += moved[e * win:(e + 1) * win]

    for e in range(ne):
        hi, a0 = bounds(e)

        def window(w, carry, e=e, a0=a0):
            aw = pl.multiple_of(a0 + w * win, 8)
            out_ref[0, e, pl.ds(aw, win), :] += _dot_nt(one_hot(e, aw), payload)
            return carry

        lax.fori_loop(1, (hi - a0 + win - 1) // win, window, 0)


def route_compact(starts, rank_t, aff_t, cap, tt):
    b, ne, n = aff_t.shape
    grid_spec = pltpu.PrefetchScalarGridSpec(
        num_scalar_prefetch=1,
        grid=(b, n // tt),
        in_specs=[pl.BlockSpec((1, ne, tt), lambda bi, k, st: (bi, 0, k)),
                  pl.BlockSpec((1, ne, tt), lambda bi, k, st: (bi, 0, k))],
        out_specs=pl.BlockSpec((1, ne, cap + ROUTE_WIN, LANES), lambda bi, k, st: (bi, 0, 0, 0)),
    )
    return pl.pallas_call(
        functools.partial(_compact_kernel, tt=tt, ne=ne, cap=cap),
        out_shape=jax.ShapeDtypeStruct((b, ne, cap + ROUTE_WIN, LANES), F32),
        grid_spec=grid_spec,
        compiler_params=_cparams("parallel", "arbitrary"),
        name="route_compact",
    )(starts, rank_t, aff_t)


def _qk_head_of_lane():
    half = QK_W // 2
    return (np.arange(QK_W) % half) // (DK // 2)


def _in_proj_perm():
    half = DK // 2
    first = [h * DK + i for h in range(N_HEADS) for i in range(half)]
    second = [h * DK + half + i for h in range(N_HEADS) for i in range(half)]
    qperm = np.array(first + second)
    rest = np.arange(2 * QK_W, 2 * QK_W + 2 * V_W + POOL_W)
    return np.concatenate([qperm, QK_W + qperm, rest])


def _static_tables():
    qk_head = _qk_head_of_lane()
    v_head = np.arange(V_W) // DV
    hmask = (qk_head[None, :] == np.arange(N_HEADS)[:, None])
    bd = (qk_head[:, None] == v_head[None, :])
    avg = (v_head[:, None] == v_head[None, :]).astype(np.float32) / DV
    return {
        "hmask": jnp.asarray(hmask[:, None, :], BF16),
        "bd_f32": jnp.asarray(bd, F32),
        "bd_bf16": jnp.asarray(bd, BF16),
        "avg": jnp.asarray(avg, BF16),
    }


def _decay_tables(lg_f, lg_b):
    pos = jnp.arange(CHUNK, dtype=F32)
    diff = pos[:, None] - pos[None, :]
    low = diff >= 0
    up = diff < 0
    m_f = jnp.where(low, jnp.exp(lg_f[:, None, None] * jnp.where(low, diff, 0.0)), 0.0)
    m_b = jnp.where(up, jnp.exp(lg_b[:, None, None] * jnp.where(up, -diff, 0.0)), 0.0)
    m = m_f + m_b
    mpair = m.reshape(N_HEADS // 2, 2, CHUNK, CHUNK).transpose(0, 2, 1, 3).reshape(
        N_HEADS // 2, CHUNK, 2 * CHUNK)
    qk_head = _qk_head_of_lane()
    v_head = np.arange(V_W) // DV
    qdec_f = jnp.exp(lg_f[None, :] * (pos[:, None] + 1.0))[:, v_head]
    qdec_b = jnp.exp(lg_b[None, :] * (CHUNK - pos[:, None]))[:, v_head]
    kdec_f = jnp.exp(lg_f[None, :] * (CHUNK - 1.0 - pos[:, None]))[:, qk_head]
    kdec_b = jnp.exp(lg_b[None, :] * pos[:, None])[:, qk_head]
    cdec_f = jnp.exp(lg_f * CHUNK)[None, v_head]
    cdec_b = jnp.exp(lg_b * CHUNK)[None, v_head]
    return {"mpair": mpair, "qdec_f": qdec_f, "qdec_b": qdec_b, "kdec_f": kdec_f,
            "kdec_b": kdec_b, "cdec_f": cdec_f, "cdec_b": cdec_b}


def _rope_tables(n):
    t = jnp.arange(n)
    row = (t // GRID_W).astype(F32)
    col = (t % GRID_W).astype(F32)
    n_freq = DK // 4
    inv = ROPE_BASE ** (-jnp.arange(n_freq, dtype=F32) / n_freq)
    ang = jnp.concatenate([row[:, None] * inv, col[:, None] * inv], axis=-1)
    return jnp.tile(jnp.cos(ang), (1, N_HEADS)), jnp.tile(jnp.sin(ang), (1, N_HEADS))


def _moe_residual(x, h_bf, aff_t, wg, wu, wd, layer, norm_w, g, merged):
    b, n, d = h_bf.shape
    cap = EC_FACTOR * n // N_EXPERTS
    ne = N_EXPERTS
    tt = min(n, 256)
    nk = n // tt
    rank_t, st = route_select(aff_t, cap, tt)
    starts = st[:, :, :nk + 1].transpose(0, 2, 1).reshape(-1)
    rows = route_compact(starts, rank_t, aff_t, cap, tt)[:, :, :cap]
    idx = (rows[..., 0] * 64.0 + rows[..., 1]).astype(jnp.int32)
    pieces = rows[..., VAL_ROWS:VAL_ROWS + 3 * ne].reshape(b, ne, cap, 3, ne)
    own = pieces[:, jnp.arange(ne), :, :, jnp.arange(ne)]
    gate = ((own[..., 0] + own[..., 1]) + own[..., 2]).transpose(1, 0, 2)
    flat = idx + (jnp.arange(b, dtype=jnp.int32) * n)[:, None, None]
    if merged:
        flat = flat.transpose(1, 0, 2)
        gate = gate.transpose(1, 0, 2).reshape(1, ne, b * cap)
    xs = gather_rows(h_bf.reshape(b * n, d), flat.reshape(-1))
    xs = xs.reshape(gate.shape + (d,))
    y = expert_ffn(xs, gate[..., None], wg, wu, wd, layer)
    return combine_post(starts, rank_t, y, x, norm_w, g, tt, cap, merged)


def kernel(x, c, ctx, c_ctx, w_ada, b_ada, norm_pre_mix, norm_post_mix, norm_pre_ffn, norm_post_ffn, w_in, ret_decay_fwd, ret_decay_bwd, ret_gn, pool_w, pool_scale, w_out, w_router, w_gate, w_up, w_down):
    b, n, d = x.shape
    lc = ctx.shape[1]
    depth = w_ada.shape[0]
    rope = _rope_tables(n)
    static = _static_tables()
    perm = _in_proj_perm()

    cc = jnp.concatenate([c, c_ctx[None, :], jnp.zeros((7, d), F32)], axis=0)
    mods = ada_modulation(cc, w_ada, b_ada)
    wg, wu, wd = cast_expert_weights(jnp.swapaxes(w_gate, 2, 3), jnp.swapaxes(w_up, 2, 3), w_down)

    tm = 512
    for l in range(depth):
        last = l == depth - 1
        mx = mods[l, :b].reshape(b, 1, 6, d)
        sh1, sc1, g1, sh2, sc2, g2 = [mx[:, :, i] for i in range(6)]
        mc = jnp.broadcast_to(mods[l, b].reshape(1, 1, 6, d), (b, 1, 6, d))
        csh1, csc1, cg1, csh2, csc2, cg2 = [mc[:, :, i] for i in range(6)]
        lg_f = jax.nn.log_sigmoid(ret_decay_fwd[l].astype(F32))
        lg_b = jax.nn.log_sigmoid(ret_decay_bwd[l].astype(F32))
        tabs = dict(static, **_decay_tables(lg_f, lg_b))
        w_in_p = w_in[l][:, perm].astype(BF16)
        w_out_bf = w_out[l].astype(BF16)
        pool_w_bf = pool_w[l].astype(BF16)

        qk_c, v_c, gate_c, p_c = premix(ctx, norm_pre_mix[l], csh1, csc1, w_in_p, None, lc)
        zero = jnp.zeros((b, DK, V_W), F32)
        sf_c, sb_c, s_f, s_b = state_scan(qk_c, v_c, zero, zero, tabs)
        qk_x, v_x, gate_x, p_x = premix(x, norm_pre_mix[l], sh1, sc1, w_in_p, rope, tm)
        sf_x, sb_x, _, _ = state_scan(qk_x, v_x, s_f, s_b, tabs)
        pool_x = pool_mixer(p_x, pool_w_bf, pool_scale[l], GRID_W, min(n, 2048))
        x = retention_mixer(qk_x, v_x, gate_x, pool_x, x, sf_x, sb_x, tabs, ret_gn[l], w_out_bf,
                            norm_post_mix[l], g1, tm)
        if not last:
            pool_c = pool_mixer(p_c, pool_w_bf, pool_scale[l], lc, lc)
            ctx = retention_mixer(qk_c, v_c, gate_c, pool_c, ctx, sf_c, sb_c, tabs, ret_gn[l],
                                  w_out_bf, norm_post_mix[l], cg1, lc)

        wr_t = w_router[l].T
        h_x, aff_x = router(x, norm_pre_ffn[l], sh2, sc2, wr_t, tm)
        x = _moe_residual(x, h_x, aff_x, wg, wu, wd, l, norm_post_ffn[l], g2, False)
        if not last:
            h_c, aff_c = router(ctx, norm_pre_ffn[l], csh2, csc2, wr_t, lc)
            ctx = _moe_residual(ctx, h_c, aff_c, wg, wu, wd, l, norm_post_ffn[l], cg2, True)
    return x
```

```python
import functools

import jax
import jax.numpy as jnp
import numpy as np
from jax import lax
from jax.experimental import pallas as pl
from jax.experimental.pallas import tpu as pltpu
from jax.experimental.pallas import tpu_sc as plsc

F32 = jnp.float32
BF16 = jnp.bfloat16

D_MODEL = 1024
GRID_W = 64
N_HEADS = 8
DV = 64
DK = 32
QK_W = N_HEADS * DK
V_W = N_HEADS * DV
POOL_W = 512
POOL_WINDOWS = (2, 4, 8, 16)
POOL_GROUP_DIM = 128
CHUNK = 128
ROPE_BASE = 10000.0
N_EXPERTS = 16
EC_FACTOR = 2
EPS = 1e-6
MAX_HALF_WINDOW = max(POOL_WINDOWS) // 2

VMEM_LIMIT = 56 * 1024 * 1024


def _cparams(*sem):
    return pltpu.CompilerParams(dimension_semantics=sem, vmem_limit_bytes=VMEM_LIMIT)


def _dot(a, b):
    return jnp.dot(a, b, preferred_element_type=F32)


def _dot_nt(a, b, precision=None):
    return lax.dot_general(a, b, (((1,), (1,)), ((), ())), precision=precision,
                           preferred_element_type=F32)


def _ada_kernel(cc_ref, w_ref, b_ref, o_ref):
    s = cc_ref[...]
    s = s * jax.nn.sigmoid(s)
    o_ref[0] = _dot(s.astype(BF16), w_ref[0].astype(BF16)) + b_ref[0]


def ada_modulation(cc, w_ada, b_ada):
    depth, d, d6 = w_ada.shape
    rows = cc.shape[0]
    tn = 1536
    return pl.pallas_call(
        _ada_kernel,
        out_shape=jax.ShapeDtypeStruct((depth, rows, d6), F32),
        grid=(depth, d6 // tn),
        in_specs=[
            pl.BlockSpec((rows, d), lambda l, j: (0, 0)),
            pl.BlockSpec((1, d, tn), lambda l, j: (l, 0, j)),
            pl.BlockSpec((1, 1, tn), lambda l, j: (l, 0, j)),
        ],
        out_specs=pl.BlockSpec((1, rows, tn), lambda l, j: (l, 0, j)),
        compiler_params=_cparams("parallel", "parallel"),
        name="ada_modulation",
    )(cc, w_ada, b_ada.reshape(depth, 1, d6))


def _norm_mod(xf, g, sh, sc):
    y = xf * lax.rsqrt(jnp.mean(xf * xf, axis=-1, keepdims=True) + EPS)
    return (y * g) * (1.0 + sc) + sh


def _premix_kernel(x_ref, g_ref, sh_ref, sc_ref, w_ref, *rest, rope):
    if rope:
        cos_ref, sin_ref, qk_ref, v_ref, gate_ref, p_ref = rest
    else:
        qk_ref, v_ref, gate_ref, p_ref = rest
    hb = _norm_mod(x_ref[0], g_ref[...], sh_ref[0], sc_ref[0]).astype(BF16)
    zqk = _dot(hb, w_ref[:, 0:2 * QK_W])
    half = QK_W // 2
    q1, q2 = zqk[:, 0:half], zqk[:, half:2 * half]
    k1 = zqk[:, 2 * half:3 * half] * (DK ** -0.5)
    k2 = zqk[:, 3 * half:4 * half] * (DK ** -0.5)
    if rope:
        cos, sin = cos_ref[...], sin_ref[...]
        q1, q2 = q1 * cos - q2 * sin, q1 * sin + q2 * cos
        k1, k2 = k1 * cos - k2 * sin, k1 * sin + k2 * cos
    qk_ref[0] = jnp.concatenate([q1, q2, k1, k2], axis=1).astype(BF16)
    o = 2 * QK_W
    v_ref[0] = _dot(hb, w_ref[:, o:o + V_W]).astype(BF16)
    gate_ref[0] = _dot(hb, w_ref[:, o + V_W:o + 2 * V_W])
    p_ref[0] = _dot(hb, w_ref[:, o + 2 * V_W:o + 2 * V_W + POOL_W])


def premix(x, norm_w, shift, scale, w_in_p, rope, tm):
    b, n, d = x.shape
    in_w = w_in_p.shape[1]
    row = lambda bi, i: (bi, i, 0)
    in_specs = [
        pl.BlockSpec((1, tm, d), row),
        pl.BlockSpec((1, d), lambda bi, i: (0, 0)),
        pl.BlockSpec((1, 1, d), lambda bi, i: (bi, 0, 0)),
        pl.BlockSpec((1, 1, d), lambda bi, i: (bi, 0, 0)),
        pl.BlockSpec((d, in_w), lambda bi, i: (0, 0)),
    ]
    args = [x, norm_w.reshape(1, d), shift, scale, w_in_p]
    if rope is not None:
        half = QK_W // 2
        in_specs += [pl.BlockSpec((tm, half), lambda bi, i: (i, 0))] * 2
        args += list(rope)
    return pl.pallas_call(
        functools.partial(_premix_kernel, rope=rope is not None),
        out_shape=(
            jax.ShapeDtypeStruct((b, n, 2 * QK_W), BF16),
            jax.ShapeDtypeStruct((b, n, V_W), BF16),
            jax.ShapeDtypeStruct((b, n, V_W), F32),
            jax.ShapeDtypeStruct((b, n, POOL_W), F32),
        ),
        grid=(b, n // tm),
        in_specs=in_specs,
        out_specs=(
            pl.BlockSpec((1, tm, 2 * QK_W), row),
            pl.BlockSpec((1, tm, V_W), row),
            pl.BlockSpec((1, tm, V_W), row),
            pl.BlockSpec((1, tm, POOL_W), row),
        ),
        compiler_params=_cparams("parallel", "parallel"),
        name="premix",
    )(*args)


def _fold_state(s):
    g = 2 * N_HEADS
    top = s[0:g]
    bot = s[QK_W // 2:QK_W // 2 + g]
    for h in range(1, N_HEADS):
        top = top + s[h * g:(h + 1) * g]
        bot = bot + s[QK_W // 2 + h * g:QK_W // 2 + (h + 1) * g]
    return jnp.concatenate([top, bot], axis=0)


def _expand_state(c):
    g = 2 * N_HEADS
    return jnp.concatenate([c[0:g]] * N_HEADS + [c[g:2 * g]] * N_HEADS, axis=0)


def _state_kernel(kf_ref, vf_ref, kb_ref, vb_ref, s0f_ref, s0b_ref, kdf_ref, kdb_ref,
                  cdf_ref, cdb_ref, bd_ref, sf_out, sb_out, ff_out, fb_out, sf_acc, sb_acc, *, cps):
    c = pl.program_id(1)
    nc = pl.num_programs(1)
    bd = bd_ref[...]

    @pl.when(c == 0)
    def _():
        sf_acc[...] = _expand_state(s0f_ref[0]) * bd
        sb_acc[...] = _expand_state(s0b_ref[0]) * bd

    def update(s, k, v, kd_ref, cd_ref):
        kd = k.astype(F32) * kd_ref[...]
        f = _dot(kd.T.astype(BF16), v)
        return s * cd_ref[...] + f * bd

    sf = sf_acc[...]
    sb = sb_acc[...]
    for i in range(cps):
        j = cps - 1 - i
        fs = slice(i * CHUNK, (i + 1) * CHUNK)
        bs = slice(j * CHUNK, (j + 1) * CHUNK)
        sf_out[0, i] = _fold_state(sf).astype(BF16)
        sb_out[0, j] = _fold_state(sb).astype(BF16)
        sf = update(sf, kf_ref[0, fs, :], vf_ref[0, fs, :], kdf_ref, cdf_ref)
        sb = update(sb, kb_ref[0, bs, :], vb_ref[0, bs, :], kdb_ref, cdb_ref)
    sf_acc[...] = sf
    sb_acc[...] = sb

    @pl.when(c == nc - 1)
    def _():
        ff_out[0] = _fold_state(sf)
        fb_out[0] = _fold_state(sb)


def state_scan(qk, v, s0f, s0b, tabs):
    b, n, _ = qk.shape
    nc = n // CHUNK
    cps = min(nc, 4)
    ns = nc // cps
    rows = cps * CHUNK
    const = lambda shape: pl.BlockSpec(shape, lambda bi, c: (0,) * len(shape))
    return pl.pallas_call(
        functools.partial(_state_kernel, cps=cps),
        out_shape=(
            jax.ShapeDtypeStruct((b, nc, DK, V_W), BF16),
            jax.ShapeDtypeStruct((b, nc, DK, V_W), BF16),
            jax.ShapeDtypeStruct((b, DK, V_W), F32),
            jax.ShapeDtypeStruct((b, DK, V_W), F32),
        ),
        grid=(b, ns),
        in_specs=[
            pl.BlockSpec((1, rows, QK_W), lambda bi, c: (bi, c, 1)),
            pl.BlockSpec((1, rows, V_W), lambda bi, c: (bi, c, 0)),
            pl.BlockSpec((1, rows, QK_W), lambda bi, c: (bi, ns - 1 - c, 1)),
            pl.BlockSpec((1, rows, V_W), lambda bi, c: (bi, ns - 1 - c, 0)),
            pl.BlockSpec((1, DK, V_W), lambda bi, c: (bi, 0, 0)),
            pl.BlockSpec((1, DK, V_W), lambda bi, c: (bi, 0, 0)),
            const((CHUNK, QK_W)), const((CHUNK, QK_W)),
            const((1, V_W)), const((1, V_W)),
            const((QK_W, V_W)),
        ],
        out_specs=(
            pl.BlockSpec((1, cps, DK, V_W), lambda bi, c: (bi, c, 0, 0)),
            pl.BlockSpec((1, cps, DK, V_W), lambda bi, c: (bi, ns - 1 - c, 0, 0)),
            pl.BlockSpec((1, DK, V_W), lambda bi, c: (bi, 0, 0)),
            pl.BlockSpec((1, DK, V_W), lambda bi, c: (bi, 0, 0)),
        ),
        scratch_shapes=[pltpu.VMEM((QK_W, V_W), F32), pltpu.VMEM((QK_W, V_W), F32)],
        compiler_params=_cparams("parallel", "arbitrary"),
        name="state_scan",
    )(qk, v, qk, v, s0f, s0b, tabs["kdec_f"], tabs["kdec_b"], tabs["cdec_f"], tabs["cdec_b"],
      tabs["bd_f32"])


def _pool_kernel(*refs, gw, tile, halo, rows_total):
    if halo:
        prev_ref, cur_ref, next_ref, pw_ref, ps_ref, o_ref, buf = refs
    else:
        cur_ref, pw_ref, ps_ref, o_ref, buf = refs
    i = pl.program_id(1)
    last = pl.num_programs(1) - 1
    m = MAX_HALF_WINDOW
    span = tile + 2 * halo
    zeros_m = jnp.zeros((m, POOL_W), F32)
    buf[0:m] = zeros_m
    buf[m + span:2 * m + span] = zeros_m
    if halo:
        buf[m:m + halo] = jnp.where(i > 0, prev_ref[0], 0.0)
        buf[m + halo + tile:m + span] = jnp.where(i < last, next_ref[0], 0.0)
    buf[m + halo:m + halo + tile] = cur_ref[0]

    pos = lax.broadcasted_iota(jnp.int32, (span, 1), 0)
    col = pos % gw
    tpos = lax.broadcasted_iota(jnp.int32, (tile, 1), 0)
    tcol = tpos % gw
    trow = i * (tile // gw) + tpos // gw
    for gi, w in enumerate(POOL_WINDOWS):
        cs = slice(gi * POOL_GROUP_DIM, (gi + 1) * POOL_GROUP_DIM)
        hw = w // 2
        s = None
        for d in range(-hw, hw):
            valid = (col + d >= 0) & (col + d < gw)
            term = jnp.where(valid, buf[m + d:m + d + span, cs], 0.0)
            s = term if s is None else s + term
        if halo:
            acc = None
            for d in range(-hw, hw):
                start = halo + d * gw
                term = s[start:start + tile]
                acc = term if acc is None else acc + term
            cnt_r = jnp.minimum(trow + hw, rows_total) - jnp.maximum(trow - hw, 0)
        else:
            acc = s
            cnt_r = 1
        cnt_c = jnp.minimum(tcol + hw, gw) - jnp.maximum(tcol - hw, 0)
        cnt = (cnt_r * cnt_c).astype(F32)
        xg = buf[m + halo:m + halo + tile, cs]
        diff = (acc / cnt - xg).astype(BF16)
        y = _dot(diff, pw_ref[gi]) * ps_ref[:, cs]
        o_ref[0, :, cs] = y.astype(BF16)


def pool_mixer(p, pool_w_bf, pool_scale, gw, tile):
    b, n, pw = p.shape
    rows_total = n // gw
    halo = MAX_HALF_WINDOW * gw if rows_total > 1 else 0
    in_specs, args = [], []
    if halo:
        r = tile // halo
        nh = n // halo
        in_specs = [
            pl.BlockSpec((1, halo, pw), lambda bi, i: (bi, jnp.maximum(i * r - 1, 0), 0)),
            pl.BlockSpec((1, tile, pw), lambda bi, i: (bi, i, 0)),
            pl.BlockSpec((1, halo, pw), lambda bi, i: (bi, jnp.minimum((i + 1) * r, nh - 1), 0)),
        ]
        args = [p, p, p]
    else:
        in_specs = [pl.BlockSpec((1, tile, pw), lambda bi, i: (bi, i, 0))]
        args = [p]
    ng = len(POOL_WINDOWS)
    in_specs += [
        pl.BlockSpec((ng, POOL_GROUP_DIM, POOL_GROUP_DIM), lambda bi, i: (0, 0, 0)),
        pl.BlockSpec((1, pw), lambda bi, i: (0, 0)),
    ]
    args += [pool_w_bf, pool_scale.reshape(1, pw)]
    return pl.pallas_call(
        functools.partial(_pool_kernel, gw=gw, tile=tile, halo=halo, rows_total=rows_total),
        out_shape=jax.ShapeDtypeStruct((b, n, pw), BF16),
        grid=(b, n // tile),
        in_specs=in_specs,
        out_specs=pl.BlockSpec((1, tile, pw), lambda bi, i: (bi, i, 0)),
        scratch_shapes=[pltpu.VMEM((tile + 2 * halo + 2 * MAX_HALF_WINDOW, pw), F32)],
        compiler_params=_cparams("parallel", "parallel"),
        name="pool_mixer",
    )(*args)


def _ret_kernel(qk_ref, v_ref, gate_ref, py_ref, x_ref, sf_ref, sb_ref, mp_ref, qdf_ref, qdb_ref,
                hm_ref, bd_ref, avg_ref, gn_ref, wo_ref, nw_ref, g1_ref, o_ref, o_buf, *, nch):
    bd = bd_ref[...]
    avg = avg_ref[...]
    lane = lax.broadcasted_iota(jnp.int32, (1, 2 * DV), 1)
    lo_mask = (lane < DV).astype(BF16)
    hi_mask = (lane >= DV).astype(BF16)
    for ci in range(nch):
        rs = slice(ci * CHUNK, (ci + 1) * CHUNK)
        q = qk_ref[0, rs, 0:QK_W]
        k = qk_ref[0, rs, QK_W:2 * QK_W]
        vv = v_ref[0, rs, :]
        inner = []
        for j in range(N_HEADS // 2):
            kp = jnp.concatenate([k * hm_ref[2 * j], k * hm_ref[2 * j + 1]], axis=0)
            sc = _dot_nt(q, kp) * mp_ref[j]
            vpair = vv[:, 2 * DV * j:2 * DV * (j + 1)]
            vp = jnp.concatenate([vpair * lo_mask, vpair * hi_mask], axis=0)
            inner.append(_dot(sc.astype(BF16), vp))
        o = jnp.concatenate(inner, axis=1)
        sf_bd = _expand_state(sf_ref[0, ci]) * bd
        sb_bd = _expand_state(sb_ref[0, ci]) * bd
        o_buf[rs, :] = o + _dot(q, sf_bd) * qdf_ref[...] + _dot(q, sb_bd) * qdb_ref[...]

    def head_mean(a):
        hi = a.astype(BF16)
        lo = (a - hi.astype(F32)).astype(BF16)
        return _dot(hi, avg) + _dot(lo, avg)

    o = o_buf[...]
    dlt = o - head_mean(o)
    var = head_mean(dlt * dlt)
    yn = dlt * lax.rsqrt(var + EPS) * gn_ref[...]
    g = gate_ref[0]
    r = (g * jax.nn.sigmoid(g) * yn).astype(BF16)
    mix = _dot(r, wo_ref[0:V_W, :]) + _dot(py_ref[0], wo_ref[V_W:V_W + POOL_W, :])
    y = mix * lax.rsqrt(jnp.mean(mix * mix, axis=-1, keepdims=True) + EPS) * nw_ref[...]
    o_ref[0] = x_ref[0] + g1_ref[0] * y


def retention_mixer(qk, v, gate, pool_y, x, sf, sb, tabs, gn, w_out_bf, norm_w, g1, tm):
    b, n, d = x.shape
    nch = tm // CHUNK
    row = lambda bi, i: (bi, i, 0)
    const = lambda shape: pl.BlockSpec(shape, lambda bi, i: (0,) * len(shape))
    return pl.pallas_call(
        functools.partial(_ret_kernel, nch=nch),
        out_shape=jax.ShapeDtypeStruct((b, n, d), F32),
        grid=(b, n // tm),
        in_specs=[
            pl.BlockSpec((1, tm, 2 * QK_W), row),
            pl.BlockSpec((1, tm, V_W), row),
            pl.BlockSpec((1, tm, V_W), row),
            pl.BlockSpec((1, tm, POOL_W), row),
            pl.BlockSpec((1, tm, d), row),
            pl.BlockSpec((1, nch, DK, V_W), lambda bi, i: (bi, i, 0, 0)),
            pl.BlockSpec((1, nch, DK, V_W), lambda bi, i: (bi, i, 0, 0)),
            const((N_HEADS // 2, CHUNK, 2 * CHUNK)),
            const((CHUNK, V_W)), const((CHUNK, V_W)),
            const((N_HEADS, 1, QK_W)),
            const((QK_W, V_W)),
            const((V_W, V_W)),
            const((1, V_W)),
            const((V_W + POOL_W, d)),
            const((1, d)),
            pl.BlockSpec((1, 1, d), lambda bi, i: (bi, 0, 0)),
        ],
        out_specs=pl.BlockSpec((1, tm, d), row),
        scratch_shapes=[pltpu.VMEM((tm, V_W), F32)],
        compiler_params=_cparams("parallel", "parallel"),
        name="retention_mixer",
    )(qk, v, gate, pool_y, x, sf, sb, tabs["mpair"], tabs["qdec_f"], tabs["qdec_b"], tabs["hmask"],
      tabs["bd_bf16"], tabs["avg"], gn.reshape(1, V_W), w_out_bf, norm_w.reshape(1, d), g1)


def _packed_width(d, dtype):
    return d * jnp.dtype(dtype).itemsize // 4


def _pack_rows(h):
    bits = lax.bitcast_convert_type(h.astype(F32), jnp.uint32)
    if h.dtype.itemsize == 4:
        return bits
    half = h.shape[1] // 2
    return (bits[:, half:] & jnp.uint32(0xFFFF0000)) | (bits[:, :half] >> 16)


def _unpack_rows(w, dtype):
    if jnp.dtype(dtype).itemsize == 4:
        return lax.bitcast_convert_type(w, dtype)
    lo = lax.bitcast_convert_type(w << 16, F32)
    hi = lax.bitcast_convert_type(w & jnp.uint32(0xFFFF0000), F32)
    return jnp.concatenate([lo, hi], axis=1).astype(dtype)


def _router_kernel(x_ref, g_ref, sh_ref, sc_ref, wr_ref, h_ref, aff_ref):
    h = _norm_mod(x_ref[0], g_ref[...], sh_ref[0], sc_ref[0])
    h_hi = h.astype(BF16)
    h_ref[0] = _pack_rows(h_hi)
    ne = wr_ref.shape[0]
    h_lo = (h - h_hi.astype(F32)).astype(BF16)
    w = wr_ref[...]
    w_hi = w.astype(BF16)
    w_lo = (w - w_hi.astype(F32)).astype(BF16)
    both = _dot_nt(jnp.concatenate([w_hi, w_lo], axis=0), h_hi)
    logits = both[0:ne] + both[ne:2 * ne] + _dot_nt(w_hi, h_lo)
    mx = jnp.max(logits, axis=0, keepdims=True)
    e = jnp.exp(logits - mx)
    aff_ref[0] = e / jnp.sum(e, axis=0, keepdims=True)


def router(x, norm_w, shift, scale, w_router_t, tm):
    b, n, d = x.shape
    ne = w_router_t.shape[0]
    return pl.pallas_call(
        _router_kernel,
        out_shape=(jax.ShapeDtypeStruct((b, n, _packed_width(d, BF16)), jnp.uint32),
                   jax.ShapeDtypeStruct((b, ne, n), F32)),
        grid=(b, n // tm),
        in_specs=[
            pl.BlockSpec((1, tm, d), lambda bi, i: (bi, i, 0)),
            pl.BlockSpec((1, d), lambda bi, i: (0, 0)),
            pl.BlockSpec((1, 1, d), lambda bi, i: (bi, 0, 0)),
            pl.BlockSpec((1, 1, d), lambda bi, i: (bi, 0, 0)),
            pl.BlockSpec((ne, d), lambda bi, i: (0, 0)),
        ],
        out_specs=(
            pl.BlockSpec((1, tm, _packed_width(d, BF16)), lambda bi, i: (bi, i, 0)),
            pl.BlockSpec((1, ne, tm), lambda bi, i: (bi, 0, i)),
        ),
        compiler_params=_cparams("parallel", "parallel"),
        name="router",
    )(x, norm_w.reshape(1, d), shift, scale, w_router_t)


SC_CORES = 2
SC_SUBCORES = 16
GATHER_CHUNK = 64


def gather_rows(table, idx):
    rows, width = idx.shape[0], table.shape[1]
    workers = SC_CORES * SC_SUBCORES
    per_worker = rows // workers
    assert rows == per_worker * workers and per_worker % GATHER_CHUNK == 0
    mesh = plsc.VectorSubcoreMesh(core_axis_name="c", subcore_axis_name="s")

    def body(table_hbm, idx_hbm, out_hbm, idx_v, rows_v, sem):
        wid = lax.axis_index("s") * SC_CORES + lax.axis_index("c")
        base = wid * per_worker

        @pl.loop(0, per_worker // GATHER_CHUNK)
        def _(i):
            off = pl.multiple_of(base + i * GATHER_CHUNK, GATHER_CHUNK)
            pltpu.sync_copy(idx_hbm.at[pl.ds(off, GATHER_CHUNK)], idx_v)
            pltpu.async_copy(table_hbm.at[idx_v], rows_v, sem).wait()
            pltpu.sync_copy(rows_v, out_hbm.at[pl.ds(off, GATHER_CHUNK)])

    return pl.kernel(
        body,
        out_type=jax.ShapeDtypeStruct((rows, width), table.dtype),
        mesh=mesh,
        scratch_types=[
            pltpu.VMEM((GATHER_CHUNK,), jnp.int32),
            pltpu.VMEM((GATHER_CHUNK, width), table.dtype),
            pltpu.SemaphoreType.DMA,
        ],
        name="gather_rows",
    )(table, idx)


def _ffn_kernel(x_ref, gate_ref, wg_ref, wu_ref, wd_ref, o_ref, *, rt):
    rows = x_ref.shape[2]
    for r0 in range(0, rows, rt):
        rs = slice(r0, r0 + rt)
        xt = _unpack_rows(x_ref[0, 0, rs, :], wg_ref.dtype)
        a = _dot_nt(xt, wg_ref[0, 0])
        u = _dot_nt(xt, wu_ref[0, 0])
        hh = (a * jax.nn.sigmoid(a) * u).astype(BF16)
        gate_col = jnp.broadcast_to(gate_ref[0, 0, :, rs], (LANES, rt)).T[:, 0:1]
        o_ref[0, 0, rs, :] = (_dot(hh, wd_ref[0, 0]) * gate_col).astype(BF16)


def expert_ffn(xs, gate, wg_t, wu_t, wd, layer):
    g, ne, rows, dp = xs.shape
    f, d = wd.shape[2], wd.shape[3]
    rt = min(rows, 256)
    return pl.pallas_call(
        functools.partial(_ffn_kernel, rt=rt),
        out_shape=jax.ShapeDtypeStruct((g, ne, rows, d), BF16),
        grid=(ne, g),
        in_specs=[
            pl.BlockSpec((1, 1, rows, dp), lambda e, gi: (gi, e, 0, 0)),
            pl.BlockSpec((1, 1, 1, rows), lambda e, gi: (gi, e, 0, 0)),
            pl.BlockSpec((1, 1, f, d), lambda e, gi: (layer, e, 0, 0)),
            pl.BlockSpec((1, 1, f, d), lambda e, gi: (layer, e, 0, 0)),
            pl.BlockSpec((1, 1, f, d), lambda e, gi: (layer, e, 0, 0)),
        ],
        out_specs=pl.BlockSpec((1, 1, rows, d), lambda e, gi: (gi, e, 0, 0)),
        compiler_params=_cparams("arbitrary", "arbitrary"),
        name="expert_ffn",
    )(xs, gate, wg_t, wu_t, wd)


WEIGHT_CAST_CHUNKS = 4


def _cast_kernel(g_ref, u_ref, d_ref, go_ref, uo_ref, do_ref):
    go_ref[...] = g_ref[...].astype(BF16)
    uo_ref[...] = u_ref[...].astype(BF16)
    do_ref[...] = d_ref[...].astype(BF16)


def cast_expert_weights(w_gate_t, w_up_t, w_down):
    depth, ne, f, d = w_down.shape
    spec = pl.BlockSpec((1, 1, f // WEIGHT_CAST_CHUNKS, d), lambda l, e, c: (l, e, c, 0))
    out = jax.ShapeDtypeStruct(w_down.shape, BF16)
    return pl.pallas_call(
        _cast_kernel,
        out_shape=(out, out, out),
        grid=(depth, ne, WEIGHT_CAST_CHUNKS),
        in_specs=[spec, spec, spec],
        out_specs=(spec, spec, spec),
        compiler_params=_cparams("parallel", "parallel", "parallel"),
        name="cast_expert_weights",
    )(w_gate_t, w_up_t, w_down)


ROUTE_WIN = 64
LANES = 128
VAL_ROWS = 8


def _combine_kernel(st_ref, rank_ref, y_hbm, x_ref, nw_ref, g_ref, o_ref, buf, xbuf, sem, xsem,
                    acc_ref, *, tt, ne, cap, merged):
    b = pl.program_id(0)
    k = pl.program_id(1)
    nk = pl.num_programs(1)
    step = b * nk + k
    slot = step % 2
    rows_total = y_hbm.shape[2]
    win = ROUTE_WIN
    g = 0 if merged else b
    base = b * cap if merged else 0
    lane = lax.broadcasted_iota(jnp.int32, (1, LANES), 1)

    def tile_bounds(bb, kk, e):
        off = bb * cap if merged else 0
        lo = off + st_ref[(bb * (nk + 1) + kk) * ne + e]
        hi = off + st_ref[(bb * (nk + 1) + kk + 1) * ne + e]
        return lo, hi, jnp.minimum((lo // 16) * 16, rows_total - win)

    def bounds(e):
        return tile_bounds(b, k, e)

    def win_copy(bb, e, a0, sl):
        src = y_hbm.at[0 if merged else bb, e, pl.ds(pl.multiple_of(a0, 16), win), :]
        return pltpu.make_async_copy(src, buf.at[sl, pl.ds(e * win, win), :], sem.at[sl, e])

    def start_tile(bb, kk, sl):
        for e in range(ne):
            win_copy(bb, e, tile_bounds(bb, kk, e)[2], sl).start()

    @pl.when(step == 0)
    def _():
        start_tile(b, k, slot)

    nxt = step + 1

    @pl.when(nxt < pl.num_programs(0) * nk)
    def _():
        start_tile(nxt // nk, nxt % nk, 1 - slot)

    rkf = rank_ref[0].astype(F32)
    rk_t = jnp.concatenate([rkf, jnp.full((LANES - ne, tt), -1.0, F32)], axis=0).T
    rk_t = rk_t.astype(jnp.int32)

    def target(e, origin):
        col = rk_t[:, e:e + 1]
        return jnp.where(col >= 0, col + (base - origin), -1)

    pieces = []
    for j in range(ne // 2):
        t_even = target(2 * j, bounds(2 * j)[2])
        t_odd = target(2 * j + 1, bounds(2 * j + 1)[2])
        t_odd = jnp.where((t_odd >= 0) & (t_odd < win), t_odd + win, -1)
        pieces.append((jnp.where(lane < win, t_even, t_odd) == lane).astype(BF16))
    p = jnp.concatenate(pieces, axis=1)
    for e in range(ne):
        win_copy(b, e, bounds(e)[2], slot).wait()
    acc_ref[...] = _dot(p, buf[slot])

    for e in range(ne):
        lo, hi, a0 = bounds(e)

        def extra(w, carry, e=e, a0=a0):
            start = a0 + w * win
            aw = pl.multiple_of(jnp.minimum(start, rows_total - win), 16)
            cp = pltpu.make_async_copy(y_hbm.at[g, e, pl.ds(aw, win), :], xbuf, xsem)
            cp.start()
            cp.wait()
            col = rk_t[:, e:e + 1]
            ok = (col >= 0) & (col + base >= start)
            px = (jnp.where(ok, col + (base - aw), -1) == lane[:, 0:win]).astype(BF16)
            acc_ref[...] += _dot(px, xbuf[...])
            return carry

        lax.fori_loop(1, (hi - a0 + win - 1) // win, extra, 0)

    y = acc_ref[...]
    yn = y * lax.rsqrt(jnp.mean(y * y, axis=-1, keepdims=True) + EPS) * nw_ref[...]
    o_ref[0] = x_ref[0] + g_ref[0] * yn


def combine_post(starts, rank_t, y, x, norm_w, g, tt, cap, merged):
    b, n, d = x.shape
    _, ne, rows, _ = y.shape
    grid_spec = pltpu.PrefetchScalarGridSpec(
        num_scalar_prefetch=1,
        grid=(b, n // tt),
        in_specs=[
            pl.BlockSpec((1, ne, tt), lambda bi, k, st: (bi, 0, k)),
            pl.BlockSpec(memory_space=pl.ANY),
            pl.BlockSpec((1, tt, d), lambda bi, k, st: (bi, k, 0)),
            pl.BlockSpec((1, d), lambda bi, k, st: (0, 0)),
            pl.BlockSpec((1, 1, d), lambda bi, k, st: (bi, 0, 0)),
        ],
        out_specs=pl.BlockSpec((1, tt, d), lambda bi, k, st: (bi, k, 0)),
        scratch_shapes=[
            pltpu.VMEM((2, ne * ROUTE_WIN, d), BF16),
            pltpu.VMEM((ROUTE_WIN, d), BF16),
            pltpu.SemaphoreType.DMA((2, ne)),
            pltpu.SemaphoreType.DMA(()),
            pltpu.VMEM((tt, d), F32),
        ],
    )
    return pl.pallas_call(
        functools.partial(_combine_kernel, tt=tt, ne=ne, cap=cap, merged=merged),
        out_shape=jax.ShapeDtypeStruct((b, n, d), F32),
        grid_spec=grid_spec,
        compiler_params=_cparams("arbitrary", "arbitrary"),
        name="combine_post",
    )(starts, rank_t, y, x, norm_w.reshape(1, d), g)


def _select_kernel(aff_ref, tri_ref, rank_ref, st_ref, *, cap, tt):
    a = aff_ref[0]
    ne, n = a.shape
    bits = lax.bitcast_convert_type(a, jnp.int32)

    def search(_, c):
        lo, hi = c
        mid = lo + ((hi - lo) >> 1)
        cnt = jnp.sum((bits >= mid).astype(F32), axis=1, keepdims=True)
        ge = cnt >= cap
        return jnp.where(ge, mid, lo), jnp.where(ge, hi, mid)

    lo0 = jnp.zeros((ne, 1), jnp.int32)
    hi0 = jnp.full((ne, 1), 0x7F800000, jnp.int32)
    thr, _ = lax.fori_loop(0, 31, search, (lo0, hi0))
    gt = bits > thr
    eq = bits == thr
    need = cap - jnp.sum(gt.astype(F32), axis=1, keepdims=True)
    m = jnp.concatenate([gt, eq], axis=0).astype(BF16)
    tri = tri_ref[...]
    lane = lax.broadcasted_iota(jnp.int32, (1, LANES), 1)
    off = jnp.zeros((2 * ne, 1), F32)
    st = jnp.zeros((ne, LANES), jnp.int32)
    for j in range(n // LANES):
        cs = slice(j * LANES, (j + 1) * LANES)
        if (j * LANES) % tt == 0:
            off_sel = off[:ne] + jnp.minimum(off[ne:], need)
            st = jnp.where(lane == (j * LANES) // tt, off_sel.astype(jnp.int32), st)
        mj = m[:, cs]
        pj = _dot(mj, tri) + off
        pe = pj[ne:]
        sel = gt[:, cs] | (eq[:, cs] & (pe < need))
        rank_ref[0, :, cs] = jnp.where(sel, pj[:ne] + jnp.minimum(pe, need), -1.0).astype(jnp.int32)
        off = off + jnp.sum(mj.astype(F32), axis=1, keepdims=True)
    st_ref[0] = jnp.where(lane == n // tt, cap, st)


def route_select(aff_t, cap, tt):
    b, ne, n = aff_t.shape
    tri = jnp.asarray(np.triu(np.ones((LANES, LANES), np.float32), 1), BF16)
    return pl.pallas_call(
        functools.partial(_select_kernel, cap=cap, tt=tt),
        out_shape=(jax.ShapeDtypeStruct((b, ne, n), jnp.int32),
                   jax.ShapeDtypeStruct((b, ne, LANES), jnp.int32)),
        grid=(b,),
        in_specs=[pl.BlockSpec((1, ne, n), lambda bi: (bi, 0, 0)),
                  pl.BlockSpec((LANES, LANES), lambda bi: (0, 0))],
        out_specs=(pl.BlockSpec((1, ne, n), lambda bi: (bi, 0, 0)),
                   pl.BlockSpec((1, ne, LANES), lambda bi: (bi, 0, 0))),
        compiler_params=_cparams("parallel"),
        name="route_select",
    )(aff_t, tri)


def _compact_kernel(st_ref, rank_ref, aff_ref, idx_ref, gate_ref, out_ref, *, tt, ne, cap, n):
    b = pl.program_id(0)
    k = pl.program_id(1)
    nk = pl.num_programs(1)
    win = ROUTE_WIN

    @pl.when(k == 0)
    def _():
        out_ref[...] = jnp.zeros_like(out_ref)

    rk = rank_ref[0]
    a = aff_ref[0]
    g1 = a.astype(BF16).astype(F32)
    r1 = a - g1
    g2 = r1.astype(BF16).astype(F32)
    g3 = r1 - g2
    tok = k * tt + lax.broadcasted_iota(jnp.int32, (1, tt), 1)
    ids = jnp.concatenate([(tok >> 6).astype(F32), (tok & 63).astype(F32),
                           jnp.zeros((VAL_ROWS - 2, tt), F32)], axis=0)
    pad = jnp.zeros((LANES - VAL_ROWS - 3 * ne, tt), F32)
    payload = jnp.concatenate([ids, g1, g2, g3, pad], axis=0).astype(BF16)
    sub = lax.broadcasted_iota(jnp.int32, (win, 1), 0)

    def bounds(e):
        lo = st_ref[(b * (nk + 1) + k) * ne + e]
        hi = st_ref[(b * (nk + 1) + k + 1) * ne + e]
        return hi, (lo // 8) * 8

    def one_hot(e, aw):
        return ((rk[e:e + 1, :] - aw) == sub).astype(BF16)

    p_all = jnp.concatenate([one_hot(e, bounds(e)[1]) for e in range(ne)], axis=0)
    moved = _dot_nt(p_all, payload)
    for e in range(ne):
        a0 = pl.multiple_of(bounds(e)[1], 8)
        out_ref[e, pl.ds(a0, win), :] += moved[e * win:(e + 1) * win]

    for e in range(ne):
        hi, a0 = bounds(e)

        def window(w, carry, e=e, a0=a0):
            aw = pl.multiple_of(a0 + w * win, 8)
            out_ref[e, pl.ds(aw, win), :] += _dot_nt(one_hot(e, aw), payload)
            return carry

        lax.fori_loop(1, (hi - a0 + win - 1) // win, window, 0)

    @pl.when(k == nk - 1)
    def _():
        for e in range(ne):
            t = out_ref[e].T
            ids_e = (t[0:1] * 64.0 + t[1:2]).astype(jnp.int32) + b * n
            g_e = (t[VAL_ROWS + e:VAL_ROWS + e + 1] + t[VAL_ROWS + ne + e:VAL_ROWS + ne + e + 1]) \
                + t[VAL_ROWS + 2 * ne + e:VAL_ROWS + 2 * ne + e + 1]
            idx_ref[0, e:e + 1, :] = ids_e[:, 0:cap]
            gate_ref[0, e:e + 1, :] = g_e[:, 0:cap]


def route_compact(starts, rank_t, aff_t, cap, tt):
    b, ne, n = aff_t.shape
    rows = -(-(cap + ROUTE_WIN) // LANES) * LANES
    grid_spec = pltpu.PrefetchScalarGridSpec(
        num_scalar_prefetch=1,
        grid=(b, n // tt),
        in_specs=[pl.BlockSpec((1, ne, tt), lambda bi, k, st: (bi, 0, k)),
                  pl.BlockSpec((1, ne, tt), lambda bi, k, st: (bi, 0, k))],
        out_specs=(pl.BlockSpec((1, ne, cap), lambda bi, k, st: (bi, 0, 0)),
                   pl.BlockSpec((1, ne, cap), lambda bi, k, st: (bi, 0, 0))),
        scratch_shapes=[pltpu.VMEM((ne, rows, LANES), F32)],
    )
    return pl.pallas_call(
        functools.partial(_compact_kernel, tt=tt, ne=ne, cap=cap, n=n),
        out_shape=(jax.ShapeDtypeStruct((b, ne, cap), jnp.int32),
                   jax.ShapeDtypeStruct((b, ne, cap), F32)),
        grid_spec=grid_spec,
        compiler_params=_cparams("parallel", "arbitrary"),
        name="route_compact",
    )(starts, rank_t, aff_t)


def _qk_head_of_lane():
    half = QK_W // 2
    return (np.arange(QK_W) % half) // (DK // 2)


def _in_proj_perm():
    half = DK // 2
    first = [h * DK + i for h in range(N_HEADS) for i in range(half)]
    second = [h * DK + half + i for h in range(N_HEADS) for i in range(half)]
    qperm = np.array(first + second)
    rest = np.arange(2 * QK_W, 2 * QK_W + 2 * V_W + POOL_W)
    return np.concatenate([qperm, QK_W + qperm, rest])


def _static_tables():
    qk_head = _qk_head_of_lane()
    v_head = np.arange(V_W) // DV
    hmask = (qk_head[None, :] == np.arange(N_HEADS)[:, None])
    bd = (qk_head[:, None] == v_head[None, :])
    avg = (v_head[:, None] == v_head[None, :]).astype(np.float32) / DV
    return {
        "hmask": jnp.asarray(hmask[:, None, :], BF16),
        "bd_f32": jnp.asarray(bd, F32),
        "bd_bf16": jnp.asarray(bd, BF16),
        "avg": jnp.asarray(avg, BF16),
    }


def _decay_tables(lg_f, lg_b):
    pos = jnp.arange(CHUNK, dtype=F32)
    diff = pos[:, None] - pos[None, :]
    low = diff >= 0
    up = diff < 0
    m_f = jnp.where(low, jnp.exp(lg_f[:, None, None] * jnp.where(low, diff, 0.0)), 0.0)
    m_b = jnp.where(up, jnp.exp(lg_b[:, None, None] * jnp.where(up, -diff, 0.0)), 0.0)
    m = m_f + m_b
    mpair = m.reshape(N_HEADS // 2, 2, CHUNK, CHUNK).transpose(0, 2, 1, 3).reshape(
        N_HEADS // 2, CHUNK, 2 * CHUNK)
    qk_head = _qk_head_of_lane()
    v_head = np.arange(V_W) // DV
    qdec_f = jnp.exp(lg_f[None, :] * (pos[:, None] + 1.0))[:, v_head]
    qdec_b = jnp.exp(lg_b[None, :] * (CHUNK - pos[:, None]))[:, v_head]
    kdec_f = jnp.exp(lg_f[None, :] * (CHUNK - 1.0 - pos[:, None]))[:, qk_head]
    kdec_b = jnp.exp(lg_b[None, :] * pos[:, None])[:, qk_head]
    cdec_f = jnp.exp(lg_f * CHUNK)[None, v_head]
    cdec_b = jnp.exp(lg_b * CHUNK)[None, v_head]
    return {"mpair": mpair, "qdec_f": qdec_f, "qdec_b": qdec_b, "kdec_f": kdec_f,
            "kdec_b": kdec_b, "cdec_f": cdec_f, "cdec_b": cdec_b}


def _rope_tables(n):
    t = jnp.arange(n)
    row = (t // GRID_W).astype(F32)
    col = (t % GRID_W).astype(F32)
    n_freq = DK // 4
    inv = ROPE_BASE ** (-jnp.arange(n_freq, dtype=F32) / n_freq)
    ang = jnp.concatenate([row[:, None] * inv, col[:, None] * inv], axis=-1)
    return jnp.tile(jnp.cos(ang), (1, N_HEADS)), jnp.tile(jnp.sin(ang), (1, N_HEADS))


def _moe_residual(x, h_bf, aff_t, wg, wu, wd, layer, norm_w, g, merged):
    b, n, d = h_bf.shape
    cap = EC_FACTOR * n // N_EXPERTS
    ne = N_EXPERTS
    tt = min(n, 256)
    nk = n // tt
    rank_t, st = route_select(aff_t, cap, tt)
    starts = st[:, :, :nk + 1].transpose(0, 2, 1).reshape(-1)
    flat, gate = route_compact(starts, rank_t, aff_t, cap, tt)
    if merged:
        flat = flat.transpose(1, 0, 2)
        gate = gate.transpose(1, 0, 2).reshape(1, ne, b * cap)
    xs = gather_rows(h_bf.reshape(b * n, d), flat.reshape(-1))
    xs = xs.reshape(gate.shape + (d,))
    y = expert_ffn(xs, gate[:, :, None, :], wg, wu, wd, layer)
    return combine_post(starts, rank_t, y, x, norm_w, g, tt, cap, merged)


def kernel(x, c, ctx, c_ctx, w_ada, b_ada, norm_pre_mix, norm_post_mix, norm_pre_ffn, norm_post_ffn, w_in, ret_decay_fwd, ret_decay_bwd, ret_gn, pool_w, pool_scale, w_out, w_router, w_gate, w_up, w_down):
    b, n, d = x.shape
    lc = ctx.shape[1]
    depth = w_ada.shape[0]
    rope = _rope_tables(n)
    static = _static_tables()
    perm = _in_proj_perm()

    cc = jnp.concatenate([c, c_ctx[None, :], jnp.zeros((7, d), F32)], axis=0)
    mods = ada_modulation(cc, w_ada, b_ada)
    wg, wu, wd = cast_expert_weights(jnp.swapaxes(w_gate, 2, 3), jnp.swapaxes(w_up, 2, 3), w_down)

    tm = 512
    for l in range(depth):
        last = l == depth - 1
        mx = mods[l, :b].reshape(b, 1, 6, d)
        sh1, sc1, g1, sh2, sc2, g2 = [mx[:, :, i] for i in range(6)]
        mc = jnp.broadcast_to(mods[l, b].reshape(1, 1, 6, d), (b, 1, 6, d))
        csh1, csc1, cg1, csh2, csc2, cg2 = [mc[:, :, i] for i in range(6)]
        lg_f = jax.nn.log_sigmoid(ret_decay_fwd[l].astype(F32))
        lg_b = jax.nn.log_sigmoid(ret_decay_bwd[l].astype(F32))
        tabs = dict(static, **_decay_tables(lg_f, lg_b))
        w_in_p = w_in[l][:, perm].astype(BF16)
        w_out_bf = w_out[l].astype(BF16)
        pool_w_bf = pool_w[l].astype(BF16)

        qk_c, v_c, gate_c, p_c = premix(ctx, norm_pre_mix[l], csh1, csc1, w_in_p, None, lc)
        zero = jnp.zeros((b, DK, V_W), F32)
        sf_c, sb_c, s_f, s_b = state_scan(qk_c, v_c, zero, zero, tabs)
        qk_x, v_x, gate_x, p_x = premix(x, norm_pre_mix[l], sh1, sc1, w_in_p, rope, tm)
        sf_x, sb_x, _, _ = state_scan(qk_x, v_x, s_f, s_b, tabs)
        pool_x = pool_mixer(p_x, pool_w_bf, pool_scale[l], GRID_W, min(n, 2048))
        x = retention_mixer(qk_x, v_x, gate_x, pool_x, x, sf_x, sb_x, tabs, ret_gn[l], w_out_bf,
                            norm_post_mix[l], g1, tm)
        if not last:
            pool_c = pool_mixer(p_c, pool_w_bf, pool_scale[l], lc, lc)
            ctx = retention_mixer(qk_c, v_c, gate_c, pool_c, ctx, sf_c, sb_c, tabs, ret_gn[l],
                                  w_out_bf, norm_post_mix[l], cg1, lc)

        wr_t = w_router[l].T
        h_x, aff_x = router(x, norm_pre_ffn[l], sh2, sc2, wr_t, tm)
        x = _moe_residual(x, h_x, aff_x, wg, wu, wd, l, norm_post_ffn[l], g2, False)
        if not last:
            h_c, aff_c = router(ctx, norm_pre_ffn[l], csh2, csc2, wr_t, lc)
            ctx = _moe_residual(ctx, h_c, aff_c, wg, wu, wd, l, norm_post_ffn[l], cg2, True)
    return x
```

```python
import functools

import jax
import jax.numpy as jnp
import numpy as np
from jax import lax
from jax.experimental import pallas as pl
from jax.experimental.pallas import tpu as pltpu
from jax.experimental.pallas import tpu_sc as plsc

F32 = jnp.float32
BF16 = jnp.bfloat16

D_MODEL = 1024
GRID_W = 64
N_HEADS = 8
DV = 64
DK = 32
QK_W = N_HEADS * DK
V_W = N_HEADS * DV
POOL_W = 512
POOL_WINDOWS = (2, 4, 8, 16)
POOL_GROUP_DIM = 128
CHUNK = 128
ROPE_BASE = 10000.0
N_EXPERTS = 16
EC_FACTOR = 2
EPS = 1e-6
MAX_HALF_WINDOW = max(POOL_WINDOWS) // 2

VMEM_LIMIT = 56 * 1024 * 1024


def _cparams(*sem):
    return pltpu.CompilerParams(dimension_semantics=sem, vmem_limit_bytes=VMEM_LIMIT)


def _dot(a, b):
    return jnp.dot(a, b, preferred_element_type=F32)


def _dot_nt(a, b, precision=None):
    return lax.dot_general(a, b, (((1,), (1,)), ((), ())), precision=precision,
                           preferred_element_type=F32)


def _ada_kernel(cc_ref, w_ref, b_ref, o_ref):
    s = cc_ref[...]
    s = s * jax.nn.sigmoid(s)
    o_ref[0] = _dot(s.astype(BF16), w_ref[0].astype(BF16)) + b_ref[0]


def ada_modulation(cc, w_ada, b_ada):
    depth, d, d6 = w_ada.shape
    rows = cc.shape[0]
    tn = 1536
    return pl.pallas_call(
        _ada_kernel,
        out_shape=jax.ShapeDtypeStruct((depth, rows, d6), F32),
        grid=(depth, d6 // tn),
        in_specs=[
            pl.BlockSpec((rows, d), lambda l, j: (0, 0)),
            pl.BlockSpec((1, d, tn), lambda l, j: (l, 0, j)),
            pl.BlockSpec((1, 1, tn), lambda l, j: (l, 0, j)),
        ],
        out_specs=pl.BlockSpec((1, rows, tn), lambda l, j: (l, 0, j)),
        compiler_params=_cparams("parallel", "parallel"),
        name="ada_modulation",
    )(cc, w_ada, b_ada.reshape(depth, 1, d6))


def _norm_mod(xf, g, sh, sc):
    y = xf * lax.rsqrt(jnp.mean(xf * xf, axis=-1, keepdims=True) + EPS)
    return (y * g) * (1.0 + sc) + sh


def _premix_kernel(x_ref, g_ref, sh_ref, sc_ref, w_ref, *rest, rope):
    if rope:
        cos_ref, sin_ref, qk_ref, v_ref, gate_ref, p_ref = rest
    else:
        qk_ref, v_ref, gate_ref, p_ref = rest
    hb = _norm_mod(x_ref[0], g_ref[...], sh_ref[0], sc_ref[0]).astype(BF16)
    zqk = _dot(hb, w_ref[:, 0:2 * QK_W])
    half = QK_W // 2
    q1, q2 = zqk[:, 0:half], zqk[:, half:2 * half]
    k1 = zqk[:, 2 * half:3 * half] * (DK ** -0.5)
    k2 = zqk[:, 3 * half:4 * half] * (DK ** -0.5)
    if rope:
        cos, sin = cos_ref[...], sin_ref[...]
        q1, q2 = q1 * cos - q2 * sin, q1 * sin + q2 * cos
        k1, k2 = k1 * cos - k2 * sin, k1 * sin + k2 * cos
    qk_ref[0] = jnp.concatenate([q1, q2, k1, k2], axis=1).astype(BF16)
    o = 2 * QK_W
    v_ref[0] = _dot(hb, w_ref[:, o:o + V_W]).astype(BF16)
    gate_ref[0] = _dot(hb, w_ref[:, o + V_W:o + 2 * V_W])
    p_ref[0] = _dot(hb, w_ref[:, o + 2 * V_W:o + 2 * V_W + POOL_W])


def premix(x, norm_w, shift, scale, w_in_p, rope, tm):
    b, n, d = x.shape
    in_w = w_in_p.shape[1]
    row = lambda bi, i: (bi, i, 0)
    in_specs = [
        pl.BlockSpec((1, tm, d), row),
        pl.BlockSpec((1, d), lambda bi, i: (0, 0)),
        pl.BlockSpec((1, 1, d), lambda bi, i: (bi, 0, 0)),
        pl.BlockSpec((1, 1, d), lambda bi, i: (bi, 0, 0)),
        pl.BlockSpec((d, in_w), lambda bi, i: (0, 0)),
    ]
    args = [x, norm_w.reshape(1, d), shift, scale, w_in_p]
    if rope is not None:
        half = QK_W // 2
        in_specs += [pl.BlockSpec((tm, half), lambda bi, i: (i, 0))] * 2
        args += list(rope)
    return pl.pallas_call(
        functools.partial(_premix_kernel, rope=rope is not None),
        out_shape=(
            jax.ShapeDtypeStruct((b, n, 2 * QK_W), BF16),
            jax.ShapeDtypeStruct((b, n, V_W), BF16),
            jax.ShapeDtypeStruct((b, n, V_W), F32),
            jax.ShapeDtypeStruct((b, n, POOL_W), F32),
        ),
        grid=(b, n // tm),
        in_specs=in_specs,
        out_specs=(
            pl.BlockSpec((1, tm, 2 * QK_W), row),
            pl.BlockSpec((1, tm, V_W), row),
            pl.BlockSpec((1, tm, V_W), row),
            pl.BlockSpec((1, tm, POOL_W), row),
        ),
        compiler_params=_cparams("parallel", "parallel"),
        name="premix",
    )(*args)


def _fold_state(s):
    g = 2 * N_HEADS
    top = s[0:g]
    bot = s[QK_W // 2:QK_W // 2 + g]
    for h in range(1, N_HEADS):
        top = top + s[h * g:(h + 1) * g]
        bot = bot + s[QK_W // 2 + h * g:QK_W // 2 + (h + 1) * g]
    return jnp.concatenate([top, bot], axis=0)


def _expand_state(c):
    g = 2 * N_HEADS
    return jnp.concatenate([c[0:g]] * N_HEADS + [c[g:2 * g]] * N_HEADS, axis=0)


def _state_kernel(kf_ref, vf_ref, kb_ref, vb_ref, s0f_ref, s0b_ref, kdf_ref, kdb_ref,
                  cdf_ref, cdb_ref, bd_ref, sf_out, sb_out, ff_out, fb_out, sf_acc, sb_acc, *, cps):
    c = pl.program_id(1)
    nc = pl.num_programs(1)
    bd = bd_ref[...]

    @pl.when(c == 0)
    def _():
        sf_acc[...] = _expand_state(s0f_ref[0]) * bd
        sb_acc[...] = _expand_state(s0b_ref[0]) * bd

    def update(s, k, v, kd_ref, cd_ref):
        kd = k.astype(F32) * kd_ref[...]
        f = _dot(kd.T.astype(BF16), v)
        return s * cd_ref[...] + f * bd

    sf = sf_acc[...]
    sb = sb_acc[...]
    for i in range(cps):
        j = cps - 1 - i
        fs = slice(i * CHUNK, (i + 1) * CHUNK)
        bs = slice(j * CHUNK, (j + 1) * CHUNK)
        sf_out[0, i] = _fold_state(sf).astype(BF16)
        sb_out[0, j] = _fold_state(sb).astype(BF16)
        sf = update(sf, kf_ref[0, fs, :], vf_ref[0, fs, :], kdf_ref, cdf_ref)
        sb = update(sb, kb_ref[0, bs, :], vb_ref[0, bs, :], kdb_ref, cdb_ref)
    sf_acc[...] = sf
    sb_acc[...] = sb

    @pl.when(c == nc - 1)
    def _():
        ff_out[0] = _fold_state(sf)
        fb_out[0] = _fold_state(sb)


def state_scan(qk, v, s0f, s0b, tabs):
    b, n, _ = qk.shape
    nc = n // CHUNK
    cps = min(nc, 4)
    ns = nc // cps
    rows = cps * CHUNK
    const = lambda shape: pl.BlockSpec(shape, lambda bi, c: (0,) * len(shape))
    return pl.pallas_call(
        functools.partial(_state_kernel, cps=cps),
        out_shape=(
            jax.ShapeDtypeStruct((b, nc, DK, V_W), BF16),
            jax.ShapeDtypeStruct((b, nc, DK, V_W), BF16),
            jax.ShapeDtypeStruct((b, DK, V_W), F32),
            jax.ShapeDtypeStruct((b, DK, V_W), F32),
        ),
        grid=(b, ns),
        in_specs=[
            pl.BlockSpec((1, rows, QK_W), lambda bi, c: (bi, c, 1)),
            pl.BlockSpec((1, rows, V_W), lambda bi, c: (bi, c, 0)),
            pl.BlockSpec((1, rows, QK_W), lambda bi, c: (bi, ns - 1 - c, 1)),
            pl.BlockSpec((1, rows, V_W), lambda bi, c: (bi, ns - 1 - c, 0)),
            pl.BlockSpec((1, DK, V_W), lambda bi, c: (bi, 0, 0)),
            pl.BlockSpec((1, DK, V_W), lambda bi, c: (bi, 0, 0)),
            const((CHUNK, QK_W)), const((CHUNK, QK_W)),
            const((1, V_W)), const((1, V_W)),
            const((QK_W, V_W)),
        ],
        out_specs=(
            pl.BlockSpec((1, cps, DK, V_W), lambda bi, c: (bi, c, 0, 0)),
            pl.BlockSpec((1, cps, DK, V_W), lambda bi, c: (bi, ns - 1 - c, 0, 0)),
            pl.BlockSpec((1, DK, V_W), lambda bi, c: (bi, 0, 0)),
            pl.BlockSpec((1, DK, V_W), lambda bi, c: (bi, 0, 0)),
        ),
        scratch_shapes=[pltpu.VMEM((QK_W, V_W), F32), pltpu.VMEM((QK_W, V_W), F32)],
        compiler_params=_cparams("parallel", "arbitrary"),
        name="state_scan",
    )(qk, v, qk, v, s0f, s0b, tabs["kdec_f"], tabs["kdec_b"], tabs["cdec_f"], tabs["cdec_b"],
      tabs["bd_f32"])


def _pool_kernel(*refs, gw, tile, halo, rows_total):
    if halo:
        prev_ref, cur_ref, next_ref, pw_ref, ps_ref, o_ref, buf = refs
    else:
        cur_ref, pw_ref, ps_ref, o_ref, buf = refs
    i = pl.program_id(1)
    last = pl.num_programs(1) - 1
    m = MAX_HALF_WINDOW
    span = tile + 2 * halo
    zeros_m = jnp.zeros((m, POOL_W), F32)
    buf[0:m] = zeros_m
    buf[m + span:2 * m + span] = zeros_m
    if halo:
        buf[m:m + halo] = jnp.where(i > 0, prev_ref[0], 0.0)
        buf[m + halo + tile:m + span] = jnp.where(i < last, next_ref[0], 0.0)
    buf[m + halo:m + halo + tile] = cur_ref[0]

    pos = lax.broadcasted_iota(jnp.int32, (span, 1), 0)
    col = pos % gw
    tpos = lax.broadcasted_iota(jnp.int32, (tile, 1), 0)
    tcol = tpos % gw
    trow = i * (tile // gw) + tpos // gw
    for gi, w in enumerate(POOL_WINDOWS):
        cs = slice(gi * POOL_GROUP_DIM, (gi + 1) * POOL_GROUP_DIM)
        hw = w // 2
        s = None
        for d in range(-hw, hw):
            valid = (col + d >= 0) & (col + d < gw)
            term = jnp.where(valid, buf[m + d:m + d + span, cs], 0.0)
            s = term if s is None else s + term
        if halo:
            acc = None
            for d in range(-hw, hw):
                start = halo + d * gw
                term = s[start:start + tile]
                acc = term if acc is None else acc + term
            cnt_r = jnp.minimum(trow + hw, rows_total) - jnp.maximum(trow - hw, 0)
        else:
            acc = s
            cnt_r = 1
        cnt_c = jnp.minimum(tcol + hw, gw) - jnp.maximum(tcol - hw, 0)
        cnt = (cnt_r * cnt_c).astype(F32)
        xg = buf[m + halo:m + halo + tile, cs]
        diff = (acc / cnt - xg).astype(BF16)
        y = _dot(diff, pw_ref[gi]) * ps_ref[:, cs]
        o_ref[0, :, cs] = y.astype(BF16)


def pool_mixer(p, pool_w_bf, pool_scale, gw, tile):
    b, n, pw = p.shape
    rows_total = n // gw
    halo = MAX_HALF_WINDOW * gw if rows_total > 1 else 0
    in_specs, args = [], []
    if halo:
        r = tile // halo
        nh = n // halo
        in_specs = [
            pl.BlockSpec((1, halo, pw), lambda bi, i: (bi, jnp.maximum(i * r - 1, 0), 0)),
            pl.BlockSpec((1, tile, pw), lambda bi, i: (bi, i, 0)),
            pl.BlockSpec((1, halo, pw), lambda bi, i: (bi, jnp.minimum((i + 1) * r, nh - 1), 0)),
        ]
        args = [p, p, p]
    else:
        in_specs = [pl.BlockSpec((1, tile, pw), lambda bi, i: (bi, i, 0))]
        args = [p]
    ng = len(POOL_WINDOWS)
    in_specs += [
        pl.BlockSpec((ng, POOL_GROUP_DIM, POOL_GROUP_DIM), lambda bi, i: (0, 0, 0)),
        pl.BlockSpec((1, pw), lambda bi, i: (0, 0)),
    ]
    args += [pool_w_bf, pool_scale.reshape(1, pw)]
    return pl.pallas_call(
        functools.partial(_pool_kernel, gw=gw, tile=tile, halo=halo, rows_total=rows_total),
        out_shape=jax.ShapeDtypeStruct((b, n, pw), BF16),
        grid=(b, n // tile),
        in_specs=in_specs,
        out_specs=pl.BlockSpec((1, tile, pw), lambda bi, i: (bi, i, 0)),
        scratch_shapes=[pltpu.VMEM((tile + 2 * halo + 2 * MAX_HALF_WINDOW, pw), F32)],
        compiler_params=_cparams("parallel", "parallel"),
        name="pool_mixer",
    )(*args)


def _ret_kernel(qk_ref, v_ref, gate_ref, py_ref, x_ref, sf_ref, sb_ref, mp_ref, qdf_ref, qdb_ref,
                hm_ref, bd_ref, avg_ref, gn_ref, wo_ref, nw_ref, g1_ref, o_ref, o_buf, *, nch):
    bd = bd_ref[...]
    avg = avg_ref[...]
    lane = lax.broadcasted_iota(jnp.int32, (1, 2 * DV), 1)
    lo_mask = (lane < DV).astype(BF16)
    hi_mask = (lane >= DV).astype(BF16)
    for ci in range(nch):
        rs = slice(ci * CHUNK, (ci + 1) * CHUNK)
        q = qk_ref[0, rs, 0:QK_W]
        k = qk_ref[0, rs, QK_W:2 * QK_W]
        vv = v_ref[0, rs, :]
        inner = []
        for j in range(N_HEADS // 2):
            kp = jnp.concatenate([k * hm_ref[2 * j], k * hm_ref[2 * j + 1]], axis=0)
            sc = _dot_nt(q, kp) * mp_ref[j]
            vpair = vv[:, 2 * DV * j:2 * DV * (j + 1)]
            vp = jnp.concatenate([vpair * lo_mask, vpair * hi_mask], axis=0)
            inner.append(_dot(sc.astype(BF16), vp))
        o = jnp.concatenate(inner, axis=1)
        sf_bd = _expand_state(sf_ref[0, ci]) * bd
        sb_bd = _expand_state(sb_ref[0, ci]) * bd
        o_buf[rs, :] = o + _dot(q, sf_bd) * qdf_ref[...] + _dot(q, sb_bd) * qdb_ref[...]

    def head_mean(a):
        hi = a.astype(BF16)
        lo = (a - hi.astype(F32)).astype(BF16)
        return _dot(hi, avg) + _dot(lo, avg)

    o = o_buf[...]
    dlt = o - head_mean(o)
    var = head_mean(dlt * dlt)
    yn = dlt * lax.rsqrt(var + EPS) * gn_ref[...]
    g = gate_ref[0]
    r = (g * jax.nn.sigmoid(g) * yn).astype(BF16)
    mix = _dot(r, wo_ref[0:V_W, :]) + _dot(py_ref[0], wo_ref[V_W:V_W + POOL_W, :])
    y = mix * lax.rsqrt(jnp.mean(mix * mix, axis=-1, keepdims=True) + EPS) * nw_ref[...]
    o_ref[0] = x_ref[0] + g1_ref[0] * y


def retention_mixer(qk, v, gate, pool_y, x, sf, sb, tabs, gn, w_out_bf, norm_w, g1, tm):
    b, n, d = x.shape
    nch = tm // CHUNK
    row = lambda bi, i: (bi, i, 0)
    const = lambda shape: pl.BlockSpec(shape, lambda bi, i: (0,) * len(shape))
    return pl.pallas_call(
        functools.partial(_ret_kernel, nch=nch),
        out_shape=jax.ShapeDtypeStruct((b, n, d), F32),
        grid=(b, n // tm),
        in_specs=[
            pl.BlockSpec((1, tm, 2 * QK_W), row),
            pl.BlockSpec((1, tm, V_W), row),
            pl.BlockSpec((1, tm, V_W), row),
            pl.BlockSpec((1, tm, POOL_W), row),
            pl.BlockSpec((1, tm, d), row),
            pl.BlockSpec((1, nch, DK, V_W), lambda bi, i: (bi, i, 0, 0)),
            pl.BlockSpec((1, nch, DK, V_W), lambda bi, i: (bi, i, 0, 0)),
            const((N_HEADS // 2, CHUNK, 2 * CHUNK)),
            const((CHUNK, V_W)), const((CHUNK, V_W)),
            const((N_HEADS, 1, QK_W)),
            const((QK_W, V_W)),
            const((V_W, V_W)),
            const((1, V_W)),
            const((V_W + POOL_W, d)),
            const((1, d)),
            pl.BlockSpec((1, 1, d), lambda bi, i: (bi, 0, 0)),
        ],
        out_specs=pl.BlockSpec((1, tm, d), row),
        scratch_shapes=[pltpu.VMEM((tm, V_W), F32)],
        compiler_params=_cparams("parallel", "parallel"),
        name="retention_mixer",
    )(qk, v, gate, pool_y, x, sf, sb, tabs["mpair"], tabs["qdec_f"], tabs["qdec_b"], tabs["hmask"],
      tabs["bd_bf16"], tabs["avg"], gn.reshape(1, V_W), w_out_bf, norm_w.reshape(1, d), g1)


def _packed_width(d, dtype):
    return d * jnp.dtype(dtype).itemsize // 4


def _pack_rows(h):
    bits = lax.bitcast_convert_type(h.astype(F32), jnp.uint32)
    if h.dtype.itemsize == 4:
        return bits
    half = h.shape[1] // 2
    return (bits[:, half:] & jnp.uint32(0xFFFF0000)) | (bits[:, :half] >> 16)


def _unpack_rows(w, dtype):
    if jnp.dtype(dtype).itemsize == 4:
        return lax.bitcast_convert_type(w, dtype)
    lo = lax.bitcast_convert_type(w << 16, F32)
    hi = lax.bitcast_convert_type(w & jnp.uint32(0xFFFF0000), F32)
    return jnp.concatenate([lo, hi], axis=1).astype(dtype)


def _router_kernel(x_ref, g_ref, sh_ref, sc_ref, wr_ref, h_ref, aff_ref):
    h = _norm_mod(x_ref[0], g_ref[...], sh_ref[0], sc_ref[0])
    h_hi = h.astype(BF16)
    h_ref[0] = _pack_rows(h_hi)
    ne = wr_ref.shape[0]
    h_lo = (h - h_hi.astype(F32)).astype(BF16)
    w = wr_ref[...]
    w_hi = w.astype(BF16)
    w_lo = (w - w_hi.astype(F32)).astype(BF16)
    both = _dot_nt(jnp.concatenate([w_hi, w_lo], axis=0), h_hi)
    logits = both[0:ne] + both[ne:2 * ne] + _dot_nt(w_hi, h_lo)
    mx = jnp.max(logits, axis=0, keepdims=True)
    e = jnp.exp(logits - mx)
    aff_ref[0] = e / jnp.sum(e, axis=0, keepdims=True)


def router(x, norm_w, shift, scale, w_router_t, tm):
    b, n, d = x.shape
    ne = w_router_t.shape[0]
    return pl.pallas_call(
        _router_kernel,
        out_shape=(jax.ShapeDtypeStruct((b, n, _packed_width(d, BF16)), jnp.uint32),
                   jax.ShapeDtypeStruct((b, ne, n), F32)),
        grid=(b, n // tm),
        in_specs=[
            pl.BlockSpec((1, tm, d), lambda bi, i: (bi, i, 0)),
            pl.BlockSpec((1, d), lambda bi, i: (0, 0)),
            pl.BlockSpec((1, 1, d), lambda bi, i: (bi, 0, 0)),
            pl.BlockSpec((1, 1, d), lambda bi, i: (bi, 0, 0)),
            pl.BlockSpec((ne, d), lambda bi, i: (0, 0)),
        ],
        out_specs=(
            pl.BlockSpec((1, tm, _packed_width(d, BF16)), lambda bi, i: (bi, i, 0)),
            pl.BlockSpec((1, ne, tm), lambda bi, i: (bi, 0, i)),
        ),
        compiler_params=_cparams("parallel", "parallel"),
        name="router",
    )(x, norm_w.reshape(1, d), shift, scale, w_router_t)


SC_CORES = 2
SC_SUBCORES = 16
GATHER_CHUNK = 64


def gather_rows(table, idx):
    rows, width = idx.shape[0], table.shape[1]
    workers = SC_CORES * SC_SUBCORES
    per_worker = rows // workers
    assert rows == per_worker * workers and per_worker % GATHER_CHUNK == 0
    mesh = plsc.VectorSubcoreMesh(core_axis_name="c", subcore_axis_name="s")

    def body(table_hbm, idx_hbm, out_hbm, idx_v, rows_v, sem):
        wid = lax.axis_index("s") * SC_CORES + lax.axis_index("c")
        base = wid * per_worker

        @pl.loop(0, per_worker // GATHER_CHUNK)
        def _(i):
            off = pl.multiple_of(base + i * GATHER_CHUNK, GATHER_CHUNK)
            pltpu.sync_copy(idx_hbm.at[pl.ds(off, GATHER_CHUNK)], idx_v)
            pltpu.async_copy(table_hbm.at[idx_v], rows_v, sem).wait()
            pltpu.sync_copy(rows_v, out_hbm.at[pl.ds(off, GATHER_CHUNK)])

    return pl.kernel(
        body,
        out_type=jax.ShapeDtypeStruct((rows, width), table.dtype),
        mesh=mesh,
        scratch_types=[
            pltpu.VMEM((GATHER_CHUNK,), jnp.int32),
            pltpu.VMEM((GATHER_CHUNK, width), table.dtype),
            pltpu.SemaphoreType.DMA,
        ],
        name="gather_rows",
    )(table, idx)


FFN_CHUNKS = 8
FFN_VMEM_LIMIT = 60 * 1024 * 1024


def _ffn_kernel(x_ref, gate_ref, wg_hbm, wu_hbm, wd_hbm, o_ref, wbf, stage, sem, *, rt, layer, cps):
    e = pl.program_id(0)
    g = pl.program_id(1)
    ne = pl.num_programs(0)
    slot = e % 2
    mats = (wg_hbm, wu_hbm, wd_hbm)
    pair = wbf.shape[2] // (FFN_CHUNKS // 2)
    size_a = (pair // 2) // 16 * 16
    sizes = (size_a, pair - size_a)

    def chunk_copies(ee, half, par):
        off = pl.multiple_of(half * pair + par * size_a, 16)
        n = sizes[par]
        return off, n, [pltpu.make_async_copy(m.at[layer, ee, pl.ds(off, n), :],
                                              stage.at[i, pl.ds(0, n), :], sem.at[i])
                        for i, m in enumerate(mats)]

    def start(ee, half, par):
        for cp in chunk_copies(ee, half, par)[2]:
            cp.start()

    def finish(ee, sl, half, par):
        off, n, cps_ = chunk_copies(ee, half, par)
        for i, cp in enumerate(cps_):
            cp.wait()
            wbf[sl, i, pl.ds(off, n), :] = stage[i, 0:n, :].astype(wbf.dtype)

    def for_chunk(c, fn):
        if isinstance(c, int):
            fn(c // 2, c % 2)
        else:
            for par in (0, 1):
                pl.when(c % 2 == par)(functools.partial(fn, c // 2, par))

    @pl.when((e == 0) & (g == 0))
    def _():
        for c in range(FFN_CHUNKS):
            start(0, c // 2, c % 2)
            finish(0, 0, c // 2, c % 2)

    has_next = e + 1 < ne
    c0 = 0 if cps == FFN_CHUNKS else g * cps

    @pl.when(has_next)
    def _():
        for_chunk(c0, functools.partial(start, e + 1))

    rows = x_ref.shape[2]
    for r0 in range(0, rows, rt):
        rs = slice(r0, r0 + rt)
        xt = _unpack_rows(x_ref[0, 0, rs, :], wbf.dtype)
        a = _dot_nt(xt, wbf[slot, 0])
        u = _dot_nt(xt, wbf[slot, 1])
        hh = (a * jax.nn.sigmoid(a) * u).astype(wbf.dtype)
        gate_col = jnp.broadcast_to(gate_ref[0, 0, :, rs], (LANES, rt)).T[:, 0:1]
        o_ref[0, 0, rs, :] = (_dot(hh, wbf[slot, 2]) * gate_col).astype(o_ref.dtype)

    @pl.when(has_next)
    def _():
        for_chunk(c0, functools.partial(finish, e + 1, 1 - slot))
        for i in range(1, cps):
            for_chunk(c0 + i, functools.partial(start, e + 1))
            for_chunk(c0 + i, functools.partial(finish, e + 1, 1 - slot))


def expert_ffn(xs, gate, wg_t, wu_t, wd, layer):
    g, ne, rows, dp = xs.shape
    f, d = wd.shape[2], wd.shape[3]
    rt = min(rows, 256)
    pair = f // (FFN_CHUNKS // 2)
    assert pair * (FFN_CHUNKS // 2) == f and pair % 16 == 0 and FFN_CHUNKS % g == 0
    stage_rows = pair - (pair // 2) // 16 * 16
    any_spec = pl.BlockSpec(memory_space=pl.ANY)
    return pl.pallas_call(
        functools.partial(_ffn_kernel, rt=rt, layer=layer, cps=FFN_CHUNKS // g),
        out_shape=jax.ShapeDtypeStruct((g, ne, rows, d), BF16),
        grid=(ne, g),
        in_specs=[
            pl.BlockSpec((1, 1, rows, dp), lambda e, gi: (gi, e, 0, 0)),
            pl.BlockSpec((1, 1, 1, rows), lambda e, gi: (gi, e, 0, 0)),
            any_spec, any_spec, any_spec,
        ],
        out_specs=pl.BlockSpec((1, 1, rows, d), lambda e, gi: (gi, e, 0, 0)),
        scratch_shapes=[
            pltpu.VMEM((2, 3, f, d), BF16),
            pltpu.VMEM((3, stage_rows, d), F32),
            pltpu.SemaphoreType.DMA((3,)),
        ],
        compiler_params=pltpu.CompilerParams(dimension_semantics=("arbitrary", "arbitrary"),
                                             vmem_limit_bytes=FFN_VMEM_LIMIT),
        name="expert_ffn",
    )(xs, gate, wg_t, wu_t, wd)


ROUTE_WIN = 64
LANES = 128
VAL_ROWS = 8


def _combine_kernel(st_ref, rank_ref, y_hbm, x_ref, nw_ref, g_ref, o_ref, buf, xbuf, sem, xsem,
                    acc_ref, *, tt, ne, cap, merged):
    b = pl.program_id(0)
    k = pl.program_id(1)
    nk = pl.num_programs(1)
    step = b * nk + k
    slot = step % 2
    rows_total = y_hbm.shape[2]
    win = ROUTE_WIN
    g = 0 if merged else b
    base = b * cap if merged else 0
    lane = lax.broadcasted_iota(jnp.int32, (1, LANES), 1)

    def tile_bounds(bb, kk, e):
        off = bb * cap if merged else 0
        lo = off + st_ref[(bb * (nk + 1) + kk) * ne + e]
        hi = off + st_ref[(bb * (nk + 1) + kk + 1) * ne + e]
        return lo, hi, jnp.minimum((lo // 16) * 16, rows_total - win)

    def bounds(e):
        return tile_bounds(b, k, e)

    def win_copy(bb, e, a0, sl):
        src = y_hbm.at[0 if merged else bb, e, pl.ds(pl.multiple_of(a0, 16), win), :]
        return pltpu.make_async_copy(src, buf.at[sl, pl.ds(e * win, win), :], sem.at[sl, e])

    def start_tile(bb, kk, sl):
        for e in range(ne):
            win_copy(bb, e, tile_bounds(bb, kk, e)[2], sl).start()

    @pl.when(step == 0)
    def _():
        start_tile(b, k, slot)

    nxt = step + 1

    @pl.when(nxt < pl.num_programs(0) * nk)
    def _():
        start_tile(nxt // nk, nxt % nk, 1 - slot)

    rkf = rank_ref[0].astype(F32)
    rk_t = jnp.concatenate([rkf, jnp.full((LANES - ne, tt), -1.0, F32)], axis=0).T
    rk_t = rk_t.astype(jnp.int32)

    def target(e, origin):
        col = rk_t[:, e:e + 1]
        return jnp.where(col >= 0, col + (base - origin), -1)

    pieces = []
    for j in range(ne // 2):
        t_even = target(2 * j, bounds(2 * j)[2])
        t_odd = target(2 * j + 1, bounds(2 * j + 1)[2])
        t_odd = jnp.where((t_odd >= 0) & (t_odd < win), t_odd + win, -1)
        pieces.append((jnp.where(lane < win, t_even, t_odd) == lane).astype(BF16))
    p = jnp.concatenate(pieces, axis=1)
    for e in range(ne):
        win_copy(b, e, bounds(e)[2], slot).wait()
    acc_ref[...] = _dot(p, buf[slot])

    for e in range(ne):
        lo, hi, a0 = bounds(e)

        def extra(w, carry, e=e, a0=a0):
            start = a0 + w * win
            aw = pl.multiple_of(jnp.minimum(start, rows_total - win), 16)
            cp = pltpu.make_async_copy(y_hbm.at[g, e, pl.ds(aw, win), :], xbuf, xsem)
            cp.start()
            cp.wait()
            col = rk_t[:, e:e + 1]
            ok = (col >= 0) & (col + base >= start)
            px = (jnp.where(ok, col + (base - aw), -1) == lane[:, 0:win]).astype(BF16)
            acc_ref[...] += _dot(px, xbuf[...])
            return carry

        lax.fori_loop(1, (hi - a0 + win - 1) // win, extra, 0)

    y = acc_ref[...]
    yn = y * lax.rsqrt(jnp.mean(y * y, axis=-1, keepdims=True) + EPS) * nw_ref[...]
    o_ref[0] = x_ref[0] + g_ref[0] * yn


def combine_post(starts, rank_t, y, x, norm_w, g, tt, cap, merged):
    b, n, d = x.shape
    _, ne, rows, _ = y.shape
    grid_spec = pltpu.PrefetchScalarGridSpec(
        num_scalar_prefetch=1,
        grid=(b, n // tt),
        in_specs=[
            pl.BlockSpec((1, ne, tt), lambda bi, k, st: (bi, 0, k)),
            pl.BlockSpec(memory_space=pl.ANY),
            pl.BlockSpec((1, tt, d), lambda bi, k, st: (bi, k, 0)),
            pl.BlockSpec((1, d), lambda bi, k, st: (0, 0)),
            pl.BlockSpec((1, 1, d), lambda bi, k, st: (bi, 0, 0)),
        ],
        out_specs=pl.BlockSpec((1, tt, d), lambda bi, k, st: (bi, k, 0)),
        scratch_shapes=[
            pltpu.VMEM((2, ne * ROUTE_WIN, d), BF16),
            pltpu.VMEM((ROUTE_WIN, d), BF16),
            pltpu.SemaphoreType.DMA((2, ne)),
            pltpu.SemaphoreType.DMA(()),
            pltpu.VMEM((tt, d), F32),
        ],
    )
    return pl.pallas_call(
        functools.partial(_combine_kernel, tt=tt, ne=ne, cap=cap, merged=merged),
        out_shape=jax.ShapeDtypeStruct((b, n, d), F32),
        grid_spec=grid_spec,
        compiler_params=_cparams("arbitrary", "arbitrary"),
        name="combine_post",
    )(starts, rank_t, y, x, norm_w.reshape(1, d), g)


def _select_kernel(aff_ref, tri_ref, rank_ref, st_ref, *, cap, tt):
    a = aff_ref[0]
    ne, n = a.shape
    bits = lax.bitcast_convert_type(a, jnp.int32)

    def search(_, c):
        lo, hi = c
        mid = lo + ((hi - lo) >> 1)
        cnt = jnp.sum((bits >= mid).astype(F32), axis=1, keepdims=True)
        ge = cnt >= cap
        return jnp.where(ge, mid, lo), jnp.where(ge, hi, mid)

    lo0 = jnp.zeros((ne, 1), jnp.int32)
    hi0 = jnp.full((ne, 1), 0x7F800000, jnp.int32)
    thr, _ = lax.fori_loop(0, 31, search, (lo0, hi0))
    gt = bits > thr
    eq = bits == thr
    need = cap - jnp.sum(gt.astype(F32), axis=1, keepdims=True)
    m = jnp.concatenate([gt, eq], axis=0).astype(BF16)
    tri = tri_ref[...]
    lane = lax.broadcasted_iota(jnp.int32, (1, LANES), 1)
    off = jnp.zeros((2 * ne, 1), F32)
    st = jnp.zeros((ne, LANES), jnp.int32)
    for j in range(n // LANES):
        cs = slice(j * LANES, (j + 1) * LANES)
        if (j * LANES) % tt == 0:
            off_sel = off[:ne] + jnp.minimum(off[ne:], need)
            st = jnp.where(lane == (j * LANES) // tt, off_sel.astype(jnp.int32), st)
        mj = m[:, cs]
        pj = _dot(mj, tri) + off
        pe = pj[ne:]
        sel = gt[:, cs] | (eq[:, cs] & (pe < need))
        rank_ref[0, :, cs] = jnp.where(sel, pj[:ne] + jnp.minimum(pe, need), -1.0).astype(jnp.int32)
        off = off + jnp.sum(mj.astype(F32), axis=1, keepdims=True)
    st_ref[0] = jnp.where(lane == n // tt, cap, st)


def route_select(aff_t, cap, tt):
    b, ne, n = aff_t.shape
    tri = jnp.asarray(np.triu(np.ones((LANES, LANES), np.float32), 1), BF16)
    return pl.pallas_call(
        functools.partial(_select_kernel, cap=cap, tt=tt),
        out_shape=(jax.ShapeDtypeStruct((b, ne, n), jnp.int32),
                   jax.ShapeDtypeStruct((b, ne, LANES), jnp.int32)),
        grid=(b,),
        in_specs=[pl.BlockSpec((1, ne, n), lambda bi: (bi, 0, 0)),
                  pl.BlockSpec((LANES, LANES), lambda bi: (0, 0))],
        out_specs=(pl.BlockSpec((1, ne, n), lambda bi: (bi, 0, 0)),
                   pl.BlockSpec((1, ne, LANES), lambda bi: (bi, 0, 0))),
        compiler_params=_cparams("parallel"),
        name="route_select",
    )(aff_t, tri)


def _compact_kernel(st_ref, rank_ref, aff_ref, idx_ref, gate_ref, out_ref, *, tt, ne, cap, n):
    b = pl.program_id(0)
    k = pl.program_id(1)
    nk = pl.num_programs(1)
    win = ROUTE_WIN

    @pl.when(k == 0)
    def _():
        out_ref[...] = jnp.zeros_like(out_ref)

    rk = rank_ref[0]
    a = aff_ref[0]
    g1 = a.astype(BF16).astype(F32)
    r1 = a - g1
    g2 = r1.astype(BF16).astype(F32)
    g3 = r1 - g2
    tok = k * tt + lax.broadcasted_iota(jnp.int32, (1, tt), 1)
    ids = jnp.concatenate([(tok >> 6).astype(F32), (tok & 63).astype(F32),
                           jnp.zeros((VAL_ROWS - 2, tt), F32)], axis=0)
    pad = jnp.zeros((LANES - VAL_ROWS - 3 * ne, tt), F32)
    payload = jnp.concatenate([ids, g1, g2, g3, pad], axis=0).astype(BF16)
    sub = lax.broadcasted_iota(jnp.int32, (win, 1), 0)

    def bounds(e):
        lo = st_ref[(b * (nk + 1) + k) * ne + e]
        hi = st_ref[(b * (nk + 1) + k + 1) * ne + e]
        return hi, (lo // 8) * 8

    def one_hot(e, aw):
        return ((rk[e:e + 1, :] - aw) == sub).astype(BF16)

    p_all = jnp.concatenate([one_hot(e, bounds(e)[1]) for e in range(ne)], axis=0)
    moved = _dot_nt(p_all, payload)
    for e in range(ne):
        a0 = pl.multiple_of(bounds(e)[1], 8)
        out_ref[e, pl.ds(a0, win), :] += moved[e * win:(e + 1) * win]

    for e in range(ne):
        hi, a0 = bounds(e)

        def window(w, carry, e=e, a0=a0):
            aw = pl.multiple_of(a0 + w * win, 8)
            out_ref[e, pl.ds(aw, win), :] += _dot_nt(one_hot(e, aw), payload)
            return carry

        lax.fori_loop(1, (hi - a0 + win - 1) // win, window, 0)

    @pl.when(k == nk - 1)
    def _():
        for e in range(ne):
            t = out_ref[e].T
            ids_e = (t[0:1] * 64.0 + t[1:2]).astype(jnp.int32) + b * n
            g_e = (t[VAL_ROWS + e:VAL_ROWS + e + 1] + t[VAL_ROWS + ne + e:VAL_ROWS + ne + e + 1]) \
                + t[VAL_ROWS + 2 * ne + e:VAL_ROWS + 2 * ne + e + 1]
            idx_ref[0, e:e + 1, :] = ids_e[:, 0:cap]
            gate_ref[0, e:e + 1, :] = g_e[:, 0:cap]


def route_compact(starts, rank_t, aff_t, cap, tt):
    b, ne, n = aff_t.shape
    rows = -(-(cap + ROUTE_WIN) // LANES) * LANES
    grid_spec = pltpu.PrefetchScalarGridSpec(
        num_scalar_prefetch=1,
        grid=(b, n // tt),
        in_specs=[pl.BlockSpec((1, ne, tt), lambda bi, k, st: (bi, 0, k)),
                  pl.BlockSpec((1, ne, tt), lambda bi, k, st: (bi, 0, k))],
        out_specs=(pl.BlockSpec((1, ne, cap), lambda bi, k, st: (bi, 0, 0)),
                   pl.BlockSpec((1, ne, cap), lambda bi, k, st: (bi, 0, 0))),
        scratch_shapes=[pltpu.VMEM((ne, rows, LANES), F32)],
    )
    return pl.pallas_call(
        functools.partial(_compact_kernel, tt=tt, ne=ne, cap=cap, n=n),
        out_shape=(jax.ShapeDtypeStruct((b, ne, cap), jnp.int32),
                   jax.ShapeDtypeStruct((b, ne, cap), F32)),
        grid_spec=grid_spec,
        compiler_params=_cparams("parallel", "arbitrary"),
        name="route_compact",
    )(starts, rank_t, aff_t)


def _qk_head_of_lane():
    half = QK_W // 2
    return (np.arange(QK_W) % half) // (DK // 2)


def _in_proj_perm():
    half = DK // 2
    first = [h * DK + i for h in range(N_HEADS) for i in range(half)]
    second = [h * DK + half + i for h in range(N_HEADS) for i in range(half)]
    qperm = np.array(first + second)
    rest = np.arange(2 * QK_W, 2 * QK_W + 2 * V_W + POOL_W)
    return np.concatenate([qperm, QK_W + qperm, rest])


def _static_tables():
    qk_head = _qk_head_of_lane()
    v_head = np.arange(V_W) // DV
    hmask = (qk_head[None, :] == np.arange(N_HEADS)[:, None])
    bd = (qk_head[:, None] == v_head[None, :])
    avg = (v_head[:, None] == v_head[None, :]).astype(np.float32) / DV
    return {
        "hmask": jnp.asarray(hmask[:, None, :], BF16),
        "bd_f32": jnp.asarray(bd, F32),
        "bd_bf16": jnp.asarray(bd, BF16),
        "avg": jnp.asarray(avg, BF16),
    }


def _decay_tables(lg_f, lg_b):
    pos = jnp.arange(CHUNK, dtype=F32)
    diff = pos[:, None] - pos[None, :]
    low = diff >= 0
    up = diff < 0
    m_f = jnp.where(low, jnp.exp(lg_f[:, None, None] * jnp.where(low, diff, 0.0)), 0.0)
    m_b = jnp.where(up, jnp.exp(lg_b[:, None, None] * jnp.where(up, -diff, 0.0)), 0.0)
    m = m_f + m_b
    mpair = m.reshape(N_HEADS // 2, 2, CHUNK, CHUNK).transpose(0, 2, 1, 3).reshape(
        N_HEADS // 2, CHUNK, 2 * CHUNK)
    qk_head = _qk_head_of_lane()
    v_head = np.arange(V_W) // DV
    qdec_f = jnp.exp(lg_f[None, :] * (pos[:, None] + 1.0))[:, v_head]
    qdec_b = jnp.exp(lg_b[None, :] * (CHUNK - pos[:, None]))[:, v_head]
    kdec_f = jnp.exp(lg_f[None, :] * (CHUNK - 1.0 - pos[:, None]))[:, qk_head]
    kdec_b = jnp.exp(lg_b[None, :] * pos[:, None])[:, qk_head]
    cdec_f = jnp.exp(lg_f * CHUNK)[None, v_head]
    cdec_b = jnp.exp(lg_b * CHUNK)[None, v_head]
    return {"mpair": mpair, "qdec_f": qdec_f, "qdec_b": qdec_b, "kdec_f": kdec_f,
            "kdec_b": kdec_b, "cdec_f": cdec_f, "cdec_b": cdec_b}


def _rope_tables(n):
    t = jnp.arange(n)
    row = (t // GRID_W).astype(F32)
    col = (t % GRID_W).astype(F32)
    n_freq = DK // 4
    inv = ROPE_BASE ** (-jnp.arange(n_freq, dtype=F32) / n_freq)
    ang = jnp.concatenate([row[:, None] * inv, col[:, None] * inv], axis=-1)
    return jnp.tile(jnp.cos(ang), (1, N_HEADS)), jnp.tile(jnp.sin(ang), (1, N_HEADS))


def _moe_residual(x, h_bf, aff_t, wg, wu, wd, layer, norm_w, g, merged):
    b, n, d = h_bf.shape
    cap = EC_FACTOR * n // N_EXPERTS
    ne = N_EXPERTS
    tt = min(n, 256)
    nk = n // tt
    rank_t, st = route_select(aff_t, cap, tt)
    starts = st[:, :, :nk + 1].transpose(0, 2, 1).reshape(-1)
    flat, gate = route_compact(starts, rank_t, aff_t, cap, tt)
    if merged:
        flat = flat.transpose(1, 0, 2)
        gate = gate.transpose(1, 0, 2).reshape(1, ne, b * cap)
    xs = gather_rows(h_bf.reshape(b * n, d), flat.reshape(-1))
    xs = xs.reshape(gate.shape + (d,))
    y = expert_ffn(xs, gate[:, :, None, :], wg, wu, wd, layer)
    return combine_post(starts, rank_t, y, x, norm_w, g, tt, cap, merged)


def kernel(x, c, ctx, c_ctx, w_ada, b_ada, norm_pre_mix, norm_post_mix, norm_pre_ffn, norm_post_ffn, w_in, ret_decay_fwd, ret_decay_bwd, ret_gn, pool_w, pool_scale, w_out, w_router, w_gate, w_up, w_down):
    b, n, d = x.shape
    lc = ctx.shape[1]
    depth = w_ada.shape[0]
    rope = _rope_tables(n)
    static = _static_tables()
    perm = _in_proj_perm()

    cc = jnp.concatenate([c, c_ctx[None, :], jnp.zeros((7, d), F32)], axis=0)
    mods = ada_modulation(cc, w_ada, b_ada)
    wg, wu, wd = jnp.swapaxes(w_gate, 2, 3), jnp.swapaxes(w_up, 2, 3), w_down

    tm = 512
    for l in range(depth):
        last = l == depth - 1
        mx = mods[l, :b].reshape(b, 1, 6, d)
        sh1, sc1, g1, sh2, sc2, g2 = [mx[:, :, i] for i in range(6)]
        mc = jnp.broadcast_to(mods[l, b].reshape(1, 1, 6, d), (b, 1, 6, d))
        csh1, csc1, cg1, csh2, csc2, cg2 = [mc[:, :, i] for i in range(6)]
        lg_f = jax.nn.log_sigmoid(ret_decay_fwd[l].astype(F32))
        lg_b = jax.nn.log_sigmoid(ret_decay_bwd[l].astype(F32))
        tabs = dict(static, **_decay_tables(lg_f, lg_b))
        w_in_p = w_in[l][:, perm].astype(BF16)
        w_out_bf = w_out[l].astype(BF16)
        pool_w_bf = pool_w[l].astype(BF16)

        qk_c, v_c, gate_c, p_c = premix(ctx, norm_pre_mix[l], csh1, csc1, w_in_p, None, lc)
        zero = jnp.zeros((b, DK, V_W), F32)
        sf_c, sb_c, s_f, s_b = state_scan(qk_c, v_c, zero, zero, tabs)
        qk_x, v_x, gate_x, p_x = premix(x, norm_pre_mix[l], sh1, sc1, w_in_p, rope, tm)
        sf_x, sb_x, _, _ = state_scan(qk_x, v_x, s_f, s_b, tabs)
        pool_x = pool_mixer(p_x, pool_w_bf, pool_scale[l], GRID_W, min(n, 2048))
        x = retention_mixer(qk_x, v_x, gate_x, pool_x, x, sf_x, sb_x, tabs, ret_gn[l], w_out_bf,
                            norm_post_mix[l], g1, tm)
        if not last:
            pool_c = pool_mixer(p_c, pool_w_bf, pool_scale[l], lc, lc)
            ctx = retention_mixer(qk_c, v_c, gate_c, pool_c, ctx, sf_c, sb_c, tabs, ret_gn[l],
                                  w_out_bf, norm_post_mix[l], cg1, lc)

        wr_t = w_router[l].T
        h_x, aff_x = router(x, norm_pre_ffn[l], sh2, sc2, wr_t, tm)
        x = _moe_residual(x, h_x, aff_x, wg, wu, wd, l, norm_post_ffn[l], g2, False)
        if not last:
            h_c, aff_c = router(ctx, norm_pre_ffn[l], csh2, csc2, wr_t, lc)
            ctx = _moe_residual(ctx, h_c, aff_c, wg, wu, wd, l, norm_post_ffn[l], cg2, True)
    return x
```

```python
import functools

import jax
import jax.numpy as jnp
import numpy as np
from jax import lax
from jax.experimental import pallas as pl
from jax.experimental.pallas import tpu as pltpu
from jax.experimental.pallas import tpu_sc as plsc

F32 = jnp.float32
BF16 = jnp.bfloat16

D_MODEL = 1024
GRID_W = 64
N_HEADS = 8
DV = 64
DK = 32
QK_W = N_HEADS * DK
V_W = N_HEADS * DV
POOL_W = 512
POOL_WINDOWS = (2, 4, 8, 16)
POOL_GROUP_DIM = 128
CHUNK = 128
ROPE_BASE = 10000.0
N_EXPERTS = 16
EC_FACTOR = 2
EPS = 1e-6
MAX_HALF_WINDOW = max(POOL_WINDOWS) // 2

VMEM_LIMIT = 56 * 1024 * 1024


def _cparams(*sem):
    return pltpu.CompilerParams(dimension_semantics=sem, vmem_limit_bytes=VMEM_LIMIT)


def _dot(a, b):
    return jnp.dot(a, b, preferred_element_type=F32)


def _dot_nt(a, b, precision=None):
    return lax.dot_general(a, b, (((1,), (1,)), ((), ())), precision=precision,
                           preferred_element_type=F32)


def _ada_kernel(cc_ref, w_ref, b_ref, o_ref):
    s = cc_ref[...]
    s = s * jax.nn.sigmoid(s)
    o_ref[0] = _dot(s.astype(BF16), w_ref[0].astype(BF16)) + b_ref[0]


def ada_modulation(cc, w_ada, b_ada):
    depth, d, d6 = w_ada.shape
    rows = cc.shape[0]
    tn = 1536
    return pl.pallas_call(
        _ada_kernel,
        out_shape=jax.ShapeDtypeStruct((depth, rows, d6), F32),
        grid=(depth, d6 // tn),
        in_specs=[
            pl.BlockSpec((rows, d), lambda l, j: (0, 0)),
            pl.BlockSpec((1, d, tn), lambda l, j: (l, 0, j)),
            pl.BlockSpec((1, 1, tn), lambda l, j: (l, 0, j)),
        ],
        out_specs=pl.BlockSpec((1, rows, tn), lambda l, j: (l, 0, j)),
        compiler_params=_cparams("parallel", "parallel"),
        name="ada_modulation",
    )(cc, w_ada, b_ada.reshape(depth, 1, d6))


def _norm_mod(xf, g, sh, sc):
    y = xf * lax.rsqrt(jnp.mean(xf * xf, axis=-1, keepdims=True) + EPS)
    return (y * g) * (1.0 + sc) + sh


def _premix_kernel(x_ref, g_ref, sh_ref, sc_ref, w_ref, *rest, rope):
    if rope:
        cos_ref, sin_ref, qk_ref, v_ref, gate_ref, p_ref = rest
    else:
        qk_ref, v_ref, gate_ref, p_ref = rest
    hb = _norm_mod(x_ref[0], g_ref[...], sh_ref[0], sc_ref[0]).astype(BF16)
    zqk = _dot(hb, w_ref[:, 0:2 * QK_W])
    half = QK_W // 2
    q1, q2 = zqk[:, 0:half], zqk[:, half:2 * half]
    k1 = zqk[:, 2 * half:3 * half] * (DK ** -0.5)
    k2 = zqk[:, 3 * half:4 * half] * (DK ** -0.5)
    if rope:
        cos, sin = cos_ref[...], sin_ref[...]
        q1, q2 = q1 * cos - q2 * sin, q1 * sin + q2 * cos
        k1, k2 = k1 * cos - k2 * sin, k1 * sin + k2 * cos
    qk_ref[0] = jnp.concatenate([q1, q2, k1, k2], axis=1).astype(BF16)
    o = 2 * QK_W
    v_ref[0] = _dot(hb, w_ref[:, o:o + V_W]).astype(BF16)
    gate_ref[0] = _dot(hb, w_ref[:, o + V_W:o + 2 * V_W])
    p_ref[0] = _dot(hb, w_ref[:, o + 2 * V_W:o + 2 * V_W + POOL_W])


def premix(x, norm_w, shift, scale, w_in_p, rope, tm):
    b, n, d = x.shape
    in_w = w_in_p.shape[1]
    row = lambda bi, i: (bi, i, 0)
    in_specs = [
        pl.BlockSpec((1, tm, d), row),
        pl.BlockSpec((1, d), lambda bi, i: (0, 0)),
        pl.BlockSpec((1, 1, d), lambda bi, i: (bi, 0, 0)),
        pl.BlockSpec((1, 1, d), lambda bi, i: (bi, 0, 0)),
        pl.BlockSpec((d, in_w), lambda bi, i: (0, 0)),
    ]
    args = [x, norm_w.reshape(1, d), shift, scale, w_in_p]
    if rope is not None:
        half = QK_W // 2
        in_specs += [pl.BlockSpec((tm, half), lambda bi, i: (i, 0))] * 2
        args += list(rope)
    return pl.pallas_call(
        functools.partial(_premix_kernel, rope=rope is not None),
        out_shape=(
            jax.ShapeDtypeStruct((b, n, 2 * QK_W), BF16),
            jax.ShapeDtypeStruct((b, n, V_W), BF16),
            jax.ShapeDtypeStruct((b, n, V_W), F32),
            jax.ShapeDtypeStruct((b, n, POOL_W), F32),
        ),
        grid=(b, n // tm),
        in_specs=in_specs,
        out_specs=(
            pl.BlockSpec((1, tm, 2 * QK_W), row),
            pl.BlockSpec((1, tm, V_W), row),
            pl.BlockSpec((1, tm, V_W), row),
            pl.BlockSpec((1, tm, POOL_W), row),
        ),
        compiler_params=_cparams("parallel", "parallel"),
        name="premix",
    )(*args)


def _fold_state(s):
    g = 2 * N_HEADS
    top = s[0:g]
    bot = s[QK_W // 2:QK_W // 2 + g]
    for h in range(1, N_HEADS):
        top = top + s[h * g:(h + 1) * g]
        bot = bot + s[QK_W // 2 + h * g:QK_W // 2 + (h + 1) * g]
    return jnp.concatenate([top, bot], axis=0)


def _expand_state(c):
    g = 2 * N_HEADS
    return jnp.concatenate([c[0:g]] * N_HEADS + [c[g:2 * g]] * N_HEADS, axis=0)


def _state_kernel(kf_ref, vf_ref, kb_ref, vb_ref, s0f_ref, s0b_ref, kdf_ref, kdb_ref,
                  cdf_ref, cdb_ref, bd_ref, sf_out, sb_out, ff_out, fb_out, sf_acc, sb_acc, *, cps):
    c = pl.program_id(1)
    nc = pl.num_programs(1)
    bd = bd_ref[...]

    @pl.when(c == 0)
    def _():
        sf_acc[...] = _expand_state(s0f_ref[0]) * bd
        sb_acc[...] = _expand_state(s0b_ref[0]) * bd

    def update(s, k, v, kd_ref, cd_ref):
        kd = k.astype(F32) * kd_ref[...]
        f = _dot(kd.T.astype(BF16), v)
        return s * cd_ref[...] + f * bd

    sf = sf_acc[...]
    sb = sb_acc[...]
    for i in range(cps):
        j = cps - 1 - i
        fs = slice(i * CHUNK, (i + 1) * CHUNK)
        bs = slice(j * CHUNK, (j + 1) * CHUNK)
        sf_out[0, i] = _fold_state(sf).astype(BF16)
        sb_out[0, j] = _fold_state(sb).astype(BF16)
        sf = update(sf, kf_ref[0, fs, :], vf_ref[0, fs, :], kdf_ref, cdf_ref)
        sb = update(sb, kb_ref[0, bs, :], vb_ref[0, bs, :], kdb_ref, cdb_ref)
    sf_acc[...] = sf
    sb_acc[...] = sb

    @pl.when(c == nc - 1)
    def _():
        ff_out[0] = _fold_state(sf)
        fb_out[0] = _fold_state(sb)


def state_scan(qk, v, s0f, s0b, tabs):
    b, n, _ = qk.shape
    nc = n // CHUNK
    cps = min(nc, 4)
    ns = nc // cps
    rows = cps * CHUNK
    const = lambda shape: pl.BlockSpec(shape, lambda bi, c: (0,) * len(shape))
    return pl.pallas_call(
        functools.partial(_state_kernel, cps=cps),
        out_shape=(
            jax.ShapeDtypeStruct((b, nc, DK, V_W), BF16),
            jax.ShapeDtypeStruct((b, nc, DK, V_W), BF16),
            jax.ShapeDtypeStruct((b, DK, V_W), F32),
            jax.ShapeDtypeStruct((b, DK, V_W), F32),
        ),
        grid=(b, ns),
        in_specs=[
            pl.BlockSpec((1, rows, QK_W), lambda bi, c: (bi, c, 1)),
            pl.BlockSpec((1, rows, V_W), lambda bi, c: (bi, c, 0)),
            pl.BlockSpec((1, rows, QK_W), lambda bi, c: (bi, ns - 1 - c, 1)),
            pl.BlockSpec((1, rows, V_W), lambda bi, c: (bi, ns - 1 - c, 0)),
            pl.BlockSpec((1, DK, V_W), lambda bi, c: (bi, 0, 0)),
            pl.BlockSpec((1, DK, V_W), lambda bi, c: (bi, 0, 0)),
            const((CHUNK, QK_W)), const((CHUNK, QK_W)),
            const((1, V_W)), const((1, V_W)),
            const((QK_W, V_W)),
        ],
        out_specs=(
            pl.BlockSpec((1, cps, DK, V_W), lambda bi, c: (bi, c, 0, 0)),
            pl.BlockSpec((1, cps, DK, V_W), lambda bi, c: (bi, ns - 1 - c, 0, 0)),
            pl.BlockSpec((1, DK, V_W), lambda bi, c: (bi, 0, 0)),
            pl.BlockSpec((1, DK, V_W), lambda bi, c: (bi, 0, 0)),
        ),
        scratch_shapes=[pltpu.VMEM((QK_W, V_W), F32), pltpu.VMEM((QK_W, V_W), F32)],
        compiler_params=_cparams("parallel", "arbitrary"),
        name="state_scan",
    )(qk, v, qk, v, s0f, s0b, tabs["kdec_f"], tabs["kdec_b"], tabs["cdec_f"], tabs["cdec_b"],
      tabs["bd_f32"])


def _pool_kernel(*refs, gw, tile, halo, rows_total):
    if halo:
        prev_ref, cur_ref, next_ref, pw_ref, ps_ref, o_ref, buf = refs
    else:
        cur_ref, pw_ref, ps_ref, o_ref, buf = refs
    i = pl.program_id(1)
    last = pl.num_programs(1) - 1
    m = MAX_HALF_WINDOW
    span = tile + 2 * halo
    zeros_m = jnp.zeros((m, POOL_W), F32)
    buf[0:m] = zeros_m
    buf[m + span:2 * m + span] = zeros_m
    if halo:
        buf[m:m + halo] = jnp.where(i > 0, prev_ref[0], 0.0)
        buf[m + halo + tile:m + span] = jnp.where(i < last, next_ref[0], 0.0)
    buf[m + halo:m + halo + tile] = cur_ref[0]

    pos = lax.broadcasted_iota(jnp.int32, (span, 1), 0)
    col = pos % gw
    tpos = lax.broadcasted_iota(jnp.int32, (tile, 1), 0)
    tcol = tpos % gw
    trow = i * (tile // gw) + tpos // gw
    for gi, w in enumerate(POOL_WINDOWS):
        cs = slice(gi * POOL_GROUP_DIM, (gi + 1) * POOL_GROUP_DIM)
        hw = w // 2
        s = None
        for d in range(-hw, hw):
            valid = (col + d >= 0) & (col + d < gw)
            term = jnp.where(valid, buf[m + d:m + d + span, cs], 0.0)
            s = term if s is None else s + term
        if halo:
            acc = None
            for d in range(-hw, hw):
                start = halo + d * gw
                term = s[start:start + tile]
                acc = term if acc is None else acc + term
            cnt_r = jnp.minimum(trow + hw, rows_total) - jnp.maximum(trow - hw, 0)
        else:
            acc = s
            cnt_r = 1
        cnt_c = jnp.minimum(tcol + hw, gw) - jnp.maximum(tcol - hw, 0)
        cnt = (cnt_r * cnt_c).astype(F32)
        xg = buf[m + halo:m + halo + tile, cs]
        diff = (acc / cnt - xg).astype(BF16)
        y = _dot(diff, pw_ref[gi]) * ps_ref[:, cs]
        o_ref[0, :, cs] = y.astype(BF16)


def pool_mixer(p, pool_w_bf, pool_scale, gw, tile):
    b, n, pw = p.shape
    rows_total = n // gw
    halo = MAX_HALF_WINDOW * gw if rows_total > 1 else 0
    in_specs, args = [], []
    if halo:
        r = tile // halo
        nh = n // halo
        in_specs = [
            pl.BlockSpec((1, halo, pw), lambda bi, i: (bi, jnp.maximum(i * r - 1, 0), 0)),
            pl.BlockSpec((1, tile, pw), lambda bi, i: (bi, i, 0)),
            pl.BlockSpec((1, halo, pw), lambda bi, i: (bi, jnp.minimum((i + 1) * r, nh - 1), 0)),
        ]
        args = [p, p, p]
    else:
        in_specs = [pl.BlockSpec((1, tile, pw), lambda bi, i: (bi, i, 0))]
        args = [p]
    ng = len(POOL_WINDOWS)
    in_specs += [
        pl.BlockSpec((ng, POOL_GROUP_DIM, POOL_GROUP_DIM), lambda bi, i: (0, 0, 0)),
        pl.BlockSpec((1, pw), lambda bi, i: (0, 0)),
    ]
    args += [pool_w_bf, pool_scale.reshape(1, pw)]
    return pl.pallas_call(
        functools.partial(_pool_kernel, gw=gw, tile=tile, halo=halo, rows_total=rows_total),
        out_shape=jax.ShapeDtypeStruct((b, n, pw), BF16),
        grid=(b, n // tile),
        in_specs=in_specs,
        out_specs=pl.BlockSpec((1, tile, pw), lambda bi, i: (bi, i, 0)),
        scratch_shapes=[pltpu.VMEM((tile + 2 * halo + 2 * MAX_HALF_WINDOW, pw), F32)],
        compiler_params=_cparams("parallel", "parallel"),
        name="pool_mixer",
    )(*args)


def _ret_kernel(qk_ref, v_ref, gate_ref, py_ref, x_ref, sf_ref, sb_ref, mp_ref, qdf_ref, qdb_ref,
                hm_ref, bd_ref, avg_ref, gn_ref, wo_ref, nw_ref, g1_ref, o_ref, o_buf, *, nch):
    bd = bd_ref[...]
    avg = avg_ref[...]
    lane = lax.broadcasted_iota(jnp.int32, (1, 2 * DV), 1)
    lo_mask = (lane < DV).astype(BF16)
    hi_mask = (lane >= DV).astype(BF16)
    for ci in range(nch):
        rs = slice(ci * CHUNK, (ci + 1) * CHUNK)
        q = qk_ref[0, rs, 0:QK_W]
        k = qk_ref[0, rs, QK_W:2 * QK_W]
        vv = v_ref[0, rs, :]
        inner = []
        for j in range(N_HEADS // 2):
            kp = jnp.concatenate([k * hm_ref[2 * j], k * hm_ref[2 * j + 1]], axis=0)
            sc = _dot_nt(q, kp) * mp_ref[j]
            vpair = vv[:, 2 * DV * j:2 * DV * (j + 1)]
            vp = jnp.concatenate([vpair * lo_mask, vpair * hi_mask], axis=0)
            inner.append(_dot(sc.astype(BF16), vp))
        o = jnp.concatenate(inner, axis=1)
        sf_bd = _expand_state(sf_ref[0, ci]) * bd
        sb_bd = _expand_state(sb_ref[0, ci]) * bd
        o_buf[rs, :] = o + _dot(q, sf_bd) * qdf_ref[...] + _dot(q, sb_bd) * qdb_ref[...]

    def head_mean(a):
        hi = a.astype(BF16)
        lo = (a - hi.astype(F32)).astype(BF16)
        return _dot(hi, avg) + _dot(lo, avg)

    o = o_buf[...]
    dlt = o - head_mean(o)
    var = head_mean(dlt * dlt)
    yn = dlt * lax.rsqrt(var + EPS) * gn_ref[...]
    g = gate_ref[0]
    r = (g * jax.nn.sigmoid(g) * yn).astype(BF16)
    mix = _dot(r, wo_ref[0:V_W, :]) + _dot(py_ref[0], wo_ref[V_W:V_W + POOL_W, :])
    y = mix * lax.rsqrt(jnp.mean(mix * mix, axis=-1, keepdims=True) + EPS) * nw_ref[...]
    o_ref[0] = x_ref[0] + g1_ref[0] * y


def retention_mixer(qk, v, gate, pool_y, x, sf, sb, tabs, gn, w_out_bf, norm_w, g1, tm):
    b, n, d = x.shape
    nch = tm // CHUNK
    row = lambda bi, i: (bi, i, 0)
    const = lambda shape: pl.BlockSpec(shape, lambda bi, i: (0,) * len(shape))
    return pl.pallas_call(
        functools.partial(_ret_kernel, nch=nch),
        out_shape=jax.ShapeDtypeStruct((b, n, d), F32),
        grid=(b, n // tm),
        in_specs=[
            pl.BlockSpec((1, tm, 2 * QK_W), row),
            pl.BlockSpec((1, tm, V_W), row),
            pl.BlockSpec((1, tm, V_W), row),
            pl.BlockSpec((1, tm, POOL_W), row),
            pl.BlockSpec((1, tm, d), row),
            pl.BlockSpec((1, nch, DK, V_W), lambda bi, i: (bi, i, 0, 0)),
            pl.BlockSpec((1, nch, DK, V_W), lambda bi, i: (bi, i, 0, 0)),
            const((N_HEADS // 2, CHUNK, 2 * CHUNK)),
            const((CHUNK, V_W)), const((CHUNK, V_W)),
            const((N_HEADS, 1, QK_W)),
            const((QK_W, V_W)),
            const((V_W, V_W)),
            const((1, V_W)),
            const((V_W + POOL_W, d)),
            const((1, d)),
            pl.BlockSpec((1, 1, d), lambda bi, i: (bi, 0, 0)),
        ],
        out_specs=pl.BlockSpec((1, tm, d), row),
        scratch_shapes=[pltpu.VMEM((tm, V_W), F32)],
        compiler_params=_cparams("parallel", "parallel"),
        name="retention_mixer",
    )(qk, v, gate, pool_y, x, sf, sb, tabs["mpair"], tabs["qdec_f"], tabs["qdec_b"], tabs["hmask"],
      tabs["bd_bf16"], tabs["avg"], gn.reshape(1, V_W), w_out_bf, norm_w.reshape(1, d), g1)


def _packed_width(d, dtype):
    return d * jnp.dtype(dtype).itemsize // 4


def _pack_rows(h):
    bits = lax.bitcast_convert_type(h.astype(F32), jnp.uint32)
    if h.dtype.itemsize == 4:
        return bits
    half = h.shape[1] // 2
    return (bits[:, half:] & jnp.uint32(0xFFFF0000)) | (bits[:, :half] >> 16)


def _unpack_rows(w, dtype):
    if jnp.dtype(dtype).itemsize == 4:
        return lax.bitcast_convert_type(w, dtype)
    lo = lax.bitcast_convert_type(w << 16, F32)
    hi = lax.bitcast_convert_type(w & jnp.uint32(0xFFFF0000), F32)
    return jnp.concatenate([lo, hi], axis=1).astype(dtype)


def _router_kernel(x_ref, g_ref, sh_ref, sc_ref, wr_ref, h_ref, aff_ref):
    h = _norm_mod(x_ref[0], g_ref[...], sh_ref[0], sc_ref[0])
    h_hi = h.astype(BF16)
    h_ref[0] = _pack_rows(h_hi)
    ne = wr_ref.shape[0]
    h_lo = (h - h_hi.astype(F32)).astype(BF16)
    w = wr_ref[...]
    w_hi = w.astype(BF16)
    w_lo = (w - w_hi.astype(F32)).astype(BF16)
    both = _dot_nt(jnp.concatenate([w_hi, w_lo], axis=0), h_hi)
    logits = both[0:ne] + both[ne:2 * ne] + _dot_nt(w_hi, h_lo)
    mx = jnp.max(logits, axis=0, keepdims=True)
    e = jnp.exp(logits - mx)
    aff_ref[0] = e / jnp.sum(e, axis=0, keepdims=True)


def router(x, norm_w, shift, scale, w_router_t, tm):
    b, n, d = x.shape
    ne = w_router_t.shape[0]
    return pl.pallas_call(
        _router_kernel,
        out_shape=(jax.ShapeDtypeStruct((b, n, _packed_width(d, BF16)), jnp.uint32),
                   jax.ShapeDtypeStruct((b, ne, n), F32)),
        grid=(b, n // tm),
        in_specs=[
            pl.BlockSpec((1, tm, d), lambda bi, i: (bi, i, 0)),
            pl.BlockSpec((1, d), lambda bi, i: (0, 0)),
            pl.BlockSpec((1, 1, d), lambda bi, i: (bi, 0, 0)),
            pl.BlockSpec((1, 1, d), lambda bi, i: (bi, 0, 0)),
            pl.BlockSpec((ne, d), lambda bi, i: (0, 0)),
        ],
        out_specs=(
            pl.BlockSpec((1, tm, _packed_width(d, BF16)), lambda bi, i: (bi, i, 0)),
            pl.BlockSpec((1, ne, tm), lambda bi, i: (bi, 0, i)),
        ),
        compiler_params=_cparams("parallel", "parallel"),
        name="router",
    )(x, norm_w.reshape(1, d), shift, scale, w_router_t)


SC_CORES = 2
SC_SUBCORES = 16
GATHER_CHUNK = 64


def gather_rows(table, idx):
    rows, width = idx.shape[0], table.shape[1]
    workers = SC_CORES * SC_SUBCORES
    per_worker = rows // workers
    assert rows == per_worker * workers and per_worker % GATHER_CHUNK == 0
    mesh = plsc.VectorSubcoreMesh(core_axis_name="c", subcore_axis_name="s")

    def body(table_hbm, idx_hbm, out_hbm, idx_v, rows_v, sem):
        wid = lax.axis_index("s") * SC_CORES + lax.axis_index("c")
        base = wid * per_worker

        @pl.loop(0, per_worker // GATHER_CHUNK)
        def _(i):
            off = pl.multiple_of(base + i * GATHER_CHUNK, GATHER_CHUNK)
            pltpu.sync_copy(idx_hbm.at[pl.ds(off, GATHER_CHUNK)], idx_v)
            pltpu.async_copy(table_hbm.at[idx_v], rows_v, sem).wait()
            pltpu.sync_copy(rows_v, out_hbm.at[pl.ds(off, GATHER_CHUNK)])

    return pl.kernel(
        body,
        out_type=jax.ShapeDtypeStruct((rows, width), table.dtype),
        mesh=mesh,
        scratch_types=[
            pltpu.VMEM((GATHER_CHUNK,), jnp.int32),
            pltpu.VMEM((GATHER_CHUNK, width), table.dtype),
            pltpu.SemaphoreType.DMA,
        ],
        name="gather_rows",
    )(table, idx)


FFN_CHUNKS = 8
FFN_VMEM_LIMIT = 60 * 1024 * 1024


def _ffn_spans(f):
    chunk = f // FFN_CHUNKS
    span = -(-chunk // 16) * 16
    offs = [c * chunk // 16 * 16 for c in range(FFN_CHUNKS)]
    assert chunk * FFN_CHUNKS == f
    assert all(o + span >= (c + 1) * chunk and o + span <= f for c, o in enumerate(offs))
    return chunk, span


def _ffn_kernel(x_ref, gate_ref, wg_hbm, wu_hbm, wd_hbm, o_ref, wbf, stage, sem, *, rt, layer, cps):
    e = pl.program_id(0)
    g = pl.program_id(1)
    ne = pl.num_programs(0)
    slot = e % 2
    nxt = jnp.minimum(e + 1, ne - 1)
    mats = (wg_hbm, wu_hbm, wd_hbm)
    chunk, span = _ffn_spans(wbf.shape[2])

    def offset(c):
        return pl.multiple_of(c * chunk // 16 * 16, 16)

    def copies(ee, c, st):
        return [pltpu.make_async_copy(m.at[layer, ee, pl.ds(offset(c), span), :], stage.at[st, i],
                                      sem.at[st, i]) for i, m in enumerate(mats)]

    def start(ee, c, st):
        for cp in copies(ee, c, st):
            cp.start()

    def wait(ee, c, st):
        for cp in copies(ee, c, st):
            cp.wait()

    def cast(c, st, i, sl):
        wbf[sl, i, pl.ds(offset(c), span), :] = stage[st, i].astype(wbf.dtype)

    @pl.when((e == 0) & (g == 0))
    def _():
        for c in range(FFN_CHUNKS):
            start(0, c, 0)
            wait(0, c, 0)
            for i in range(3):
                cast(c, 0, i, 0)

    def compute_tile(t):
        rs = slice(t * rt, (t + 1) * rt)
        xt = _unpack_rows(x_ref[0, 0, rs, :], wbf.dtype)
        a = _dot_nt(xt, wbf[slot, 0])
        u = _dot_nt(xt, wbf[slot, 1])
        hh = (a * jax.nn.sigmoid(a) * u).astype(wbf.dtype)
        gate_col = jnp.broadcast_to(gate_ref[0, 0, :, rs], (LANES, rt)).T[:, 0:1]
        o_ref[0, 0, rs, :] = (_dot(hh, wbf[slot, 2]) * gate_col).astype(o_ref.dtype)

    c0 = g * cps
    ntiles = x_ref.shape[2] // rt
    start(nxt, c0, 0)
    compute_tile(0)
    wait(nxt, c0, 0)
    if cps > 1:
        start(nxt, c0 + 1, 1)
    for t in range(1, ntiles):
        compute_tile(t)
        if t <= 3:
            cast(c0, 0, t - 1, 1 - slot)
    for i in range(max(ntiles - 1, 0), 3):
        cast(c0, 0, i, 1 - slot)
    for j in range(1, cps):
        st = j % 2
        wait(nxt, c0 + j, st)
        if j + 1 < cps:
            start(nxt, c0 + j + 1, 1 - st)
        for i in range(3):
            cast(c0 + j, st, i, 1 - slot)


def expert_ffn(xs, gate, wg_t, wu_t, wd, layer):
    g, ne, rows, dp = xs.shape
    f, d = wd.shape[2], wd.shape[3]
    rt = min(rows, 256)
    assert FFN_CHUNKS % g == 0 and rows % rt == 0
    cps = FFN_CHUNKS // g
    span = _ffn_spans(f)[1]
    any_spec = pl.BlockSpec(memory_space=pl.ANY)
    return pl.pallas_call(
        functools.partial(_ffn_kernel, rt=rt, layer=layer, cps=cps),
        out_shape=jax.ShapeDtypeStruct((g, ne, rows, d), BF16),
        grid=(ne, g),
        in_specs=[
            pl.BlockSpec((1, 1, rows, dp), lambda e, gi: (gi, e, 0, 0)),
            pl.BlockSpec((1, 1, 1, rows), lambda e, gi: (gi, e, 0, 0)),
            any_spec, any_spec, any_spec,
        ],
        out_specs=pl.BlockSpec((1, 1, rows, d), lambda e, gi: (gi, e, 0, 0)),
        scratch_shapes=[
            pltpu.VMEM((2, 3, f, d), BF16),
            pltpu.VMEM((min(cps, 2), 3, span, d), F32),
            pltpu.SemaphoreType.DMA((min(cps, 2), 3)),
        ],
        compiler_params=pltpu.CompilerParams(dimension_semantics=("arbitrary", "arbitrary"),
                                             vmem_limit_bytes=FFN_VMEM_LIMIT),
        name="expert_ffn",
    )(xs, gate, wg_t, wu_t, wd)


ROUTE_WIN = 64
LANES = 128
VAL_ROWS = 8


def _combine_kernel(st_ref, rank_ref, y_hbm, x_ref, nw_ref, g_ref, o_ref, buf, xbuf, sem, xsem,
                    acc_ref, *, tt, ne, cap, merged):
    b = pl.program_id(0)
    k = pl.program_id(1)
    nk = pl.num_programs(1)
    step = b * nk + k
    slot = step % 2
    rows_total = y_hbm.shape[2]
    win = ROUTE_WIN
    g = 0 if merged else b
    base = b * cap if merged else 0
    lane = lax.broadcasted_iota(jnp.int32, (1, LANES), 1)

    def tile_bounds(bb, kk, e):
        off = bb * cap if merged else 0
        lo = off + st_ref[(bb * (nk + 1) + kk) * ne + e]
        hi = off + st_ref[(bb * (nk + 1) + kk + 1) * ne + e]
        return lo, hi, jnp.minimum((lo // 16) * 16, rows_total - win)

    def bounds(e):
        return tile_bounds(b, k, e)

    def win_copy(bb, e, a0, sl):
        src = y_hbm.at[0 if merged else bb, e, pl.ds(pl.multiple_of(a0, 16), win), :]
        return pltpu.make_async_copy(src, buf.at[sl, pl.ds(e * win, win), :], sem.at[sl, e])

    def start_tile(bb, kk, sl):
        for e in range(ne):
            win_copy(bb, e, tile_bounds(bb, kk, e)[2], sl).start()

    @pl.when(step == 0)
    def _():
        start_tile(b, k, slot)

    nxt = step + 1

    @pl.when(nxt < pl.num_programs(0) * nk)
    def _():
        start_tile(nxt // nk, nxt % nk, 1 - slot)

    rkf = rank_ref[0].astype(F32)
    rk_t = jnp.concatenate([rkf, jnp.full((LANES - ne, tt), -1.0, F32)], axis=0).T
    rk_t = rk_t.astype(jnp.int32)

    def target(e, origin):
        col = rk_t[:, e:e + 1]
        return jnp.where(col >= 0, col + (base - origin), -1)

    pieces = []
    for j in range(ne // 2):
        t_even = target(2 * j, bounds(2 * j)[2])
        t_odd = target(2 * j + 1, bounds(2 * j + 1)[2])
        t_odd = jnp.where((t_odd >= 0) & (t_odd < win), t_odd + win, -1)
        pieces.append((jnp.where(lane < win, t_even, t_odd) == lane).astype(BF16))
    p = jnp.concatenate(pieces, axis=1)
    for e in range(ne):
        win_copy(b, e, bounds(e)[2], slot).wait()
    acc_ref[...] = _dot(p, buf[slot])

    for e in range(ne):
        lo, hi, a0 = bounds(e)

        def extra(w, carry, e=e, a0=a0):
            start = a0 + w * win
            aw = pl.multiple_of(jnp.minimum(start, rows_total - win), 16)
            cp = pltpu.make_async_copy(y_hbm.at[g, e, pl.ds(aw, win), :], xbuf, xsem)
            cp.start()
            cp.wait()
            col = rk_t[:, e:e + 1]
            ok = (col >= 0) & (col + base >= start)
            px = (jnp.where(ok, col + (base - aw), -1) == lane[:, 0:win]).astype(BF16)
            acc_ref[...] += _dot(px, xbuf[...])
            return carry

        lax.fori_loop(1, (hi - a0 + win - 1) // win, extra, 0)

    y = acc_ref[...]
    yn = y * lax.rsqrt(jnp.mean(y * y, axis=-1, keepdims=True) + EPS) * nw_ref[...]
    o_ref[0] = x_ref[0] + g_ref[0] * yn


def combine_post(starts, rank_t, y, x, norm_w, g, tt, cap, merged):
    b, n, d = x.shape
    _, ne, rows, _ = y.shape
    grid_spec = pltpu.PrefetchScalarGridSpec(
        num_scalar_prefetch=1,
        grid=(b, n // tt),
        in_specs=[
            pl.BlockSpec((1, ne, tt), lambda bi, k, st: (bi, 0, k)),
            pl.BlockSpec(memory_space=pl.ANY),
            pl.BlockSpec((1, tt, d), lambda bi, k, st: (bi, k, 0)),
            pl.BlockSpec((1, d), lambda bi, k, st: (0, 0)),
            pl.BlockSpec((1, 1, d), lambda bi, k, st: (bi, 0, 0)),
        ],
        out_specs=pl.BlockSpec((1, tt, d), lambda bi, k, st: (bi, k, 0)),
        scratch_shapes=[
            pltpu.VMEM((2, ne * ROUTE_WIN, d), BF16),
            pltpu.VMEM((ROUTE_WIN, d), BF16),
            pltpu.SemaphoreType.DMA((2, ne)),
            pltpu.SemaphoreType.DMA(()),
            pltpu.VMEM((tt, d), F32),
        ],
    )
    return pl.pallas_call(
        functools.partial(_combine_kernel, tt=tt, ne=ne, cap=cap, merged=merged),
        out_shape=jax.ShapeDtypeStruct((b, n, d), F32),
        grid_spec=grid_spec,
        compiler_params=_cparams("arbitrary", "arbitrary"),
        name="combine_post",
    )(starts, rank_t, y, x, norm_w.reshape(1, d), g)


def _select_kernel(aff_ref, tri_ref, rank_ref, st_ref, *, cap, tt):
    a = aff_ref[0]
    ne, n = a.shape
    bits = lax.bitcast_convert_type(a, jnp.int32)

    def search(_, c):
        lo, hi = c
        mid = lo + ((hi - lo) >> 1)
        cnt = jnp.sum((bits >= mid).astype(F32), axis=1, keepdims=True)
        ge = cnt >= cap
        return jnp.where(ge, mid, lo), jnp.where(ge, hi, mid)

    lo0 = jnp.zeros((ne, 1), jnp.int32)
    hi0 = jnp.full((ne, 1), 0x7F800000, jnp.int32)
    thr, _ = lax.fori_loop(0, 31, search, (lo0, hi0))
    gt = bits > thr
    eq = bits == thr
    need = cap - jnp.sum(gt.astype(F32), axis=1, keepdims=True)
    m = jnp.concatenate([gt, eq], axis=0).astype(BF16)
    tri = tri_ref[...]
    lane = lax.broadcasted_iota(jnp.int32, (1, LANES), 1)
    off = jnp.zeros((2 * ne, 1), F32)
    st = jnp.zeros((ne, LANES), jnp.int32)
    for j in range(n // LANES):
        cs = slice(j * LANES, (j + 1) * LANES)
        if (j * LANES) % tt == 0:
            off_sel = off[:ne] + jnp.minimum(off[ne:], need)
            st = jnp.where(lane == (j * LANES) // tt, off_sel.astype(jnp.int32), st)
        mj = m[:, cs]
        pj = _dot(mj, tri) + off
        pe = pj[ne:]
        sel = gt[:, cs] | (eq[:, cs] & (pe < need))
        rank_ref[0, :, cs] = jnp.where(sel, pj[:ne] + jnp.minimum(pe, need), -1.0).astype(jnp.int32)
        off = off + jnp.sum(mj.astype(F32), axis=1, keepdims=True)
    st_ref[0] = jnp.where(lane == n // tt, cap, st)


def route_select(aff_t, cap, tt):
    b, ne, n = aff_t.shape
    tri = jnp.asarray(np.triu(np.ones((LANES, LANES), np.float32), 1), BF16)
    return pl.pallas_call(
        functools.partial(_select_kernel, cap=cap, tt=tt),
        out_shape=(jax.ShapeDtypeStruct((b, ne, n), jnp.int32),
                   jax.ShapeDtypeStruct((b, ne, LANES), jnp.int32)),
        grid=(b,),
        in_specs=[pl.BlockSpec((1, ne, n), lambda bi: (bi, 0, 0)),
                  pl.BlockSpec((LANES, LANES), lambda bi: (0, 0))],
        out_specs=(pl.BlockSpec((1, ne, n), lambda bi: (bi, 0, 0)),
                   pl.BlockSpec((1, ne, LANES), lambda bi: (bi, 0, 0))),
        compiler_params=_cparams("parallel"),
        name="route_select",
    )(aff_t, tri)


def _compact_kernel(st_ref, rank_ref, aff_ref, idx_ref, gate_ref, out_ref, *, tt, ne, cap, n):
    b = pl.program_id(0)
    k = pl.program_id(1)
    nk = pl.num_programs(1)
    win = ROUTE_WIN

    @pl.when(k == 0)
    def _():
        out_ref[...] = jnp.zeros_like(out_ref)

    rk = rank_ref[0]
    a = aff_ref[0]
    g1 = a.astype(BF16).astype(F32)
    r1 = a - g1
    g2 = r1.astype(BF16).astype(F32)
    g3 = r1 - g2
    tok = k * tt + lax.broadcasted_iota(jnp.int32, (1, tt), 1)
    ids = jnp.concatenate([(tok >> 6).astype(F32), (tok & 63).astype(F32),
                           jnp.zeros((VAL_ROWS - 2, tt), F32)], axis=0)
    pad = jnp.zeros((LANES - VAL_ROWS - 3 * ne, tt), F32)
    payload = jnp.concatenate([ids, g1, g2, g3, pad], axis=0).astype(BF16)
    sub = lax.broadcasted_iota(jnp.int32, (win, 1), 0)

    def bounds(e):
        lo = st_ref[(b * (nk + 1) + k) * ne + e]
        hi = st_ref[(b * (nk + 1) + k + 1) * ne + e]
        return hi, (lo // 8) * 8

    def one_hot(e, aw):
        return ((rk[e:e + 1, :] - aw) == sub).astype(BF16)

    p_all = jnp.concatenate([one_hot(e, bounds(e)[1]) for e in range(ne)], axis=0)
    moved = _dot_nt(p_all, payload)
    for e in range(ne):
        a0 = pl.multiple_of(bounds(e)[1], 8)
        out_ref[e, pl.ds(a0, win), :] += moved[e * win:(e + 1) * win]

    for e in range(ne):
        hi, a0 = bounds(e)

        def window(w, carry, e=e, a0=a0):
            aw = pl.multiple_of(a0 + w * win, 8)
            out_ref[e, pl.ds(aw, win), :] += _dot_nt(one_hot(e, aw), payload)
            return carry

        lax.fori_loop(1, (hi - a0 + win - 1) // win, window, 0)

    @pl.when(k == nk - 1)
    def _():
        for e in range(ne):
            t = out_ref[e].T
            ids_e = (t[0:1] * 64.0 + t[1:2]).astype(jnp.int32) + b * n
            g_e = (t[VAL_ROWS + e:VAL_ROWS + e + 1] + t[VAL_ROWS + ne + e:VAL_ROWS + ne + e + 1]) \
                + t[VAL_ROWS + 2 * ne + e:VAL_ROWS + 2 * ne + e + 1]
            idx_ref[0, e:e + 1, :] = ids_e[:, 0:cap]
            gate_ref[0, e:e + 1, :] = g_e[:, 0:cap]


def route_compact(starts, rank_t, aff_t, cap, tt):
    b, ne, n = aff_t.shape
    rows = -(-(cap + ROUTE_WIN) // LANES) * LANES
    grid_spec = pltpu.PrefetchScalarGridSpec(
        num_scalar_prefetch=1,
        grid=(b, n // tt),
        in_specs=[pl.BlockSpec((1, ne, tt), lambda bi, k, st: (bi, 0, k)),
                  pl.BlockSpec((1, ne, tt), lambda bi, k, st: (bi, 0, k))],
        out_specs=(pl.BlockSpec((1, ne, cap), lambda bi, k, st: (bi, 0, 0)),
                   pl.BlockSpec((1, ne, cap), lambda bi, k, st: (bi, 0, 0))),
        scratch_shapes=[pltpu.VMEM((ne, rows, LANES), F32)],
    )
    return pl.pallas_call(
        functools.partial(_compact_kernel, tt=tt, ne=ne, cap=cap, n=n),
        out_shape=(jax.ShapeDtypeStruct((b, ne, cap), jnp.int32),
                   jax.ShapeDtypeStruct((b, ne, cap), F32)),
        grid_spec=grid_spec,
        compiler_params=_cparams("parallel", "arbitrary"),
        name="route_compact",
    )(starts, rank_t, aff_t)


def _qk_head_of_lane():
    half = QK_W // 2
    return (np.arange(QK_W) % half) // (DK // 2)


def _in_proj_perm():
    half = DK // 2
    first = [h * DK + i for h in range(N_HEADS) for i in range(half)]
    second = [h * DK + half + i for h in range(N_HEADS) for i in range(half)]
    qperm = np.array(first + second)
    rest = np.arange(2 * QK_W, 2 * QK_W + 2 * V_W + POOL_W)
    return np.concatenate([qperm, QK_W + qperm, rest])


def _static_tables():
    qk_head = _qk_head_of_lane()
    v_head = np.arange(V_W) // DV
    hmask = (qk_head[None, :] == np.arange(N_HEADS)[:, None])
    bd = (qk_head[:, None] == v_head[None, :])
    avg = (v_head[:, None] == v_head[None, :]).astype(np.float32) / DV
    return {
        "hmask": jnp.asarray(hmask[:, None, :], BF16),
        "bd_f32": jnp.asarray(bd, F32),
        "bd_bf16": jnp.asarray(bd, BF16),
        "avg": jnp.asarray(avg, BF16),
    }


def _decay_tables(lg_f, lg_b):
    pos = jnp.arange(CHUNK, dtype=F32)
    diff = pos[:, None] - pos[None, :]
    low = diff >= 0
    up = diff < 0
    m_f = jnp.where(low, jnp.exp(lg_f[:, None, None] * jnp.where(low, diff, 0.0)), 0.0)
    m_b = jnp.where(up, jnp.exp(lg_b[:, None, None] * jnp.where(up, -diff, 0.0)), 0.0)
    m = m_f + m_b
    mpair = m.reshape(N_HEADS // 2, 2, CHUNK, CHUNK).transpose(0, 2, 1, 3).reshape(
        N_HEADS // 2, CHUNK, 2 * CHUNK)
    qk_head = _qk_head_of_lane()
    v_head = np.arange(V_W) // DV
    qdec_f = jnp.exp(lg_f[None, :] * (pos[:, None] + 1.0))[:, v_head]
    qdec_b = jnp.exp(lg_b[None, :] * (CHUNK - pos[:, None]))[:, v_head]
    kdec_f = jnp.exp(lg_f[None, :] * (CHUNK - 1.0 - pos[:, None]))[:, qk_head]
    kdec_b = jnp.exp(lg_b[None, :] * pos[:, None])[:, qk_head]
    cdec_f = jnp.exp(lg_f * CHUNK)[None, v_head]
    cdec_b = jnp.exp(lg_b * CHUNK)[None, v_head]
    return {"mpair": mpair, "qdec_f": qdec_f, "qdec_b": qdec_b, "kdec_f": kdec_f,
            "kdec_b": kdec_b, "cdec_f": cdec_f, "cdec_b": cdec_b}


def _rope_tables(n):
    t = jnp.arange(n)
    row = (t // GRID_W).astype(F32)
    col = (t % GRID_W).astype(F32)
    n_freq = DK // 4
    inv = ROPE_BASE ** (-jnp.arange(n_freq, dtype=F32) / n_freq)
    ang = jnp.concatenate([row[:, None] * inv, col[:, None] * inv], axis=-1)
    return jnp.tile(jnp.cos(ang), (1, N_HEADS)), jnp.tile(jnp.sin(ang), (1, N_HEADS))


def _moe_residual(x, h_bf, aff_t, wg, wu, wd, layer, norm_w, g, merged):
    b, n, d = h_bf.shape
    cap = EC_FACTOR * n // N_EXPERTS
    ne = N_EXPERTS
    tt = min(n, 256)
    nk = n // tt
    rank_t, st = route_select(aff_t, cap, tt)
    starts = st[:, :, :nk + 1].transpose(0, 2, 1).reshape(-1)
    flat, gate = route_compact(starts, rank_t, aff_t, cap, tt)
    if merged:
        flat = flat.transpose(1, 0, 2)
        gate = gate.transpose(1, 0, 2).reshape(1, ne, b * cap)
    xs = gather_rows(h_bf.reshape(b * n, d), flat.reshape(-1))
    xs = xs.reshape(gate.shape + (d,))
    y = expert_ffn(xs, gate[:, :, None, :], wg, wu, wd, layer)
    return combine_post(starts, rank_t, y, x, norm_w, g, tt, cap, merged)


def kernel(x, c, ctx, c_ctx, w_ada, b_ada, norm_pre_mix, norm_post_mix, norm_pre_ffn, norm_post_ffn, w_in, ret_decay_fwd, ret_decay_bwd, ret_gn, pool_w, pool_scale, w_out, w_router, w_gate, w_up, w_down):
    b, n, d = x.shape
    lc = ctx.shape[1]
    depth = w_ada.shape[0]
    rope = _rope_tables(n)
    static = _static_tables()
    perm = _in_proj_perm()

    cc = jnp.concatenate([c, c_ctx[None, :], jnp.zeros((7, d), F32)], axis=0)
    mods = ada_modulation(cc, w_ada, b_ada)
    wg, wu, wd = jnp.swapaxes(w_gate, 2, 3), jnp.swapaxes(w_up, 2, 3), w_down

    tm = 512
    for l in range(depth):
        last = l == depth - 1
        mx = mods[l, :b].reshape(b, 1, 6, d)
        sh1, sc1, g1, sh2, sc2, g2 = [mx[:, :, i] for i in range(6)]
        mc = jnp.broadcast_to(mods[l, b].reshape(1, 1, 6, d), (b, 1, 6, d))
        csh1, csc1, cg1, csh2, csc2, cg2 = [mc[:, :, i] for i in range(6)]
        lg_f = jax.nn.log_sigmoid(ret_decay_fwd[l].astype(F32))
        lg_b = jax.nn.log_sigmoid(ret_decay_bwd[l].astype(F32))
        tabs = dict(static, **_decay_tables(lg_f, lg_b))
        w_in_p = w_in[l][:, perm].astype(BF16)
        w_out_bf = w_out[l].astype(BF16)
        pool_w_bf = pool_w[l].astype(BF16)

        qk_c, v_c, gate_c, p_c = premix(ctx, norm_pre_mix[l], csh1, csc1, w_in_p, None, lc)
        zero = jnp.zeros((b, DK, V_W), F32)
        sf_c, sb_c, s_f, s_b = state_scan(qk_c, v_c, zero, zero, tabs)
        qk_x, v_x, gate_x, p_x = premix(x, norm_pre_mix[l], sh1, sc1, w_in_p, rope, tm)
        sf_x, sb_x, _, _ = state_scan(qk_x, v_x, s_f, s_b, tabs)
        pool_x = pool_mixer(p_x, pool_w_bf, pool_scale[l], GRID_W, min(n, 2048))
        x = retention_mixer(qk_x, v_x, gate_x, pool_x, x, sf_x, sb_x, tabs, ret_gn[l], w_out_bf,
                            norm_post_mix[l], g1, tm)
        if not last:
            pool_c = pool_mixer(p_c, pool_w_bf, pool_scale[l], lc, lc)
            ctx = retention_mixer(qk_c, v_c, gate_c, pool_c, ctx, sf_c, sb_c, tabs, ret_gn[l],
                                  w_out_bf, norm_post_mix[l], cg1, lc)

        wr_t = w_router[l].T
        h_x, aff_x = router(x, norm_pre_ffn[l], sh2, sc2, wr_t, tm)
        x = _moe_residual(x, h_x, aff_x, wg, wu, wd, l, norm_post_ffn[l], g2, False)
        if not last:
            h_c, aff_c = router(ctx, norm_pre_ffn[l], csh2, csc2, wr_t, lc)
            ctx = _moe_residual(ctx, h_c, aff_c, wg, wu, wd, l, norm_post_ffn[l], cg2, True)
    return x
```

```python
import functools

import jax
import jax.numpy as jnp
import numpy as np
from jax import lax
from jax.experimental import pallas as pl
from jax.experimental.pallas import tpu as pltpu
from jax.experimental.pallas import tpu_sc as plsc

F32 = jnp.float32
BF16 = jnp.bfloat16

D_MODEL = 1024
GRID_W = 64
N_HEADS = 8
DV = 64
DK = 32
QK_W = N_HEADS * DK
V_W = N_HEADS * DV
POOL_W = 512
POOL_WINDOWS = (2, 4, 8, 16)
POOL_GROUP_DIM = 128
CHUNK = 128
ROPE_BASE = 10000.0
N_EXPERTS = 16
EC_FACTOR = 2
EPS = 1e-6
MAX_HALF_WINDOW = max(POOL_WINDOWS) // 2

VMEM_LIMIT = 56 * 1024 * 1024


def _cparams(*sem):
    return pltpu.CompilerParams(dimension_semantics=sem, vmem_limit_bytes=VMEM_LIMIT)


def _dot(a, b):
    return jnp.dot(a, b, preferred_element_type=F32)


def _dot_nt(a, b, precision=None):
    return lax.dot_general(a, b, (((1,), (1,)), ((), ())), precision=precision,
                           preferred_element_type=F32)


def _ada_kernel(cc_ref, w_ref, b_ref, o_ref):
    s = cc_ref[...]
    s = s * jax.nn.sigmoid(s)
    o_ref[0] = _dot(s.astype(BF16), w_ref[0].astype(BF16)) + b_ref[0]


def ada_modulation(cc, w_ada, b_ada):
    depth, d, d6 = w_ada.shape
    rows = cc.shape[0]
    tn = 1536
    return pl.pallas_call(
        _ada_kernel,
        out_shape=jax.ShapeDtypeStruct((depth, rows, d6), F32),
        grid=(depth, d6 // tn),
        in_specs=[
            pl.BlockSpec((rows, d), lambda l, j: (0, 0)),
            pl.BlockSpec((1, d, tn), lambda l, j: (l, 0, j)),
            pl.BlockSpec((1, 1, tn), lambda l, j: (l, 0, j)),
        ],
        out_specs=pl.BlockSpec((1, rows, tn), lambda l, j: (l, 0, j)),
        compiler_params=_cparams("parallel", "parallel"),
        name="ada_modulation",
    )(cc, w_ada, b_ada.reshape(depth, 1, d6))


def _norm_mod(xf, g, sh, sc):
    y = xf * lax.rsqrt(jnp.mean(xf * xf, axis=-1, keepdims=True) + EPS)
    return (y * g) * (1.0 + sc) + sh


def _premix_kernel(x_ref, g_ref, sh_ref, sc_ref, w_ref, *rest, rope):
    if rope:
        cos_ref, sin_ref, qk_ref, v_ref, gate_ref, p_ref = rest
    else:
        qk_ref, v_ref, gate_ref, p_ref = rest
    hb = _norm_mod(x_ref[0], g_ref[...], sh_ref[0], sc_ref[0]).astype(BF16)
    zqk = _dot(hb, w_ref[:, 0:2 * QK_W])
    half = QK_W // 2
    q1, q2 = zqk[:, 0:half], zqk[:, half:2 * half]
    k1 = zqk[:, 2 * half:3 * half] * (DK ** -0.5)
    k2 = zqk[:, 3 * half:4 * half] * (DK ** -0.5)
    if rope:
        cos, sin = cos_ref[...], sin_ref[...]
        q1, q2 = q1 * cos - q2 * sin, q1 * sin + q2 * cos
        k1, k2 = k1 * cos - k2 * sin, k1 * sin + k2 * cos
    qk_ref[0] = jnp.concatenate([q1, q2, k1, k2], axis=1).astype(BF16)
    o = 2 * QK_W
    v_ref[0] = _dot(hb, w_ref[:, o:o + V_W]).astype(BF16)
    gate_ref[0] = _dot(hb, w_ref[:, o + V_W:o + 2 * V_W])
    p_ref[0] = _dot(hb, w_ref[:, o + 2 * V_W:o + 2 * V_W + POOL_W])


def premix(x, norm_w, shift, scale, w_in_p, rope, tm):
    b, n, d = x.shape
    in_w = w_in_p.shape[1]
    row = lambda bi, i: (bi, i, 0)
    in_specs = [
        pl.BlockSpec((1, tm, d), row),
        pl.BlockSpec((1, d), lambda bi, i: (0, 0)),
        pl.BlockSpec((1, 1, d), lambda bi, i: (bi, 0, 0)),
        pl.BlockSpec((1, 1, d), lambda bi, i: (bi, 0, 0)),
        pl.BlockSpec((d, in_w), lambda bi, i: (0, 0)),
    ]
    args = [x, norm_w.reshape(1, d), shift, scale, w_in_p]
    if rope is not None:
        half = QK_W // 2
        in_specs += [pl.BlockSpec((tm, half), lambda bi, i: (i, 0))] * 2
        args += list(rope)
    return pl.pallas_call(
        functools.partial(_premix_kernel, rope=rope is not None),
        out_shape=(
            jax.ShapeDtypeStruct((b, n, 2 * QK_W), BF16),
            jax.ShapeDtypeStruct((b, n, V_W), BF16),
            jax.ShapeDtypeStruct((b, n, V_W), F32),
            jax.ShapeDtypeStruct((b, n, POOL_W), F32),
        ),
        grid=(b, n // tm),
        in_specs=in_specs,
        out_specs=(
            pl.BlockSpec((1, tm, 2 * QK_W), row),
            pl.BlockSpec((1, tm, V_W), row),
            pl.BlockSpec((1, tm, V_W), row),
            pl.BlockSpec((1, tm, POOL_W), row),
        ),
        compiler_params=_cparams("parallel", "parallel"),
        name="premix",
    )(*args)


def _fold_state(s):
    g = 2 * N_HEADS
    top = s[0:g]
    bot = s[QK_W // 2:QK_W // 2 + g]
    for h in range(1, N_HEADS):
        top = top + s[h * g:(h + 1) * g]
        bot = bot + s[QK_W // 2 + h * g:QK_W // 2 + (h + 1) * g]
    return jnp.concatenate([top, bot], axis=0)


def _expand_state(c):
    g = 2 * N_HEADS
    return jnp.concatenate([c[0:g]] * N_HEADS + [c[g:2 * g]] * N_HEADS, axis=0)


def _state_kernel(kf_ref, vf_ref, kb_ref, vb_ref, s0f_ref, s0b_ref, kdf_ref, kdb_ref,
                  cdf_ref, cdb_ref, bd_ref, sf_out, sb_out, ff_out, fb_out, sf_acc, sb_acc, *, cps):
    c = pl.program_id(1)
    nc = pl.num_programs(1)
    bd = bd_ref[...]

    @pl.when(c == 0)
    def _():
        sf_acc[...] = _expand_state(s0f_ref[0]) * bd
        sb_acc[...] = _expand_state(s0b_ref[0]) * bd

    def update(s, k, v, kd_ref, cd_ref):
        kd = k.astype(F32) * kd_ref[...]
        f = _dot(kd.T.astype(BF16), v)
        return s * cd_ref[...] + f * bd

    sf = sf_acc[...]
    sb = sb_acc[...]
    for i in range(cps):
        j = cps - 1 - i
        fs = slice(i * CHUNK, (i + 1) * CHUNK)
        bs = slice(j * CHUNK, (j + 1) * CHUNK)
        sf_out[0, i] = _fold_state(sf).astype(BF16)
        sb_out[0, j] = _fold_state(sb).astype(BF16)
        sf = update(sf, kf_ref[0, fs, :], vf_ref[0, fs, :], kdf_ref, cdf_ref)
        sb = update(sb, kb_ref[0, bs, :], vb_ref[0, bs, :], kdb_ref, cdb_ref)
    sf_acc[...] = sf
    sb_acc[...] = sb

    @pl.when(c == nc - 1)
    def _():
        ff_out[0] = _fold_state(sf)
        fb_out[0] = _fold_state(sb)


def state_scan(qk, v, s0f, s0b, tabs):
    b, n, _ = qk.shape
    nc = n // CHUNK
    cps = min(nc, 4)
    ns = nc // cps
    rows = cps * CHUNK
    const = lambda shape: pl.BlockSpec(shape, lambda bi, c: (0,) * len(shape))
    return pl.pallas_call(
        functools.partial(_state_kernel, cps=cps),
        out_shape=(
            jax.ShapeDtypeStruct((b, nc, DK, V_W), BF16),
            jax.ShapeDtypeStruct((b, nc, DK, V_W), BF16),
            jax.ShapeDtypeStruct((b, DK, V_W), F32),
            jax.ShapeDtypeStruct((b, DK, V_W), F32),
        ),
        grid=(b, ns),
        in_specs=[
            pl.BlockSpec((1, rows, QK_W), lambda bi, c: (bi, c, 1)),
            pl.BlockSpec((1, rows, V_W), lambda bi, c: (bi, c, 0)),
            pl.BlockSpec((1, rows, QK_W), lambda bi, c: (bi, ns - 1 - c, 1)),
            pl.BlockSpec((1, rows, V_W), lambda bi, c: (bi, ns - 1 - c, 0)),
            pl.BlockSpec((1, DK, V_W), lambda bi, c: (bi, 0, 0)),
            pl.BlockSpec((1, DK, V_W), lambda bi, c: (bi, 0, 0)),
            const((CHUNK, QK_W)), const((CHUNK, QK_W)),
            const((1, V_W)), const((1, V_W)),
            const((QK_W, V_W)),
        ],
        out_specs=(
            pl.BlockSpec((1, cps, DK, V_W), lambda bi, c: (bi, c, 0, 0)),
            pl.BlockSpec((1, cps, DK, V_W), lambda bi, c: (bi, ns - 1 - c, 0, 0)),
            pl.BlockSpec((1, DK, V_W), lambda bi, c: (bi, 0, 0)),
            pl.BlockSpec((1, DK, V_W), lambda bi, c: (bi, 0, 0)),
        ),
        scratch_shapes=[pltpu.VMEM((QK_W, V_W), F32), pltpu.VMEM((QK_W, V_W), F32)],
        compiler_params=_cparams("parallel", "arbitrary"),
        name="state_scan",
    )(qk, v, qk, v, s0f, s0b, tabs["kdec_f"], tabs["kdec_b"], tabs["cdec_f"], tabs["cdec_b"],
      tabs["bd_f32"])


def _pool_kernel(*refs, gw, tile, halo, rows_total):
    if halo:
        prev_ref, cur_ref, next_ref, pw_ref, ps_ref, o_ref, buf = refs
    else:
        cur_ref, pw_ref, ps_ref, o_ref, buf = refs
    i = pl.program_id(1)
    last = pl.num_programs(1) - 1
    m = MAX_HALF_WINDOW
    span = tile + 2 * halo
    zeros_m = jnp.zeros((m, POOL_W), F32)
    buf[0:m] = zeros_m
    buf[m + span:2 * m + span] = zeros_m
    if halo:
        buf[m:m + halo] = jnp.where(i > 0, prev_ref[0], 0.0)
        buf[m + halo + tile:m + span] = jnp.where(i < last, next_ref[0], 0.0)
    buf[m + halo:m + halo + tile] = cur_ref[0]

    pos = lax.broadcasted_iota(jnp.int32, (span, 1), 0)
    col = pos % gw
    tpos = lax.broadcasted_iota(jnp.int32, (tile, 1), 0)
    tcol = tpos % gw
    trow = i * (tile // gw) + tpos // gw
    for gi, w in enumerate(POOL_WINDOWS):
        cs = slice(gi * POOL_GROUP_DIM, (gi + 1) * POOL_GROUP_DIM)
        hw = w // 2
        s = None
        for d in range(-hw, hw):
            valid = (col + d >= 0) & (col + d < gw)
            term = jnp.where(valid, buf[m + d:m + d + span, cs], 0.0)
            s = term if s is None else s + term
        if halo:
            acc = None
            for d in range(-hw, hw):
                start = halo + d * gw
                term = s[start:start + tile]
                acc = term if acc is None else acc + term
            cnt_r = jnp.minimum(trow + hw, rows_total) - jnp.maximum(trow - hw, 0)
        else:
            acc = s
            cnt_r = 1
        cnt_c = jnp.minimum(tcol + hw, gw) - jnp.maximum(tcol - hw, 0)
        cnt = (cnt_r * cnt_c).astype(F32)
        xg = buf[m + halo:m + halo + tile, cs]
        diff = (acc / cnt - xg).astype(BF16)
        y = _dot(diff, pw_ref[gi]) * ps_ref[:, cs]
        o_ref[0, :, cs] = y.astype(BF16)


def pool_mixer(p, pool_w_bf, pool_scale, gw, tile):
    b, n, pw = p.shape
    rows_total = n // gw
    halo = MAX_HALF_WINDOW * gw if rows_total > 1 else 0
    in_specs, args = [], []
    if halo:
        r = tile // halo
        nh = n // halo
        in_specs = [
            pl.BlockSpec((1, halo, pw), lambda bi, i: (bi, jnp.maximum(i * r - 1, 0), 0)),
            pl.BlockSpec((1, tile, pw), lambda bi, i: (bi, i, 0)),
            pl.BlockSpec((1, halo, pw), lambda bi, i: (bi, jnp.minimum((i + 1) * r, nh - 1), 0)),
        ]
        args = [p, p, p]
    else:
        in_specs = [pl.BlockSpec((1, tile, pw), lambda bi, i: (bi, i, 0))]
        args = [p]
    ng = len(POOL_WINDOWS)
    in_specs += [
        pl.BlockSpec((ng, POOL_GROUP_DIM, POOL_GROUP_DIM), lambda bi, i: (0, 0, 0)),
        pl.BlockSpec((1, pw), lambda bi, i: (0, 0)),
    ]
    args += [pool_w_bf, pool_scale.reshape(1, pw)]
    return pl.pallas_call(
        functools.partial(_pool_kernel, gw=gw, tile=tile, halo=halo, rows_total=rows_total),
        out_shape=jax.ShapeDtypeStruct((b, n, pw), BF16),
        grid=(b, n // tile),
        in_specs=in_specs,
        out_specs=pl.BlockSpec((1, tile, pw), lambda bi, i: (bi, i, 0)),
        scratch_shapes=[pltpu.VMEM((tile + 2 * halo + 2 * MAX_HALF_WINDOW, pw), F32)],
        compiler_params=_cparams("parallel", "parallel"),
        name="pool_mixer",
    )(*args)


def _ret_kernel(qk_ref, v_ref, gate_ref, py_ref, x_ref, sf_ref, sb_ref, mp_ref, qdf_ref, qdb_ref,
                hm_ref, bd_ref, avg_ref, gn_ref, wo_ref, nw_ref, g1_ref, o_ref, o_buf, *, nch):
    bd = bd_ref[...]
    avg = avg_ref[...]
    lane = lax.broadcasted_iota(jnp.int32, (1, 2 * DV), 1)
    lo_mask = (lane < DV).astype(BF16)
    hi_mask = (lane >= DV).astype(BF16)
    for ci in range(nch):
        rs = slice(ci * CHUNK, (ci + 1) * CHUNK)
        q = qk_ref[0, rs, 0:QK_W]
        k = qk_ref[0, rs, QK_W:2 * QK_W]
        vv = v_ref[0, rs, :]
        inner = []
        for j in range(N_HEADS // 2):
            kp = jnp.concatenate([k * hm_ref[2 * j], k * hm_ref[2 * j + 1]], axis=0)
            sc = _dot_nt(q, kp) * mp_ref[j]
            vpair = vv[:, 2 * DV * j:2 * DV * (j + 1)]
            vp = jnp.concatenate([vpair * lo_mask, vpair * hi_mask], axis=0)
            inner.append(_dot(sc.astype(BF16), vp))
        o = jnp.concatenate(inner, axis=1)
        sf_bd = _expand_state(sf_ref[0, ci]) * bd
        sb_bd = _expand_state(sb_ref[0, ci]) * bd
        o_buf[rs, :] = o + _dot(q, sf_bd) * qdf_ref[...] + _dot(q, sb_bd) * qdb_ref[...]

    def head_mean(a):
        hi = a.astype(BF16)
        lo = (a - hi.astype(F32)).astype(BF16)
        return _dot(hi, avg) + _dot(lo, avg)

    o = o_buf[...]
    dlt = o - head_mean(o)
    var = head_mean(dlt * dlt)
    yn = dlt * lax.rsqrt(var + EPS) * gn_ref[...]
    g = gate_ref[0]
    r = (g * jax.nn.sigmoid(g) * yn).astype(BF16)
    mix = _dot(r, wo_ref[0:V_W, :]) + _dot(py_ref[0], wo_ref[V_W:V_W + POOL_W, :])
    y = mix * lax.rsqrt(jnp.mean(mix * mix, axis=-1, keepdims=True) + EPS) * nw_ref[...]
    o_ref[0] = x_ref[0] + g1_ref[0] * y


def retention_mixer(qk, v, gate, pool_y, x, sf, sb, tabs, gn, w_out_bf, norm_w, g1, tm):
    b, n, d = x.shape
    nch = tm // CHUNK
    row = lambda bi, i: (bi, i, 0)
    const = lambda shape: pl.BlockSpec(shape, lambda bi, i: (0,) * len(shape))
    return pl.pallas_call(
        functools.partial(_ret_kernel, nch=nch),
        out_shape=jax.ShapeDtypeStruct((b, n, d), F32),
        grid=(b, n // tm),
        in_specs=[
            pl.BlockSpec((1, tm, 2 * QK_W), row),
            pl.BlockSpec((1, tm, V_W), row),
            pl.BlockSpec((1, tm, V_W), row),
            pl.BlockSpec((1, tm, POOL_W), row),
            pl.BlockSpec((1, tm, d), row),
            pl.BlockSpec((1, nch, DK, V_W), lambda bi, i: (bi, i, 0, 0)),
            pl.BlockSpec((1, nch, DK, V_W), lambda bi, i: (bi, i, 0, 0)),
            const((N_HEADS // 2, CHUNK, 2 * CHUNK)),
            const((CHUNK, V_W)), const((CHUNK, V_W)),
            const((N_HEADS, 1, QK_W)),
            const((QK_W, V_W)),
            const((V_W, V_W)),
            const((1, V_W)),
            const((V_W + POOL_W, d)),
            const((1, d)),
            pl.BlockSpec((1, 1, d), lambda bi, i: (bi, 0, 0)),
        ],
        out_specs=pl.BlockSpec((1, tm, d), row),
        scratch_shapes=[pltpu.VMEM((tm, V_W), F32)],
        compiler_params=_cparams("parallel", "parallel"),
        name="retention_mixer",
    )(qk, v, gate, pool_y, x, sf, sb, tabs["mpair"], tabs["qdec_f"], tabs["qdec_b"], tabs["hmask"],
      tabs["bd_bf16"], tabs["avg"], gn.reshape(1, V_W), w_out_bf, norm_w.reshape(1, d), g1)


def _packed_width(d, dtype):
    return d * jnp.dtype(dtype).itemsize // 4


def _pack_rows(h):
    bits = lax.bitcast_convert_type(h.astype(F32), jnp.uint32)
    if h.dtype.itemsize == 4:
        return bits
    half = h.shape[1] // 2
    return (bits[:, half:] & jnp.uint32(0xFFFF0000)) | (bits[:, :half] >> 16)


def _unpack_rows(w, dtype):
    if jnp.dtype(dtype).itemsize == 4:
        return lax.bitcast_convert_type(w, dtype)
    lo = lax.bitcast_convert_type(w << 16, F32)
    hi = lax.bitcast_convert_type(w & jnp.uint32(0xFFFF0000), F32)
    return jnp.concatenate([lo, hi], axis=1).astype(dtype)


def _router_kernel(x_ref, g_ref, sh_ref, sc_ref, wr_ref, h_ref, aff_ref):
    h = _norm_mod(x_ref[0], g_ref[...], sh_ref[0], sc_ref[0])
    h_hi = h.astype(BF16)
    h_ref[0] = _pack_rows(h_hi)
    ne = wr_ref.shape[0]
    h_lo = (h - h_hi.astype(F32)).astype(BF16)
    w = wr_ref[...]
    w_hi = w.astype(BF16)
    w_lo = (w - w_hi.astype(F32)).astype(BF16)
    both = _dot_nt(jnp.concatenate([w_hi, w_lo], axis=0), h_hi)
    logits = both[0:ne] + both[ne:2 * ne] + _dot_nt(w_hi, h_lo)
    mx = jnp.max(logits, axis=0, keepdims=True)
    e = jnp.exp(logits - mx)
    aff_ref[0] = e / jnp.sum(e, axis=0, keepdims=True)


def router(x, norm_w, shift, scale, w_router_t, tm):
    b, n, d = x.shape
    ne = w_router_t.shape[0]
    return pl.pallas_call(
        _router_kernel,
        out_shape=(jax.ShapeDtypeStruct((b, n, _packed_width(d, BF16)), jnp.uint32),
                   jax.ShapeDtypeStruct((b, ne, n), F32)),
        grid=(b, n // tm),
        in_specs=[
            pl.BlockSpec((1, tm, d), lambda bi, i: (bi, i, 0)),
            pl.BlockSpec((1, d), lambda bi, i: (0, 0)),
            pl.BlockSpec((1, 1, d), lambda bi, i: (bi, 0, 0)),
            pl.BlockSpec((1, 1, d), lambda bi, i: (bi, 0, 0)),
            pl.BlockSpec((ne, d), lambda bi, i: (0, 0)),
        ],
        out_specs=(
            pl.BlockSpec((1, tm, _packed_width(d, BF16)), lambda bi, i: (bi, i, 0)),
            pl.BlockSpec((1, ne, tm), lambda bi, i: (bi, 0, i)),
        ),
        compiler_params=_cparams("parallel", "parallel"),
        name="router",
    )(x, norm_w.reshape(1, d), shift, scale, w_router_t)


SC_CORES = 2
SC_SUBCORES = 16
GATHER_CHUNK = 64


def gather_rows(table, idx):
    rows, width = idx.shape[0], table.shape[1]
    workers = SC_CORES * SC_SUBCORES
    per_worker = rows // workers
    assert rows == per_worker * workers and per_worker % GATHER_CHUNK == 0
    mesh = plsc.VectorSubcoreMesh(core_axis_name="c", subcore_axis_name="s")

    def body(table_hbm, idx_hbm, out_hbm, idx_v, rows_v, sem):
        wid = lax.axis_index("s") * SC_CORES + lax.axis_index("c")
        base = wid * per_worker

        @pl.loop(0, per_worker // GATHER_CHUNK)
        def _(i):
            off = pl.multiple_of(base + i * GATHER_CHUNK, GATHER_CHUNK)
            pltpu.sync_copy(idx_hbm.at[pl.ds(off, GATHER_CHUNK)], idx_v)
            pltpu.async_copy(table_hbm.at[idx_v], rows_v, sem).wait()
            pltpu.sync_copy(rows_v, out_hbm.at[pl.ds(off, GATHER_CHUNK)])

    return pl.kernel(
        body,
        out_type=jax.ShapeDtypeStruct((rows, width), table.dtype),
        mesh=mesh,
        scratch_types=[
            pltpu.VMEM((GATHER_CHUNK,), jnp.int32),
            pltpu.VMEM((GATHER_CHUNK, width), table.dtype),
            pltpu.SemaphoreType.DMA,
        ],
        name="gather_rows",
    )(table, idx)


FFN_CHUNKS = 8
FFN_VMEM_LIMIT = 60 * 1024 * 1024


def _ffn_spans(f):
    chunk = f // FFN_CHUNKS
    span = -(-chunk // 16) * 16
    offs = [c * chunk // 16 * 16 for c in range(FFN_CHUNKS)]
    assert chunk * FFN_CHUNKS == f
    assert all(o + span >= (c + 1) * chunk and o + span <= f for c, o in enumerate(offs))
    return chunk, span


def _ffn_kernel(*refs, rt, layer, cps, extra_rows):
    if extra_rows:
        (x_ref, gate_ref, xc_ref, gc_ref, wg_hbm, wu_hbm, wd_hbm, o_ref, oc_ref, wbf, stage, sem) = refs
    else:
        x_ref, gate_ref, wg_hbm, wu_hbm, wd_hbm, o_ref, wbf, stage, sem = refs
    e = pl.program_id(0)
    g = pl.program_id(1)
    ne = pl.num_programs(0)
    slot = e % 2
    nxt = jnp.minimum(e + 1, ne - 1)
    mats = (wg_hbm, wu_hbm, wd_hbm)
    chunk, span = _ffn_spans(wbf.shape[2])

    def offset(c):
        return pl.multiple_of(c * chunk // 16 * 16, 16)

    def copies(ee, c, st):
        return [pltpu.make_async_copy(m.at[layer, ee, pl.ds(offset(c), span), :], stage.at[st, i],
                                      sem.at[st, i]) for i, m in enumerate(mats)]

    def start(ee, c, st):
        for cp in copies(ee, c, st):
            cp.start()

    def wait(ee, c, st):
        for cp in copies(ee, c, st):
            cp.wait()

    def cast(c, st, i, sl):
        wbf[sl, i, pl.ds(offset(c), span), :] = stage[st, i].astype(wbf.dtype)

    @pl.when((e == 0) & (g == 0))
    def _():
        for c in range(FFN_CHUNKS):
            start(0, c, 0)
            wait(0, c, 0)
            for i in range(3):
                cast(c, 0, i, 0)

    def swiglu(words, gate_row):
        r = words.shape[0]
        xt = _unpack_rows(words, wbf.dtype)
        a = _dot_nt(xt, wbf[slot, 0])
        u = _dot_nt(xt, wbf[slot, 1])
        hh = (a * jax.nn.sigmoid(a) * u).astype(wbf.dtype)
        gate_col = jnp.broadcast_to(gate_row, (LANES, r)).T[:, 0:1]
        return (_dot(hh, wbf[slot, 2]) * gate_col).astype(o_ref.dtype)

    def compute_tile(t):
        rs = slice(t * rt, (t + 1) * rt)
        o_ref[0, 0, rs, :] = swiglu(x_ref[0, 0, rs, :], gate_ref[0, 0, :, rs])

    if extra_rows:
        @pl.when(g == pl.num_programs(1) - 1)
        def _():
            oc_ref[0, 0] = swiglu(xc_ref[0, 0], gc_ref[0, 0])

    c0 = g * cps
    ntiles = x_ref.shape[2] // rt
    start(nxt, c0, 0)
    compute_tile(0)
    wait(nxt, c0, 0)
    if cps > 1:
        start(nxt, c0 + 1, 1)
    for t in range(1, ntiles):
        compute_tile(t)
        if t <= 3:
            cast(c0, 0, t - 1, 1 - slot)
    for i in range(max(ntiles - 1, 0), 3):
        cast(c0, 0, i, 1 - slot)
    for j in range(1, cps):
        st = j % 2
        wait(nxt, c0 + j, st)
        if j + 1 < cps:
            start(nxt, c0 + j + 1, 1 - st)
        for i in range(3):
            cast(c0 + j, st, i, 1 - slot)


def expert_ffn(xs, gate, wg_t, wu_t, wd, layer, extra=None):
    g, ne, rows, dp = xs.shape
    f, d = wd.shape[2], wd.shape[3]
    rt = min(rows, 256)
    assert FFN_CHUNKS % g == 0 and rows % rt == 0
    cps = FFN_CHUNKS // g
    span = _ffn_spans(f)[1]
    any_spec = pl.BlockSpec(memory_space=pl.ANY)
    in_specs = [pl.BlockSpec((1, 1, rows, dp), lambda e, gi: (gi, e, 0, 0)),
                pl.BlockSpec((1, 1, 1, rows), lambda e, gi: (gi, e, 0, 0))]
    out_specs = pl.BlockSpec((1, 1, rows, d), lambda e, gi: (gi, e, 0, 0))
    out_shape = jax.ShapeDtypeStruct((g, ne, rows, d), BF16)
    args = [xs, gate]
    if extra is not None:
        rc = extra[0].shape[2]
        in_specs += [pl.BlockSpec((1, 1, rc, dp), lambda e, gi: (0, e, 0, 0)),
                     pl.BlockSpec((1, 1, 1, rc), lambda e, gi: (0, e, 0, 0))]
        out_specs = (out_specs, pl.BlockSpec((1, 1, rc, d), lambda e, gi: (0, e, 0, 0)))
        out_shape = (out_shape, jax.ShapeDtypeStruct((1, ne, rc, d), BF16))
        args += list(extra)
    return pl.pallas_call(
        functools.partial(_ffn_kernel, rt=rt, layer=layer, cps=cps, extra_rows=extra is not None),
        out_shape=out_shape,
        grid=(ne, g),
        in_specs=in_specs + [any_spec, any_spec, any_spec],
        out_specs=out_specs,
        scratch_shapes=[
            pltpu.VMEM((2, 3, f, d), BF16),
            pltpu.VMEM((min(cps, 2), 3, span, d), F32),
            pltpu.SemaphoreType.DMA((min(cps, 2), 3)),
        ],
        compiler_params=pltpu.CompilerParams(dimension_semantics=("arbitrary", "arbitrary"),
                                             vmem_limit_bytes=FFN_VMEM_LIMIT),
        name="expert_ffn",
    )(*args, wg_t, wu_t, wd)


ROUTE_WIN = 64
LANES = 128
VAL_ROWS = 8


def _combine_kernel(st_ref, rank_ref, y_hbm, x_ref, nw_ref, g_ref, o_ref, buf, xbuf, sem, xsem,
                    acc_ref, *, tt, ne, cap, merged):
    b = pl.program_id(0)
    k = pl.program_id(1)
    nk = pl.num_programs(1)
    step = b * nk + k
    slot = step % 2
    rows_total = y_hbm.shape[2]
    win = ROUTE_WIN
    g = 0 if merged else b
    base = b * cap if merged else 0
    lane = lax.broadcasted_iota(jnp.int32, (1, LANES), 1)

    def tile_bounds(bb, kk, e):
        off = bb * cap if merged else 0
        lo = off + st_ref[(bb * (nk + 1) + kk) * ne + e]
        hi = off + st_ref[(bb * (nk + 1) + kk + 1) * ne + e]
        return lo, hi, jnp.minimum((lo // 16) * 16, rows_total - win)

    def bounds(e):
        return tile_bounds(b, k, e)

    def win_copy(bb, e, a0, sl):
        src = y_hbm.at[0 if merged else bb, e, pl.ds(pl.multiple_of(a0, 16), win), :]
        return pltpu.make_async_copy(src, buf.at[sl, pl.ds(e * win, win), :], sem.at[sl, e])

    def start_tile(bb, kk, sl):
        for e in range(ne):
            win_copy(bb, e, tile_bounds(bb, kk, e)[2], sl).start()

    @pl.when(step == 0)
    def _():
        start_tile(b, k, slot)

    nxt = step + 1

    @pl.when(nxt < pl.num_programs(0) * nk)
    def _():
        start_tile(nxt // nk, nxt % nk, 1 - slot)

    rkf = rank_ref[0].astype(F32)
    rk_t = jnp.concatenate([rkf, jnp.full((LANES - ne, tt), -1.0, F32)], axis=0).T
    rk_t = rk_t.astype(jnp.int32)

    def target(e, origin):
        col = rk_t[:, e:e + 1]
        return jnp.where(col >= 0, col + (base - origin), -1)

    pieces = []
    for j in range(ne // 2):
        t_even = target(2 * j, bounds(2 * j)[2])
        t_odd = target(2 * j + 1, bounds(2 * j + 1)[2])
        t_odd = jnp.where((t_odd >= 0) & (t_odd < win), t_odd + win, -1)
        pieces.append((jnp.where(lane < win, t_even, t_odd) == lane).astype(BF16))
    p = jnp.concatenate(pieces, axis=1)
    for e in range(ne):
        win_copy(b, e, bounds(e)[2], slot).wait()
    acc_ref[...] = _dot(p, buf[slot])

    for e in range(ne):
        lo, hi, a0 = bounds(e)

        def extra(w, carry, e=e, a0=a0):
            start = a0 + w * win
            aw = pl.multiple_of(jnp.minimum(start, rows_total - win), 16)
            cp = pltpu.make_async_copy(y_hbm.at[g, e, pl.ds(aw, win), :], xbuf, xsem)
            cp.start()
            cp.wait()
            col = rk_t[:, e:e + 1]
            ok = (col >= 0) & (col + base >= start)
            px = (jnp.where(ok, col + (base - aw), -1) == lane[:, 0:win]).astype(BF16)
            acc_ref[...] += _dot(px, xbuf[...])
            return carry

        lax.fori_loop(1, (hi - a0 + win - 1) // win, extra, 0)

    y = acc_ref[...]
    yn = y * lax.rsqrt(jnp.mean(y * y, axis=-1, keepdims=True) + EPS) * nw_ref[...]
    o_ref[0] = x_ref[0] + g_ref[0] * yn


def combine_post(starts, rank_t, y, x, norm_w, g, tt, cap, merged):
    b, n, d = x.shape
    _, ne, rows, _ = y.shape
    grid_spec = pltpu.PrefetchScalarGridSpec(
        num_scalar_prefetch=1,
        grid=(b, n // tt),
        in_specs=[
            pl.BlockSpec((1, ne, tt), lambda bi, k, st: (bi, 0, k)),
            pl.BlockSpec(memory_space=pl.ANY),
            pl.BlockSpec((1, tt, d), lambda bi, k, st: (bi, k, 0)),
            pl.BlockSpec((1, d), lambda bi, k, st: (0, 0)),
            pl.BlockSpec((1, 1, d), lambda bi, k, st: (bi, 0, 0)),
        ],
        out_specs=pl.BlockSpec((1, tt, d), lambda bi, k, st: (bi, k, 0)),
        scratch_shapes=[
            pltpu.VMEM((2, ne * ROUTE_WIN, d), BF16),
            pltpu.VMEM((ROUTE_WIN, d), BF16),
            pltpu.SemaphoreType.DMA((2, ne)),
            pltpu.SemaphoreType.DMA(()),
            pltpu.VMEM((tt, d), F32),
        ],
    )
    return pl.pallas_call(
        functools.partial(_combine_kernel, tt=tt, ne=ne, cap=cap, merged=merged),
        out_shape=jax.ShapeDtypeStruct((b, n, d), F32),
        grid_spec=grid_spec,
        compiler_params=_cparams("arbitrary", "arbitrary"),
        name="combine_post",
    )(starts, rank_t, y, x, norm_w.reshape(1, d), g)


def _select_kernel(aff_ref, tri_ref, rank_ref, st_ref, *, cap, tt):
    a = aff_ref[0]
    ne, n = a.shape
    bits = lax.bitcast_convert_type(a, jnp.int32)

    def search(_, c):
        lo, hi = c
        mid = lo + ((hi - lo) >> 1)
        cnt = jnp.sum((bits >= mid).astype(F32), axis=1, keepdims=True)
        ge = cnt >= cap
        return jnp.where(ge, mid, lo), jnp.where(ge, hi, mid)

    lo0 = jnp.zeros((ne, 1), jnp.int32)
    hi0 = jnp.full((ne, 1), 0x7F800000, jnp.int32)
    thr, _ = lax.fori_loop(0, 31, search, (lo0, hi0))
    gt = bits > thr
    eq = bits == thr
    need = cap - jnp.sum(gt.astype(F32), axis=1, keepdims=True)
    m = jnp.concatenate([gt, eq], axis=0).astype(BF16)
    tri = tri_ref[...]
    lane = lax.broadcasted_iota(jnp.int32, (1, LANES), 1)
    off = jnp.zeros((2 * ne, 1), F32)
    st = jnp.zeros((ne, LANES), jnp.int32)
    for j in range(n // LANES):
        cs = slice(j * LANES, (j + 1) * LANES)
        if (j * LANES) % tt == 0:
            off_sel = off[:ne] + jnp.minimum(off[ne:], need)
            st = jnp.where(lane == (j * LANES) // tt, off_sel.astype(jnp.int32), st)
        mj = m[:, cs]
        pj = _dot(mj, tri) + off
        pe = pj[ne:]
        sel = gt[:, cs] | (eq[:, cs] & (pe < need))
        rank_ref[0, :, cs] = jnp.where(sel, pj[:ne] + jnp.minimum(pe, need), -1.0).astype(jnp.int32)
        off = off + jnp.sum(mj.astype(F32), axis=1, keepdims=True)
    st_ref[0] = jnp.where(lane == n // tt, cap, st)


def route_select(aff_t, cap, tt):
    b, ne, n = aff_t.shape
    tri = jnp.asarray(np.triu(np.ones((LANES, LANES), np.float32), 1), BF16)
    return pl.pallas_call(
        functools.partial(_select_kernel, cap=cap, tt=tt),
        out_shape=(jax.ShapeDtypeStruct((b, ne, n), jnp.int32),
                   jax.ShapeDtypeStruct((b, ne, LANES), jnp.int32)),
        grid=(b,),
        in_specs=[pl.BlockSpec((1, ne, n), lambda bi: (bi, 0, 0)),
                  pl.BlockSpec((LANES, LANES), lambda bi: (0, 0))],
        out_specs=(pl.BlockSpec((1, ne, n), lambda bi: (bi, 0, 0)),
                   pl.BlockSpec((1, ne, LANES), lambda bi: (bi, 0, 0))),
        compiler_params=_cparams("parallel"),
        name="route_select",
    )(aff_t, tri)


def _compact_kernel(st_ref, rank_ref, aff_ref, idx_ref, gate_ref, out_ref, *, tt, ne, cap, n):
    b = pl.program_id(0)
    k = pl.program_id(1)
    nk = pl.num_programs(1)
    win = ROUTE_WIN

    @pl.when(k == 0)
    def _():
        out_ref[...] = jnp.zeros_like(out_ref)

    rk = rank_ref[0]
    a = aff_ref[0]
    g1 = a.astype(BF16).astype(F32)
    r1 = a - g1
    g2 = r1.astype(BF16).astype(F32)
    g3 = r1 - g2
    tok = k * tt + lax.broadcasted_iota(jnp.int32, (1, tt), 1)
    ids = jnp.concatenate([(tok >> 6).astype(F32), (tok & 63).astype(F32),
                           jnp.zeros((VAL_ROWS - 2, tt), F32)], axis=0)
    pad = jnp.zeros((LANES - VAL_ROWS - 3 * ne, tt), F32)
    payload = jnp.concatenate([ids, g1, g2, g3, pad], axis=0).astype(BF16)
    sub = lax.broadcasted_iota(jnp.int32, (win, 1), 0)

    def bounds(e):
        lo = st_ref[(b * (nk + 1) + k) * ne + e]
        hi = st_ref[(b * (nk + 1) + k + 1) * ne + e]
        return hi, (lo // 8) * 8

    def one_hot(e, aw):
        return ((rk[e:e + 1, :] - aw) == sub).astype(BF16)

    p_all = jnp.concatenate([one_hot(e, bounds(e)[1]) for e in range(ne)], axis=0)
    moved = _dot_nt(p_all, payload)
    for e in range(ne):
        a0 = pl.multiple_of(bounds(e)[1], 8)
        out_ref[e, pl.ds(a0, win), :] += moved[e * win:(e + 1) * win]

    for e in range(ne):
        hi, a0 = bounds(e)

        def window(w, carry, e=e, a0=a0):
            aw = pl.multiple_of(a0 + w * win, 8)
            out_ref[e, pl.ds(aw, win), :] += _dot_nt(one_hot(e, aw), payload)
            return carry

        lax.fori_loop(1, (hi - a0 + win - 1) // win, window, 0)

    @pl.when(k == nk - 1)
    def _():
        for e in range(ne):
            t = out_ref[e].T
            ids_e = (t[0:1] * 64.0 + t[1:2]).astype(jnp.int32) + b * n
            g_e = (t[VAL_ROWS + e:VAL_ROWS + e + 1] + t[VAL_ROWS + ne + e:VAL_ROWS + ne + e + 1]) \
                + t[VAL_ROWS + 2 * ne + e:VAL_ROWS + 2 * ne + e + 1]
            idx_ref[0, e:e + 1, :] = ids_e[:, 0:cap]
            gate_ref[0, e:e + 1, :] = g_e[:, 0:cap]


def route_compact(starts, rank_t, aff_t, cap, tt):
    b, ne, n = aff_t.shape
    rows = -(-(cap + ROUTE_WIN) // LANES) * LANES
    grid_spec = pltpu.PrefetchScalarGridSpec(
        num_scalar_prefetch=1,
        grid=(b, n // tt),
        in_specs=[pl.BlockSpec((1, ne, tt), lambda bi, k, st: (bi, 0, k)),
                  pl.BlockSpec((1, ne, tt), lambda bi, k, st: (bi, 0, k))],
        out_specs=(pl.BlockSpec((1, ne, cap), lambda bi, k, st: (bi, 0, 0)),
                   pl.BlockSpec((1, ne, cap), lambda bi, k, st: (bi, 0, 0))),
        scratch_shapes=[pltpu.VMEM((ne, rows, LANES), F32)],
    )
    return pl.pallas_call(
        functools.partial(_compact_kernel, tt=tt, ne=ne, cap=cap, n=n),
        out_shape=(jax.ShapeDtypeStruct((b, ne, cap), jnp.int32),
                   jax.ShapeDtypeStruct((b, ne, cap), F32)),
        grid_spec=grid_spec,
        compiler_params=_cparams("parallel", "arbitrary"),
        name="route_compact",
    )(starts, rank_t, aff_t)


def _qk_head_of_lane():
    half = QK_W // 2
    return (np.arange(QK_W) % half) // (DK // 2)


def _in_proj_perm():
    half = DK // 2
    first = [h * DK + i for h in range(N_HEADS) for i in range(half)]
    second = [h * DK + half + i for h in range(N_HEADS) for i in range(half)]
    qperm = np.array(first + second)
    rest = np.arange(2 * QK_W, 2 * QK_W + 2 * V_W + POOL_W)
    return np.concatenate([qperm, QK_W + qperm, rest])


def _static_tables():
    qk_head = _qk_head_of_lane()
    v_head = np.arange(V_W) // DV
    hmask = (qk_head[None, :] == np.arange(N_HEADS)[:, None])
    bd = (qk_head[:, None] == v_head[None, :])
    avg = (v_head[:, None] == v_head[None, :]).astype(np.float32) / DV
    return {
        "hmask": jnp.asarray(hmask[:, None, :], BF16),
        "bd_f32": jnp.asarray(bd, F32),
        "bd_bf16": jnp.asarray(bd, BF16),
        "avg": jnp.asarray(avg, BF16),
    }


def _decay_tables(lg_f, lg_b):
    pos = jnp.arange(CHUNK, dtype=F32)
    diff = pos[:, None] - pos[None, :]
    low = diff >= 0
    up = diff < 0
    m_f = jnp.where(low, jnp.exp(lg_f[:, None, None] * jnp.where(low, diff, 0.0)), 0.0)
    m_b = jnp.where(up, jnp.exp(lg_b[:, None, None] * jnp.where(up, -diff, 0.0)), 0.0)
    m = m_f + m_b
    mpair = m.reshape(N_HEADS // 2, 2, CHUNK, CHUNK).transpose(0, 2, 1, 3).reshape(
        N_HEADS // 2, CHUNK, 2 * CHUNK)
    qk_head = _qk_head_of_lane()
    v_head = np.arange(V_W) // DV
    qdec_f = jnp.exp(lg_f[None, :] * (pos[:, None] + 1.0))[:, v_head]
    qdec_b = jnp.exp(lg_b[None, :] * (CHUNK - pos[:, None]))[:, v_head]
    kdec_f = jnp.exp(lg_f[None, :] * (CHUNK - 1.0 - pos[:, None]))[:, qk_head]
    kdec_b = jnp.exp(lg_b[None, :] * pos[:, None])[:, qk_head]
    cdec_f = jnp.exp(lg_f * CHUNK)[None, v_head]
    cdec_b = jnp.exp(lg_b * CHUNK)[None, v_head]
    return {"mpair": mpair, "qdec_f": qdec_f, "qdec_b": qdec_b, "kdec_f": kdec_f,
            "kdec_b": kdec_b, "cdec_f": cdec_f, "cdec_b": cdec_b}


def _rope_tables(n):
    t = jnp.arange(n)
    row = (t // GRID_W).astype(F32)
    col = (t % GRID_W).astype(F32)
    n_freq = DK // 4
    inv = ROPE_BASE ** (-jnp.arange(n_freq, dtype=F32) / n_freq)
    ang = jnp.concatenate([row[:, None] * inv, col[:, None] * inv], axis=-1)
    return jnp.tile(jnp.cos(ang), (1, N_HEADS)), jnp.tile(jnp.sin(ang), (1, N_HEADS))


def _route(h_pk, aff_t, merged):
    b, n, d = h_pk.shape
    cap = EC_FACTOR * n // N_EXPERTS
    ne = N_EXPERTS
    tt = min(n, 256)
    nk = n // tt
    rank_t, st = route_select(aff_t, cap, tt)
    starts = st[:, :, :nk + 1].transpose(0, 2, 1).reshape(-1)
    flat, gate = route_compact(starts, rank_t, aff_t, cap, tt)
    if merged:
        flat = flat.transpose(1, 0, 2)
        gate = gate.transpose(1, 0, 2).reshape(1, ne, b * cap)
    xs = gather_rows(h_pk.reshape(b * n, d), flat.reshape(-1))
    xs = xs.reshape(gate.shape + (d,))
    return xs, gate[:, :, None, :], (starts, rank_t, tt, cap, merged)


def _combine(x, y, info, norm_w, g):
    starts, rank_t, tt, cap, merged = info
    return combine_post(starts, rank_t, y, x, norm_w, g, tt, cap, merged)


def kernel(x, c, ctx, c_ctx, w_ada, b_ada, norm_pre_mix, norm_post_mix, norm_pre_ffn, norm_post_ffn, w_in, ret_decay_fwd, ret_decay_bwd, ret_gn, pool_w, pool_scale, w_out, w_router, w_gate, w_up, w_down):
    b, n, d = x.shape
    lc = ctx.shape[1]
    depth = w_ada.shape[0]
    rope = _rope_tables(n)
    static = _static_tables()
    perm = _in_proj_perm()

    cc = jnp.concatenate([c, c_ctx[None, :], jnp.zeros((7, d), F32)], axis=0)
    mods = ada_modulation(cc, w_ada, b_ada)
    wg, wu, wd = jnp.swapaxes(w_gate, 2, 3), jnp.swapaxes(w_up, 2, 3), w_down

    tm = 512
    for l in range(depth):
        last = l == depth - 1
        mx = mods[l, :b].reshape(b, 1, 6, d)
        sh1, sc1, g1, sh2, sc2, g2 = [mx[:, :, i] for i in range(6)]
        mc = jnp.broadcast_to(mods[l, b].reshape(1, 1, 6, d), (b, 1, 6, d))
        csh1, csc1, cg1, csh2, csc2, cg2 = [mc[:, :, i] for i in range(6)]
        lg_f = jax.nn.log_sigmoid(ret_decay_fwd[l].astype(F32))
        lg_b = jax.nn.log_sigmoid(ret_decay_bwd[l].astype(F32))
        tabs = dict(static, **_decay_tables(lg_f, lg_b))
        w_in_p = w_in[l][:, perm].astype(BF16)
        w_out_bf = w_out[l].astype(BF16)
        pool_w_bf = pool_w[l].astype(BF16)

        qk_c, v_c, gate_c, p_c = premix(ctx, norm_pre_mix[l], csh1, csc1, w_in_p, None, lc)
        zero = jnp.zeros((b, DK, V_W), F32)
        sf_c, sb_c, s_f, s_b = state_scan(qk_c, v_c, zero, zero, tabs)
        qk_x, v_x, gate_x, p_x = premix(x, norm_pre_mix[l], sh1, sc1, w_in_p, rope, tm)
        sf_x, sb_x, _, _ = state_scan(qk_x, v_x, s_f, s_b, tabs)
        pool_x = pool_mixer(p_x, pool_w_bf, pool_scale[l], GRID_W, min(n, 2048))
        x = retention_mixer(qk_x, v_x, gate_x, pool_x, x, sf_x, sb_x, tabs, ret_gn[l], w_out_bf,
                            norm_post_mix[l], g1, tm)
        if not last:
            pool_c = pool_mixer(p_c, pool_w_bf, pool_scale[l], lc, lc)
            ctx = retention_mixer(qk_c, v_c, gate_c, pool_c, ctx, sf_c, sb_c, tabs, ret_gn[l],
                                  w_out_bf, norm_post_mix[l], cg1, lc)

        wr_t = w_router[l].T
        h_x, aff_x = router(x, norm_pre_ffn[l], sh2, sc2, wr_t, tm)
        xs_x, gate_x2, info_x = _route(h_x, aff_x, False)
        if last:
            y_x = expert_ffn(xs_x, gate_x2, wg, wu, wd, l)
        else:
            h_c, aff_c = router(ctx, norm_pre_ffn[l], csh2, csc2, wr_t, lc)
            xs_c, gate_c2, info_c = _route(h_c, aff_c, True)
            y_x, y_c = expert_ffn(xs_x, gate_x2, wg, wu, wd, l, extra=(xs_c, gate_c2))
            ctx = _combine(ctx, y_c, info_c, norm_post_ffn[l], cg2)
        x = _combine(x, y_x, info_x, norm_post_ffn[l], g2)
    return x
```

```python
import functools

import jax
import jax.numpy as jnp
import numpy as np
from jax import lax
from jax.experimental import pallas as pl
from jax.experimental.pallas import tpu as pltpu
from jax.experimental.pallas import tpu_sc as plsc

F32 = jnp.float32
BF16 = jnp.bfloat16

D_MODEL = 1024
GRID_W = 64
N_HEADS = 8
DV = 64
DK = 32
QK_W = N_HEADS * DK
V_W = N_HEADS * DV
POOL_W = 512
POOL_WINDOWS = (2, 4, 8, 16)
POOL_GROUP_DIM = 128
CHUNK = 128
ROPE_BASE = 10000.0
N_EXPERTS = 16
EC_FACTOR = 2
EPS = 1e-6
MAX_HALF_WINDOW = max(POOL_WINDOWS) // 2

VMEM_LIMIT = 56 * 1024 * 1024


def _cparams(*sem):
    return pltpu.CompilerParams(dimension_semantics=sem, vmem_limit_bytes=VMEM_LIMIT)


def _dot(a, b):
    return jnp.dot(a, b, preferred_element_type=F32)


def _dot_nt(a, b, precision=None):
    return lax.dot_general(a, b, (((1,), (1,)), ((), ())), precision=precision,
                           preferred_element_type=F32)


def _ada_kernel(cc_ref, w_ref, b_ref, o_ref):
    s = cc_ref[...]
    s = s * jax.nn.sigmoid(s)
    o_ref[0] = _dot(s.astype(BF16), w_ref[0].astype(BF16)) + b_ref[0]


def ada_modulation(cc, w_ada, b_ada):
    depth, d, d6 = w_ada.shape
    rows = cc.shape[0]
    tn = 1536
    return pl.pallas_call(
        _ada_kernel,
        out_shape=jax.ShapeDtypeStruct((depth, rows, d6), F32),
        grid=(depth, d6 // tn),
        in_specs=[
            pl.BlockSpec((rows, d), lambda l, j: (0, 0)),
            pl.BlockSpec((1, d, tn), lambda l, j: (l, 0, j)),
            pl.BlockSpec((1, 1, tn), lambda l, j: (l, 0, j)),
        ],
        out_specs=pl.BlockSpec((1, rows, tn), lambda l, j: (l, 0, j)),
        compiler_params=_cparams("parallel", "parallel"),
        name="ada_modulation",
    )(cc, w_ada, b_ada.reshape(depth, 1, d6))


def _norm_mod(xf, g, sh, sc):
    y = xf * lax.rsqrt(jnp.mean(xf * xf, axis=-1, keepdims=True) + EPS)
    return (y * g) * (1.0 + sc) + sh


def _premix_kernel(x_ref, g_ref, sh_ref, sc_ref, w_ref, *rest, rope):
    if rope:
        cos_ref, sin_ref, qk_ref, v_ref, gate_ref, p_ref = rest
    else:
        qk_ref, v_ref, gate_ref, p_ref = rest
    hb = _norm_mod(x_ref[0], g_ref[...], sh_ref[0], sc_ref[0]).astype(BF16)
    zqk = _dot(hb, w_ref[:, 0:2 * QK_W])
    half = QK_W // 2
    q1, q2 = zqk[:, 0:half], zqk[:, half:2 * half]
    k1 = zqk[:, 2 * half:3 * half] * (DK ** -0.5)
    k2 = zqk[:, 3 * half:4 * half] * (DK ** -0.5)
    if rope:
        cos, sin = cos_ref[...], sin_ref[...]
        q1, q2 = q1 * cos - q2 * sin, q1 * sin + q2 * cos
        k1, k2 = k1 * cos - k2 * sin, k1 * sin + k2 * cos
    qk_ref[0] = jnp.concatenate([q1, q2, k1, k2], axis=1).astype(BF16)
    o = 2 * QK_W
    v_ref[0] = _dot(hb, w_ref[:, o:o + V_W]).astype(BF16)
    gate_ref[0] = _dot(hb, w_ref[:, o + V_W:o + 2 * V_W])
    p_ref[0] = _dot(hb, w_ref[:, o + 2 * V_W:o + 2 * V_W + POOL_W])


def premix(x, norm_w, shift, scale, w_in_p, rope, tm):
    b, n, d = x.shape
    in_w = w_in_p.shape[1]
    row = lambda bi, i: (bi, i, 0)
    in_specs = [
        pl.BlockSpec((1, tm, d), row),
        pl.BlockSpec((1, d), lambda bi, i: (0, 0)),
        pl.BlockSpec((1, 1, d), lambda bi, i: (bi, 0, 0)),
        pl.BlockSpec((1, 1, d), lambda bi, i: (bi, 0, 0)),
        pl.BlockSpec((d, in_w), lambda bi, i: (0, 0)),
    ]
    args = [x, norm_w.reshape(1, d), shift, scale, w_in_p]
    if rope is not None:
        half = QK_W // 2
        in_specs += [pl.BlockSpec((tm, half), lambda bi, i: (i, 0))] * 2
        args += list(rope)
    return pl.pallas_call(
        functools.partial(_premix_kernel, rope=rope is not None),
        out_shape=(
            jax.ShapeDtypeStruct((b, n, 2 * QK_W), BF16),
            jax.ShapeDtypeStruct((b, n, V_W), BF16),
            jax.ShapeDtypeStruct((b, n, V_W), F32),
            jax.ShapeDtypeStruct((b, n, POOL_W), F32),
        ),
        grid=(b, n // tm),
        in_specs=in_specs,
        out_specs=(
            pl.BlockSpec((1, tm, 2 * QK_W), row),
            pl.BlockSpec((1, tm, V_W), row),
            pl.BlockSpec((1, tm, V_W), row),
            pl.BlockSpec((1, tm, POOL_W), row),
        ),
        compiler_params=_cparams("parallel", "parallel"),
        name="premix",
    )(*args)


def _fold_state(s):
    g = 2 * N_HEADS
    top = s[0:g]
    bot = s[QK_W // 2:QK_W // 2 + g]
    for h in range(1, N_HEADS):
        top = top + s[h * g:(h + 1) * g]
        bot = bot + s[QK_W // 2 + h * g:QK_W // 2 + (h + 1) * g]
    return jnp.concatenate([top, bot], axis=0)


def _expand_state(c):
    g = 2 * N_HEADS
    return jnp.concatenate([c[0:g]] * N_HEADS + [c[g:2 * g]] * N_HEADS, axis=0)


def _state_kernel(kf_ref, vf_ref, kb_ref, vb_ref, s0f_ref, s0b_ref, kdf_ref, kdb_ref,
                  cdf_ref, cdb_ref, bd_ref, sf_out, sb_out, ff_out, fb_out, sf_acc, sb_acc, *, cps):
    c = pl.program_id(1)
    nc = pl.num_programs(1)
    bd = bd_ref[...]

    @pl.when(c == 0)
    def _():
        sf_acc[...] = _expand_state(s0f_ref[0]) * bd
        sb_acc[...] = _expand_state(s0b_ref[0]) * bd

    def update(s, k, v, kd_ref, cd_ref):
        kd = k.astype(F32) * kd_ref[...]
        f = _dot(kd.T.astype(BF16), v)
        return s * cd_ref[...] + f * bd

    sf = sf_acc[...]
    sb = sb_acc[...]
    for i in range(cps):
        j = cps - 1 - i
        fs = slice(i * CHUNK, (i + 1) * CHUNK)
        bs = slice(j * CHUNK, (j + 1) * CHUNK)
        sf_out[0, i] = _fold_state(sf).astype(BF16)
        sb_out[0, j] = _fold_state(sb).astype(BF16)
        sf = update(sf, kf_ref[0, fs, :], vf_ref[0, fs, :], kdf_ref, cdf_ref)
        sb = update(sb, kb_ref[0, bs, :], vb_ref[0, bs, :], kdb_ref, cdb_ref)
    sf_acc[...] = sf
    sb_acc[...] = sb

    @pl.when(c == nc - 1)
    def _():
        ff_out[0] = _fold_state(sf)
        fb_out[0] = _fold_state(sb)


def state_scan(qk, v, s0f, s0b, tabs):
    b, n, _ = qk.shape
    nc = n // CHUNK
    cps = min(nc, 4)
    ns = nc // cps
    rows = cps * CHUNK
    const = lambda shape: pl.BlockSpec(shape, lambda bi, c: (0,) * len(shape))
    return pl.pallas_call(
        functools.partial(_state_kernel, cps=cps),
        out_shape=(
            jax.ShapeDtypeStruct((b, nc, DK, V_W), BF16),
            jax.ShapeDtypeStruct((b, nc, DK, V_W), BF16),
            jax.ShapeDtypeStruct((b, DK, V_W), F32),
            jax.ShapeDtypeStruct((b, DK, V_W), F32),
        ),
        grid=(b, ns),
        in_specs=[
            pl.BlockSpec((1, rows, QK_W), lambda bi, c: (bi, c, 1)),
            pl.BlockSpec((1, rows, V_W), lambda bi, c: (bi, c, 0)),
            pl.BlockSpec((1, rows, QK_W), lambda bi, c: (bi, ns - 1 - c, 1)),
            pl.BlockSpec((1, rows, V_W), lambda bi, c: (bi, ns - 1 - c, 0)),
            pl.BlockSpec((1, DK, V_W), lambda bi, c: (bi, 0, 0)),
            pl.BlockSpec((1, DK, V_W), lambda bi, c: (bi, 0, 0)),
            const((CHUNK, QK_W)), const((CHUNK, QK_W)),
            const((1, V_W)), const((1, V_W)),
            const((QK_W, V_W)),
        ],
        out_specs=(
            pl.BlockSpec((1, cps, DK, V_W), lambda bi, c: (bi, c, 0, 0)),
            pl.BlockSpec((1, cps, DK, V_W), lambda bi, c: (bi, ns - 1 - c, 0, 0)),
            pl.BlockSpec((1, DK, V_W), lambda bi, c: (bi, 0, 0)),
            pl.BlockSpec((1, DK, V_W), lambda bi, c: (bi, 0, 0)),
        ),
        scratch_shapes=[pltpu.VMEM((QK_W, V_W), F32), pltpu.VMEM((QK_W, V_W), F32)],
        compiler_params=_cparams("parallel", "arbitrary"),
        name="state_scan",
    )(qk, v, qk, v, s0f, s0b, tabs["kdec_f"], tabs["kdec_b"], tabs["cdec_f"], tabs["cdec_b"],
      tabs["bd_f32"])


def _pool_kernel(*refs, gw, tile, halo, rows_total):
    if halo:
        prev_ref, cur_ref, next_ref, pw_ref, ps_ref, o_ref, buf = refs
    else:
        cur_ref, pw_ref, ps_ref, o_ref, buf = refs
    i = pl.program_id(1)
    last = pl.num_programs(1) - 1
    m = MAX_HALF_WINDOW
    span = tile + 2 * halo
    zeros_m = jnp.zeros((m, POOL_W), F32)
    buf[0:m] = zeros_m
    buf[m + span:2 * m + span] = zeros_m
    if halo:
        buf[m:m + halo] = jnp.where(i > 0, prev_ref[0], 0.0)
        buf[m + halo + tile:m + span] = jnp.where(i < last, next_ref[0], 0.0)
    buf[m + halo:m + halo + tile] = cur_ref[0]

    pos = lax.broadcasted_iota(jnp.int32, (span, 1), 0)
    col = pos % gw
    tpos = lax.broadcasted_iota(jnp.int32, (tile, 1), 0)
    tcol = tpos % gw
    trow = i * (tile // gw) + tpos // gw
    for gi, w in enumerate(POOL_WINDOWS):
        cs = slice(gi * POOL_GROUP_DIM, (gi + 1) * POOL_GROUP_DIM)
        hw = w // 2
        s = None
        for d in range(-hw, hw):
            valid = (col + d >= 0) & (col + d < gw)
            term = jnp.where(valid, buf[m + d:m + d + span, cs], 0.0)
            s = term if s is None else s + term
        if halo:
            acc = None
            for d in range(-hw, hw):
                start = halo + d * gw
                term = s[start:start + tile]
                acc = term if acc is None else acc + term
            cnt_r = jnp.minimum(trow + hw, rows_total) - jnp.maximum(trow - hw, 0)
        else:
            acc = s
            cnt_r = 1
        cnt_c = jnp.minimum(tcol + hw, gw) - jnp.maximum(tcol - hw, 0)
        cnt = (cnt_r * cnt_c).astype(F32)
        xg = buf[m + halo:m + halo + tile, cs]
        diff = (acc / cnt - xg).astype(BF16)
        y = _dot(diff, pw_ref[gi]) * ps_ref[:, cs]
        o_ref[0, :, cs] = y.astype(BF16)


def pool_mixer(p, pool_w_bf, pool_scale, gw, tile):
    b, n, pw = p.shape
    rows_total = n // gw
    halo = MAX_HALF_WINDOW * gw if rows_total > 1 else 0
    in_specs, args = [], []
    if halo:
        r = tile // halo
        nh = n // halo
        in_specs = [
            pl.BlockSpec((1, halo, pw), lambda bi, i: (bi, jnp.maximum(i * r - 1, 0), 0)),
            pl.BlockSpec((1, tile, pw), lambda bi, i: (bi, i, 0)),
            pl.BlockSpec((1, halo, pw), lambda bi, i: (bi, jnp.minimum((i + 1) * r, nh - 1), 0)),
        ]
        args = [p, p, p]
    else:
        in_specs = [pl.BlockSpec((1, tile, pw), lambda bi, i: (bi, i, 0))]
        args = [p]
    ng = len(POOL_WINDOWS)
    in_specs += [
        pl.BlockSpec((ng, POOL_GROUP_DIM, POOL_GROUP_DIM), lambda bi, i: (0, 0, 0)),
        pl.BlockSpec((1, pw), lambda bi, i: (0, 0)),
    ]
    args += [pool_w_bf, pool_scale.reshape(1, pw)]
    return pl.pallas_call(
        functools.partial(_pool_kernel, gw=gw, tile=tile, halo=halo, rows_total=rows_total),
        out_shape=jax.ShapeDtypeStruct((b, n, pw), BF16),
        grid=(b, n // tile),
        in_specs=in_specs,
        out_specs=pl.BlockSpec((1, tile, pw), lambda bi, i: (bi, i, 0)),
        scratch_shapes=[pltpu.VMEM((tile + 2 * halo + 2 * MAX_HALF_WINDOW, pw), F32)],
        compiler_params=_cparams("parallel", "parallel"),
        name="pool_mixer",
    )(*args)


def _ret_kernel(qk_ref, v_ref, gate_ref, py_ref, x_ref, sf_ref, sb_ref, mp_ref, qdf_ref, qdb_ref,
                hm_ref, bd_ref, avg_ref, gn_ref, wo_ref, nw_ref, g1_ref, o_ref, o_buf, *, nch):
    bd = bd_ref[...]
    avg = avg_ref[...]
    lane = lax.broadcasted_iota(jnp.int32, (1, 2 * DV), 1)
    lo_mask = (lane < DV).astype(BF16)
    hi_mask = (lane >= DV).astype(BF16)
    for ci in range(nch):
        rs = slice(ci * CHUNK, (ci + 1) * CHUNK)
        q = qk_ref[0, rs, 0:QK_W]
        k = qk_ref[0, rs, QK_W:2 * QK_W]
        vv = v_ref[0, rs, :]
        inner = []
        for j in range(N_HEADS // 2):
            kp = jnp.concatenate([k * hm_ref[2 * j], k * hm_ref[2 * j + 1]], axis=0)
            sc = _dot_nt(q, kp) * mp_ref[j]
            vpair = vv[:, 2 * DV * j:2 * DV * (j + 1)]
            vp = jnp.concatenate([vpair * lo_mask, vpair * hi_mask], axis=0)
            inner.append(_dot(sc.astype(BF16), vp))
        o = jnp.concatenate(inner, axis=1)
        sf_bd = _expand_state(sf_ref[0, ci]) * bd
        sb_bd = _expand_state(sb_ref[0, ci]) * bd
        o_buf[rs, :] = o + _dot(q, sf_bd) * qdf_ref[...] + _dot(q, sb_bd) * qdb_ref[...]

    def head_mean(a):
        hi = a.astype(BF16)
        lo = (a - hi.astype(F32)).astype(BF16)
        return _dot(hi, avg) + _dot(lo, avg)

    o = o_buf[...]
    dlt = o - head_mean(o)
    var = head_mean(dlt * dlt)
    yn = dlt * lax.rsqrt(var + EPS) * gn_ref[...]
    g = gate_ref[0]
    r = (g * jax.nn.sigmoid(g) * yn).astype(BF16)
    mix = _dot(r, wo_ref[0:V_W, :]) + _dot(py_ref[0], wo_ref[V_W:V_W + POOL_W, :])
    y = mix * lax.rsqrt(jnp.mean(mix * mix, axis=-1, keepdims=True) + EPS) * nw_ref[...]
    o_ref[0] = x_ref[0] + g1_ref[0] * y


def retention_mixer(qk, v, gate, pool_y, x, sf, sb, tabs, gn, w_out_bf, norm_w, g1, tm):
    b, n, d = x.shape
    nch = tm // CHUNK
    row = lambda bi, i: (bi, i, 0)
    const = lambda shape: pl.BlockSpec(shape, lambda bi, i: (0,) * len(shape))
    return pl.pallas_call(
        functools.partial(_ret_kernel, nch=nch),
        out_shape=jax.ShapeDtypeStruct((b, n, d), F32),
        grid=(b, n // tm),
        in_specs=[
            pl.BlockSpec((1, tm, 2 * QK_W), row),
            pl.BlockSpec((1, tm, V_W), row),
            pl.BlockSpec((1, tm, V_W), row),
            pl.BlockSpec((1, tm, POOL_W), row),
            pl.BlockSpec((1, tm, d), row),
            pl.BlockSpec((1, nch, DK, V_W), lambda bi, i: (bi, i, 0, 0)),
            pl.BlockSpec((1, nch, DK, V_W), lambda bi, i: (bi, i, 0, 0)),
            const((N_HEADS // 2, CHUNK, 2 * CHUNK)),
            const((CHUNK, V_W)), const((CHUNK, V_W)),
            const((N_HEADS, 1, QK_W)),
            const((QK_W, V_W)),
            const((V_W, V_W)),
            const((1, V_W)),
            const((V_W + POOL_W, d)),
            const((1, d)),
            pl.BlockSpec((1, 1, d), lambda bi, i: (bi, 0, 0)),
        ],
        out_specs=pl.BlockSpec((1, tm, d), row),
        scratch_shapes=[pltpu.VMEM((tm, V_W), F32)],
        compiler_params=_cparams("parallel", "parallel"),
        name="retention_mixer",
    )(qk, v, gate, pool_y, x, sf, sb, tabs["mpair"], tabs["qdec_f"], tabs["qdec_b"], tabs["hmask"],
      tabs["bd_bf16"], tabs["avg"], gn.reshape(1, V_W), w_out_bf, norm_w.reshape(1, d), g1)


def _packed_width(d, dtype):
    return d * jnp.dtype(dtype).itemsize // 4


def _pack_rows(h):
    bits = lax.bitcast_convert_type(h.astype(F32), jnp.uint32)
    if h.dtype.itemsize == 4:
        return bits
    half = h.shape[1] // 2
    return (bits[:, half:] & jnp.uint32(0xFFFF0000)) | (bits[:, :half] >> 16)


def _unpack_rows(w, dtype):
    if jnp.dtype(dtype).itemsize == 4:
        return lax.bitcast_convert_type(w, dtype)
    lo = lax.bitcast_convert_type(w << 16, F32)
    hi = lax.bitcast_convert_type(w & jnp.uint32(0xFFFF0000), F32)
    return jnp.concatenate([lo, hi], axis=1).astype(dtype)


def _router_kernel(x_ref, g_ref, sh_ref, sc_ref, wr_ref, h_ref, aff_ref):
    h = _norm_mod(x_ref[0], g_ref[...], sh_ref[0], sc_ref[0])
    h_hi = h.astype(BF16)
    h_ref[0] = _pack_rows(h_hi)
    ne = wr_ref.shape[0]
    h_lo = (h - h_hi.astype(F32)).astype(BF16)
    w = wr_ref[...]
    w_hi = w.astype(BF16)
    w_lo = (w - w_hi.astype(F32)).astype(BF16)
    both = _dot_nt(jnp.concatenate([w_hi, w_lo], axis=0), h_hi)
    logits = both[0:ne] + both[ne:2 * ne] + _dot_nt(w_hi, h_lo)
    mx = jnp.max(logits, axis=0, keepdims=True)
    e = jnp.exp(logits - mx)
    aff_ref[0] = e / jnp.sum(e, axis=0, keepdims=True)


def router(x, norm_w, shift, scale, w_router_t, tm):
    b, n, d = x.shape
    ne = w_router_t.shape[0]
    return pl.pallas_call(
        _router_kernel,
        out_shape=(jax.ShapeDtypeStruct((b, n, _packed_width(d, BF16)), jnp.uint32),
                   jax.ShapeDtypeStruct((b, ne, n), F32)),
        grid=(b, n // tm),
        in_specs=[
            pl.BlockSpec((1, tm, d), lambda bi, i: (bi, i, 0)),
            pl.BlockSpec((1, d), lambda bi, i: (0, 0)),
            pl.BlockSpec((1, 1, d), lambda bi, i: (bi, 0, 0)),
            pl.BlockSpec((1, 1, d), lambda bi, i: (bi, 0, 0)),
            pl.BlockSpec((ne, d), lambda bi, i: (0, 0)),
        ],
        out_specs=(
            pl.BlockSpec((1, tm, _packed_width(d, BF16)), lambda bi, i: (bi, i, 0)),
            pl.BlockSpec((1, ne, tm), lambda bi, i: (bi, 0, i)),
        ),
        compiler_params=_cparams("parallel", "parallel"),
        name="router",
    )(x, norm_w.reshape(1, d), shift, scale, w_router_t)


SC_CORES = 2
SC_SUBCORES = 16
GATHER_CHUNK = 64


def gather_rows(table, idx):
    rows, width = idx.shape[0], table.shape[1]
    workers = SC_CORES * SC_SUBCORES
    per_worker = rows // workers
    assert rows == per_worker * workers and per_worker % GATHER_CHUNK == 0
    mesh = plsc.VectorSubcoreMesh(core_axis_name="c", subcore_axis_name="s")

    def body(table_hbm, idx_hbm, out_hbm, idx_v, rows_v, sem):
        wid = lax.axis_index("s") * SC_CORES + lax.axis_index("c")
        base = wid * per_worker

        @pl.loop(0, per_worker // GATHER_CHUNK)
        def _(i):
            off = pl.multiple_of(base + i * GATHER_CHUNK, GATHER_CHUNK)
            pltpu.sync_copy(idx_hbm.at[pl.ds(off, GATHER_CHUNK)], idx_v)
            pltpu.async_copy(table_hbm.at[idx_v], rows_v, sem).wait()
            pltpu.sync_copy(rows_v, out_hbm.at[pl.ds(off, GATHER_CHUNK)])

    return pl.kernel(
        body,
        out_type=jax.ShapeDtypeStruct((rows, width), table.dtype),
        mesh=mesh,
        scratch_types=[
            pltpu.VMEM((GATHER_CHUNK,), jnp.int32),
            pltpu.VMEM((GATHER_CHUNK, width), table.dtype),
            pltpu.SemaphoreType.DMA,
        ],
        name="gather_rows",
    )(table, idx)


FFN_CHUNKS = 16
FFN_VMEM_LIMIT = 60 * 1024 * 1024


def _ffn_spans(f):
    chunk = f // FFN_CHUNKS
    assert chunk * FFN_CHUNKS == f
    span = max(-(-((c + 1) * chunk - c * chunk // 16 * 16) // 16) * 16 for c in range(FFN_CHUNKS))
    offs = [min(c * chunk // 16 * 16, f - span) for c in range(FFN_CHUNKS)]
    assert all(o % 16 == 0 and o <= c * chunk and o + span >= (c + 1) * chunk
               for c, o in enumerate(offs))
    return chunk, span


def _ffn_kernel(*refs, rt, layer, cps, extra_rows):
    if extra_rows:
        (x_ref, gate_ref, xc_ref, gc_ref, wg_hbm, wu_hbm, wd_hbm, o_ref, oc_ref, wbf, stage, sem) = refs
    else:
        x_ref, gate_ref, wg_hbm, wu_hbm, wd_hbm, o_ref, wbf, stage, sem = refs
    e = pl.program_id(0)
    g = pl.program_id(1)
    ne = pl.num_programs(0)
    slot = e % 2
    nxt = jnp.minimum(e + 1, ne - 1)
    mats = (wg_hbm, wu_hbm, wd_hbm)
    chunk, span = _ffn_spans(wbf.shape[2])

    def offset(c):
        last = wbf.shape[2] - span
        if isinstance(c, int):
            return min(c * chunk // 16 * 16, last)
        return pl.multiple_of(jnp.minimum(c * chunk // 16 * 16, last), 16)

    def copies(ee, c, st):
        return [pltpu.make_async_copy(m.at[layer, ee, pl.ds(offset(c), span), :], stage.at[st, i],
                                      sem.at[st, i]) for i, m in enumerate(mats)]

    def start(ee, c, st):
        for cp in copies(ee, c, st):
            cp.start()

    def wait(ee, c, st):
        for cp in copies(ee, c, st):
            cp.wait()

    def cast(c, st, i, sl):
        wbf[sl, i, pl.ds(offset(c), span), :] = stage[st, i].astype(wbf.dtype)

    @pl.when((e == 0) & (g == 0))
    def _():
        for c in range(FFN_CHUNKS):
            start(0, c, 0)
            wait(0, c, 0)
            for i in range(3):
                cast(c, 0, i, 0)

    def swiglu(words, gate_row):
        r = words.shape[0]
        xt = _unpack_rows(words, wbf.dtype)
        a = _dot_nt(xt, wbf[slot, 0])
        u = _dot_nt(xt, wbf[slot, 1])
        hh = (a * jax.nn.sigmoid(a) * u).astype(wbf.dtype)
        gate_col = jnp.broadcast_to(gate_row, (LANES, r)).T[:, 0:1]
        return (_dot(hh, wbf[slot, 2]) * gate_col).astype(o_ref.dtype)

    def compute_tile(t):
        rs = slice(t * rt, (t + 1) * rt)
        o_ref[0, 0, rs, :] = swiglu(x_ref[0, 0, rs, :], gate_ref[0, 0, :, rs])

    if extra_rows:
        @pl.when(g == pl.num_programs(1) - 1)
        def _():
            oc_ref[0, 0] = swiglu(xc_ref[0, 0], gc_ref[0, 0])

    c0 = g * cps
    ntiles = x_ref.shape[2] // rt
    start(nxt, c0, 0)
    for t in range(ntiles):
        compute_tile(t)
        if 1 <= t <= cps:
            for i in range(3):
                cast(c0 + t - 1, (t - 1) % 2, i, 1 - slot)
        if t < cps:
            wait(nxt, c0 + t, t % 2)
            if t + 1 < cps:
                start(nxt, c0 + t + 1, (t + 1) % 2)
    if cps == ntiles:
        for i in range(3):
            cast(c0 + cps - 1, (cps - 1) % 2, i, 1 - slot)


def expert_ffn(xs, gate, wg_t, wu_t, wd, layer, extra=None):
    g, ne, rows, dp = xs.shape
    f, d = wd.shape[2], wd.shape[3]
    assert FFN_CHUNKS % g == 0
    cps = FFN_CHUNKS // g
    rt = min(rows, 256, rows // cps)
    assert rows % rt == 0 and rt % 16 == 0
    span = _ffn_spans(f)[1]
    any_spec = pl.BlockSpec(memory_space=pl.ANY)
    in_specs = [pl.BlockSpec((1, 1, rows, dp), lambda e, gi: (gi, e, 0, 0)),
                pl.BlockSpec((1, 1, 1, rows), lambda e, gi: (gi, e, 0, 0))]
    out_specs = pl.BlockSpec((1, 1, rows, d), lambda e, gi: (gi, e, 0, 0))
    out_shape = jax.ShapeDtypeStruct((g, ne, rows, d), BF16)
    args = [xs, gate]
    if extra is not None:
        rc = extra[0].shape[2]
        in_specs += [pl.BlockSpec((1, 1, rc, dp), lambda e, gi: (0, e, 0, 0)),
                     pl.BlockSpec((1, 1, 1, rc), lambda e, gi: (0, e, 0, 0))]
        out_specs = (out_specs, pl.BlockSpec((1, 1, rc, d), lambda e, gi: (0, e, 0, 0)))
        out_shape = (out_shape, jax.ShapeDtypeStruct((1, ne, rc, d), BF16))
        args += list(extra)
    return pl.pallas_call(
        functools.partial(_ffn_kernel, rt=rt, layer=layer, cps=cps, extra_rows=extra is not None),
        out_shape=out_shape,
        grid=(ne, g),
        in_specs=in_specs + [any_spec, any_spec, any_spec],
        out_specs=out_specs,
        scratch_shapes=[
            pltpu.VMEM((2, 3, f, d), BF16),
            pltpu.VMEM((min(cps, 2), 3, span, d), F32),
            pltpu.SemaphoreType.DMA((min(cps, 2), 3)),
        ],
        compiler_params=pltpu.CompilerParams(dimension_semantics=("arbitrary", "arbitrary"),
                                             vmem_limit_bytes=FFN_VMEM_LIMIT),
        name="expert_ffn",
    )(*args, wg_t, wu_t, wd)


ROUTE_WIN = 64
LANES = 128
VAL_ROWS = 8


def _combine_kernel(st_ref, rank_ref, *refs, tt, ne, cap, merged, nparts):
    y_parts = refs[:nparts]
    x_ref, nw_ref, g_ref, o_ref, buf, xbuf, sem, xsem, acc_ref = refs[nparts:]
    b = pl.program_id(0)
    k = pl.program_id(1)
    nk = pl.num_programs(1)
    step = b * nk + k
    slot = step % 2
    rows_total = y_parts[0].shape[2]
    per_part = pl.num_programs(0) // nparts
    win = ROUTE_WIN
    base = b * cap if merged else 0
    lane = lax.broadcasted_iota(jnp.int32, (1, LANES), 1)

    def tile_bounds(bb, kk, e):
        off = bb * cap if merged else 0
        lo = off + st_ref[(bb * (nk + 1) + kk) * ne + e]
        hi = off + st_ref[(bb * (nk + 1) + kk + 1) * ne + e]
        return lo, hi, jnp.minimum((lo // 16) * 16, rows_total - win)

    def bounds(e):
        return tile_bounds(b, k, e)

    def for_part(bb, fn):
        if merged:
            fn(y_parts[0], 0)
        elif nparts == 1:
            fn(y_parts[0], bb)
        else:
            for pi in range(nparts):
                pl.when(bb // per_part == pi)(
                    functools.partial(fn, y_parts[pi], bb - pi * per_part))

    def win_copy(y_ref, gi, e, a0, sl):
        src = y_ref.at[gi, e, pl.ds(pl.multiple_of(a0, 16), win), :]
        return pltpu.make_async_copy(src, buf.at[sl, pl.ds(e * win, win), :], sem.at[sl, e])

    def start_tile(bb, kk, sl):
        def go(y_ref, gi):
            for e in range(ne):
                win_copy(y_ref, gi, e, tile_bounds(bb, kk, e)[2], sl).start()
        for_part(bb, go)

    @pl.when(step == 0)
    def _():
        start_tile(b, k, slot)

    nxt = step + 1

    @pl.when(nxt < pl.num_programs(0) * nk)
    def _():
        start_tile(nxt // nk, nxt % nk, 1 - slot)

    rkf = rank_ref[0].astype(F32)
    rk_t = jnp.concatenate([rkf, jnp.full((LANES - ne, tt), -1.0, F32)], axis=0).T
    rk_t = rk_t.astype(jnp.int32)

    def target(e, origin):
        col = rk_t[:, e:e + 1]
        return jnp.where(col >= 0, col + (base - origin), -1)

    pieces = []
    for j in range(ne // 2):
        t_even = target(2 * j, bounds(2 * j)[2])
        t_odd = target(2 * j + 1, bounds(2 * j + 1)[2])
        t_odd = jnp.where((t_odd >= 0) & (t_odd < win), t_odd + win, -1)
        pieces.append((jnp.where(lane < win, t_even, t_odd) == lane).astype(BF16))
    p = jnp.concatenate(pieces, axis=1)
    for e in range(ne):
        win_copy(y_parts[0], 0, e, 0, slot).wait()
    acc_ref[...] = _dot(p, buf[slot])

    for e in range(ne):
        lo, hi, a0 = bounds(e)

        def extra(w, carry, e=e, a0=a0):
            start = a0 + w * win
            aw = pl.multiple_of(jnp.minimum(start, rows_total - win), 16)

            def fetch(y_ref, gi):
                cp = pltpu.make_async_copy(y_ref.at[gi, e, pl.ds(aw, win), :], xbuf, xsem)
                cp.start()
                cp.wait()

            for_part(b, fetch)
            col = rk_t[:, e:e + 1]
            ok = (col >= 0) & (col + base >= start)
            px = (jnp.where(ok, col + (base - aw), -1) == lane[:, 0:win]).astype(BF16)
            acc_ref[...] += _dot(px, xbuf[...])
            return carry

        lax.fori_loop(1, (hi - a0 + win - 1) // win, extra, 0)

    y = acc_ref[...]
    yn = y * lax.rsqrt(jnp.mean(y * y, axis=-1, keepdims=True) + EPS) * nw_ref[...]
    o_ref[0] = x_ref[0] + g_ref[0] * yn


def combine_post(starts, rank_t, ys, x, norm_w, g, tt, cap, merged):
    ys = tuple(ys) if isinstance(ys, (tuple, list)) else (ys,)
    b, n, d = x.shape
    _, ne, rows, _ = ys[0].shape
    assert b % len(ys) == 0 and (not merged or len(ys) == 1)
    grid_spec = pltpu.PrefetchScalarGridSpec(
        num_scalar_prefetch=1,
        grid=(b, n // tt),
        in_specs=[
            pl.BlockSpec((1, ne, tt), lambda bi, k, st: (bi, 0, k)),
            *[pl.BlockSpec(memory_space=pl.ANY)] * len(ys),
            pl.BlockSpec((1, tt, d), lambda bi, k, st: (bi, k, 0)),
            pl.BlockSpec((1, d), lambda bi, k, st: (0, 0)),
            pl.BlockSpec((1, 1, d), lambda bi, k, st: (bi, 0, 0)),
        ],
        out_specs=pl.BlockSpec((1, tt, d), lambda bi, k, st: (bi, k, 0)),
        scratch_shapes=[
            pltpu.VMEM((2, ne * ROUTE_WIN, d), BF16),
            pltpu.VMEM((ROUTE_WIN, d), BF16),
            pltpu.SemaphoreType.DMA((2, ne)),
            pltpu.SemaphoreType.DMA(()),
            pltpu.VMEM((tt, d), F32),
        ],
    )
    return pl.pallas_call(
        functools.partial(_combine_kernel, tt=tt, ne=ne, cap=cap, merged=merged, nparts=len(ys)),
        out_shape=jax.ShapeDtypeStruct((b, n, d), F32),
        grid_spec=grid_spec,
        compiler_params=_cparams("arbitrary", "arbitrary"),
        name="combine_post",
    )(starts, rank_t, *ys, x, norm_w.reshape(1, d), g)


def _select_kernel(aff_ref, tri_ref, rank_ref, st_ref, *, cap, tt):
    a = aff_ref[0]
    ne, n = a.shape
    bits = lax.bitcast_convert_type(a, jnp.int32)

    def search(_, c):
        lo, hi = c
        mid = lo + ((hi - lo) >> 1)
        cnt = jnp.sum((bits >= mid).astype(F32), axis=1, keepdims=True)
        ge = cnt >= cap
        return jnp.where(ge, mid, lo), jnp.where(ge, hi, mid)

    lo0 = jnp.zeros((ne, 1), jnp.int32)
    hi0 = jnp.full((ne, 1), 0x7F800000, jnp.int32)
    thr, _ = lax.fori_loop(0, 31, search, (lo0, hi0))
    gt = bits > thr
    eq = bits == thr
    need = cap - jnp.sum(gt.astype(F32), axis=1, keepdims=True)
    m = jnp.concatenate([gt, eq], axis=0).astype(BF16)
    tri = tri_ref[...]
    lane = lax.broadcasted_iota(jnp.int32, (1, LANES), 1)
    off = jnp.zeros((2 * ne, 1), F32)
    st = jnp.zeros((ne, LANES), jnp.int32)
    for j in range(n // LANES):
        cs = slice(j * LANES, (j + 1) * LANES)
        if (j * LANES) % tt == 0:
            off_sel = off[:ne] + jnp.minimum(off[ne:], need)
            st = jnp.where(lane == (j * LANES) // tt, off_sel.astype(jnp.int32), st)
        mj = m[:, cs]
        pj = _dot(mj, tri) + off
        pe = pj[ne:]
        sel = gt[:, cs] | (eq[:, cs] & (pe < need))
        rank_ref[0, :, cs] = jnp.where(sel, pj[:ne] + jnp.minimum(pe, need), -1.0).astype(jnp.int32)
        off = off + jnp.sum(mj.astype(F32), axis=1, keepdims=True)
    st_ref[0] = jnp.where(lane == n // tt, cap, st)


def route_select(aff_t, cap, tt):
    b, ne, n = aff_t.shape
    tri = jnp.asarray(np.triu(np.ones((LANES, LANES), np.float32), 1), BF16)
    return pl.pallas_call(
        functools.partial(_select_kernel, cap=cap, tt=tt),
        out_shape=(jax.ShapeDtypeStruct((b, ne, n), jnp.int32),
                   jax.ShapeDtypeStruct((b, ne, LANES), jnp.int32)),
        grid=(b,),
        in_specs=[pl.BlockSpec((1, ne, n), lambda bi: (bi, 0, 0)),
                  pl.BlockSpec((LANES, LANES), lambda bi: (0, 0))],
        out_specs=(pl.BlockSpec((1, ne, n), lambda bi: (bi, 0, 0)),
                   pl.BlockSpec((1, ne, LANES), lambda bi: (bi, 0, 0))),
        compiler_params=_cparams("parallel"),
        name="route_select",
    )(aff_t, tri)


def _compact_kernel(st_ref, rank_ref, aff_ref, idx_ref, gate_ref, out_ref, *, tt, ne, cap, n):
    b = pl.program_id(0)
    k = pl.program_id(1)
    nk = pl.num_programs(1)
    win = ROUTE_WIN

    @pl.when(k == 0)
    def _():
        out_ref[...] = jnp.zeros_like(out_ref)

    rk = rank_ref[0]
    a = aff_ref[0]
    g1 = a.astype(BF16).astype(F32)
    r1 = a - g1
    g2 = r1.astype(BF16).astype(F32)
    g3 = r1 - g2
    tok = k * tt + lax.broadcasted_iota(jnp.int32, (1, tt), 1)
    ids = jnp.concatenate([(tok >> 6).astype(F32), (tok & 63).astype(F32),
                           jnp.zeros((VAL_ROWS - 2, tt), F32)], axis=0)
    pad = jnp.zeros((LANES - VAL_ROWS - 3 * ne, tt), F32)
    payload = jnp.concatenate([ids, g1, g2, g3, pad], axis=0).astype(BF16)
    sub = lax.broadcasted_iota(jnp.int32, (win, 1), 0)

    def bounds(e):
        lo = st_ref[(b * (nk + 1) + k) * ne + e]
        hi = st_ref[(b * (nk + 1) + k + 1) * ne + e]
        return hi, (lo // 8) * 8

    def one_hot(e, aw):
        return ((rk[e:e + 1, :] - aw) == sub).astype(BF16)

    p_all = jnp.concatenate([one_hot(e, bounds(e)[1]) for e in range(ne)], axis=0)
    moved = _dot_nt(p_all, payload)
    for e in range(ne):
        a0 = pl.multiple_of(bounds(e)[1], 8)
        out_ref[e, pl.ds(a0, win), :] += moved[e * win:(e + 1) * win]

    for e in range(ne):
        hi, a0 = bounds(e)

        def window(w, carry, e=e, a0=a0):
            aw = pl.multiple_of(a0 + w * win, 8)
            out_ref[e, pl.ds(aw, win), :] += _dot_nt(one_hot(e, aw), payload)
            return carry

        lax.fori_loop(1, (hi - a0 + win - 1) // win, window, 0)

    @pl.when(k == nk - 1)
    def _():
        for e in range(ne):
            t = out_ref[e].T
            ids_e = (t[0:1] * 64.0 + t[1:2]).astype(jnp.int32) + b * n
            g_e = (t[VAL_ROWS + e:VAL_ROWS + e + 1] + t[VAL_ROWS + ne + e:VAL_ROWS + ne + e + 1]) \
                + t[VAL_ROWS + 2 * ne + e:VAL_ROWS + 2 * ne + e + 1]
            idx_ref[0, e:e + 1, :] = ids_e[:, 0:cap]
            gate_ref[0, e:e + 1, :] = g_e[:, 0:cap]


def route_compact(starts, rank_t, aff_t, cap, tt):
    b, ne, n = aff_t.shape
    rows = -(-(cap + ROUTE_WIN) // LANES) * LANES
    grid_spec = pltpu.PrefetchScalarGridSpec(
        num_scalar_prefetch=1,
        grid=(b, n // tt),
        in_specs=[pl.BlockSpec((1, ne, tt), lambda bi, k, st: (bi, 0, k)),
                  pl.BlockSpec((1, ne, tt), lambda bi, k, st: (bi, 0, k))],
        out_specs=(pl.BlockSpec((1, ne, cap), lambda bi, k, st: (bi, 0, 0)),
                   pl.BlockSpec((1, ne, cap), lambda bi, k, st: (bi, 0, 0))),
        scratch_shapes=[pltpu.VMEM((ne, rows, LANES), F32)],
    )
    return pl.pallas_call(
        functools.partial(_compact_kernel, tt=tt, ne=ne, cap=cap, n=n),
        out_shape=(jax.ShapeDtypeStruct((b, ne, cap), jnp.int32),
                   jax.ShapeDtypeStruct((b, ne, cap), F32)),
        grid_spec=grid_spec,
        compiler_params=_cparams("parallel", "arbitrary"),
        name="route_compact",
    )(starts, rank_t, aff_t)


def _qk_head_of_lane():
    half = QK_W // 2
    return (np.arange(QK_W) % half) // (DK // 2)


def _in_proj_perm():
    half = DK // 2
    first = [h * DK + i for h in range(N_HEADS) for i in range(half)]
    second = [h * DK + half + i for h in range(N_HEADS) for i in range(half)]
    qperm = np.array(first + second)
    rest = np.arange(2 * QK_W, 2 * QK_W + 2 * V_W + POOL_W)
    return np.concatenate([qperm, QK_W + qperm, rest])


def _static_tables():
    qk_head = _qk_head_of_lane()
    v_head = np.arange(V_W) // DV
    hmask = (qk_head[None, :] == np.arange(N_HEADS)[:, None])
    bd = (qk_head[:, None] == v_head[None, :])
    avg = (v_head[:, None] == v_head[None, :]).astype(np.float32) / DV
    return {
        "hmask": jnp.asarray(hmask[:, None, :], BF16),
        "bd_f32": jnp.asarray(bd, F32),
        "bd_bf16": jnp.asarray(bd, BF16),
        "avg": jnp.asarray(avg, BF16),
    }


def _decay_tables(lg_f, lg_b):
    pos = jnp.arange(CHUNK, dtype=F32)
    diff = pos[:, None] - pos[None, :]
    low = diff >= 0
    up = diff < 0
    m_f = jnp.where(low, jnp.exp(lg_f[:, None, None] * jnp.where(low, diff, 0.0)), 0.0)
    m_b = jnp.where(up, jnp.exp(lg_b[:, None, None] * jnp.where(up, -diff, 0.0)), 0.0)
    m = m_f + m_b
    mpair = m.reshape(N_HEADS // 2, 2, CHUNK, CHUNK).transpose(0, 2, 1, 3).reshape(
        N_HEADS // 2, CHUNK, 2 * CHUNK)
    qk_head = _qk_head_of_lane()
    v_head = np.arange(V_W) // DV
    qdec_f = jnp.exp(lg_f[None, :] * (pos[:, None] + 1.0))[:, v_head]
    qdec_b = jnp.exp(lg_b[None, :] * (CHUNK - pos[:, None]))[:, v_head]
    kdec_f = jnp.exp(lg_f[None, :] * (CHUNK - 1.0 - pos[:, None]))[:, qk_head]
    kdec_b = jnp.exp(lg_b[None, :] * pos[:, None])[:, qk_head]
    cdec_f = jnp.exp(lg_f * CHUNK)[None, v_head]
    cdec_b = jnp.exp(lg_b * CHUNK)[None, v_head]
    return {"mpair": mpair, "qdec_f": qdec_f, "qdec_b": qdec_b, "kdec_f": kdec_f,
            "kdec_b": kdec_b, "cdec_f": cdec_f, "cdec_b": cdec_b}


def _rope_tables(n):
    t = jnp.arange(n)
    row = (t // GRID_W).astype(F32)
    col = (t % GRID_W).astype(F32)
    n_freq = DK // 4
    inv = ROPE_BASE ** (-jnp.arange(n_freq, dtype=F32) / n_freq)
    ang = jnp.concatenate([row[:, None] * inv, col[:, None] * inv], axis=-1)
    return jnp.tile(jnp.cos(ang), (1, N_HEADS)), jnp.tile(jnp.sin(ang), (1, N_HEADS))


ROUTE_PARTS = 2


def _route(h_pk, aff_t, merged, parts=1):
    b, n, d = h_pk.shape
    cap = EC_FACTOR * n // N_EXPERTS
    ne = N_EXPERTS
    tt = min(n, 256)
    nk = n // tt
    rank_t, st = route_select(aff_t, cap, tt)
    starts = st[:, :, :nk + 1].transpose(0, 2, 1)
    info = (starts.reshape(-1), rank_t, tt, cap, merged)
    if merged:
        flat, gate = route_compact(starts.reshape(-1), rank_t, aff_t, cap, tt)
        flat = flat.transpose(1, 0, 2)
        gate = gate.transpose(1, 0, 2).reshape(1, ne, b * cap)
        xs = gather_rows(h_pk.reshape(b * n, d), flat.reshape(-1)).reshape(gate.shape + (d,))
        return xs, gate[:, :, None, :], info
    per = b // parts
    xs_list, gate_list = [], []
    for i in range(parts):
        sl = slice(i * per, (i + 1) * per)
        flat, gate = route_compact(starts[sl].reshape(-1), rank_t[sl], aff_t[sl], cap, tt)
        xs = gather_rows(h_pk[sl].reshape(per * n, d), flat.reshape(-1))
        xs_list.append(xs.reshape(gate.shape + (d,)))
        gate_list.append(gate[:, :, None, :])
    return xs_list, gate_list, info


def _combine(x, y, info, norm_w, g):
    starts, rank_t, tt, cap, merged = info
    return combine_post(starts, rank_t, y, x, norm_w, g, tt, cap, merged)


def kernel(x, c, ctx, c_ctx, w_ada, b_ada, norm_pre_mix, norm_post_mix, norm_pre_ffn, norm_post_ffn, w_in, ret_decay_fwd, ret_decay_bwd, ret_gn, pool_w, pool_scale, w_out, w_router, w_gate, w_up, w_down):
    b, n, d = x.shape
    lc = ctx.shape[1]
    depth = w_ada.shape[0]
    rope = _rope_tables(n)
    static = _static_tables()
    perm = _in_proj_perm()

    cc = jnp.concatenate([c, c_ctx[None, :], jnp.zeros((7, d), F32)], axis=0)
    mods = ada_modulation(cc, w_ada, b_ada)
    wg, wu, wd = jnp.swapaxes(w_gate, 2, 3), jnp.swapaxes(w_up, 2, 3), w_down

    tm = 512
    for l in range(depth):
        last = l == depth - 1
        mx = mods[l, :b].reshape(b, 1, 6, d)
        sh1, sc1, g1, sh2, sc2, g2 = [mx[:, :, i] for i in range(6)]
        mc = jnp.broadcast_to(mods[l, b].reshape(1, 1, 6, d), (b, 1, 6, d))
        csh1, csc1, cg1, csh2, csc2, cg2 = [mc[:, :, i] for i in range(6)]
        lg_f = jax.nn.log_sigmoid(ret_decay_fwd[l].astype(F32))
        lg_b = jax.nn.log_sigmoid(ret_decay_bwd[l].astype(F32))
        tabs = dict(static, **_decay_tables(lg_f, lg_b))
        w_in_p = w_in[l][:, perm].astype(BF16)
        w_out_bf = w_out[l].astype(BF16)
        pool_w_bf = pool_w[l].astype(BF16)

        qk_c, v_c, gate_c, p_c = premix(ctx, norm_pre_mix[l], csh1, csc1, w_in_p, None, lc)
        zero = jnp.zeros((b, DK, V_W), F32)
        sf_c, sb_c, s_f, s_b = state_scan(qk_c, v_c, zero, zero, tabs)
        qk_x, v_x, gate_x, p_x = premix(x, norm_pre_mix[l], sh1, sc1, w_in_p, rope, tm)
        sf_x, sb_x, _, _ = state_scan(qk_x, v_x, s_f, s_b, tabs)
        pool_x = pool_mixer(p_x, pool_w_bf, pool_scale[l], GRID_W, min(n, 2048))
        x = retention_mixer(qk_x, v_x, gate_x, pool_x, x, sf_x, sb_x, tabs, ret_gn[l], w_out_bf,
                            norm_post_mix[l], g1, tm)
        if not last:
            pool_c = pool_mixer(p_c, pool_w_bf, pool_scale[l], lc, lc)
            ctx = retention_mixer(qk_c, v_c, gate_c, pool_c, ctx, sf_c, sb_c, tabs, ret_gn[l],
                                  w_out_bf, norm_post_mix[l], cg1, lc)

        wr_t = w_router[l].T
        h_x, aff_x = router(x, norm_pre_ffn[l], sh2, sc2, wr_t, tm)
        xs_x, gate_x2, info_x = _route(h_x, aff_x, False, parts=ROUTE_PARTS)
        y_x = [expert_ffn(xs_x[i], gate_x2[i], wg, wu, wd, l) for i in range(ROUTE_PARTS - 1)]
        if last:
            y_x.append(expert_ffn(xs_x[-1], gate_x2[-1], wg, wu, wd, l))
        else:
            h_c, aff_c = router(ctx, norm_pre_ffn[l], csh2, csc2, wr_t, lc)
            xs_c, gate_c2, info_c = _route(h_c, aff_c, True)
            y_last, y_c = expert_ffn(xs_x[-1], gate_x2[-1], wg, wu, wd, l, extra=(xs_c, gate_c2))
            y_x.append(y_last)
            ctx = _combine(ctx, y_c, info_c, norm_post_ffn[l], cg2)
        x = _combine(x, y_x, info_x, norm_post_ffn[l], g2)
    return x
```

```python
import functools

import jax
import jax.numpy as jnp
import numpy as np
from jax import lax
from jax.experimental import pallas as pl
from jax.experimental.pallas import tpu as pltpu
from jax.experimental.pallas import tpu_sc as plsc

F32 = jnp.float32
BF16 = jnp.bfloat16

D_MODEL = 1024
GRID_W = 64
N_HEADS = 8
DV = 64
DK = 32
QK_W = N_HEADS * DK
V_W = N_HEADS * DV
POOL_W = 512
POOL_WINDOWS = (2, 4, 8, 16)
POOL_GROUP_DIM = 128
CHUNK = 128
ROPE_BASE = 10000.0
N_EXPERTS = 16
EC_FACTOR = 2
EPS = 1e-6
MAX_HALF_WINDOW = max(POOL_WINDOWS) // 2

VMEM_LIMIT = 56 * 1024 * 1024


def _cparams(*sem):
    return pltpu.CompilerParams(dimension_semantics=sem, vmem_limit_bytes=VMEM_LIMIT)


def _dot(a, b):
    return jnp.dot(a, b, preferred_element_type=F32)


def _dot_nt(a, b, precision=None):
    return lax.dot_general(a, b, (((1,), (1,)), ((), ())), precision=precision,
                           preferred_element_type=F32)


def _ada_kernel(cc_ref, w_ref, b_ref, o_ref):
    s = cc_ref[...]
    s = s * jax.nn.sigmoid(s)
    o_ref[0] = _dot(s.astype(BF16), w_ref[0].astype(BF16)) + b_ref[0]


def ada_modulation(cc, w_ada, b_ada):
    depth, d, d6 = w_ada.shape
    rows = cc.shape[0]
    tn = 1536
    return pl.pallas_call(
        _ada_kernel,
        out_shape=jax.ShapeDtypeStruct((depth, rows, d6), F32),
        grid=(depth, d6 // tn),
        in_specs=[
            pl.BlockSpec((rows, d), lambda l, j: (0, 0)),
            pl.BlockSpec((1, d, tn), lambda l, j: (l, 0, j)),
            pl.BlockSpec((1, 1, tn), lambda l, j: (l, 0, j)),
        ],
        out_specs=pl.BlockSpec((1, rows, tn), lambda l, j: (l, 0, j)),
        compiler_params=_cparams("parallel", "parallel"),
        name="ada_modulation",
    )(cc, w_ada, b_ada.reshape(depth, 1, d6))


def _norm_mod(xf, g, sh, sc):
    y = xf * lax.rsqrt(jnp.mean(xf * xf, axis=-1, keepdims=True) + EPS)
    return (y * g) * (1.0 + sc) + sh


def _premix_kernel(x_ref, g_ref, sh_ref, sc_ref, w_ref, *rest, rope):
    if rope:
        cos_ref, sin_ref, qk_ref, v_ref, gate_ref, p_ref = rest
    else:
        qk_ref, v_ref, gate_ref, p_ref = rest
    hb = _norm_mod(x_ref[0], g_ref[...], sh_ref[0], sc_ref[0]).astype(BF16)
    zqk = _dot(hb, w_ref[:, 0:2 * QK_W])
    half = QK_W // 2
    q1, q2 = zqk[:, 0:half], zqk[:, half:2 * half]
    k1 = zqk[:, 2 * half:3 * half] * (DK ** -0.5)
    k2 = zqk[:, 3 * half:4 * half] * (DK ** -0.5)
    if rope:
        cos, sin = cos_ref[...], sin_ref[...]
        q1, q2 = q1 * cos - q2 * sin, q1 * sin + q2 * cos
        k1, k2 = k1 * cos - k2 * sin, k1 * sin + k2 * cos
    qk_ref[0] = jnp.concatenate([q1, q2, k1, k2], axis=1).astype(BF16)
    o = 2 * QK_W
    v_ref[0] = _dot(hb, w_ref[:, o:o + V_W]).astype(BF16)
    gate_ref[0] = _dot(hb, w_ref[:, o + V_W:o + 2 * V_W])
    p_ref[0] = _dot(hb, w_ref[:, o + 2 * V_W:o + 2 * V_W + POOL_W])


def premix(x, norm_w, shift, scale, w_in_p, rope, tm):
    b, n, d = x.shape
    in_w = w_in_p.shape[1]
    row = lambda bi, i: (bi, i, 0)
    in_specs = [
        pl.BlockSpec((1, tm, d), row),
        pl.BlockSpec((1, d), lambda bi, i: (0, 0)),
        pl.BlockSpec((1, 1, d), lambda bi, i: (bi, 0, 0)),
        pl.BlockSpec((1, 1, d), lambda bi, i: (bi, 0, 0)),
        pl.BlockSpec((d, in_w), lambda bi, i: (0, 0)),
    ]
    args = [x, norm_w.reshape(1, d), shift, scale, w_in_p]
    if rope is not None:
        half = QK_W // 2
        in_specs += [pl.BlockSpec((tm, half), lambda bi, i: (i, 0))] * 2
        args += list(rope)
    return pl.pallas_call(
        functools.partial(_premix_kernel, rope=rope is not None),
        out_shape=(
            jax.ShapeDtypeStruct((b, n, 2 * QK_W), BF16),
            jax.ShapeDtypeStruct((b, n, V_W), BF16),
            jax.ShapeDtypeStruct((b, n, V_W), F32),
            jax.ShapeDtypeStruct((b, n, POOL_W), F32),
        ),
        grid=(b, n // tm),
        in_specs=in_specs,
        out_specs=(
            pl.BlockSpec((1, tm, 2 * QK_W), row),
            pl.BlockSpec((1, tm, V_W), row),
            pl.BlockSpec((1, tm, V_W), row),
            pl.BlockSpec((1, tm, POOL_W), row),
        ),
        compiler_params=_cparams("parallel", "parallel"),
        name="premix",
    )(*args)


def _fold_state(s):
    g = 2 * N_HEADS
    top = s[0:g]
    bot = s[QK_W // 2:QK_W // 2 + g]
    for h in range(1, N_HEADS):
        top = top + s[h * g:(h + 1) * g]
        bot = bot + s[QK_W // 2 + h * g:QK_W // 2 + (h + 1) * g]
    return jnp.concatenate([top, bot], axis=0)


def _expand_state(c):
    g = 2 * N_HEADS
    return jnp.concatenate([c[0:g]] * N_HEADS + [c[g:2 * g]] * N_HEADS, axis=0)


def _state_kernel(kf_ref, vf_ref, kb_ref, vb_ref, s0f_ref, s0b_ref, kdf_ref, kdb_ref,
                  cdf_ref, cdb_ref, bd_ref, sf_out, sb_out, ff_out, fb_out, sf_acc, sb_acc, *, cps):
    c = pl.program_id(1)
    nc = pl.num_programs(1)
    bd = bd_ref[...]

    @pl.when(c == 0)
    def _():
        sf_acc[...] = _expand_state(s0f_ref[0]) * bd
        sb_acc[...] = _expand_state(s0b_ref[0]) * bd

    def update(s, k, v, kd_ref, cd_ref):
        kd = k.astype(F32) * kd_ref[...]
        f = _dot(kd.T.astype(BF16), v)
        return s * cd_ref[...] + f * bd

    sf = sf_acc[...]
    sb = sb_acc[...]
    for i in range(cps):
        j = cps - 1 - i
        fs = slice(i * CHUNK, (i + 1) * CHUNK)
        bs = slice(j * CHUNK, (j + 1) * CHUNK)
        sf_out[0, i] = _fold_state(sf).astype(BF16)
        sb_out[0, j] = _fold_state(sb).astype(BF16)
        sf = update(sf, kf_ref[0, fs, :], vf_ref[0, fs, :], kdf_ref, cdf_ref)
        sb = update(sb, kb_ref[0, bs, :], vb_ref[0, bs, :], kdb_ref, cdb_ref)
    sf_acc[...] = sf
    sb_acc[...] = sb

    @pl.when(c == nc - 1)
    def _():
        ff_out[0] = _fold_state(sf)
        fb_out[0] = _fold_state(sb)


def state_scan(qk, v, s0f, s0b, tabs):
    b, n, _ = qk.shape
    nc = n // CHUNK
    cps = min(nc, 4)
    ns = nc // cps
    rows = cps * CHUNK
    const = lambda shape: pl.BlockSpec(shape, lambda bi, c: (0,) * len(shape))
    return pl.pallas_call(
        functools.partial(_state_kernel, cps=cps),
        out_shape=(
            jax.ShapeDtypeStruct((b, nc, DK, V_W), BF16),
            jax.ShapeDtypeStruct((b, nc, DK, V_W), BF16),
            jax.ShapeDtypeStruct((b, DK, V_W), F32),
            jax.ShapeDtypeStruct((b, DK, V_W), F32),
        ),
        grid=(b, ns),
        in_specs=[
            pl.BlockSpec((1, rows, QK_W), lambda bi, c: (bi, c, 1)),
            pl.BlockSpec((1, rows, V_W), lambda bi, c: (bi, c, 0)),
            pl.BlockSpec((1, rows, QK_W), lambda bi, c: (bi, ns - 1 - c, 1)),
            pl.BlockSpec((1, rows, V_W), lambda bi, c: (bi, ns - 1 - c, 0)),
            pl.BlockSpec((1, DK, V_W), lambda bi, c: (bi, 0, 0)),
            pl.BlockSpec((1, DK, V_W), lambda bi, c: (bi, 0, 0)),
            const((CHUNK, QK_W)), const((CHUNK, QK_W)),
            const((1, V_W)), const((1, V_W)),
            const((QK_W, V_W)),
        ],
        out_specs=(
            pl.BlockSpec((1, cps, DK, V_W), lambda bi, c: (bi, c, 0, 0)),
            pl.BlockSpec((1, cps, DK, V_W), lambda bi, c: (bi, ns - 1 - c, 0, 0)),
            pl.BlockSpec((1, DK, V_W), lambda bi, c: (bi, 0, 0)),
            pl.BlockSpec((1, DK, V_W), lambda bi, c: (bi, 0, 0)),
        ),
        scratch_shapes=[pltpu.VMEM((QK_W, V_W), F32), pltpu.VMEM((QK_W, V_W), F32)],
        compiler_params=_cparams("parallel", "arbitrary"),
        name="state_scan",
    )(qk, v, qk, v, s0f, s0b, tabs["kdec_f"], tabs["kdec_b"], tabs["cdec_f"], tabs["cdec_b"],
      tabs["bd_f32"])


def _pool_kernel(*refs, gw, tile, halo, rows_total):
    if halo:
        prev_ref, cur_ref, next_ref, pw_ref, ps_ref, o_ref, buf = refs
    else:
        cur_ref, pw_ref, ps_ref, o_ref, buf = refs
    i = pl.program_id(1)
    last = pl.num_programs(1) - 1
    m = MAX_HALF_WINDOW
    span = tile + 2 * halo
    zeros_m = jnp.zeros((m, POOL_W), F32)
    buf[0:m] = zeros_m
    buf[m + span:2 * m + span] = zeros_m
    if halo:
        buf[m:m + halo] = jnp.where(i > 0, prev_ref[0], 0.0)
        buf[m + halo + tile:m + span] = jnp.where(i < last, next_ref[0], 0.0)
    buf[m + halo:m + halo + tile] = cur_ref[0]

    pos = lax.broadcasted_iota(jnp.int32, (span, 1), 0)
    col = pos % gw
    tpos = lax.broadcasted_iota(jnp.int32, (tile, 1), 0)
    tcol = tpos % gw
    trow = i * (tile // gw) + tpos // gw
    for gi, w in enumerate(POOL_WINDOWS):
        cs = slice(gi * POOL_GROUP_DIM, (gi + 1) * POOL_GROUP_DIM)
        hw = w // 2
        s = None
        for d in range(-hw, hw):
            valid = (col + d >= 0) & (col + d < gw)
            term = jnp.where(valid, buf[m + d:m + d + span, cs], 0.0)
            s = term if s is None else s + term
        if halo:
            acc = None
            for d in range(-hw, hw):
                start = halo + d * gw
                term = s[start:start + tile]
                acc = term if acc is None else acc + term
            cnt_r = jnp.minimum(trow + hw, rows_total) - jnp.maximum(trow - hw, 0)
        else:
            acc = s
            cnt_r = 1
        cnt_c = jnp.minimum(tcol + hw, gw) - jnp.maximum(tcol - hw, 0)
        cnt = (cnt_r * cnt_c).astype(F32)
        xg = buf[m + halo:m + halo + tile, cs]
        diff = (acc / cnt - xg).astype(BF16)
        y = _dot(diff, pw_ref[gi]) * ps_ref[:, cs]
        o_ref[0, :, cs] = y.astype(BF16)


def pool_mixer(p, pool_w_bf, pool_scale, gw, tile):
    b, n, pw = p.shape
    rows_total = n // gw
    halo = MAX_HALF_WINDOW * gw if rows_total > 1 else 0
    in_specs, args = [], []
    if halo:
        r = tile // halo
        nh = n // halo
        in_specs = [
            pl.BlockSpec((1, halo, pw), lambda bi, i: (bi, jnp.maximum(i * r - 1, 0), 0)),
            pl.BlockSpec((1, tile, pw), lambda bi, i: (bi, i, 0)),
            pl.BlockSpec((1, halo, pw), lambda bi, i: (bi, jnp.minimum((i + 1) * r, nh - 1), 0)),
        ]
        args = [p, p, p]
    else:
        in_specs = [pl.BlockSpec((1, tile, pw), lambda bi, i: (bi, i, 0))]
        args = [p]
    ng = len(POOL_WINDOWS)
    in_specs += [
        pl.BlockSpec((ng, POOL_GROUP_DIM, POOL_GROUP_DIM), lambda bi, i: (0, 0, 0)),
        pl.BlockSpec((1, pw), lambda bi, i: (0, 0)),
    ]
    args += [pool_w_bf, pool_scale.reshape(1, pw)]
    return pl.pallas_call(
        functools.partial(_pool_kernel, gw=gw, tile=tile, halo=halo, rows_total=rows_total),
        out_shape=jax.ShapeDtypeStruct((b, n, pw), BF16),
        grid=(b, n // tile),
        in_specs=in_specs,
        out_specs=pl.BlockSpec((1, tile, pw), lambda bi, i: (bi, i, 0)),
        scratch_shapes=[pltpu.VMEM((tile + 2 * halo + 2 * MAX_HALF_WINDOW, pw), F32)],
        compiler_params=_cparams("parallel", "parallel"),
        name="pool_mixer",
    )(*args)


def _ret_kernel(qk_ref, v_ref, gate_ref, py_ref, x_ref, sf_ref, sb_ref, mp_ref, qdf_ref, qdb_ref,
                hm_ref, bd_ref, avg_ref, gn_ref, wo_ref, nw_ref, g1_ref, o_ref, o_buf, *, nch):
    bd = bd_ref[...]
    avg = avg_ref[...]
    lane = lax.broadcasted_iota(jnp.int32, (1, 2 * DV), 1)
    lo_mask = (lane < DV).astype(BF16)
    hi_mask = (lane >= DV).astype(BF16)
    for ci in range(nch):
        rs = slice(ci * CHUNK, (ci + 1) * CHUNK)
        q = qk_ref[0, rs, 0:QK_W]
        k = qk_ref[0, rs, QK_W:2 * QK_W]
        vv = v_ref[0, rs, :]
        inner = []
        for j in range(N_HEADS // 2):
            kp = jnp.concatenate([k * hm_ref[2 * j], k * hm_ref[2 * j + 1]], axis=0)
            sc = _dot_nt(q, kp) * mp_ref[j]
            vpair = vv[:, 2 * DV * j:2 * DV * (j + 1)]
            vp = jnp.concatenate([vpair * lo_mask, vpair * hi_mask], axis=0)
            inner.append(_dot(sc.astype(BF16), vp))
        o = jnp.concatenate(inner, axis=1)
        sf_bd = _expand_state(sf_ref[0, ci]) * bd
        sb_bd = _expand_state(sb_ref[0, ci]) * bd
        o_buf[rs, :] = o + _dot(q, sf_bd) * qdf_ref[...] + _dot(q, sb_bd) * qdb_ref[...]

    def head_mean(a):
        hi = a.astype(BF16)
        lo = (a - hi.astype(F32)).astype(BF16)
        return _dot(hi, avg) + _dot(lo, avg)

    o = o_buf[...]
    dlt = o - head_mean(o)
    var = head_mean(dlt * dlt)
    yn = dlt * lax.rsqrt(var + EPS) * gn_ref[...]
    g = gate_ref[0]
    r = (g * jax.nn.sigmoid(g) * yn).astype(BF16)
    mix = _dot(r, wo_ref[0:V_W, :]) + _dot(py_ref[0], wo_ref[V_W:V_W + POOL_W, :])
    y = mix * lax.rsqrt(jnp.mean(mix * mix, axis=-1, keepdims=True) + EPS) * nw_ref[...]
    o_ref[0] = x_ref[0] + g1_ref[0] * y


def retention_mixer(qk, v, gate, pool_y, x, sf, sb, tabs, gn, w_out_bf, norm_w, g1, tm):
    b, n, d = x.shape
    nch = tm // CHUNK
    row = lambda bi, i: (bi, i, 0)
    const = lambda shape: pl.BlockSpec(shape, lambda bi, i: (0,) * len(shape))
    return pl.pallas_call(
        functools.partial(_ret_kernel, nch=nch),
        out_shape=jax.ShapeDtypeStruct((b, n, d), F32),
        grid=(b, n // tm),
        in_specs=[
            pl.BlockSpec((1, tm, 2 * QK_W), row),
            pl.BlockSpec((1, tm, V_W), row),
            pl.BlockSpec((1, tm, V_W), row),
            pl.BlockSpec((1, tm, POOL_W), row),
            pl.BlockSpec((1, tm, d), row),
            pl.BlockSpec((1, nch, DK, V_W), lambda bi, i: (bi, i, 0, 0)),
            pl.BlockSpec((1, nch, DK, V_W), lambda bi, i: (bi, i, 0, 0)),
            const((N_HEADS // 2, CHUNK, 2 * CHUNK)),
            const((CHUNK, V_W)), const((CHUNK, V_W)),
            const((N_HEADS, 1, QK_W)),
            const((QK_W, V_W)),
            const((V_W, V_W)),
            const((1, V_W)),
            const((V_W + POOL_W, d)),
            const((1, d)),
            pl.BlockSpec((1, 1, d), lambda bi, i: (bi, 0, 0)),
        ],
        out_specs=pl.BlockSpec((1, tm, d), row),
        scratch_shapes=[pltpu.VMEM((tm, V_W), F32)],
        compiler_params=_cparams("parallel", "parallel"),
        name="retention_mixer",
    )(qk, v, gate, pool_y, x, sf, sb, tabs["mpair"], tabs["qdec_f"], tabs["qdec_b"], tabs["hmask"],
      tabs["bd_bf16"], tabs["avg"], gn.reshape(1, V_W), w_out_bf, norm_w.reshape(1, d), g1)


def _packed_width(d, dtype):
    return d * jnp.dtype(dtype).itemsize // 4


def _pack_rows(h):
    bits = lax.bitcast_convert_type(h.astype(F32), jnp.uint32)
    if h.dtype.itemsize == 4:
        return bits
    half = h.shape[1] // 2
    return (bits[:, half:] & jnp.uint32(0xFFFF0000)) | (bits[:, :half] >> 16)


def _unpack_rows(w, dtype):
    if jnp.dtype(dtype).itemsize == 4:
        return lax.bitcast_convert_type(w, dtype)
    lo = lax.bitcast_convert_type(w << 16, F32)
    hi = lax.bitcast_convert_type(w & jnp.uint32(0xFFFF0000), F32)
    return jnp.concatenate([lo, hi], axis=1).astype(dtype)


def _router_kernel(x_ref, g_ref, sh_ref, sc_ref, wr_ref, h_ref, aff_ref):
    h = _norm_mod(x_ref[0], g_ref[...], sh_ref[0], sc_ref[0])
    h_hi = h.astype(BF16)
    h_ref[0] = _pack_rows(h_hi)
    ne = wr_ref.shape[0]
    h_lo = (h - h_hi.astype(F32)).astype(BF16)
    w = wr_ref[...]
    w_hi = w.astype(BF16)
    w_lo = (w - w_hi.astype(F32)).astype(BF16)
    both = _dot_nt(jnp.concatenate([w_hi, w_lo], axis=0), h_hi)
    logits = both[0:ne] + both[ne:2 * ne] + _dot_nt(w_hi, h_lo)
    mx = jnp.max(logits, axis=0, keepdims=True)
    e = jnp.exp(logits - mx)
    aff_ref[0] = e / jnp.sum(e, axis=0, keepdims=True)


def router(x, norm_w, shift, scale, w_router_t, tm):
    b, n, d = x.shape
    ne = w_router_t.shape[0]
    return pl.pallas_call(
        _router_kernel,
        out_shape=(jax.ShapeDtypeStruct((b, n, _packed_width(d, BF16)), jnp.uint32),
                   jax.ShapeDtypeStruct((b, ne, n), F32)),
        grid=(b, n // tm),
        in_specs=[
            pl.BlockSpec((1, tm, d), lambda bi, i: (bi, i, 0)),
            pl.BlockSpec((1, d), lambda bi, i: (0, 0)),
            pl.BlockSpec((1, 1, d), lambda bi, i: (bi, 0, 0)),
            pl.BlockSpec((1, 1, d), lambda bi, i: (bi, 0, 0)),
            pl.BlockSpec((ne, d), lambda bi, i: (0, 0)),
        ],
        out_specs=(
            pl.BlockSpec((1, tm, _packed_width(d, BF16)), lambda bi, i: (bi, i, 0)),
            pl.BlockSpec((1, ne, tm), lambda bi, i: (bi, 0, i)),
        ),
        compiler_params=_cparams("parallel", "parallel"),
        name="router",
    )(x, norm_w.reshape(1, d), shift, scale, w_router_t)


SC_CORES = 2
SC_SUBCORES = 16
GATHER_CHUNK = 64


def gather_rows(table, idx):
    rows, width = idx.shape[0], table.shape[1]
    workers = SC_CORES * SC_SUBCORES
    per_worker = rows // workers
    assert rows == per_worker * workers and per_worker % GATHER_CHUNK == 0
    mesh = plsc.VectorSubcoreMesh(core_axis_name="c", subcore_axis_name="s")

    def body(table_hbm, idx_hbm, out_hbm, idx_v, rows_v, sem):
        wid = lax.axis_index("s") * SC_CORES + lax.axis_index("c")
        base = wid * per_worker

        @pl.loop(0, per_worker // GATHER_CHUNK)
        def _(i):
            off = pl.multiple_of(base + i * GATHER_CHUNK, GATHER_CHUNK)
            pltpu.sync_copy(idx_hbm.at[pl.ds(off, GATHER_CHUNK)], idx_v)
            pltpu.async_copy(table_hbm.at[idx_v], rows_v, sem).wait()
            pltpu.sync_copy(rows_v, out_hbm.at[pl.ds(off, GATHER_CHUNK)])

    return pl.kernel(
        body,
        out_type=jax.ShapeDtypeStruct((rows, width), table.dtype),
        mesh=mesh,
        scratch_types=[
            pltpu.VMEM((GATHER_CHUNK,), jnp.int32),
            pltpu.VMEM((GATHER_CHUNK, width), table.dtype),
            pltpu.SemaphoreType.DMA,
        ],
        name="gather_rows",
    )(table, idx)


FFN_CHUNKS = 8
FFN_VMEM_LIMIT = 60 * 1024 * 1024


def _ffn_spans(f):
    chunk = f // FFN_CHUNKS
    assert chunk * FFN_CHUNKS == f
    span = max(-(-((c + 1) * chunk - c * chunk // 16 * 16) // 16) * 16 for c in range(FFN_CHUNKS))
    offs = [min(c * chunk // 16 * 16, f - span) for c in range(FFN_CHUNKS)]
    assert all(o % 16 == 0 and o <= c * chunk and o + span >= (c + 1) * chunk
               for c, o in enumerate(offs))
    return chunk, span


def _ffn_kernel(*refs, rt, layer, e_off, cps, extra_rows):
    if extra_rows:
        (x_ref, gate_ref, xc_ref, gc_ref, wg_hbm, wu_hbm, wd_hbm, o_ref, oc_ref, wbf, stage, sem) = refs
    else:
        x_ref, gate_ref, wg_hbm, wu_hbm, wd_hbm, o_ref, wbf, stage, sem = refs
    e = pl.program_id(0)
    g = pl.program_id(1)
    ne = pl.num_programs(0)
    slot = e % 2
    nxt = jnp.minimum(e + 1, ne - 1)
    mats = (wg_hbm, wu_hbm, wd_hbm)
    chunk, span = _ffn_spans(wbf.shape[2])

    def offset(c):
        last = wbf.shape[2] - span
        if isinstance(c, int):
            return min(c * chunk // 16 * 16, last)
        return pl.multiple_of(jnp.minimum(c * chunk // 16 * 16, last), 16)

    def copies(ee, c, st):
        return [pltpu.make_async_copy(m.at[layer, e_off + ee, pl.ds(offset(c), span), :],
                                      stage.at[st, i], sem.at[st, i]) for i, m in enumerate(mats)]

    def start(ee, c, st):
        for cp in copies(ee, c, st):
            cp.start()

    def wait(ee, c, st):
        for cp in copies(ee, c, st):
            cp.wait()

    def cast(c, st, i, sl):
        wbf[sl, i, pl.ds(offset(c), span), :] = stage[st, i].astype(wbf.dtype)

    @pl.when((e == 0) & (g == 0))
    def _():
        for c in range(FFN_CHUNKS):
            start(0, c, 0)
            wait(0, c, 0)
            for i in range(3):
                cast(c, 0, i, 0)

    def swiglu(words, gate_row):
        r = words.shape[0]
        xt = _unpack_rows(words, wbf.dtype)
        a = _dot_nt(xt, wbf[slot, 0])
        u = _dot_nt(xt, wbf[slot, 1])
        hh = (a * jax.nn.sigmoid(a) * u).astype(wbf.dtype)
        gate_col = jnp.broadcast_to(gate_row, (LANES, r)).T[:, 0:1]
        return (_dot(hh, wbf[slot, 2]) * gate_col).astype(o_ref.dtype)

    def compute_tile(t):
        rs = slice(t * rt, (t + 1) * rt)
        o_ref[0, 0, rs, :] = swiglu(x_ref[0, 0, rs, :], gate_ref[0, 0, :, rs])

    if extra_rows:
        @pl.when(g == pl.num_programs(1) - 1)
        def _():
            oc_ref[0, 0] = swiglu(xc_ref[0, 0], gc_ref[0, 0])

    c0 = g * cps
    ntiles = x_ref.shape[2] // rt
    start(nxt, c0, 0)
    for t in range(ntiles):
        compute_tile(t)
        if 1 <= t <= cps:
            for i in range(3):
                cast(c0 + t - 1, (t - 1) % 2, i, 1 - slot)
        if t < cps:
            wait(nxt, c0 + t, t % 2)
            if t + 1 < cps:
                start(nxt, c0 + t + 1, (t + 1) % 2)
    if cps == ntiles:
        for i in range(3):
            cast(c0 + cps - 1, (cps - 1) % 2, i, 1 - slot)


def expert_ffn(xs, gate, wg_t, wu_t, wd, layer, e_off=0, extra=None):
    g, ne, rows, dp = xs.shape
    f, d = wd.shape[2], wd.shape[3]
    assert FFN_CHUNKS % g == 0
    cps = FFN_CHUNKS // g
    rt = min(rows, 256, rows // cps)
    assert rows % rt == 0 and rt % 16 == 0
    span = _ffn_spans(f)[1]
    any_spec = pl.BlockSpec(memory_space=pl.ANY)
    in_specs = [pl.BlockSpec((1, 1, rows, dp), lambda e, gi: (gi, e, 0, 0)),
                pl.BlockSpec((1, 1, 1, rows), lambda e, gi: (gi, e, 0, 0))]
    out_specs = pl.BlockSpec((1, 1, rows, d), lambda e, gi: (gi, e, 0, 0))
    out_shape = jax.ShapeDtypeStruct((g, ne, rows, d), BF16)
    args = [xs, gate]
    if extra is not None:
        rc = extra[0].shape[2]
        in_specs += [pl.BlockSpec((1, 1, rc, dp), lambda e, gi: (0, e, 0, 0)),
                     pl.BlockSpec((1, 1, 1, rc), lambda e, gi: (0, e, 0, 0))]
        out_specs = (out_specs, pl.BlockSpec((1, 1, rc, d), lambda e, gi: (0, e, 0, 0)))
        out_shape = (out_shape, jax.ShapeDtypeStruct((1, ne, rc, d), BF16))
        args += list(extra)
    return pl.pallas_call(
        functools.partial(_ffn_kernel, rt=rt, layer=layer, e_off=e_off, cps=cps,
                          extra_rows=extra is not None),
        out_shape=out_shape,
        grid=(ne, g),
        in_specs=in_specs + [any_spec, any_spec, any_spec],
        out_specs=out_specs,
        scratch_shapes=[
            pltpu.VMEM((2, 3, f, d), BF16),
            pltpu.VMEM((min(cps, 2), 3, span, d), F32),
            pltpu.SemaphoreType.DMA((min(cps, 2), 3)),
        ],
        compiler_params=pltpu.CompilerParams(dimension_semantics=("arbitrary", "arbitrary"),
                                             vmem_limit_bytes=FFN_VMEM_LIMIT),
        name="expert_ffn",
    )(*args, wg_t, wu_t, wd)


ROUTE_WIN = 64
LANES = 128
VAL_ROWS = 8


def _combine_kernel(st_ref, rank_ref, *refs, tt, ne, cap, merged, nparts):
    y_parts = refs[:nparts]
    x_ref, nw_ref, g_ref, o_ref, buf, xbuf, sem, xsem, acc_ref = refs[nparts:]
    b = pl.program_id(0)
    k = pl.program_id(1)
    nk = pl.num_programs(1)
    step = b * nk + k
    slot = step % 2
    rows_total = y_parts[0].shape[2]
    per_part = ne // nparts
    win = ROUTE_WIN
    base = b * cap if merged else 0
    lane = lax.broadcasted_iota(jnp.int32, (1, LANES), 1)

    def tile_bounds(bb, kk, e):
        off = bb * cap if merged else 0
        lo = off + st_ref[(bb * (nk + 1) + kk) * ne + e]
        hi = off + st_ref[(bb * (nk + 1) + kk + 1) * ne + e]
        return lo, hi, jnp.minimum((lo // 16) * 16, rows_total - win)

    def bounds(e):
        return tile_bounds(b, k, e)

    def rows_of(bb, e, a0, count):
        y_ref = y_parts[e // per_part]
        return y_ref.at[0 if merged else bb, e % per_part, pl.ds(pl.multiple_of(a0, 16), count), :]

    def win_copy(bb, e, a0, sl):
        return pltpu.make_async_copy(rows_of(bb, e, a0, win), buf.at[sl, pl.ds(e * win, win), :],
                                     sem.at[sl, e])

    def start_tile(bb, kk, sl):
        for e in range(ne):
            win_copy(bb, e, tile_bounds(bb, kk, e)[2], sl).start()

    @pl.when(step == 0)
    def _():
        start_tile(b, k, slot)

    nxt = step + 1

    @pl.when(nxt < pl.num_programs(0) * nk)
    def _():
        start_tile(nxt // nk, nxt % nk, 1 - slot)

    rkf = rank_ref[0].astype(F32)
    rk_t = jnp.concatenate([rkf, jnp.full((LANES - ne, tt), -1.0, F32)], axis=0).T
    rk_t = rk_t.astype(jnp.int32)

    def target(e, origin):
        col = rk_t[:, e:e + 1]
        return jnp.where(col >= 0, col + (base - origin), -1)

    pieces = []
    for j in range(ne // 2):
        t_even = target(2 * j, bounds(2 * j)[2])
        t_odd = target(2 * j + 1, bounds(2 * j + 1)[2])
        t_odd = jnp.where((t_odd >= 0) & (t_odd < win), t_odd + win, -1)
        pieces.append((jnp.where(lane < win, t_even, t_odd) == lane).astype(BF16))
    p = jnp.concatenate(pieces, axis=1)
    for e in range(ne):
        win_copy(b, e, bounds(e)[2], slot).wait()
    acc_ref[...] = _dot(p, buf[slot])

    for e in range(ne):
        lo, hi, a0 = bounds(e)

        def extra(w, carry, e=e, a0=a0):
            start = a0 + w * win
            aw = pl.multiple_of(jnp.minimum(start, rows_total - win), 16)

            cp = pltpu.make_async_copy(rows_of(b, e, aw, win), xbuf, xsem)
            cp.start()
            cp.wait()
            col = rk_t[:, e:e + 1]
            ok = (col >= 0) & (col + base >= start)
            px = (jnp.where(ok, col + (base - aw), -1) == lane[:, 0:win]).astype(BF16)
            acc_ref[...] += _dot(px, xbuf[...])
            return carry

        lax.fori_loop(1, (hi - a0 + win - 1) // win, extra, 0)

    y = acc_ref[...]
    yn = y * lax.rsqrt(jnp.mean(y * y, axis=-1, keepdims=True) + EPS) * nw_ref[...]
    o_ref[0] = x_ref[0] + g_ref[0] * yn


def combine_post(starts, rank_t, ys, x, norm_w, g, tt, cap, merged):
    ys = tuple(ys) if isinstance(ys, (tuple, list)) else (ys,)
    b, n, d = x.shape
    rows = ys[0].shape[2]
    ne = sum(y.shape[1] for y in ys)
    assert all(y.shape == ys[0].shape for y in ys)
    grid_spec = pltpu.PrefetchScalarGridSpec(
        num_scalar_prefetch=1,
        grid=(b, n // tt),
        in_specs=[
            pl.BlockSpec((1, ne, tt), lambda bi, k, st: (bi, 0, k)),
            *[pl.BlockSpec(memory_space=pl.ANY)] * len(ys),
            pl.BlockSpec((1, tt, d), lambda bi, k, st: (bi, k, 0)),
            pl.BlockSpec((1, d), lambda bi, k, st: (0, 0)),
            pl.BlockSpec((1, 1, d), lambda bi, k, st: (bi, 0, 0)),
        ],
        out_specs=pl.BlockSpec((1, tt, d), lambda bi, k, st: (bi, k, 0)),
        scratch_shapes=[
            pltpu.VMEM((2, ne * ROUTE_WIN, d), BF16),
            pltpu.VMEM((ROUTE_WIN, d), BF16),
            pltpu.SemaphoreType.DMA((2, ne)),
            pltpu.SemaphoreType.DMA(()),
            pltpu.VMEM((tt, d), F32),
        ],
    )
    return pl.pallas_call(
        functools.partial(_combine_kernel, tt=tt, ne=ne, cap=cap, merged=merged, nparts=len(ys)),
        out_shape=jax.ShapeDtypeStruct((b, n, d), F32),
        grid_spec=grid_spec,
        compiler_params=_cparams("arbitrary", "arbitrary"),
        name="combine_post",
    )(starts, rank_t, *ys, x, norm_w.reshape(1, d), g)


def _select_kernel(aff_ref, tri_ref, rank_ref, st_ref, *, cap, tt):
    a = aff_ref[0]
    ne, n = a.shape
    bits = lax.bitcast_convert_type(a, jnp.int32)

    def search(_, c):
        lo, hi = c
        mid = lo + ((hi - lo) >> 1)
        cnt = jnp.sum((bits >= mid).astype(F32), axis=1, keepdims=True)
        ge = cnt >= cap
        return jnp.where(ge, mid, lo), jnp.where(ge, hi, mid)

    lo0 = jnp.zeros((ne, 1), jnp.int32)
    hi0 = jnp.full((ne, 1), 0x7F800000, jnp.int32)
    thr, _ = lax.fori_loop(0, 31, search, (lo0, hi0))
    gt = bits > thr
    eq = bits == thr
    need = cap - jnp.sum(gt.astype(F32), axis=1, keepdims=True)
    m = jnp.concatenate([gt, eq], axis=0).astype(BF16)
    tri = tri_ref[...]
    lane = lax.broadcasted_iota(jnp.int32, (1, LANES), 1)
    off = jnp.zeros((2 * ne, 1), F32)
    st = jnp.zeros((ne, LANES), jnp.int32)
    for j in range(n // LANES):
        cs = slice(j * LANES, (j + 1) * LANES)
        if (j * LANES) % tt == 0:
            off_sel = off[:ne] + jnp.minimum(off[ne:], need)
            st = jnp.where(lane == (j * LANES) // tt, off_sel.astype(jnp.int32), st)
        mj = m[:, cs]
        pj = _dot(mj, tri) + off
        pe = pj[ne:]
        sel = gt[:, cs] | (eq[:, cs] & (pe < need))
        rank_ref[0, :, cs] = jnp.where(sel, pj[:ne] + jnp.minimum(pe, need), -1.0).astype(jnp.int32)
        off = off + jnp.sum(mj.astype(F32), axis=1, keepdims=True)
    st_ref[0] = jnp.where(lane == n // tt, cap, st)


def route_select(aff_t, cap, tt):
    b, ne, n = aff_t.shape
    tri = jnp.asarray(np.triu(np.ones((LANES, LANES), np.float32), 1), BF16)
    return pl.pallas_call(
        functools.partial(_select_kernel, cap=cap, tt=tt),
        out_shape=(jax.ShapeDtypeStruct((b, ne, n), jnp.int32),
                   jax.ShapeDtypeStruct((b, ne, LANES), jnp.int32)),
        grid=(b,),
        in_specs=[pl.BlockSpec((1, ne, n), lambda bi: (bi, 0, 0)),
                  pl.BlockSpec((LANES, LANES), lambda bi: (0, 0))],
        out_specs=(pl.BlockSpec((1, ne, n), lambda bi: (bi, 0, 0)),
                   pl.BlockSpec((1, ne, LANES), lambda bi: (bi, 0, 0))),
        compiler_params=_cparams("parallel"),
        name="route_select",
    )(aff_t, tri)


def _compact_kernel(st_ref, rank_ref, aff_ref, idx_ref, gate_ref, out_ref, *, tt, ne, cap, n):
    b = pl.program_id(0)
    k = pl.program_id(1)
    nk = pl.num_programs(1)
    win = ROUTE_WIN

    @pl.when(k == 0)
    def _():
        out_ref[...] = jnp.zeros_like(out_ref)

    rk = rank_ref[0]
    a = aff_ref[0]
    g1 = a.astype(BF16).astype(F32)
    r1 = a - g1
    g2 = r1.astype(BF16).astype(F32)
    g3 = r1 - g2
    tok = k * tt + lax.broadcasted_iota(jnp.int32, (1, tt), 1)
    ids = jnp.concatenate([(tok >> 6).astype(F32), (tok & 63).astype(F32),
                           jnp.zeros((VAL_ROWS - 2, tt), F32)], axis=0)
    pad = jnp.zeros((LANES - VAL_ROWS - 3 * ne, tt), F32)
    payload = jnp.concatenate([ids, g1, g2, g3, pad], axis=0).astype(BF16)
    sub = lax.broadcasted_iota(jnp.int32, (win, 1), 0)

    def bounds(e):
        lo = st_ref[(b * (nk + 1) + k) * ne + e]
        hi = st_ref[(b * (nk + 1) + k + 1) * ne + e]
        return hi, (lo // 8) * 8

    def one_hot(e, aw):
        return ((rk[e:e + 1, :] - aw) == sub).astype(BF16)

    p_all = jnp.concatenate([one_hot(e, bounds(e)[1]) for e in range(ne)], axis=0)
    moved = _dot_nt(p_all, payload)
    for e in range(ne):
        a0 = pl.multiple_of(bounds(e)[1], 8)
        out_ref[e, pl.ds(a0, win), :] += moved[e * win:(e + 1) * win]

    for e in range(ne):
        hi, a0 = bounds(e)

        def window(w, carry, e=e, a0=a0):
            aw = pl.multiple_of(a0 + w * win, 8)
            out_ref[e, pl.ds(aw, win), :] += _dot_nt(one_hot(e, aw), payload)
            return carry

        lax.fori_loop(1, (hi - a0 + win - 1) // win, window, 0)

    @pl.when(k == nk - 1)
    def _():
        for e in range(ne):
            t = out_ref[e].T
            ids_e = (t[0:1] * 64.0 + t[1:2]).astype(jnp.int32) + b * n
            g_e = (t[VAL_ROWS + e:VAL_ROWS + e + 1] + t[VAL_ROWS + ne + e:VAL_ROWS + ne + e + 1]) \
                + t[VAL_ROWS + 2 * ne + e:VAL_ROWS + 2 * ne + e + 1]
            idx_ref[0, e:e + 1, :] = ids_e[:, 0:cap]
            gate_ref[0, e:e + 1, :] = g_e[:, 0:cap]


def route_compact(starts, rank_t, aff_t, cap, tt):
    b, ne, n = aff_t.shape
    rows = -(-(cap + ROUTE_WIN) // LANES) * LANES
    grid_spec = pltpu.PrefetchScalarGridSpec(
        num_scalar_prefetch=1,
        grid=(b, n // tt),
        in_specs=[pl.BlockSpec((1, ne, tt), lambda bi, k, st: (bi, 0, k)),
                  pl.BlockSpec((1, ne, tt), lambda bi, k, st: (bi, 0, k))],
        out_specs=(pl.BlockSpec((1, ne, cap), lambda bi, k, st: (bi, 0, 0)),
                   pl.BlockSpec((1, ne, cap), lambda bi, k, st: (bi, 0, 0))),
        scratch_shapes=[pltpu.VMEM((ne, rows, LANES), F32)],
    )
    return pl.pallas_call(
        functools.partial(_compact_kernel, tt=tt, ne=ne, cap=cap, n=n),
        out_shape=(jax.ShapeDtypeStruct((b, ne, cap), jnp.int32),
                   jax.ShapeDtypeStruct((b, ne, cap), F32)),
        grid_spec=grid_spec,
        compiler_params=_cparams("parallel", "arbitrary"),
        name="route_compact",
    )(starts, rank_t, aff_t)


def _qk_head_of_lane():
    half = QK_W // 2
    return (np.arange(QK_W) % half) // (DK // 2)


def _in_proj_perm():
    half = DK // 2
    first = [h * DK + i for h in range(N_HEADS) for i in range(half)]
    second = [h * DK + half + i for h in range(N_HEADS) for i in range(half)]
    qperm = np.array(first + second)
    rest = np.arange(2 * QK_W, 2 * QK_W + 2 * V_W + POOL_W)
    return np.concatenate([qperm, QK_W + qperm, rest])


def _static_tables():
    qk_head = _qk_head_of_lane()
    v_head = np.arange(V_W) // DV
    hmask = (qk_head[None, :] == np.arange(N_HEADS)[:, None])
    bd = (qk_head[:, None] == v_head[None, :])
    avg = (v_head[:, None] == v_head[None, :]).astype(np.float32) / DV
    return {
        "hmask": jnp.asarray(hmask[:, None, :], BF16),
        "bd_f32": jnp.asarray(bd, F32),
        "bd_bf16": jnp.asarray(bd, BF16),
        "avg": jnp.asarray(avg, BF16),
    }


def _decay_tables(lg_f, lg_b):
    pos = jnp.arange(CHUNK, dtype=F32)
    diff = pos[:, None] - pos[None, :]
    low = diff >= 0
    up = diff < 0
    m_f = jnp.where(low, jnp.exp(lg_f[:, None, None] * jnp.where(low, diff, 0.0)), 0.0)
    m_b = jnp.where(up, jnp.exp(lg_b[:, None, None] * jnp.where(up, -diff, 0.0)), 0.0)
    m = m_f + m_b
    mpair = m.reshape(N_HEADS // 2, 2, CHUNK, CHUNK).transpose(0, 2, 1, 3).reshape(
        N_HEADS // 2, CHUNK, 2 * CHUNK)
    qk_head = _qk_head_of_lane()
    v_head = np.arange(V_W) // DV
    qdec_f = jnp.exp(lg_f[None, :] * (pos[:, None] + 1.0))[:, v_head]
    qdec_b = jnp.exp(lg_b[None, :] * (CHUNK - pos[:, None]))[:, v_head]
    kdec_f = jnp.exp(lg_f[None, :] * (CHUNK - 1.0 - pos[:, None]))[:, qk_head]
    kdec_b = jnp.exp(lg_b[None, :] * pos[:, None])[:, qk_head]
    cdec_f = jnp.exp(lg_f * CHUNK)[None, v_head]
    cdec_b = jnp.exp(lg_b * CHUNK)[None, v_head]
    return {"mpair": mpair, "qdec_f": qdec_f, "qdec_b": qdec_b, "kdec_f": kdec_f,
            "kdec_b": kdec_b, "cdec_f": cdec_f, "cdec_b": cdec_b}


def _rope_tables(n):
    t = jnp.arange(n)
    row = (t // GRID_W).astype(F32)
    col = (t % GRID_W).astype(F32)
    n_freq = DK // 4
    inv = ROPE_BASE ** (-jnp.arange(n_freq, dtype=F32) / n_freq)
    ang = jnp.concatenate([row[:, None] * inv, col[:, None] * inv], axis=-1)
    return jnp.tile(jnp.cos(ang), (1, N_HEADS)), jnp.tile(jnp.sin(ang), (1, N_HEADS))


EXPERT_GROUPS = 2


def _route(h_pk, aff_t, merged):
    b, n, d = h_pk.shape
    cap = EC_FACTOR * n // N_EXPERTS
    ne = N_EXPERTS
    per = ne // EXPERT_GROUPS
    tt = min(n, 256)
    nk = n // tt
    rank_t, st = route_select(aff_t, cap, tt)
    starts = st[:, :, :nk + 1].transpose(0, 2, 1).reshape(-1)
    flat, gate = route_compact(starts, rank_t, aff_t, cap, tt)
    table = h_pk.reshape(b * n, d)
    xs_list, gate_list = [], []
    for i in range(EXPERT_GROUPS):
        fl, gt = flat[:, i * per:(i + 1) * per], gate[:, i * per:(i + 1) * per]
        if merged:
            fl = fl.transpose(1, 0, 2)
            gt = gt.transpose(1, 0, 2).reshape(1, per, b * cap)
        xs_list.append(gather_rows(table, fl.reshape(-1)).reshape(gt.shape + (d,)))
        gate_list.append(gt[:, :, None, :])
    return xs_list, gate_list, (starts, rank_t, tt, cap, merged)


def _combine(x, y, info, norm_w, g):
    starts, rank_t, tt, cap, merged = info
    return combine_post(starts, rank_t, y, x, norm_w, g, tt, cap, merged)


def kernel(x, c, ctx, c_ctx, w_ada, b_ada, norm_pre_mix, norm_post_mix, norm_pre_ffn, norm_post_ffn, w_in, ret_decay_fwd, ret_decay_bwd, ret_gn, pool_w, pool_scale, w_out, w_router, w_gate, w_up, w_down):
    b, n, d = x.shape
    lc = ctx.shape[1]
    depth = w_ada.shape[0]
    rope = _rope_tables(n)
    static = _static_tables()
    perm = _in_proj_perm()

    cc = jnp.concatenate([c, c_ctx[None, :], jnp.zeros((7, d), F32)], axis=0)
    mods = ada_modulation(cc, w_ada, b_ada)
    wg, wu, wd = jnp.swapaxes(w_gate, 2, 3), jnp.swapaxes(w_up, 2, 3), w_down

    tm = 512
    for l in range(depth):
        last = l == depth - 1
        mx = mods[l, :b].reshape(b, 1, 6, d)
        sh1, sc1, g1, sh2, sc2, g2 = [mx[:, :, i] for i in range(6)]
        mc = jnp.broadcast_to(mods[l, b].reshape(1, 1, 6, d), (b, 1, 6, d))
        csh1, csc1, cg1, csh2, csc2, cg2 = [mc[:, :, i] for i in range(6)]
        lg_f = jax.nn.log_sigmoid(ret_decay_fwd[l].astype(F32))
        lg_b = jax.nn.log_sigmoid(ret_decay_bwd[l].astype(F32))
        tabs = dict(static, **_decay_tables(lg_f, lg_b))
        w_in_p = w_in[l][:, perm].astype(BF16)
        w_out_bf = w_out[l].astype(BF16)
        pool_w_bf = pool_w[l].astype(BF16)

        qk_c, v_c, gate_c, p_c = premix(ctx, norm_pre_mix[l], csh1, csc1, w_in_p, None, lc)
        zero = jnp.zeros((b, DK, V_W), F32)
        sf_c, sb_c, s_f, s_b = state_scan(qk_c, v_c, zero, zero, tabs)
        qk_x, v_x, gate_x, p_x = premix(x, norm_pre_mix[l], sh1, sc1, w_in_p, rope, tm)
        sf_x, sb_x, _, _ = state_scan(qk_x, v_x, s_f, s_b, tabs)
        pool_x = pool_mixer(p_x, pool_w_bf, pool_scale[l], GRID_W, min(n, 2048))
        x = retention_mixer(qk_x, v_x, gate_x, pool_x, x, sf_x, sb_x, tabs, ret_gn[l], w_out_bf,
                            norm_post_mix[l], g1, tm)
        if not last:
            pool_c = pool_mixer(p_c, pool_w_bf, pool_scale[l], lc, lc)
            ctx = retention_mixer(qk_c, v_c, gate_c, pool_c, ctx, sf_c, sb_c, tabs, ret_gn[l],
                                  w_out_bf, norm_post_mix[l], cg1, lc)

        wr_t = w_router[l].T
        h_x, aff_x = router(x, norm_pre_ffn[l], sh2, sc2, wr_t, tm)
        per = N_EXPERTS // EXPERT_GROUPS
        xs_x, gate_x2, info_x = _route(h_x, aff_x, False)
        if last:
            y_x = [expert_ffn(xs_x[i], gate_x2[i], wg, wu, wd, l, e_off=i * per)
                   for i in range(EXPERT_GROUPS)]
        else:
            h_c, aff_c = router(ctx, norm_pre_ffn[l], csh2, csc2, wr_t, lc)
            xs_c, gate_c2, info_c = _route(h_c, aff_c, True)
            pairs = [expert_ffn(xs_x[i], gate_x2[i], wg, wu, wd, l, e_off=i * per,
                                extra=(xs_c[i], gate_c2[i])) for i in range(EXPERT_GROUPS)]
            y_x = [p[0] for p in pairs]
            ctx = _combine(ctx, [p[1] for p in pairs], info_c, norm_post_ffn[l], cg2)
        x = _combine(x, y_x, info_x, norm_post_ffn[l], g2)
    return x
```

```python
import functools

import jax
import jax.numpy as jnp
import numpy as np
from jax import lax
from jax.experimental import pallas as pl
from jax.experimental.pallas import tpu as pltpu
from jax.experimental.pallas import tpu_sc as plsc

F32 = jnp.float32
BF16 = jnp.bfloat16

D_MODEL = 1024
GRID_W = 64
N_HEADS = 8
DV = 64
DK = 32
QK_W = N_HEADS * DK
V_W = N_HEADS * DV
POOL_W = 512
POOL_WINDOWS = (2, 4, 8, 16)
POOL_GROUP_DIM = 128
CHUNK = 128
ROPE_BASE = 10000.0
N_EXPERTS = 16
EC_FACTOR = 2
EPS = 1e-6
MAX_HALF_WINDOW = max(POOL_WINDOWS) // 2

VMEM_LIMIT = 56 * 1024 * 1024


def _cparams(*sem):
    return pltpu.CompilerParams(dimension_semantics=sem, vmem_limit_bytes=VMEM_LIMIT)


def _dot(a, b):
    return jnp.dot(a, b, preferred_element_type=F32)


def _dot_nt(a, b, precision=None):
    return lax.dot_general(a, b, (((1,), (1,)), ((), ())), precision=precision,
                           preferred_element_type=F32)


def _ada_kernel(cc_ref, w_ref, b_ref, o_ref):
    s = cc_ref[...]
    s = s * jax.nn.sigmoid(s)
    o_ref[0] = _dot(s.astype(BF16), w_ref[0].astype(BF16)) + b_ref[0]


def ada_modulation(cc, w_ada, b_ada):
    depth, d, d6 = w_ada.shape
    rows = cc.shape[0]
    tn = 1536
    return pl.pallas_call(
        _ada_kernel,
        out_shape=jax.ShapeDtypeStruct((depth, rows, d6), F32),
        grid=(depth, d6 // tn),
        in_specs=[
            pl.BlockSpec((rows, d), lambda l, j: (0, 0)),
            pl.BlockSpec((1, d, tn), lambda l, j: (l, 0, j)),
            pl.BlockSpec((1, 1, tn), lambda l, j: (l, 0, j)),
        ],
        out_specs=pl.BlockSpec((1, rows, tn), lambda l, j: (l, 0, j)),
        compiler_params=_cparams("parallel", "parallel"),
        name="ada_modulation",
    )(cc, w_ada, b_ada.reshape(depth, 1, d6))


def _norm_mod(xf, g, sh, sc):
    y = xf * lax.rsqrt(jnp.mean(xf * xf, axis=-1, keepdims=True) + EPS)
    return (y * g) * (1.0 + sc) + sh


def _premix_kernel(x_ref, g_ref, sh_ref, sc_ref, w_ref, *rest, rope):
    if rope:
        cos_ref, sin_ref, qk_ref, v_ref, gate_ref, p_ref = rest
    else:
        qk_ref, v_ref, gate_ref, p_ref = rest
    hb = _norm_mod(x_ref[0], g_ref[...], sh_ref[0], sc_ref[0]).astype(BF16)
    zqk = _dot(hb, w_ref[:, 0:2 * QK_W])
    half = QK_W // 2
    q1, q2 = zqk[:, 0:half], zqk[:, half:2 * half]
    k1 = zqk[:, 2 * half:3 * half] * (DK ** -0.5)
    k2 = zqk[:, 3 * half:4 * half] * (DK ** -0.5)
    if rope:
        cos, sin = cos_ref[...], sin_ref[...]
        q1, q2 = q1 * cos - q2 * sin, q1 * sin + q2 * cos
        k1, k2 = k1 * cos - k2 * sin, k1 * sin + k2 * cos
    qk_ref[0] = jnp.concatenate([q1, q2, k1, k2], axis=1).astype(BF16)
    o = 2 * QK_W
    v_ref[0] = _dot(hb, w_ref[:, o:o + V_W]).astype(BF16)
    gate_ref[0] = _dot(hb, w_ref[:, o + V_W:o + 2 * V_W])
    p_ref[0] = _dot(hb, w_ref[:, o + 2 * V_W:o + 2 * V_W + POOL_W])


def premix(x, norm_w, shift, scale, w_in_p, rope, tm):
    b, n, d = x.shape
    in_w = w_in_p.shape[1]
    row = lambda bi, i: (bi, i, 0)
    in_specs = [
        pl.BlockSpec((1, tm, d), row),
        pl.BlockSpec((1, d), lambda bi, i: (0, 0)),
        pl.BlockSpec((1, 1, d), lambda bi, i: (bi, 0, 0)),
        pl.BlockSpec((1, 1, d), lambda bi, i: (bi, 0, 0)),
        pl.BlockSpec((d, in_w), lambda bi, i: (0, 0)),
    ]
    args = [x, norm_w.reshape(1, d), shift, scale, w_in_p]
    if rope is not None:
        half = QK_W // 2
        in_specs += [pl.BlockSpec((tm, half), lambda bi, i: (i, 0))] * 2
        args += list(rope)
    return pl.pallas_call(
        functools.partial(_premix_kernel, rope=rope is not None),
        out_shape=(
            jax.ShapeDtypeStruct((b, n, 2 * QK_W), BF16),
            jax.ShapeDtypeStruct((b, n, V_W), BF16),
            jax.ShapeDtypeStruct((b, n, V_W), F32),
            jax.ShapeDtypeStruct((b, n, POOL_W), F32),
        ),
        grid=(b, n // tm),
        in_specs=in_specs,
        out_specs=(
            pl.BlockSpec((1, tm, 2 * QK_W), row),
            pl.BlockSpec((1, tm, V_W), row),
            pl.BlockSpec((1, tm, V_W), row),
            pl.BlockSpec((1, tm, POOL_W), row),
        ),
        compiler_params=_cparams("parallel", "parallel"),
        name="premix",
    )(*args)


def _fold_state(s):
    g = 2 * N_HEADS
    top = s[0:g]
    bot = s[QK_W // 2:QK_W // 2 + g]
    for h in range(1, N_HEADS):
        top = top + s[h * g:(h + 1) * g]
        bot = bot + s[QK_W // 2 + h * g:QK_W // 2 + (h + 1) * g]
    return jnp.concatenate([top, bot], axis=0)


def _expand_state(c):
    g = 2 * N_HEADS
    return jnp.concatenate([c[0:g]] * N_HEADS + [c[g:2 * g]] * N_HEADS, axis=0)


def _state_kernel(kf_ref, vf_ref, kb_ref, vb_ref, s0f_ref, s0b_ref, kdf_ref, kdb_ref,
                  cdf_ref, cdb_ref, bd_ref, sf_out, sb_out, ff_out, fb_out, sf_acc, sb_acc, *, cps):
    c = pl.program_id(1)
    nc = pl.num_programs(1)
    bd = bd_ref[...]

    @pl.when(c == 0)
    def _():
        sf_acc[...] = _expand_state(s0f_ref[0]) * bd
        sb_acc[...] = _expand_state(s0b_ref[0]) * bd

    def update(s, k, v, kd_ref, cd_ref):
        kd = k.astype(F32) * kd_ref[...]
        f = _dot(kd.T.astype(BF16), v)
        return s * cd_ref[...] + f * bd

    sf = sf_acc[...]
    sb = sb_acc[...]
    for i in range(cps):
        j = cps - 1 - i
        fs = slice(i * CHUNK, (i + 1) * CHUNK)
        bs = slice(j * CHUNK, (j + 1) * CHUNK)
        sf_out[0, i] = _fold_state(sf).astype(BF16)
        sb_out[0, j] = _fold_state(sb).astype(BF16)
        sf = update(sf, kf_ref[0, fs, :], vf_ref[0, fs, :], kdf_ref, cdf_ref)
        sb = update(sb, kb_ref[0, bs, :], vb_ref[0, bs, :], kdb_ref, cdb_ref)
    sf_acc[...] = sf
    sb_acc[...] = sb

    @pl.when(c == nc - 1)
    def _():
        ff_out[0] = _fold_state(sf)
        fb_out[0] = _fold_state(sb)


def state_scan(qk, v, s0f, s0b, tabs):
    b, n, _ = qk.shape
    nc = n // CHUNK
    cps = min(nc, 4)
    ns = nc // cps
    rows = cps * CHUNK
    const = lambda shape: pl.BlockSpec(shape, lambda bi, c: (0,) * len(shape))
    return pl.pallas_call(
        functools.partial(_state_kernel, cps=cps),
        out_shape=(
            jax.ShapeDtypeStruct((b, nc, DK, V_W), BF16),
            jax.ShapeDtypeStruct((b, nc, DK, V_W), BF16),
            jax.ShapeDtypeStruct((b, DK, V_W), F32),
            jax.ShapeDtypeStruct((b, DK, V_W), F32),
        ),
        grid=(b, ns),
        in_specs=[
            pl.BlockSpec((1, rows, QK_W), lambda bi, c: (bi, c, 1)),
            pl.BlockSpec((1, rows, V_W), lambda bi, c: (bi, c, 0)),
            pl.BlockSpec((1, rows, QK_W), lambda bi, c: (bi, ns - 1 - c, 1)),
            pl.BlockSpec((1, rows, V_W), lambda bi, c: (bi, ns - 1 - c, 0)),
            pl.BlockSpec((1, DK, V_W), lambda bi, c: (bi, 0, 0)),
            pl.BlockSpec((1, DK, V_W), lambda bi, c: (bi, 0, 0)),
            const((CHUNK, QK_W)), const((CHUNK, QK_W)),
            const((1, V_W)), const((1, V_W)),
            const((QK_W, V_W)),
        ],
        out_specs=(
            pl.BlockSpec((1, cps, DK, V_W), lambda bi, c: (bi, c, 0, 0)),
            pl.BlockSpec((1, cps, DK, V_W), lambda bi, c: (bi, ns - 1 - c, 0, 0)),
            pl.BlockSpec((1, DK, V_W), lambda bi, c: (bi, 0, 0)),
            pl.BlockSpec((1, DK, V_W), lambda bi, c: (bi, 0, 0)),
        ),
        scratch_shapes=[pltpu.VMEM((QK_W, V_W), F32), pltpu.VMEM((QK_W, V_W), F32)],
        compiler_params=_cparams("parallel", "arbitrary"),
        name="state_scan",
    )(qk, v, qk, v, s0f, s0b, tabs["kdec_f"], tabs["kdec_b"], tabs["cdec_f"], tabs["cdec_b"],
      tabs["bd_f32"])


def _pool_kernel(*refs, gw, tile, halo, rows_total):
    if halo:
        prev_ref, cur_ref, next_ref, pw_ref, ps_ref, o_ref, buf = refs
    else:
        cur_ref, pw_ref, ps_ref, o_ref, buf = refs
    i = pl.program_id(1)
    last = pl.num_programs(1) - 1
    m = MAX_HALF_WINDOW
    span = tile + 2 * halo
    zeros_m = jnp.zeros((m, POOL_W), F32)
    buf[0:m] = zeros_m
    buf[m + span:2 * m + span] = zeros_m
    if halo:
        buf[m:m + halo] = jnp.where(i > 0, prev_ref[0], 0.0)
        buf[m + halo + tile:m + span] = jnp.where(i < last, next_ref[0], 0.0)
    buf[m + halo:m + halo + tile] = cur_ref[0]

    pos = lax.broadcasted_iota(jnp.int32, (span, 1), 0)
    col = pos % gw
    tpos = lax.broadcasted_iota(jnp.int32, (tile, 1), 0)
    tcol = tpos % gw
    trow = i * (tile // gw) + tpos // gw
    for gi, w in enumerate(POOL_WINDOWS):
        cs = slice(gi * POOL_GROUP_DIM, (gi + 1) * POOL_GROUP_DIM)
        hw = w // 2
        s = None
        for d in range(-hw, hw):
            valid = (col + d >= 0) & (col + d < gw)
            term = jnp.where(valid, buf[m + d:m + d + span, cs], 0.0)
            s = term if s is None else s + term
        if halo:
            acc = None
            for d in range(-hw, hw):
                start = halo + d * gw
                term = s[start:start + tile]
                acc = term if acc is None else acc + term
            cnt_r = jnp.minimum(trow + hw, rows_total) - jnp.maximum(trow - hw, 0)
        else:
            acc = s
            cnt_r = 1
        cnt_c = jnp.minimum(tcol + hw, gw) - jnp.maximum(tcol - hw, 0)
        cnt = (cnt_r * cnt_c).astype(F32)
        xg = buf[m + halo:m + halo + tile, cs]
        diff = (acc / cnt - xg).astype(BF16)
        y = _dot(diff, pw_ref[gi]) * ps_ref[:, cs]
        o_ref[0, :, cs] = y.astype(BF16)


def pool_mixer(p, pool_w_bf, pool_scale, gw, tile):
    b, n, pw = p.shape
    rows_total = n // gw
    halo = MAX_HALF_WINDOW * gw if rows_total > 1 else 0
    in_specs, args = [], []
    if halo:
        r = tile // halo
        nh = n // halo
        in_specs = [
            pl.BlockSpec((1, halo, pw), lambda bi, i: (bi, jnp.maximum(i * r - 1, 0), 0)),
            pl.BlockSpec((1, tile, pw), lambda bi, i: (bi, i, 0)),
            pl.BlockSpec((1, halo, pw), lambda bi, i: (bi, jnp.minimum((i + 1) * r, nh - 1), 0)),
        ]
        args = [p, p, p]
    else:
        in_specs = [pl.BlockSpec((1, tile, pw), lambda bi, i: (bi, i, 0))]
        args = [p]
    ng = len(POOL_WINDOWS)
    in_specs += [
        pl.BlockSpec((ng, POOL_GROUP_DIM, POOL_GROUP_DIM), lambda bi, i: (0, 0, 0)),
        pl.BlockSpec((1, pw), lambda bi, i: (0, 0)),
    ]
    args += [pool_w_bf, pool_scale.reshape(1, pw)]
    return pl.pallas_call(
        functools.partial(_pool_kernel, gw=gw, tile=tile, halo=halo, rows_total=rows_total),
        out_shape=jax.ShapeDtypeStruct((b, n, pw), BF16),
        grid=(b, n // tile),
        in_specs=in_specs,
        out_specs=pl.BlockSpec((1, tile, pw), lambda bi, i: (bi, i, 0)),
        scratch_shapes=[pltpu.VMEM((tile + 2 * halo + 2 * MAX_HALF_WINDOW, pw), F32)],
        compiler_params=_cparams("parallel", "parallel"),
        name="pool_mixer",
    )(*args)


def _ret_kernel(qk_ref, v_ref, gate_ref, py_ref, x_ref, sf_ref, sb_ref, mp_ref, qdf_ref, qdb_ref,
                hm_ref, bd_ref, avg_ref, gn_ref, wo_ref, nw_ref, g1_ref, o_ref, o_buf, *, nch):
    bd = bd_ref[...]
    avg = avg_ref[...]
    lane = lax.broadcasted_iota(jnp.int32, (1, 2 * DV), 1)
    lo_mask = (lane < DV).astype(BF16)
    hi_mask = (lane >= DV).astype(BF16)
    for ci in range(nch):
        rs = slice(ci * CHUNK, (ci + 1) * CHUNK)
        q = qk_ref[0, rs, 0:QK_W]
        k = qk_ref[0, rs, QK_W:2 * QK_W]
        vv = v_ref[0, rs, :]
        inner = []
        for j in range(N_HEADS // 2):
            kp = jnp.concatenate([k * hm_ref[2 * j], k * hm_ref[2 * j + 1]], axis=0)
            sc = _dot_nt(q, kp) * mp_ref[j]
            vpair = vv[:, 2 * DV * j:2 * DV * (j + 1)]
            vp = jnp.concatenate([vpair * lo_mask, vpair * hi_mask], axis=0)
            inner.append(_dot(sc.astype(BF16), vp))
        o = jnp.concatenate(inner, axis=1)
        sf_bd = _expand_state(sf_ref[0, ci]) * bd
        sb_bd = _expand_state(sb_ref[0, ci]) * bd
        o_buf[rs, :] = o + _dot(q, sf_bd) * qdf_ref[...] + _dot(q, sb_bd) * qdb_ref[...]

    def head_mean(a):
        hi = a.astype(BF16)
        lo = (a - hi.astype(F32)).astype(BF16)
        return _dot(hi, avg) + _dot(lo, avg)

    o = o_buf[...]
    dlt = o - head_mean(o)
    var = head_mean(dlt * dlt)
    yn = dlt * lax.rsqrt(var + EPS) * gn_ref[...]
    g = gate_ref[0]
    r = (g * jax.nn.sigmoid(g) * yn).astype(BF16)
    mix = _dot(r, wo_ref[0:V_W, :]) + _dot(py_ref[0], wo_ref[V_W:V_W + POOL_W, :])
    y = mix * lax.rsqrt(jnp.mean(mix * mix, axis=-1, keepdims=True) + EPS) * nw_ref[...]
    o_ref[0] = x_ref[0] + g1_ref[0] * y


def retention_mixer(qk, v, gate, pool_y, x, sf, sb, tabs, gn, w_out_bf, norm_w, g1, tm):
    b, n, d = x.shape
    nch = tm // CHUNK
    row = lambda bi, i: (bi, i, 0)
    const = lambda shape: pl.BlockSpec(shape, lambda bi, i: (0,) * len(shape))
    return pl.pallas_call(
        functools.partial(_ret_kernel, nch=nch),
        out_shape=jax.ShapeDtypeStruct((b, n, d), F32),
        grid=(b, n // tm),
        in_specs=[
            pl.BlockSpec((1, tm, 2 * QK_W), row),
            pl.BlockSpec((1, tm, V_W), row),
            pl.BlockSpec((1, tm, V_W), row),
            pl.BlockSpec((1, tm, POOL_W), row),
            pl.BlockSpec((1, tm, d), row),
            pl.BlockSpec((1, nch, DK, V_W), lambda bi, i: (bi, i, 0, 0)),
            pl.BlockSpec((1, nch, DK, V_W), lambda bi, i: (bi, i, 0, 0)),
            const((N_HEADS // 2, CHUNK, 2 * CHUNK)),
            const((CHUNK, V_W)), const((CHUNK, V_W)),
            const((N_HEADS, 1, QK_W)),
            const((QK_W, V_W)),
            const((V_W, V_W)),
            const((1, V_W)),
            const((V_W + POOL_W, d)),
            const((1, d)),
            pl.BlockSpec((1, 1, d), lambda bi, i: (bi, 0, 0)),
        ],
        out_specs=pl.BlockSpec((1, tm, d), row),
        scratch_shapes=[pltpu.VMEM((tm, V_W), F32)],
        compiler_params=_cparams("parallel", "parallel"),
        name="retention_mixer",
    )(qk, v, gate, pool_y, x, sf, sb, tabs["mpair"], tabs["qdec_f"], tabs["qdec_b"], tabs["hmask"],
      tabs["bd_bf16"], tabs["avg"], gn.reshape(1, V_W), w_out_bf, norm_w.reshape(1, d), g1)


def _packed_width(d, dtype):
    return d * jnp.dtype(dtype).itemsize // 4


def _pack_rows(h):
    bits = lax.bitcast_convert_type(h.astype(F32), jnp.uint32)
    if h.dtype.itemsize == 4:
        return bits
    half = h.shape[1] // 2
    return (bits[:, half:] & jnp.uint32(0xFFFF0000)) | (bits[:, :half] >> 16)


def _unpack_rows(w, dtype):
    if jnp.dtype(dtype).itemsize == 4:
        return lax.bitcast_convert_type(w, dtype)
    lo = lax.bitcast_convert_type(w << 16, F32)
    hi = lax.bitcast_convert_type(w & jnp.uint32(0xFFFF0000), F32)
    return jnp.concatenate([lo, hi], axis=1).astype(dtype)


def _router_kernel(x_ref, g_ref, sh_ref, sc_ref, wr_ref, h_ref, aff_ref):
    h = _norm_mod(x_ref[0], g_ref[...], sh_ref[0], sc_ref[0])
    h_hi = h.astype(BF16)
    h_ref[0] = _pack_rows(h_hi)
    ne = wr_ref.shape[0]
    h_lo = (h - h_hi.astype(F32)).astype(BF16)
    w = wr_ref[...]
    w_hi = w.astype(BF16)
    w_lo = (w - w_hi.astype(F32)).astype(BF16)
    both = _dot_nt(jnp.concatenate([w_hi, w_lo], axis=0), h_hi)
    logits = both[0:ne] + both[ne:2 * ne] + _dot_nt(w_hi, h_lo)
    mx = jnp.max(logits, axis=0, keepdims=True)
    e = jnp.exp(logits - mx)
    aff_ref[0] = e / jnp.sum(e, axis=0, keepdims=True)


def router(x, norm_w, shift, scale, w_router_t, tm):
    b, n, d = x.shape
    ne = w_router_t.shape[0]
    return pl.pallas_call(
        _router_kernel,
        out_shape=(jax.ShapeDtypeStruct((b, n, _packed_width(d, BF16)), jnp.uint32),
                   jax.ShapeDtypeStruct((b, ne, n), F32)),
        grid=(b, n // tm),
        in_specs=[
            pl.BlockSpec((1, tm, d), lambda bi, i: (bi, i, 0)),
            pl.BlockSpec((1, d), lambda bi, i: (0, 0)),
            pl.BlockSpec((1, 1, d), lambda bi, i: (bi, 0, 0)),
            pl.BlockSpec((1, 1, d), lambda bi, i: (bi, 0, 0)),
            pl.BlockSpec((ne, d), lambda bi, i: (0, 0)),
        ],
        out_specs=(
            pl.BlockSpec((1, tm, _packed_width(d, BF16)), lambda bi, i: (bi, i, 0)),
            pl.BlockSpec((1, ne, tm), lambda bi, i: (bi, 0, i)),
        ),
        compiler_params=_cparams("parallel", "parallel"),
        name="router",
    )(x, norm_w.reshape(1, d), shift, scale, w_router_t)


SC_CORES = 2
SC_SUBCORES = 16
GATHER_CHUNK = 64


def gather_rows(table, idx):
    rows, width = idx.shape[0], table.shape[1]
    workers = SC_CORES * SC_SUBCORES
    per_worker = rows // workers
    assert rows == per_worker * workers and per_worker % GATHER_CHUNK == 0
    mesh = plsc.VectorSubcoreMesh(core_axis_name="c", subcore_axis_name="s")

    def body(table_hbm, idx_hbm, out_hbm, idx_v, rows_v, sem):
        wid = lax.axis_index("s") * SC_CORES + lax.axis_index("c")
        base = wid * per_worker

        @pl.loop(0, per_worker // GATHER_CHUNK)
        def _(i):
            off = pl.multiple_of(base + i * GATHER_CHUNK, GATHER_CHUNK)
            pltpu.sync_copy(idx_hbm.at[pl.ds(off, GATHER_CHUNK)], idx_v)
            pltpu.async_copy(table_hbm.at[idx_v], rows_v, sem).wait()
            pltpu.sync_copy(rows_v, out_hbm.at[pl.ds(off, GATHER_CHUNK)])

    return pl.kernel(
        body,
        out_type=jax.ShapeDtypeStruct((rows, width), table.dtype),
        mesh=mesh,
        scratch_types=[
            pltpu.VMEM((GATHER_CHUNK,), jnp.int32),
            pltpu.VMEM((GATHER_CHUNK, width), table.dtype),
            pltpu.SemaphoreType.DMA,
        ],
        name="gather_rows",
    )(table, idx)


FFN_CHUNKS = 8
FFN_VMEM_LIMIT = 60 * 1024 * 1024


def _ffn_spans(f):
    chunk = f // FFN_CHUNKS
    assert chunk * FFN_CHUNKS == f
    span = max(-(-((c + 1) * chunk - c * chunk // 16 * 16) // 16) * 16 for c in range(FFN_CHUNKS))
    offs = [min(c * chunk // 16 * 16, f - span) for c in range(FFN_CHUNKS)]
    assert all(o % 16 == 0 and o <= c * chunk and o + span >= (c + 1) * chunk
               for c, o in enumerate(offs))
    return chunk, span


def _ffn_kernel(*refs, rt, layer, e_off, cps, extra_rows):
    if extra_rows:
        (x_ref, gate_ref, xc_ref, gc_ref, wg_hbm, wu_hbm, wd_hbm, o_ref, oc_ref, wbf, stage, sem) = refs
    else:
        x_ref, gate_ref, wg_hbm, wu_hbm, wd_hbm, o_ref, wbf, stage, sem = refs
    e = pl.program_id(0)
    g = pl.program_id(1)
    ne = pl.num_programs(0)
    slot = e % 2
    nxt = jnp.minimum(e + 1, ne - 1)
    mats = (wg_hbm, wu_hbm, wd_hbm)
    chunk, span = _ffn_spans(wbf.shape[2])

    def offset(c):
        last = wbf.shape[2] - span
        if isinstance(c, int):
            return min(c * chunk // 16 * 16, last)
        return pl.multiple_of(jnp.minimum(c * chunk // 16 * 16, last), 16)

    def copies(ee, c, st):
        return [pltpu.make_async_copy(m.at[layer, e_off + ee, pl.ds(offset(c), span), :],
                                      stage.at[st, i], sem.at[st, i]) for i, m in enumerate(mats)]

    def start(ee, c, st):
        for cp in copies(ee, c, st):
            cp.start()

    def wait(ee, c, st):
        for cp in copies(ee, c, st):
            cp.wait()

    def cast(c, st, i, sl):
        wbf[sl, i, pl.ds(offset(c), span), :] = stage[st, i].astype(wbf.dtype)

    @pl.when((e == 0) & (g == 0))
    def _():
        for c in range(FFN_CHUNKS):
            start(0, c, 0)
            wait(0, c, 0)
            for i in range(3):
                cast(c, 0, i, 0)

    def swiglu(words, gate_row):
        r = words.shape[0]
        xt = _unpack_rows(words, wbf.dtype)
        a = _dot_nt(xt, wbf[slot, 0])
        u = _dot_nt(xt, wbf[slot, 1])
        hh = (a * jax.nn.sigmoid(a) * u).astype(wbf.dtype)
        gate_col = jnp.broadcast_to(gate_row, (LANES, r)).T[:, 0:1]
        return (_dot(hh, wbf[slot, 2]) * gate_col).astype(o_ref.dtype)

    def compute_tile(t):
        rs = slice(t * rt, (t + 1) * rt)
        o_ref[0, 0, rs, :] = swiglu(x_ref[0, 0, rs, :], gate_ref[0, 0, :, rs])

    if extra_rows:
        @pl.when(g == pl.num_programs(1) - 1)
        def _():
            oc_ref[0, 0] = swiglu(xc_ref[0, 0], gc_ref[0, 0])

    c0 = g * cps
    ntiles = x_ref.shape[2] // rt
    start(nxt, c0, 0)
    for t in range(ntiles):
        compute_tile(t)
        if 1 <= t <= cps:
            for i in range(3):
                cast(c0 + t - 1, (t - 1) % 2, i, 1 - slot)
        if t < cps:
            wait(nxt, c0 + t, t % 2)
            if t + 1 < cps:
                start(nxt, c0 + t + 1, (t + 1) % 2)
    if cps == ntiles:
        for i in range(3):
            cast(c0 + cps - 1, (cps - 1) % 2, i, 1 - slot)


def expert_ffn(xs, gate, wg_t, wu_t, wd, layer, e_off=0, extra=None):
    g, ne, rows, dp = xs.shape
    f, d = wd.shape[2], wd.shape[3]
    assert FFN_CHUNKS % g == 0
    cps = FFN_CHUNKS // g
    rt = min(rows, 256, rows // cps)
    assert rows % rt == 0 and rt % 16 == 0
    span = _ffn_spans(f)[1]
    any_spec = pl.BlockSpec(memory_space=pl.ANY)
    in_specs = [pl.BlockSpec((1, 1, rows, dp), lambda e, gi: (gi, e, 0, 0)),
                pl.BlockSpec((1, 1, 1, rows), lambda e, gi: (gi, e, 0, 0))]
    out_specs = pl.BlockSpec((1, 1, rows, d), lambda e, gi: (gi, e, 0, 0))
    out_shape = jax.ShapeDtypeStruct((g, ne, rows, d), BF16)
    args = [xs, gate]
    if extra is not None:
        rc = extra[0].shape[2]
        in_specs += [pl.BlockSpec((1, 1, rc, dp), lambda e, gi: (0, e, 0, 0)),
                     pl.BlockSpec((1, 1, 1, rc), lambda e, gi: (0, e, 0, 0))]
        out_specs = (out_specs, pl.BlockSpec((1, 1, rc, d), lambda e, gi: (0, e, 0, 0)))
        out_shape = (out_shape, jax.ShapeDtypeStruct((1, ne, rc, d), BF16))
        args += list(extra)
    return pl.pallas_call(
        functools.partial(_ffn_kernel, rt=rt, layer=layer, e_off=e_off, cps=cps,
                          extra_rows=extra is not None),
        out_shape=out_shape,
        grid=(ne, g),
        in_specs=in_specs + [any_spec, any_spec, any_spec],
        out_specs=out_specs,
        scratch_shapes=[
            pltpu.VMEM((2, 3, f, d), BF16),
            pltpu.VMEM((min(cps, 2), 3, span, d), F32),
            pltpu.SemaphoreType.DMA((min(cps, 2), 3)),
        ],
        compiler_params=pltpu.CompilerParams(dimension_semantics=("arbitrary", "arbitrary"),
                                             vmem_limit_bytes=FFN_VMEM_LIMIT),
        name="expert_ffn",
    )(*args, wg_t, wu_t, wd)


ROUTE_WIN = 64
COMBINE_STRIDE = 2
LANES = 128
VAL_ROWS = 8


def _combine_kernel(st_ref, rank_ref, *refs, tt, ne, cap, merged, nparts, win, stride):
    y_parts = refs[:nparts]
    x_ref, nw_ref, g_ref, spread_ref, o_ref, buf, xbuf, sem, xsem, acc_ref = refs[nparts:]
    b = pl.program_id(0)
    k = pl.program_id(1)
    nk = pl.num_programs(1)
    step = b * nk + k
    slot = step % 2
    rows_total = y_parts[0].shape[2]
    per_part = ne // nparts
    base = b * cap if merged else 0
    lane = lax.broadcasted_iota(jnp.int32, (1, LANES), 1)

    def tile_bounds(bb, kk, e):
        off = bb * cap if merged else 0
        lo = off + st_ref[(bb * (nk * stride + 1) + kk * stride) * ne + e]
        hi = off + st_ref[(bb * (nk * stride + 1) + (kk + 1) * stride) * ne + e]
        return lo, hi, jnp.minimum((lo // 16) * 16, rows_total - win)

    def bounds(e):
        return tile_bounds(b, k, e)

    def rows_of(bb, e, a0, count):
        y_ref = y_parts[e // per_part]
        return y_ref.at[0 if merged else bb, e % per_part, pl.ds(pl.multiple_of(a0, 16), count), :]

    def win_copy(bb, e, a0, sl):
        return pltpu.make_async_copy(rows_of(bb, e, a0, win), buf.at[sl, pl.ds(e * win, win), :],
                                     sem.at[sl, e])

    def start_tile(bb, kk, sl):
        for e in range(ne):
            win_copy(bb, e, tile_bounds(bb, kk, e)[2], sl).start()

    @pl.when(step == 0)
    def _():
        start_tile(b, k, slot)

    nxt = step + 1

    @pl.when(nxt < pl.num_programs(0) * nk)
    def _():
        start_tile(nxt // nk, nxt % nk, 1 - slot)

    rk = rank_ref[0]

    def token_major(rows_f32):
        return jnp.concatenate([rows_f32, jnp.full((LANES - ne, tt), -1.0, F32)], axis=0).T

    sub_e = lax.broadcasted_iota(jnp.int32, (ne, 1), 0)
    origin = jnp.zeros((ne, 1), jnp.int32)
    for e in range(ne):
        origin = jnp.where(sub_e == e, bounds(e)[2], origin)
    pos = rk + (base - origin)
    tgt = jnp.where((rk >= 0) & (pos >= 0) & (pos < win), pos, -1).astype(F32)
    tgt_b = _dot(token_major(tgt).astype(BF16), spread_ref[...])
    lane_pos = lax.broadcasted_iota(jnp.int32, (1, ne * win), 1) % win
    p = (tgt_b == lane_pos.astype(F32)).astype(BF16)
    for e in range(ne):
        win_copy(b, e, bounds(e)[2], slot).wait()
    acc_ref[...] = _dot(p, buf[slot])

    for e in range(ne):
        lo, hi, a0 = bounds(e)

        def extra(w, carry, e=e, a0=a0):
            start = a0 + w * win
            aw = pl.multiple_of(jnp.minimum(start, rows_total - win), 16)

            cp = pltpu.make_async_copy(rows_of(b, e, aw, win), xbuf, xsem)
            cp.start()
            cp.wait()
            col = token_major(rk.astype(F32)).astype(jnp.int32)[:, e:e + 1]
            ok = (col >= 0) & (col + base >= start)
            px = (jnp.where(ok, col + (base - aw), -1) == lane[:, 0:win]).astype(BF16)
            acc_ref[...] += _dot(px, xbuf[...])
            return carry

        lax.fori_loop(1, (hi - a0 + win - 1) // win, extra, 0)

    y = acc_ref[...]
    yn = y * lax.rsqrt(jnp.mean(y * y, axis=-1, keepdims=True) + EPS) * nw_ref[...]
    o_ref[0] = x_ref[0] + g_ref[0] * yn


def combine_post(starts, rank_t, ys, x, norm_w, g, tt, cap, merged):
    ys = tuple(ys) if isinstance(ys, (tuple, list)) else (ys,)
    b, n, d = x.shape
    rows = ys[0].shape[2]
    ne = sum(y.shape[1] for y in ys)
    assert all(y.shape == ys[0].shape for y in ys)
    stride = COMBINE_STRIDE if n % (tt * COMBINE_STRIDE) == 0 and rows >= ROUTE_WIN * COMBINE_STRIDE * 2 else 1
    tt = tt * stride
    win = ROUTE_WIN * stride
    assert win <= LANES
    lanes = np.arange(ne * win) // win
    spread = jnp.asarray(np.arange(LANES)[:, None] == lanes[None, :], BF16)
    grid_spec = pltpu.PrefetchScalarGridSpec(
        num_scalar_prefetch=1,
        grid=(b, n // tt),
        in_specs=[
            pl.BlockSpec((1, ne, tt), lambda bi, k, st: (bi, 0, k)),
            *[pl.BlockSpec(memory_space=pl.ANY)] * len(ys),
            pl.BlockSpec((1, tt, d), lambda bi, k, st: (bi, k, 0)),
            pl.BlockSpec((1, d), lambda bi, k, st: (0, 0)),
            pl.BlockSpec((1, 1, d), lambda bi, k, st: (bi, 0, 0)),
            pl.BlockSpec((LANES, ne * win), lambda bi, k, st: (0, 0)),
        ],
        out_specs=pl.BlockSpec((1, tt, d), lambda bi, k, st: (bi, k, 0)),
        scratch_shapes=[
            pltpu.VMEM((2, ne * win, d), BF16),
            pltpu.VMEM((win, d), BF16),
            pltpu.SemaphoreType.DMA((2, ne)),
            pltpu.SemaphoreType.DMA(()),
            pltpu.VMEM((tt, d), F32),
        ],
    )
    return pl.pallas_call(
        functools.partial(_combine_kernel, tt=tt, ne=ne, cap=cap, merged=merged, nparts=len(ys),
                          win=win, stride=stride),
        out_shape=jax.ShapeDtypeStruct((b, n, d), F32),
        grid_spec=grid_spec,
        compiler_params=_cparams("arbitrary", "arbitrary"),
        name="combine_post",
    )(starts, rank_t, *ys, x, norm_w.reshape(1, d), g, spread)


def _select_kernel(aff_ref, tri_ref, rank_ref, st_ref, *, cap, tt):
    a = aff_ref[0]
    ne, n = a.shape
    bits = lax.bitcast_convert_type(a, jnp.int32)

    def search(_, c):
        lo, hi = c
        mid = lo + ((hi - lo) >> 1)
        cnt = jnp.sum((bits >= mid).astype(F32), axis=1, keepdims=True)
        ge = cnt >= cap
        return jnp.where(ge, mid, lo), jnp.where(ge, hi, mid)

    lo0 = jnp.zeros((ne, 1), jnp.int32)
    hi0 = jnp.full((ne, 1), 0x7F800000, jnp.int32)
    thr, _ = lax.fori_loop(0, 31, search, (lo0, hi0))
    gt = bits > thr
    eq = bits == thr
    need = cap - jnp.sum(gt.astype(F32), axis=1, keepdims=True)
    m = jnp.concatenate([gt, eq], axis=0).astype(BF16)
    tri = tri_ref[...]
    lane = lax.broadcasted_iota(jnp.int32, (1, LANES), 1)
    off = jnp.zeros((2 * ne, 1), F32)
    st = jnp.zeros((ne, LANES), jnp.int32)
    for j in range(n // LANES):
        cs = slice(j * LANES, (j + 1) * LANES)
        if (j * LANES) % tt == 0:
            off_sel = off[:ne] + jnp.minimum(off[ne:], need)
            st = jnp.where(lane == (j * LANES) // tt, off_sel.astype(jnp.int32), st)
        mj = m[:, cs]
        pj = _dot(mj, tri) + off
        pe = pj[ne:]
        sel = gt[:, cs] | (eq[:, cs] & (pe < need))
        rank_ref[0, :, cs] = jnp.where(sel, pj[:ne] + jnp.minimum(pe, need), -1.0).astype(jnp.int32)
        off = off + jnp.sum(mj.astype(F32), axis=1, keepdims=True)
    st_ref[0] = jnp.where(lane == n // tt, cap, st)


def route_select(aff_t, cap, tt):
    b, ne, n = aff_t.shape
    tri = jnp.asarray(np.triu(np.ones((LANES, LANES), np.float32), 1), BF16)
    return pl.pallas_call(
        functools.partial(_select_kernel, cap=cap, tt=tt),
        out_shape=(jax.ShapeDtypeStruct((b, ne, n), jnp.int32),
                   jax.ShapeDtypeStruct((b, ne, LANES), jnp.int32)),
        grid=(b,),
        in_specs=[pl.BlockSpec((1, ne, n), lambda bi: (bi, 0, 0)),
                  pl.BlockSpec((LANES, LANES), lambda bi: (0, 0))],
        out_specs=(pl.BlockSpec((1, ne, n), lambda bi: (bi, 0, 0)),
                   pl.BlockSpec((1, ne, LANES), lambda bi: (bi, 0, 0))),
        compiler_params=_cparams("parallel"),
        name="route_select",
    )(aff_t, tri)


def _compact_kernel(st_ref, rank_ref, aff_ref, idx_ref, gate_ref, out_ref, *, tt, ne, cap, n):
    b = pl.program_id(0)
    k = pl.program_id(1)
    nk = pl.num_programs(1)
    win = ROUTE_WIN

    @pl.when(k == 0)
    def _():
        out_ref[...] = jnp.zeros_like(out_ref)

    rk = rank_ref[0]
    a = aff_ref[0]
    g1 = a.astype(BF16).astype(F32)
    r1 = a - g1
    g2 = r1.astype(BF16).astype(F32)
    g3 = r1 - g2
    tok = k * tt + lax.broadcasted_iota(jnp.int32, (1, tt), 1)
    ids = jnp.concatenate([(tok >> 6).astype(F32), (tok & 63).astype(F32),
                           jnp.zeros((VAL_ROWS - 2, tt), F32)], axis=0)
    pad = jnp.zeros((LANES - VAL_ROWS - 3 * ne, tt), F32)
    payload = jnp.concatenate([ids, g1, g2, g3, pad], axis=0).astype(BF16)
    sub = lax.broadcasted_iota(jnp.int32, (win, 1), 0)

    def bounds(e):
        lo = st_ref[(b * (nk + 1) + k) * ne + e]
        hi = st_ref[(b * (nk + 1) + k + 1) * ne + e]
        return hi, (lo // 8) * 8

    def one_hot(e, aw):
        return ((rk[e:e + 1, :] - aw) == sub).astype(BF16)

    p_all = jnp.concatenate([one_hot(e, bounds(e)[1]) for e in range(ne)], axis=0)
    moved = _dot_nt(p_all, payload)
    for e in range(ne):
        a0 = pl.multiple_of(bounds(e)[1], 8)
        out_ref[e, pl.ds(a0, win), :] += moved[e * win:(e + 1) * win]

    for e in range(ne):
        hi, a0 = bounds(e)

        def window(w, carry, e=e, a0=a0):
            aw = pl.multiple_of(a0 + w * win, 8)
            out_ref[e, pl.ds(aw, win), :] += _dot_nt(one_hot(e, aw), payload)
            return carry

        lax.fori_loop(1, (hi - a0 + win - 1) // win, window, 0)

    @pl.when(k == nk - 1)
    def _():
        for e in range(ne):
            t = out_ref[e].T
            ids_e = (t[0:1] * 64.0 + t[1:2]).astype(jnp.int32) + b * n
            g_e = (t[VAL_ROWS + e:VAL_ROWS + e + 1] + t[VAL_ROWS + ne + e:VAL_ROWS + ne + e + 1]) \
                + t[VAL_ROWS + 2 * ne + e:VAL_ROWS + 2 * ne + e + 1]
            idx_ref[0, e:e + 1, :] = ids_e[:, 0:cap]
            gate_ref[0, e:e + 1, :] = g_e[:, 0:cap]


def route_compact(starts, rank_t, aff_t, cap, tt):
    b, ne, n = aff_t.shape
    rows = -(-(cap + ROUTE_WIN) // LANES) * LANES
    grid_spec = pltpu.PrefetchScalarGridSpec(
        num_scalar_prefetch=1,
        grid=(b, n // tt),
        in_specs=[pl.BlockSpec((1, ne, tt), lambda bi, k, st: (bi, 0, k)),
                  pl.BlockSpec((1, ne, tt), lambda bi, k, st: (bi, 0, k))],
        out_specs=(pl.BlockSpec((1, ne, cap), lambda bi, k, st: (bi, 0, 0)),
                   pl.BlockSpec((1, ne, cap), lambda bi, k, st: (bi, 0, 0))),
        scratch_shapes=[pltpu.VMEM((ne, rows, LANES), F32)],
    )
    return pl.pallas_call(
        functools.partial(_compact_kernel, tt=tt, ne=ne, cap=cap, n=n),
        out_shape=(jax.ShapeDtypeStruct((b, ne, cap), jnp.int32),
                   jax.ShapeDtypeStruct((b, ne, cap), F32)),
        grid_spec=grid_spec,
        compiler_params=_cparams("parallel", "arbitrary"),
        name="route_compact",
    )(starts, rank_t, aff_t)


def _qk_head_of_lane():
    half = QK_W // 2
    return (np.arange(QK_W) % half) // (DK // 2)


def _in_proj_perm():
    half = DK // 2
    first = [h * DK + i for h in range(N_HEADS) for i in range(half)]
    second = [h * DK + half + i for h in range(N_HEADS) for i in range(half)]
    qperm = np.array(first + second)
    rest = np.arange(2 * QK_W, 2 * QK_W + 2 * V_W + POOL_W)
    return np.concatenate([qperm, QK_W + qperm, rest])


def _static_tables():
    qk_head = _qk_head_of_lane()
    v_head = np.arange(V_W) // DV
    hmask = (qk_head[None, :] == np.arange(N_HEADS)[:, None])
    bd = (qk_head[:, None] == v_head[None, :])
    avg = (v_head[:, None] == v_head[None, :]).astype(np.float32) / DV
    return {
        "hmask": jnp.asarray(hmask[:, None, :], BF16),
        "bd_f32": jnp.asarray(bd, F32),
        "bd_bf16": jnp.asarray(bd, BF16),
        "avg": jnp.asarray(avg, BF16),
    }


def _decay_tables(lg_f, lg_b):
    pos = jnp.arange(CHUNK, dtype=F32)
    diff = pos[:, None] - pos[None, :]
    low = diff >= 0
    up = diff < 0
    m_f = jnp.where(low, jnp.exp(lg_f[:, None, None] * jnp.where(low, diff, 0.0)), 0.0)
    m_b = jnp.where(up, jnp.exp(lg_b[:, None, None] * jnp.where(up, -diff, 0.0)), 0.0)
    m = m_f + m_b
    mpair = m.reshape(N_HEADS // 2, 2, CHUNK, CHUNK).transpose(0, 2, 1, 3).reshape(
        N_HEADS // 2, CHUNK, 2 * CHUNK)
    qk_head = _qk_head_of_lane()
    v_head = np.arange(V_W) // DV
    qdec_f = jnp.exp(lg_f[None, :] * (pos[:, None] + 1.0))[:, v_head]
    qdec_b = jnp.exp(lg_b[None, :] * (CHUNK - pos[:, None]))[:, v_head]
    kdec_f = jnp.exp(lg_f[None, :] * (CHUNK - 1.0 - pos[:, None]))[:, qk_head]
    kdec_b = jnp.exp(lg_b[None, :] * pos[:, None])[:, qk_head]
    cdec_f = jnp.exp(lg_f * CHUNK)[None, v_head]
    cdec_b = jnp.exp(lg_b * CHUNK)[None, v_head]
    return {"mpair": mpair, "qdec_f": qdec_f, "qdec_b": qdec_b, "kdec_f": kdec_f,
            "kdec_b": kdec_b, "cdec_f": cdec_f, "cdec_b": cdec_b}


def _rope_tables(n):
    t = jnp.arange(n)
    row = (t // GRID_W).astype(F32)
    col = (t % GRID_W).astype(F32)
    n_freq = DK // 4
    inv = ROPE_BASE ** (-jnp.arange(n_freq, dtype=F32) / n_freq)
    ang = jnp.concatenate([row[:, None] * inv, col[:, None] * inv], axis=-1)
    return jnp.tile(jnp.cos(ang), (1, N_HEADS)), jnp.tile(jnp.sin(ang), (1, N_HEADS))


EXPERT_GROUPS = 2


def _route(h_pk, aff_t, merged):
    b, n, d = h_pk.shape
    cap = EC_FACTOR * n // N_EXPERTS
    ne = N_EXPERTS
    per = ne // EXPERT_GROUPS
    tt = min(n, 256)
    nk = n // tt
    rank_t, st = route_select(aff_t, cap, tt)
    starts = st[:, :, :nk + 1].transpose(0, 2, 1).reshape(-1)
    flat, gate = route_compact(starts, rank_t, aff_t, cap, tt)
    table = h_pk.reshape(b * n, d)
    xs_list, gate_list = [], []
    for i in range(EXPERT_GROUPS):
        fl, gt = flat[:, i * per:(i + 1) * per], gate[:, i * per:(i + 1) * per]
        if merged:
            fl = fl.transpose(1, 0, 2)
            gt = gt.transpose(1, 0, 2).reshape(1, per, b * cap)
        xs_list.append(gather_rows(table, fl.reshape(-1)).reshape(gt.shape + (d,)))
        gate_list.append(gt[:, :, None, :])
    return xs_list, gate_list, (starts, rank_t, tt, cap, merged)


def _combine(x, y, info, norm_w, g):
    starts, rank_t, tt, cap, merged = info
    return combine_post(starts, rank_t, y, x, norm_w, g, tt, cap, merged)


def kernel(x, c, ctx, c_ctx, w_ada, b_ada, norm_pre_mix, norm_post_mix, norm_pre_ffn, norm_post_ffn, w_in, ret_decay_fwd, ret_decay_bwd, ret_gn, pool_w, pool_scale, w_out, w_router, w_gate, w_up, w_down):
    b, n, d = x.shape
    lc = ctx.shape[1]
    depth = w_ada.shape[0]
    rope = _rope_tables(n)
    static = _static_tables()
    perm = _in_proj_perm()

    cc = jnp.concatenate([c, c_ctx[None, :], jnp.zeros((7, d), F32)], axis=0)
    mods = ada_modulation(cc, w_ada, b_ada)
    wg, wu, wd = jnp.swapaxes(w_gate, 2, 3), jnp.swapaxes(w_up, 2, 3), w_down

    tm = 512
    for l in range(depth):
        last = l == depth - 1
        mx = mods[l, :b].reshape(b, 1, 6, d)
        sh1, sc1, g1, sh2, sc2, g2 = [mx[:, :, i] for i in range(6)]
        mc = jnp.broadcast_to(mods[l, b].reshape(1, 1, 6, d), (b, 1, 6, d))
        csh1, csc1, cg1, csh2, csc2, cg2 = [mc[:, :, i] for i in range(6)]
        lg_f = jax.nn.log_sigmoid(ret_decay_fwd[l].astype(F32))
        lg_b = jax.nn.log_sigmoid(ret_decay_bwd[l].astype(F32))
        tabs = dict(static, **_decay_tables(lg_f, lg_b))
        w_in_p = w_in[l][:, perm].astype(BF16)
        w_out_bf = w_out[l].astype(BF16)
        pool_w_bf = pool_w[l].astype(BF16)

        qk_c, v_c, gate_c, p_c = premix(ctx, norm_pre_mix[l], csh1, csc1, w_in_p, None, lc)
        zero = jnp.zeros((b, DK, V_W), F32)
        sf_c, sb_c, s_f, s_b = state_scan(qk_c, v_c, zero, zero, tabs)
        qk_x, v_x, gate_x, p_x = premix(x, norm_pre_mix[l], sh1, sc1, w_in_p, rope, tm)
        sf_x, sb_x, _, _ = state_scan(qk_x, v_x, s_f, s_b, tabs)
        pool_x = pool_mixer(p_x, pool_w_bf, pool_scale[l], GRID_W, min(n, 2048))
        x = retention_mixer(qk_x, v_x, gate_x, pool_x, x, sf_x, sb_x, tabs, ret_gn[l], w_out_bf,
                            norm_post_mix[l], g1, tm)
        if not last:
            pool_c = pool_mixer(p_c, pool_w_bf, pool_scale[l], lc, lc)
            ctx = retention_mixer(qk_c, v_c, gate_c, pool_c, ctx, sf_c, sb_c, tabs, ret_gn[l],
                                  w_out_bf, norm_post_mix[l], cg1, lc)

        wr_t = w_router[l].T
        h_x, aff_x = router(x, norm_pre_ffn[l], sh2, sc2, wr_t, tm)
        per = N_EXPERTS // EXPERT_GROUPS
        xs_x, gate_x2, info_x = _route(h_x, aff_x, False)
        if last:
            y_x = [expert_ffn(xs_x[i], gate_x2[i], wg, wu, wd, l, e_off=i * per)
                   for i in range(EXPERT_GROUPS)]
        else:
            h_c, aff_c = router(ctx, norm_pre_ffn[l], csh2, csc2, wr_t, lc)
            xs_c, gate_c2, info_c = _route(h_c, aff_c, True)
            pairs = [expert_ffn(xs_x[i], gate_x2[i], wg, wu, wd, l, e_off=i * per,
                                extra=(xs_c[i], gate_c2[i])) for i in range(EXPERT_GROUPS)]
            y_x = [p[0] for p in pairs]
            ctx = _combine(ctx, [p[1] for p in pairs], info_c, norm_post_ffn[l], cg2)
        x = _combine(x, y_x, info_x, norm_post_ffn[l], g2)
    return x
```

```python
import functools

import jax
import jax.numpy as jnp
import numpy as np
from jax import lax
from jax.experimental import pallas as pl
from jax.experimental.pallas import tpu as pltpu
from jax.experimental.pallas import tpu_sc as plsc

F32 = jnp.float32
BF16 = jnp.bfloat16

D_MODEL = 1024
GRID_W = 64
N_HEADS = 8
DV = 64
DK = 32
QK_W = N_HEADS * DK
V_W = N_HEADS * DV
POOL_W = 512
POOL_WINDOWS = (2, 4, 8, 16)
POOL_GROUP_DIM = 128
CHUNK = 128
ROPE_BASE = 10000.0
N_EXPERTS = 16
EC_FACTOR = 2
EPS = 1e-6
MAX_HALF_WINDOW = max(POOL_WINDOWS) // 2

VMEM_LIMIT = 56 * 1024 * 1024


def _cparams(*sem):
    return pltpu.CompilerParams(dimension_semantics=sem, vmem_limit_bytes=VMEM_LIMIT)


def _dot(a, b):
    return jnp.dot(a, b, preferred_element_type=F32)


def _dot_nt(a, b, precision=None):
    return lax.dot_general(a, b, (((1,), (1,)), ((), ())), precision=precision,
                           preferred_element_type=F32)


def _ada_kernel(cc_ref, w_ref, b_ref, o_ref):
    s = cc_ref[...]
    s = s * jax.nn.sigmoid(s)
    o_ref[0] = _dot(s.astype(BF16), w_ref[0].astype(BF16)) + b_ref[0]


def ada_modulation(cc, w_ada, b_ada):
    depth, d, d6 = w_ada.shape
    rows = cc.shape[0]
    tn = 1536
    return pl.pallas_call(
        _ada_kernel,
        out_shape=jax.ShapeDtypeStruct((depth, rows, d6), F32),
        grid=(depth, d6 // tn),
        in_specs=[
            pl.BlockSpec((rows, d), lambda l, j: (0, 0)),
            pl.BlockSpec((1, d, tn), lambda l, j: (l, 0, j)),
            pl.BlockSpec((1, 1, tn), lambda l, j: (l, 0, j)),
        ],
        out_specs=pl.BlockSpec((1, rows, tn), lambda l, j: (l, 0, j)),
        compiler_params=_cparams("parallel", "parallel"),
        name="ada_modulation",
    )(cc, w_ada, b_ada.reshape(depth, 1, d6))


def _norm_mod(xf, g, sh, sc):
    y = xf * lax.rsqrt(jnp.mean(xf * xf, axis=-1, keepdims=True) + EPS)
    return (y * g) * (1.0 + sc) + sh


def _premix_kernel(x_ref, g_ref, sh_ref, sc_ref, w_ref, *rest, rope):
    if rope:
        cos_ref, sin_ref, qk_ref, v_ref, gate_ref, p_ref = rest
    else:
        qk_ref, v_ref, gate_ref, p_ref = rest
    hb = _norm_mod(x_ref[0], g_ref[...], sh_ref[0], sc_ref[0]).astype(BF16)
    zqk = _dot(hb, w_ref[:, 0:2 * QK_W])
    half = QK_W // 2
    q1, q2 = zqk[:, 0:half], zqk[:, half:2 * half]
    k1 = zqk[:, 2 * half:3 * half] * (DK ** -0.5)
    k2 = zqk[:, 3 * half:4 * half] * (DK ** -0.5)
    if rope:
        cos, sin = cos_ref[...], sin_ref[...]
        q1, q2 = q1 * cos - q2 * sin, q1 * sin + q2 * cos
        k1, k2 = k1 * cos - k2 * sin, k1 * sin + k2 * cos
    qk_ref[0] = jnp.concatenate([q1, q2, k1, k2], axis=1).astype(BF16)
    o = 2 * QK_W
    v_ref[0] = _dot(hb, w_ref[:, o:o + V_W]).astype(BF16)
    gate_ref[0] = _dot(hb, w_ref[:, o + V_W:o + 2 * V_W])
    p_ref[0] = _dot(hb, w_ref[:, o + 2 * V_W:o + 2 * V_W + POOL_W])


def premix(x, norm_w, shift, scale, w_in_p, rope, tm):
    b, n, d = x.shape
    in_w = w_in_p.shape[1]
    row = lambda bi, i: (bi, i, 0)
    in_specs = [
        pl.BlockSpec((1, tm, d), row),
        pl.BlockSpec((1, d), lambda bi, i: (0, 0)),
        pl.BlockSpec((1, 1, d), lambda bi, i: (bi, 0, 0)),
        pl.BlockSpec((1, 1, d), lambda bi, i: (bi, 0, 0)),
        pl.BlockSpec((d, in_w), lambda bi, i: (0, 0)),
    ]
    args = [x, norm_w.reshape(1, d), shift, scale, w_in_p]
    if rope is not None:
        half = QK_W // 2
        in_specs += [pl.BlockSpec((tm, half), lambda bi, i: (i, 0))] * 2
        args += list(rope)
    return pl.pallas_call(
        functools.partial(_premix_kernel, rope=rope is not None),
        out_shape=(
            jax.ShapeDtypeStruct((b, n, 2 * QK_W), BF16),
            jax.ShapeDtypeStruct((b, n, V_W), BF16),
            jax.ShapeDtypeStruct((b, n, V_W), F32),
            jax.ShapeDtypeStruct((b, n, POOL_W), F32),
        ),
        grid=(b, n // tm),
        in_specs=in_specs,
        out_specs=(
            pl.BlockSpec((1, tm, 2 * QK_W), row),
            pl.BlockSpec((1, tm, V_W), row),
            pl.BlockSpec((1, tm, V_W), row),
            pl.BlockSpec((1, tm, POOL_W), row),
        ),
        compiler_params=_cparams("parallel", "parallel"),
        name="premix",
    )(*args)


def _fold_state(s):
    g = 2 * N_HEADS
    top = s[0:g]
    bot = s[QK_W // 2:QK_W // 2 + g]
    for h in range(1, N_HEADS):
        top = top + s[h * g:(h + 1) * g]
        bot = bot + s[QK_W // 2 + h * g:QK_W // 2 + (h + 1) * g]
    return jnp.concatenate([top, bot], axis=0)


def _expand_state(c):
    g = 2 * N_HEADS
    return jnp.concatenate([c[0:g]] * N_HEADS + [c[g:2 * g]] * N_HEADS, axis=0)


def _state_kernel(kf_ref, vf_ref, kb_ref, vb_ref, s0f_ref, s0b_ref, kdf_ref, kdb_ref,
                  cdf_ref, cdb_ref, bd_ref, sf_out, sb_out, ff_out, fb_out, sf_acc, sb_acc, *, cps):
    c = pl.program_id(1)
    nc = pl.num_programs(1)
    bd = bd_ref[...]

    @pl.when(c == 0)
    def _():
        sf_acc[...] = _expand_state(s0f_ref[0]) * bd
        sb_acc[...] = _expand_state(s0b_ref[0]) * bd

    def update(s, k, v, kd_ref, cd_ref):
        kd = k.astype(F32) * kd_ref[...]
        f = _dot(kd.T.astype(BF16), v)
        return s * cd_ref[...] + f * bd

    sf = sf_acc[...]
    sb = sb_acc[...]
    for i in range(cps):
        j = cps - 1 - i
        fs = slice(i * CHUNK, (i + 1) * CHUNK)
        bs = slice(j * CHUNK, (j + 1) * CHUNK)
        sf_out[0, i] = _fold_state(sf).astype(BF16)
        sb_out[0, j] = _fold_state(sb).astype(BF16)
        sf = update(sf, kf_ref[0, fs, :], vf_ref[0, fs, :], kdf_ref, cdf_ref)
        sb = update(sb, kb_ref[0, bs, :], vb_ref[0, bs, :], kdb_ref, cdb_ref)
    sf_acc[...] = sf
    sb_acc[...] = sb

    @pl.when(c == nc - 1)
    def _():
        ff_out[0] = _fold_state(sf)
        fb_out[0] = _fold_state(sb)


def state_scan(qk, v, s0f, s0b, tabs):
    b, n, _ = qk.shape
    nc = n // CHUNK
    cps = min(nc, 8)
    ns = nc // cps
    rows = cps * CHUNK
    const = lambda shape: pl.BlockSpec(shape, lambda bi, c: (0,) * len(shape))
    return pl.pallas_call(
        functools.partial(_state_kernel, cps=cps),
        out_shape=(
            jax.ShapeDtypeStruct((b, nc, DK, V_W), BF16),
            jax.ShapeDtypeStruct((b, nc, DK, V_W), BF16),
            jax.ShapeDtypeStruct((b, DK, V_W), F32),
            jax.ShapeDtypeStruct((b, DK, V_W), F32),
        ),
        grid=(b, ns),
        in_specs=[
            pl.BlockSpec((1, rows, QK_W), lambda bi, c: (bi, c, 1)),
            pl.BlockSpec((1, rows, V_W), lambda bi, c: (bi, c, 0)),
            pl.BlockSpec((1, rows, QK_W), lambda bi, c: (bi, ns - 1 - c, 1)),
            pl.BlockSpec((1, rows, V_W), lambda bi, c: (bi, ns - 1 - c, 0)),
            pl.BlockSpec((1, DK, V_W), lambda bi, c: (bi, 0, 0)),
            pl.BlockSpec((1, DK, V_W), lambda bi, c: (bi, 0, 0)),
            const((CHUNK, QK_W)), const((CHUNK, QK_W)),
            const((1, V_W)), const((1, V_W)),
            const((QK_W, V_W)),
        ],
        out_specs=(
            pl.BlockSpec((1, cps, DK, V_W), lambda bi, c: (bi, c, 0, 0)),
            pl.BlockSpec((1, cps, DK, V_W), lambda bi, c: (bi, ns - 1 - c, 0, 0)),
            pl.BlockSpec((1, DK, V_W), lambda bi, c: (bi, 0, 0)),
            pl.BlockSpec((1, DK, V_W), lambda bi, c: (bi, 0, 0)),
        ),
        scratch_shapes=[pltpu.VMEM((QK_W, V_W), F32), pltpu.VMEM((QK_W, V_W), F32)],
        compiler_params=_cparams("parallel", "arbitrary"),
        name="state_scan",
    )(qk, v, qk, v, s0f, s0b, tabs["kdec_f"], tabs["kdec_b"], tabs["cdec_f"], tabs["cdec_b"],
      tabs["bd_f32"])


def _pool_kernel(*refs, gw, tile, halo, rows_total):
    if halo:
        prev_ref, cur_ref, next_ref, pw_ref, ps_ref, o_ref, buf = refs
    else:
        cur_ref, pw_ref, ps_ref, o_ref, buf = refs
    i = pl.program_id(1)
    last = pl.num_programs(1) - 1
    m = MAX_HALF_WINDOW
    span = tile + 2 * halo
    zeros_m = jnp.zeros((m, POOL_W), F32)
    buf[0:m] = zeros_m
    buf[m + span:2 * m + span] = zeros_m
    if halo:
        buf[m:m + halo] = jnp.where(i > 0, prev_ref[0], 0.0)
        buf[m + halo + tile:m + span] = jnp.where(i < last, next_ref[0], 0.0)
    buf[m + halo:m + halo + tile] = cur_ref[0]

    pos = lax.broadcasted_iota(jnp.int32, (span, 1), 0)
    col = pos % gw
    tpos = lax.broadcasted_iota(jnp.int32, (tile, 1), 0)
    tcol = tpos % gw
    trow = i * (tile // gw) + tpos // gw
    for gi, w in enumerate(POOL_WINDOWS):
        cs = slice(gi * POOL_GROUP_DIM, (gi + 1) * POOL_GROUP_DIM)
        hw = w // 2
        s = None
        for d in range(-hw, hw):
            valid = (col + d >= 0) & (col + d < gw)
            term = jnp.where(valid, buf[m + d:m + d + span, cs], 0.0)
            s = term if s is None else s + term
        if halo:
            acc = None
            for d in range(-hw, hw):
                start = halo + d * gw
                term = s[start:start + tile]
                acc = term if acc is None else acc + term
            cnt_r = jnp.minimum(trow + hw, rows_total) - jnp.maximum(trow - hw, 0)
        else:
            acc = s
            cnt_r = 1
        cnt_c = jnp.minimum(tcol + hw, gw) - jnp.maximum(tcol - hw, 0)
        cnt = (cnt_r * cnt_c).astype(F32)
        xg = buf[m + halo:m + halo + tile, cs]
        diff = (acc / cnt - xg).astype(BF16)
        y = _dot(diff, pw_ref[gi]) * ps_ref[:, cs]
        o_ref[0, :, cs] = y.astype(BF16)


def pool_mixer(p, pool_w_bf, pool_scale, gw, tile):
    b, n, pw = p.shape
    rows_total = n // gw
    halo = MAX_HALF_WINDOW * gw if rows_total > 1 else 0
    in_specs, args = [], []
    if halo:
        r = tile // halo
        nh = n // halo
        in_specs = [
            pl.BlockSpec((1, halo, pw), lambda bi, i: (bi, jnp.maximum(i * r - 1, 0), 0)),
            pl.BlockSpec((1, tile, pw), lambda bi, i: (bi, i, 0)),
            pl.BlockSpec((1, halo, pw), lambda bi, i: (bi, jnp.minimum((i + 1) * r, nh - 1), 0)),
        ]
        args = [p, p, p]
    else:
        in_specs = [pl.BlockSpec((1, tile, pw), lambda bi, i: (bi, i, 0))]
        args = [p]
    ng = len(POOL_WINDOWS)
    in_specs += [
        pl.BlockSpec((ng, POOL_GROUP_DIM, POOL_GROUP_DIM), lambda bi, i: (0, 0, 0)),
        pl.BlockSpec((1, pw), lambda bi, i: (0, 0)),
    ]
    args += [pool_w_bf, pool_scale.reshape(1, pw)]
    return pl.pallas_call(
        functools.partial(_pool_kernel, gw=gw, tile=tile, halo=halo, rows_total=rows_total),
        out_shape=jax.ShapeDtypeStruct((b, n, pw), BF16),
        grid=(b, n // tile),
        in_specs=in_specs,
        out_specs=pl.BlockSpec((1, tile, pw), lambda bi, i: (bi, i, 0)),
        scratch_shapes=[pltpu.VMEM((tile + 2 * halo + 2 * MAX_HALF_WINDOW, pw), F32)],
        compiler_params=_cparams("parallel", "parallel"),
        name="pool_mixer",
    )(*args)


def _ret_kernel(qk_ref, v_ref, gate_ref, py_ref, x_ref, sf_ref, sb_ref, mp_ref, qdf_ref, qdb_ref,
                hm_ref, bd_ref, avg_ref, gn_ref, wo_ref, nw_ref, g1_ref, o_ref, o_buf, *, nch):
    bd = bd_ref[...]
    avg = avg_ref[...]
    lane = lax.broadcasted_iota(jnp.int32, (1, 2 * DV), 1)
    lo_mask = (lane < DV).astype(BF16)
    hi_mask = (lane >= DV).astype(BF16)
    for ci in range(nch):
        rs = slice(ci * CHUNK, (ci + 1) * CHUNK)
        q = qk_ref[0, rs, 0:QK_W]
        k = qk_ref[0, rs, QK_W:2 * QK_W]
        vv = v_ref[0, rs, :]
        inner = []
        for j in range(N_HEADS // 2):
            kp = jnp.concatenate([k * hm_ref[2 * j], k * hm_ref[2 * j + 1]], axis=0)
            sc = _dot_nt(q, kp) * mp_ref[j]
            vpair = vv[:, 2 * DV * j:2 * DV * (j + 1)]
            vp = jnp.concatenate([vpair * lo_mask, vpair * hi_mask], axis=0)
            inner.append(_dot(sc.astype(BF16), vp))
        o = jnp.concatenate(inner, axis=1)
        sf_bd = _expand_state(sf_ref[0, ci]) * bd
        sb_bd = _expand_state(sb_ref[0, ci]) * bd
        o_buf[rs, :] = o + _dot(q, sf_bd) * qdf_ref[...] + _dot(q, sb_bd) * qdb_ref[...]

    def head_mean(a):
        hi = a.astype(BF16)
        lo = (a - hi.astype(F32)).astype(BF16)
        return _dot(hi, avg) + _dot(lo, avg)

    o = o_buf[...]
    dlt = o - head_mean(o)
    var = head_mean(dlt * dlt)
    yn = dlt * lax.rsqrt(var + EPS) * gn_ref[...]
    g = gate_ref[0]
    r = (g * jax.nn.sigmoid(g) * yn).astype(BF16)
    mix = _dot(r, wo_ref[0:V_W, :]) + _dot(py_ref[0], wo_ref[V_W:V_W + POOL_W, :])
    y = mix * lax.rsqrt(jnp.mean(mix * mix, axis=-1, keepdims=True) + EPS) * nw_ref[...]
    o_ref[0] = x_ref[0] + g1_ref[0] * y


def retention_mixer(qk, v, gate, pool_y, x, sf, sb, tabs, gn, w_out_bf, norm_w, g1, tm):
    b, n, d = x.shape
    nch = tm // CHUNK
    row = lambda bi, i: (bi, i, 0)
    const = lambda shape: pl.BlockSpec(shape, lambda bi, i: (0,) * len(shape))
    return pl.pallas_call(
        functools.partial(_ret_kernel, nch=nch),
        out_shape=jax.ShapeDtypeStruct((b, n, d), F32),
        grid=(b, n // tm),
        in_specs=[
            pl.BlockSpec((1, tm, 2 * QK_W), row),
            pl.BlockSpec((1, tm, V_W), row),
            pl.BlockSpec((1, tm, V_W), row),
            pl.BlockSpec((1, tm, POOL_W), row),
            pl.BlockSpec((1, tm, d), row),
            pl.BlockSpec((1, nch, DK, V_W), lambda bi, i: (bi, i, 0, 0)),
            pl.BlockSpec((1, nch, DK, V_W), lambda bi, i: (bi, i, 0, 0)),
            const((N_HEADS // 2, CHUNK, 2 * CHUNK)),
            const((CHUNK, V_W)), const((CHUNK, V_W)),
            const((N_HEADS, 1, QK_W)),
            const((QK_W, V_W)),
            const((V_W, V_W)),
            const((1, V_W)),
            const((V_W + POOL_W, d)),
            const((1, d)),
            pl.BlockSpec((1, 1, d), lambda bi, i: (bi, 0, 0)),
        ],
        out_specs=pl.BlockSpec((1, tm, d), row),
        scratch_shapes=[pltpu.VMEM((tm, V_W), F32)],
        compiler_params=_cparams("parallel", "parallel"),
        name="retention_mixer",
    )(qk, v, gate, pool_y, x, sf, sb, tabs["mpair"], tabs["qdec_f"], tabs["qdec_b"], tabs["hmask"],
      tabs["bd_bf16"], tabs["avg"], gn.reshape(1, V_W), w_out_bf, norm_w.reshape(1, d), g1)


def _packed_width(d, dtype):
    return d * jnp.dtype(dtype).itemsize // 4


def _pack_rows(h):
    bits = lax.bitcast_convert_type(h.astype(F32), jnp.uint32)
    if h.dtype.itemsize == 4:
        return bits
    half = h.shape[1] // 2
    return (bits[:, half:] & jnp.uint32(0xFFFF0000)) | (bits[:, :half] >> 16)


def _unpack_rows(w, dtype):
    if jnp.dtype(dtype).itemsize == 4:
        return lax.bitcast_convert_type(w, dtype)
    lo = lax.bitcast_convert_type(w << 16, F32)
    hi = lax.bitcast_convert_type(w & jnp.uint32(0xFFFF0000), F32)
    return jnp.concatenate([lo, hi], axis=1).astype(dtype)


def _router_kernel(x_ref, g_ref, sh_ref, sc_ref, wr_ref, h_ref, aff_ref):
    h = _norm_mod(x_ref[0], g_ref[...], sh_ref[0], sc_ref[0])
    h_hi = h.astype(BF16)
    h_ref[0] = _pack_rows(h_hi)
    ne = wr_ref.shape[0]
    h_lo = (h - h_hi.astype(F32)).astype(BF16)
    w = wr_ref[...]
    w_hi = w.astype(BF16)
    w_lo = (w - w_hi.astype(F32)).astype(BF16)
    both = _dot_nt(jnp.concatenate([w_hi, w_lo], axis=0), h_hi)
    logits = both[0:ne] + both[ne:2 * ne] + _dot_nt(w_hi, h_lo)
    mx = jnp.max(logits, axis=0, keepdims=True)
    e = jnp.exp(logits - mx)
    aff_ref[0] = e / jnp.sum(e, axis=0, keepdims=True)


def router(x, norm_w, shift, scale, w_router_t, tm):
    b, n, d = x.shape
    ne = w_router_t.shape[0]
    return pl.pallas_call(
        _router_kernel,
        out_shape=(jax.ShapeDtypeStruct((b, n, _packed_width(d, BF16)), jnp.uint32),
                   jax.ShapeDtypeStruct((b, ne, n), F32)),
        grid=(b, n // tm),
        in_specs=[
            pl.BlockSpec((1, tm, d), lambda bi, i: (bi, i, 0)),
            pl.BlockSpec((1, d), lambda bi, i: (0, 0)),
            pl.BlockSpec((1, 1, d), lambda bi, i: (bi, 0, 0)),
            pl.BlockSpec((1, 1, d), lambda bi, i: (bi, 0, 0)),
            pl.BlockSpec((ne, d), lambda bi, i: (0, 0)),
        ],
        out_specs=(
            pl.BlockSpec((1, tm, _packed_width(d, BF16)), lambda bi, i: (bi, i, 0)),
            pl.BlockSpec((1, ne, tm), lambda bi, i: (bi, 0, i)),
        ),
        compiler_params=_cparams("parallel", "parallel"),
        name="router",
    )(x, norm_w.reshape(1, d), shift, scale, w_router_t)


SC_CORES = 2
SC_SUBCORES = 16
GATHER_CHUNK = 64


def gather_rows(table, idx):
    rows, width = idx.shape[0], table.shape[1]
    workers = SC_CORES * SC_SUBCORES
    per_worker = rows // workers
    assert rows == per_worker * workers and per_worker % GATHER_CHUNK == 0
    mesh = plsc.VectorSubcoreMesh(core_axis_name="c", subcore_axis_name="s")

    def body(table_hbm, idx_hbm, out_hbm, idx_v, rows_v, sem):
        wid = lax.axis_index("s") * SC_CORES + lax.axis_index("c")
        base = wid * per_worker

        @pl.loop(0, per_worker // GATHER_CHUNK)
        def _(i):
            off = pl.multiple_of(base + i * GATHER_CHUNK, GATHER_CHUNK)
            pltpu.sync_copy(idx_hbm.at[pl.ds(off, GATHER_CHUNK)], idx_v)
            pltpu.async_copy(table_hbm.at[idx_v], rows_v, sem).wait()
            pltpu.sync_copy(rows_v, out_hbm.at[pl.ds(off, GATHER_CHUNK)])

    return pl.kernel(
        body,
        out_type=jax.ShapeDtypeStruct((rows, width), table.dtype),
        mesh=mesh,
        scratch_types=[
            pltpu.VMEM((GATHER_CHUNK,), jnp.int32),
            pltpu.VMEM((GATHER_CHUNK, width), table.dtype),
            pltpu.SemaphoreType.DMA,
        ],
        name="gather_rows",
    )(table, idx)


FFN_CHUNKS = 8
FFN_VMEM_LIMIT = 60 * 1024 * 1024


def _ffn_spans(f):
    chunk = f // FFN_CHUNKS
    assert chunk * FFN_CHUNKS == f
    span = max(-(-((c + 1) * chunk - c * chunk // 16 * 16) // 16) * 16 for c in range(FFN_CHUNKS))
    offs = [min(c * chunk // 16 * 16, f - span) for c in range(FFN_CHUNKS)]
    assert all(o % 16 == 0 and o <= c * chunk and o + span >= (c + 1) * chunk
               for c, o in enumerate(offs))
    return chunk, span


def _ffn_kernel(*refs, rt, layer, e_off, cps, extra_rows):
    if extra_rows:
        (x_ref, gate_ref, xc_ref, gc_ref, wg_hbm, wu_hbm, wd_hbm, o_ref, oc_ref, wbf, stage, sem) = refs
    else:
        x_ref, gate_ref, wg_hbm, wu_hbm, wd_hbm, o_ref, wbf, stage, sem = refs
    e = pl.program_id(0)
    g = pl.program_id(1)
    ne = pl.num_programs(0)
    slot = e % 2
    nxt = jnp.minimum(e + 1, ne - 1)
    mats = (wg_hbm, wu_hbm, wd_hbm)
    chunk, span = _ffn_spans(wbf.shape[2])

    def offset(c):
        last = wbf.shape[2] - span
        if isinstance(c, int):
            return min(c * chunk // 16 * 16, last)
        return pl.multiple_of(jnp.minimum(c * chunk // 16 * 16, last), 16)

    def copies(ee, c, st):
        return [pltpu.make_async_copy(m.at[layer, e_off + ee, pl.ds(offset(c), span), :],
                                      stage.at[st, i], sem.at[st, i]) for i, m in enumerate(mats)]

    def start(ee, c, st):
        for cp in copies(ee, c, st):
            cp.start()

    def wait(ee, c, st):
        for cp in copies(ee, c, st):
            cp.wait()

    def cast(c, st, i, sl):
        wbf[sl, i, pl.ds(offset(c), span), :] = stage[st, i].astype(wbf.dtype)

    @pl.when((e == 0) & (g == 0))
    def _():
        for c in range(FFN_CHUNKS):
            start(0, c, 0)
            wait(0, c, 0)
            for i in range(3):
                cast(c, 0, i, 0)

    def swiglu(words, gate_row):
        r = words.shape[0]
        xt = _unpack_rows(words, wbf.dtype)
        a = _dot_nt(xt, wbf[slot, 0])
        u = _dot_nt(xt, wbf[slot, 1])
        hh = (a * jax.nn.sigmoid(a) * u).astype(wbf.dtype)
        gate_col = jnp.broadcast_to(gate_row, (LANES, r)).T[:, 0:1]
        return (_dot(hh, wbf[slot, 2]) * gate_col).astype(o_ref.dtype)

    def compute_tile(t):
        rs = slice(t * rt, (t + 1) * rt)
        o_ref[0, 0, rs, :] = swiglu(x_ref[0, 0, rs, :], gate_ref[0, 0, :, rs])

    if extra_rows:
        @pl.when(g == pl.num_programs(1) - 1)
        def _():
            oc_ref[0, 0] = swiglu(xc_ref[0, 0], gc_ref[0, 0])

    c0 = g * cps
    ntiles = x_ref.shape[2] // rt
    start(nxt, c0, 0)
    for t in range(ntiles):
        compute_tile(t)
        if 1 <= t <= cps:
            for i in range(3):
                cast(c0 + t - 1, (t - 1) % 2, i, 1 - slot)
        if t < cps:
            wait(nxt, c0 + t, t % 2)
            if t + 1 < cps:
                start(nxt, c0 + t + 1, (t + 1) % 2)
    if cps == ntiles:
        for i in range(3):
            cast(c0 + cps - 1, (cps - 1) % 2, i, 1 - slot)


def expert_ffn(xs, gate, wg_t, wu_t, wd, layer, e_off=0, extra=None):
    g, ne, rows, dp = xs.shape
    f, d = wd.shape[2], wd.shape[3]
    assert FFN_CHUNKS % g == 0
    cps = FFN_CHUNKS // g
    rt = min(rows, 256, rows // cps)
    assert rows % rt == 0 and rt % 16 == 0
    span = _ffn_spans(f)[1]
    any_spec = pl.BlockSpec(memory_space=pl.ANY)
    in_specs = [pl.BlockSpec((1, 1, rows, dp), lambda e, gi: (gi, e, 0, 0)),
                pl.BlockSpec((1, 1, 1, rows), lambda e, gi: (gi, e, 0, 0))]
    out_specs = pl.BlockSpec((1, 1, rows, d), lambda e, gi: (gi, e, 0, 0))
    out_shape = jax.ShapeDtypeStruct((g, ne, rows, d), BF16)
    args = [xs, gate]
    if extra is not None:
        rc = extra[0].shape[2]
        in_specs += [pl.BlockSpec((1, 1, rc, dp), lambda e, gi: (0, e, 0, 0)),
                     pl.BlockSpec((1, 1, 1, rc), lambda e, gi: (0, e, 0, 0))]
        out_specs = (out_specs, pl.BlockSpec((1, 1, rc, d), lambda e, gi: (0, e, 0, 0)))
        out_shape = (out_shape, jax.ShapeDtypeStruct((1, ne, rc, d), BF16))
        args += list(extra)
    return pl.pallas_call(
        functools.partial(_ffn_kernel, rt=rt, layer=layer, e_off=e_off, cps=cps,
                          extra_rows=extra is not None),
        out_shape=out_shape,
        grid=(ne, g),
        in_specs=in_specs + [any_spec, any_spec, any_spec],
        out_specs=out_specs,
        scratch_shapes=[
            pltpu.VMEM((2, 3, f, d), BF16),
            pltpu.VMEM((min(cps, 2), 3, span, d), F32),
            pltpu.SemaphoreType.DMA((min(cps, 2), 3)),
        ],
        compiler_params=pltpu.CompilerParams(dimension_semantics=("arbitrary", "arbitrary"),
                                             vmem_limit_bytes=FFN_VMEM_LIMIT),
        name="expert_ffn",
    )(*args, wg_t, wu_t, wd)


ROUTE_WIN = 64
COMBINE_STRIDE = 2
LANES = 128
VAL_ROWS = 8


def _combine_kernel(st_ref, rank_ref, *refs, tt, ne, cap, merged, nparts, win, stride):
    y_parts = refs[:nparts]
    x_ref, nw_ref, g_ref, spread_ref, o_ref, buf, xbuf, sem, xsem, acc_ref = refs[nparts:]
    b = pl.program_id(0)
    k = pl.program_id(1)
    nk = pl.num_programs(1)
    step = b * nk + k
    slot = step % 2
    rows_total = y_parts[0].shape[2]
    per_part = ne // nparts
    base = b * cap if merged else 0
    lane = lax.broadcasted_iota(jnp.int32, (1, LANES), 1)

    def tile_bounds(bb, kk, e):
        off = bb * cap if merged else 0
        lo = off + st_ref[(bb * (nk * stride + 1) + kk * stride) * ne + e]
        hi = off + st_ref[(bb * (nk * stride + 1) + (kk + 1) * stride) * ne + e]
        return lo, hi, jnp.minimum((lo // 16) * 16, rows_total - win)

    def bounds(e):
        return tile_bounds(b, k, e)

    def rows_of(bb, e, a0, count):
        y_ref = y_parts[e // per_part]
        return y_ref.at[0 if merged else bb, e % per_part, pl.ds(pl.multiple_of(a0, 16), count), :]

    def win_copy(bb, e, a0, sl):
        return pltpu.make_async_copy(rows_of(bb, e, a0, win), buf.at[sl, pl.ds(e * win, win), :],
                                     sem.at[sl, e])

    def start_tile(bb, kk, sl):
        for e in range(ne):
            win_copy(bb, e, tile_bounds(bb, kk, e)[2], sl).start()

    @pl.when(step == 0)
    def _():
        start_tile(b, k, slot)

    nxt = step + 1

    @pl.when(nxt < pl.num_programs(0) * nk)
    def _():
        start_tile(nxt // nk, nxt % nk, 1 - slot)

    rk = rank_ref[0]

    def token_major(rows_f32):
        return jnp.concatenate([rows_f32, jnp.full((LANES - ne, tt), -1.0, F32)], axis=0).T

    sub_e = lax.broadcasted_iota(jnp.int32, (ne, 1), 0)
    origin = jnp.zeros((ne, 1), jnp.int32)
    for e in range(ne):
        origin = jnp.where(sub_e == e, bounds(e)[2], origin)
    pos = rk + (base - origin)
    tgt = jnp.where((rk >= 0) & (pos >= 0) & (pos < win), pos, -1).astype(F32)
    tgt_b = _dot(token_major(tgt).astype(BF16), spread_ref[...])
    lane_pos = lax.broadcasted_iota(jnp.int32, (1, ne * win), 1) % win
    p = (tgt_b == lane_pos.astype(F32)).astype(BF16)
    for e in range(ne):
        win_copy(b, e, bounds(e)[2], slot).wait()
    acc_ref[...] = _dot(p, buf[slot])

    for e in range(ne):
        lo, hi, a0 = bounds(e)

        def extra(w, carry, e=e, a0=a0):
            start = a0 + w * win
            aw = pl.multiple_of(jnp.minimum(start, rows_total - win), 16)

            cp = pltpu.make_async_copy(rows_of(b, e, aw, win), xbuf, xsem)
            cp.start()
            cp.wait()
            col = token_major(rk.astype(F32)).astype(jnp.int32)[:, e:e + 1]
            ok = (col >= 0) & (col + base >= start)
            px = (jnp.where(ok, col + (base - aw), -1) == lane[:, 0:win]).astype(BF16)
            acc_ref[...] += _dot(px, xbuf[...])
            return carry

        lax.fori_loop(1, (hi - a0 + win - 1) // win, extra, 0)

    y = acc_ref[...]
    yn = y * lax.rsqrt(jnp.mean(y * y, axis=-1, keepdims=True) + EPS) * nw_ref[...]
    o_ref[0] = x_ref[0] + g_ref[0] * yn


def combine_post(starts, rank_t, ys, x, norm_w, g, tt, cap, merged):
    ys = tuple(ys) if isinstance(ys, (tuple, list)) else (ys,)
    b, n, d = x.shape
    rows = ys[0].shape[2]
    ne = sum(y.shape[1] for y in ys)
    assert all(y.shape == ys[0].shape for y in ys)
    stride = COMBINE_STRIDE if n % (tt * COMBINE_STRIDE) == 0 and rows >= ROUTE_WIN * COMBINE_STRIDE * 2 else 1
    tt = tt * stride
    win = ROUTE_WIN * stride
    assert win <= LANES
    lanes = np.arange(ne * win) // win
    spread = jnp.asarray(np.arange(LANES)[:, None] == lanes[None, :], BF16)
    grid_spec = pltpu.PrefetchScalarGridSpec(
        num_scalar_prefetch=1,
        grid=(b, n // tt),
        in_specs=[
            pl.BlockSpec((1, ne, tt), lambda bi, k, st: (bi, 0, k)),
            *[pl.BlockSpec(memory_space=pl.ANY)] * len(ys),
            pl.BlockSpec((1, tt, d), lambda bi, k, st: (bi, k, 0)),
            pl.BlockSpec((1, d), lambda bi, k, st: (0, 0)),
            pl.BlockSpec((1, 1, d), lambda bi, k, st: (bi, 0, 0)),
            pl.BlockSpec((LANES, ne * win), lambda bi, k, st: (0, 0)),
        ],
        out_specs=pl.BlockSpec((1, tt, d), lambda bi, k, st: (bi, k, 0)),
        scratch_shapes=[
            pltpu.VMEM((2, ne * win, d), BF16),
            pltpu.VMEM((win, d), BF16),
            pltpu.SemaphoreType.DMA((2, ne)),
            pltpu.SemaphoreType.DMA(()),
            pltpu.VMEM((tt, d), F32),
        ],
    )
    return pl.pallas_call(
        functools.partial(_combine_kernel, tt=tt, ne=ne, cap=cap, merged=merged, nparts=len(ys),
                          win=win, stride=stride),
        out_shape=jax.ShapeDtypeStruct((b, n, d), F32),
        grid_spec=grid_spec,
        compiler_params=_cparams("arbitrary", "arbitrary"),
        name="combine_post",
    )(starts, rank_t, *ys, x, norm_w.reshape(1, d), g, spread)


def _select_kernel(aff_ref, tri_ref, rank_ref, st_ref, *, cap, tt):
    a = aff_ref[0]
    ne, n = a.shape
    bits = lax.bitcast_convert_type(a, jnp.int32)

    def search(_, c):
        lo, hi = c
        mid = lo + ((hi - lo) >> 1)
        cnt = jnp.sum((bits >= mid).astype(F32), axis=1, keepdims=True)
        ge = cnt >= cap
        return jnp.where(ge, mid, lo), jnp.where(ge, hi, mid)

    lo0 = jnp.zeros((ne, 1), jnp.int32)
    hi0 = jnp.full((ne, 1), 0x7F800000, jnp.int32)
    thr, _ = lax.fori_loop(0, 31, search, (lo0, hi0))
    gt = bits > thr
    eq = bits == thr
    need = cap - jnp.sum(gt.astype(F32), axis=1, keepdims=True)
    m = jnp.concatenate([gt, eq], axis=0).astype(BF16)
    tri = tri_ref[...]
    lane = lax.broadcasted_iota(jnp.int32, (1, LANES), 1)
    off = jnp.zeros((2 * ne, 1), F32)
    st = jnp.zeros((ne, LANES), jnp.int32)
    for j in range(n // LANES):
        cs = slice(j * LANES, (j + 1) * LANES)
        if (j * LANES) % tt == 0:
            off_sel = off[:ne] + jnp.minimum(off[ne:], need)
            st = jnp.where(lane == (j * LANES) // tt, off_sel.astype(jnp.int32), st)
        mj = m[:, cs]
        pj = _dot(mj, tri) + off
        pe = pj[ne:]
        sel = gt[:, cs] | (eq[:, cs] & (pe < need))
        rank_ref[0, :, cs] = jnp.where(sel, pj[:ne] + jnp.minimum(pe, need), -1.0).astype(jnp.int32)
        off = off + jnp.sum(mj.astype(F32), axis=1, keepdims=True)
    st_ref[0] = jnp.where(lane == n // tt, cap, st)


def route_select(aff_t, cap, tt):
    b, ne, n = aff_t.shape
    tri = jnp.asarray(np.triu(np.ones((LANES, LANES), np.float32), 1), BF16)
    return pl.pallas_call(
        functools.partial(_select_kernel, cap=cap, tt=tt),
        out_shape=(jax.ShapeDtypeStruct((b, ne, n), jnp.int32),
                   jax.ShapeDtypeStruct((b, ne, LANES), jnp.int32)),
        grid=(b,),
        in_specs=[pl.BlockSpec((1, ne, n), lambda bi: (bi, 0, 0)),
                  pl.BlockSpec((LANES, LANES), lambda bi: (0, 0))],
        out_specs=(pl.BlockSpec((1, ne, n), lambda bi: (bi, 0, 0)),
                   pl.BlockSpec((1, ne, LANES), lambda bi: (bi, 0, 0))),
        compiler_params=_cparams("parallel"),
        name="route_select",
    )(aff_t, tri)


def _compact_kernel(st_ref, rank_ref, aff_ref, idx_ref, gate_ref, out_ref, *, tt, ne, cap, n):
    b = pl.program_id(0)
    k = pl.program_id(1)
    nk = pl.num_programs(1)
    win = ROUTE_WIN

    @pl.when(k == 0)
    def _():
        out_ref[...] = jnp.zeros_like(out_ref)

    rk = rank_ref[0]
    a = aff_ref[0]
    g1 = a.astype(BF16).astype(F32)
    r1 = a - g1
    g2 = r1.astype(BF16).astype(F32)
    g3 = r1 - g2
    tok = k * tt + lax.broadcasted_iota(jnp.int32, (1, tt), 1)
    ids = jnp.concatenate([(tok >> 6).astype(F32), (tok & 63).astype(F32),
                           jnp.zeros((VAL_ROWS - 2, tt), F32)], axis=0)
    pad = jnp.zeros((LANES - VAL_ROWS - 3 * ne, tt), F32)
    payload = jnp.concatenate([ids, g1, g2, g3, pad], axis=0).astype(BF16)
    sub = lax.broadcasted_iota(jnp.int32, (win, 1), 0)

    def bounds(e):
        lo = st_ref[(b * (nk + 1) + k) * ne + e]
        hi = st_ref[(b * (nk + 1) + k + 1) * ne + e]
        return hi, (lo // 8) * 8

    def one_hot(e, aw):
        return ((rk[e:e + 1, :] - aw) == sub).astype(BF16)

    p_all = jnp.concatenate([one_hot(e, bounds(e)[1]) for e in range(ne)], axis=0)
    moved = _dot_nt(p_all, payload)
    for e in range(ne):
        a0 = pl.multiple_of(bounds(e)[1], 8)
        out_ref[e, pl.ds(a0, win), :] += moved[e * win:(e + 1) * win]

    for e in range(ne):
        hi, a0 = bounds(e)

        def window(w, carry, e=e, a0=a0):
            aw = pl.multiple_of(a0 + w * win, 8)
            out_ref[e, pl.ds(aw, win), :] += _dot_nt(one_hot(e, aw), payload)
            return carry

        lax.fori_loop(1, (hi - a0 + win - 1) // win, window, 0)

    @pl.when(k == nk - 1)
    def _():
        for e in range(ne):
            t = out_ref[e].T
            ids_e = (t[0:1] * 64.0 + t[1:2]).astype(jnp.int32) + b * n
            g_e = (t[VAL_ROWS + e:VAL_ROWS + e + 1] + t[VAL_ROWS + ne + e:VAL_ROWS + ne + e + 1]) \
                + t[VAL_ROWS + 2 * ne + e:VAL_ROWS + 2 * ne + e + 1]
            idx_ref[0, e:e + 1, :] = ids_e[:, 0:cap]
            gate_ref[0, e:e + 1, :] = g_e[:, 0:cap]


def route_compact(starts, rank_t, aff_t, cap, tt):
    b, ne, n = aff_t.shape
    rows = -(-(cap + ROUTE_WIN) // LANES) * LANES
    grid_spec = pltpu.PrefetchScalarGridSpec(
        num_scalar_prefetch=1,
        grid=(b, n // tt),
        in_specs=[pl.BlockSpec((1, ne, tt), lambda bi, k, st: (bi, 0, k)),
                  pl.BlockSpec((1, ne, tt), lambda bi, k, st: (bi, 0, k))],
        out_specs=(pl.BlockSpec((1, ne, cap), lambda bi, k, st: (bi, 0, 0)),
                   pl.BlockSpec((1, ne, cap), lambda bi, k, st: (bi, 0, 0))),
        scratch_shapes=[pltpu.VMEM((ne, rows, LANES), F32)],
    )
    return pl.pallas_call(
        functools.partial(_compact_kernel, tt=tt, ne=ne, cap=cap, n=n),
        out_shape=(jax.ShapeDtypeStruct((b, ne, cap), jnp.int32),
                   jax.ShapeDtypeStruct((b, ne, cap), F32)),
        grid_spec=grid_spec,
        compiler_params=_cparams("parallel", "arbitrary"),
        name="route_compact",
    )(starts, rank_t, aff_t)


def _qk_head_of_lane():
    half = QK_W // 2
    return (np.arange(QK_W) % half) // (DK // 2)


def _in_proj_perm():
    half = DK // 2
    first = [h * DK + i for h in range(N_HEADS) for i in range(half)]
    second = [h * DK + half + i for h in range(N_HEADS) for i in range(half)]
    qperm = np.array(first + second)
    rest = np.arange(2 * QK_W, 2 * QK_W + 2 * V_W + POOL_W)
    return np.concatenate([qperm, QK_W + qperm, rest])


def _static_tables():
    qk_head = _qk_head_of_lane()
    v_head = np.arange(V_W) // DV
    hmask = (qk_head[None, :] == np.arange(N_HEADS)[:, None])
    bd = (qk_head[:, None] == v_head[None, :])
    avg = (v_head[:, None] == v_head[None, :]).astype(np.float32) / DV
    return {
        "hmask": jnp.asarray(hmask[:, None, :], BF16),
        "bd_f32": jnp.asarray(bd, F32),
        "bd_bf16": jnp.asarray(bd, BF16),
        "avg": jnp.asarray(avg, BF16),
    }


def _decay_tables(lg_f, lg_b):
    pos = jnp.arange(CHUNK, dtype=F32)
    diff = pos[:, None] - pos[None, :]
    low = diff >= 0
    up = diff < 0
    m_f = jnp.where(low, jnp.exp(lg_f[:, None, None] * jnp.where(low, diff, 0.0)), 0.0)
    m_b = jnp.where(up, jnp.exp(lg_b[:, None, None] * jnp.where(up, -diff, 0.0)), 0.0)
    m = m_f + m_b
    mpair = m.reshape(N_HEADS // 2, 2, CHUNK, CHUNK).transpose(0, 2, 1, 3).reshape(
        N_HEADS // 2, CHUNK, 2 * CHUNK)
    qk_head = _qk_head_of_lane()
    v_head = np.arange(V_W) // DV
    qdec_f = jnp.exp(lg_f[None, :] * (pos[:, None] + 1.0))[:, v_head]
    qdec_b = jnp.exp(lg_b[None, :] * (CHUNK - pos[:, None]))[:, v_head]
    kdec_f = jnp.exp(lg_f[None, :] * (CHUNK - 1.0 - pos[:, None]))[:, qk_head]
    kdec_b = jnp.exp(lg_b[None, :] * pos[:, None])[:, qk_head]
    cdec_f = jnp.exp(lg_f * CHUNK)[None, v_head]
    cdec_b = jnp.exp(lg_b * CHUNK)[None, v_head]
    return {"mpair": mpair, "qdec_f": qdec_f, "qdec_b": qdec_b, "kdec_f": kdec_f,
            "kdec_b": kdec_b, "cdec_f": cdec_f, "cdec_b": cdec_b}


def _rope_tables(n):
    t = jnp.arange(n)
    row = (t // GRID_W).astype(F32)
    col = (t % GRID_W).astype(F32)
    n_freq = DK // 4
    inv = ROPE_BASE ** (-jnp.arange(n_freq, dtype=F32) / n_freq)
    ang = jnp.concatenate([row[:, None] * inv, col[:, None] * inv], axis=-1)
    return jnp.tile(jnp.cos(ang), (1, N_HEADS)), jnp.tile(jnp.sin(ang), (1, N_HEADS))


EXPERT_GROUPS = 2


def _route(h_pk, aff_t, merged):
    b, n, d = h_pk.shape
    cap = EC_FACTOR * n // N_EXPERTS
    ne = N_EXPERTS
    per = ne // EXPERT_GROUPS
    tt = min(n, 256)
    nk = n // tt
    rank_t, st = route_select(aff_t, cap, tt)
    starts = st[:, :, :nk + 1].transpose(0, 2, 1).reshape(-1)
    flat, gate = route_compact(starts, rank_t, aff_t, cap, tt)
    table = h_pk.reshape(b * n, d)
    xs_list, gate_list = [], []
    for i in range(EXPERT_GROUPS):
        fl, gt = flat[:, i * per:(i + 1) * per], gate[:, i * per:(i + 1) * per]
        if merged:
            fl = fl.transpose(1, 0, 2)
            gt = gt.transpose(1, 0, 2).reshape(1, per, b * cap)
        xs_list.append(gather_rows(table, fl.reshape(-1)).reshape(gt.shape + (d,)))
        gate_list.append(gt[:, :, None, :])
    return xs_list, gate_list, (starts, rank_t, tt, cap, merged)


def _combine(x, y, info, norm_w, g):
    starts, rank_t, tt, cap, merged = info
    return combine_post(starts, rank_t, y, x, norm_w, g, tt, cap, merged)


def kernel(x, c, ctx, c_ctx, w_ada, b_ada, norm_pre_mix, norm_post_mix, norm_pre_ffn, norm_post_ffn, w_in, ret_decay_fwd, ret_decay_bwd, ret_gn, pool_w, pool_scale, w_out, w_router, w_gate, w_up, w_down):
    b, n, d = x.shape
    lc = ctx.shape[1]
    depth = w_ada.shape[0]
    rope = _rope_tables(n)
    static = _static_tables()
    perm = _in_proj_perm()

    cc = jnp.concatenate([c, c_ctx[None, :], jnp.zeros((7, d), F32)], axis=0)
    mods = ada_modulation(cc, w_ada, b_ada)
    wg, wu, wd = jnp.swapaxes(w_gate, 2, 3), jnp.swapaxes(w_up, 2, 3), w_down

    tm = 512
    for l in range(depth):
        last = l == depth - 1
        mx = mods[l, :b].reshape(b, 1, 6, d)
        sh1, sc1, g1, sh2, sc2, g2 = [mx[:, :, i] for i in range(6)]
        mc = jnp.broadcast_to(mods[l, b].reshape(1, 1, 6, d), (b, 1, 6, d))
        csh1, csc1, cg1, csh2, csc2, cg2 = [mc[:, :, i] for i in range(6)]
        lg_f = jax.nn.log_sigmoid(ret_decay_fwd[l].astype(F32))
        lg_b = jax.nn.log_sigmoid(ret_decay_bwd[l].astype(F32))
        tabs = dict(static, **_decay_tables(lg_f, lg_b))
        w_in_p = w_in[l][:, perm].astype(BF16)
        w_out_bf = w_out[l].astype(BF16)
        pool_w_bf = pool_w[l].astype(BF16)

        qk_c, v_c, gate_c, p_c = premix(ctx, norm_pre_mix[l], csh1, csc1, w_in_p, None, lc)
        zero = jnp.zeros((b, DK, V_W), F32)
        sf_c, sb_c, s_f, s_b = state_scan(qk_c, v_c, zero, zero, tabs)
        qk_x, v_x, gate_x, p_x = premix(x, norm_pre_mix[l], sh1, sc1, w_in_p, rope, tm)
        sf_x, sb_x, _, _ = state_scan(qk_x, v_x, s_f, s_b, tabs)
        pool_x = pool_mixer(p_x, pool_w_bf, pool_scale[l], GRID_W, min(n, 4096))
        x = retention_mixer(qk_x, v_x, gate_x, pool_x, x, sf_x, sb_x, tabs, ret_gn[l], w_out_bf,
                            norm_post_mix[l], g1, tm)
        if not last:
            pool_c = pool_mixer(p_c, pool_w_bf, pool_scale[l], lc, lc)
            ctx = retention_mixer(qk_c, v_c, gate_c, pool_c, ctx, sf_c, sb_c, tabs, ret_gn[l],
                                  w_out_bf, norm_post_mix[l], cg1, lc)

        wr_t = w_router[l].T
        h_x, aff_x = router(x, norm_pre_ffn[l], sh2, sc2, wr_t, tm)
        per = N_EXPERTS // EXPERT_GROUPS
        xs_x, gate_x2, info_x = _route(h_x, aff_x, False)
        if last:
            y_x = [expert_ffn(xs_x[i], gate_x2[i], wg, wu, wd, l, e_off=i * per)
                   for i in range(EXPERT_GROUPS)]
        else:
            h_c, aff_c = router(ctx, norm_pre_ffn[l], csh2, csc2, wr_t, lc)
            xs_c, gate_c2, info_c = _route(h_c, aff_c, True)
            pairs = [expert_ffn(xs_x[i], gate_x2[i], wg, wu, wd, l, e_off=i * per,
                                extra=(xs_c[i], gate_c2[i])) for i in range(EXPERT_GROUPS)]
            y_x = [p[0] for p in pairs]
            ctx = _combine(ctx, [p[1] for p in pairs], info_c, norm_post_ffn[l], cg2)
        x = _combine(x, y_x, info_x, norm_post_ffn[l], g2)
    return x
```

```python
import functools

import jax
import jax.numpy as jnp
import numpy as np
from jax import lax
from jax.experimental import pallas as pl
from jax.experimental.pallas import tpu as pltpu
from jax.experimental.pallas import tpu_sc as plsc

F32 = jnp.float32
BF16 = jnp.bfloat16

D_MODEL = 1024
GRID_W = 64
N_HEADS = 8
DV = 64
DK = 32
QK_W = N_HEADS * DK
V_W = N_HEADS * DV
POOL_W = 512
POOL_WINDOWS = (2, 4, 8, 16)
POOL_GROUP_DIM = 128
CHUNK = 128
ROPE_BASE = 10000.0
N_EXPERTS = 16
EC_FACTOR = 2
EPS = 1e-6
MAX_HALF_WINDOW = max(POOL_WINDOWS) // 2

VMEM_LIMIT = 56 * 1024 * 1024


def _cparams(*sem):
    return pltpu.CompilerParams(dimension_semantics=sem, vmem_limit_bytes=VMEM_LIMIT)


def _dot(a, b):
    return jnp.dot(a, b, preferred_element_type=F32)


def _dot_nt(a, b, precision=None):
    return lax.dot_general(a, b, (((1,), (1,)), ((), ())), precision=precision,
                           preferred_element_type=F32)


def _ada_kernel(cc_ref, w_ref, b_ref, o_ref):
    s = cc_ref[...]
    s = s * jax.nn.sigmoid(s)
    o_ref[0] = _dot(s.astype(BF16), w_ref[0].astype(BF16)) + b_ref[0]


def ada_modulation(cc, w_ada, b_ada):
    depth, d, d6 = w_ada.shape
    rows = cc.shape[0]
    tn = 1536
    return pl.pallas_call(
        _ada_kernel,
        out_shape=jax.ShapeDtypeStruct((depth, rows, d6), F32),
        grid=(depth, d6 // tn),
        in_specs=[
            pl.BlockSpec((rows, d), lambda l, j: (0, 0)),
            pl.BlockSpec((1, d, tn), lambda l, j: (l, 0, j)),
            pl.BlockSpec((1, 1, tn), lambda l, j: (l, 0, j)),
        ],
        out_specs=pl.BlockSpec((1, rows, tn), lambda l, j: (l, 0, j)),
        compiler_params=_cparams("parallel", "parallel"),
        name="ada_modulation",
    )(cc, w_ada, b_ada.reshape(depth, 1, d6))


def _norm_mod(xf, g, sh, sc):
    y = xf * lax.rsqrt(jnp.mean(xf * xf, axis=-1, keepdims=True) + EPS)
    return (y * g) * (1.0 + sc) + sh


def _premix_kernel(x_ref, g_ref, sh_ref, sc_ref, w_ref, *rest, rope):
    if rope:
        cos_ref, sin_ref, qk_ref, v_ref, gate_ref, p_ref = rest
    else:
        qk_ref, v_ref, gate_ref, p_ref = rest
    hb = _norm_mod(x_ref[0], g_ref[...], sh_ref[0], sc_ref[0]).astype(BF16)
    zqk = _dot(hb, w_ref[:, 0:2 * QK_W])
    half = QK_W // 2
    q1, q2 = zqk[:, 0:half], zqk[:, half:2 * half]
    k1 = zqk[:, 2 * half:3 * half] * (DK ** -0.5)
    k2 = zqk[:, 3 * half:4 * half] * (DK ** -0.5)
    if rope:
        cos, sin = cos_ref[...], sin_ref[...]
        q1, q2 = q1 * cos - q2 * sin, q1 * sin + q2 * cos
        k1, k2 = k1 * cos - k2 * sin, k1 * sin + k2 * cos
    qk_ref[0] = jnp.concatenate([q1, q2, k1, k2], axis=1).astype(BF16)
    o = 2 * QK_W
    v_ref[0] = _dot(hb, w_ref[:, o:o + V_W]).astype(BF16)
    gate_ref[0] = _dot(hb, w_ref[:, o + V_W:o + 2 * V_W])
    p_ref[0] = _dot(hb, w_ref[:, o + 2 * V_W:o + 2 * V_W + POOL_W])


def premix(x, norm_w, shift, scale, w_in_p, rope, tm):
    b, n, d = x.shape
    in_w = w_in_p.shape[1]
    row = lambda bi, i: (bi, i, 0)
    in_specs = [
        pl.BlockSpec((1, tm, d), row),
        pl.BlockSpec((1, d), lambda bi, i: (0, 0)),
        pl.BlockSpec((1, 1, d), lambda bi, i: (bi, 0, 0)),
        pl.BlockSpec((1, 1, d), lambda bi, i: (bi, 0, 0)),
        pl.BlockSpec((d, in_w), lambda bi, i: (0, 0)),
    ]
    args = [x, norm_w.reshape(1, d), shift, scale, w_in_p]
    if rope is not None:
        half = QK_W // 2
        in_specs += [pl.BlockSpec((tm, half), lambda bi, i: (i, 0))] * 2
        args += list(rope)
    return pl.pallas_call(
        functools.partial(_premix_kernel, rope=rope is not None),
        out_shape=(
            jax.ShapeDtypeStruct((b, n, 2 * QK_W), BF16),
            jax.ShapeDtypeStruct((b, n, V_W), BF16),
            jax.ShapeDtypeStruct((b, n, V_W), F32),
            jax.ShapeDtypeStruct((b, n, POOL_W), F32),
        ),
        grid=(b, n // tm),
        in_specs=in_specs,
        out_specs=(
            pl.BlockSpec((1, tm, 2 * QK_W), row),
            pl.BlockSpec((1, tm, V_W), row),
            pl.BlockSpec((1, tm, V_W), row),
            pl.BlockSpec((1, tm, POOL_W), row),
        ),
        compiler_params=_cparams("parallel", "parallel"),
        name="premix",
    )(*args)


def _fold_state(s):
    g = 2 * N_HEADS
    top = s[0:g]
    bot = s[QK_W // 2:QK_W // 2 + g]
    for h in range(1, N_HEADS):
        top = top + s[h * g:(h + 1) * g]
        bot = bot + s[QK_W // 2 + h * g:QK_W // 2 + (h + 1) * g]
    return jnp.concatenate([top, bot], axis=0)


def _expand_state(c):
    g = 2 * N_HEADS
    return jnp.concatenate([c[0:g]] * N_HEADS + [c[g:2 * g]] * N_HEADS, axis=0)


def _state_kernel(kf_ref, vf_ref, kb_ref, vb_ref, s0f_ref, s0b_ref, kdf_ref, kdb_ref,
                  cdf_ref, cdb_ref, bd_ref, sf_out, sb_out, ff_out, fb_out, sf_acc, sb_acc, *, cps):
    c = pl.program_id(1)
    nc = pl.num_programs(1)
    bd = bd_ref[...]

    @pl.when(c == 0)
    def _():
        sf_acc[...] = _expand_state(s0f_ref[0]) * bd
        sb_acc[...] = _expand_state(s0b_ref[0]) * bd

    def update(s, k, v, kd_ref, cd_ref):
        kd = k.astype(F32) * kd_ref[...]
        f = _dot(kd.T.astype(BF16), v)
        return s * cd_ref[...] + f * bd

    sf = sf_acc[...]
    sb = sb_acc[...]
    for i in range(cps):
        j = cps - 1 - i
        fs = slice(i * CHUNK, (i + 1) * CHUNK)
        bs = slice(j * CHUNK, (j + 1) * CHUNK)
        sf_out[0, i] = _fold_state(sf).astype(BF16)
        sb_out[0, j] = _fold_state(sb).astype(BF16)
        sf = update(sf, kf_ref[0, fs, :], vf_ref[0, fs, :], kdf_ref, cdf_ref)
        sb = update(sb, kb_ref[0, bs, :], vb_ref[0, bs, :], kdb_ref, cdb_ref)
    sf_acc[...] = sf
    sb_acc[...] = sb

    @pl.when(c == nc - 1)
    def _():
        ff_out[0] = _fold_state(sf)
        fb_out[0] = _fold_state(sb)


def state_scan(qk, v, s0f, s0b, tabs):
    b, n, _ = qk.shape
    nc = n // CHUNK
    cps = min(nc, 8)
    ns = nc // cps
    rows = cps * CHUNK
    const = lambda shape: pl.BlockSpec(shape, lambda bi, c: (0,) * len(shape))
    return pl.pallas_call(
        functools.partial(_state_kernel, cps=cps),
        out_shape=(
            jax.ShapeDtypeStruct((b, nc, DK, V_W), BF16),
            jax.ShapeDtypeStruct((b, nc, DK, V_W), BF16),
            jax.ShapeDtypeStruct((b, DK, V_W), F32),
            jax.ShapeDtypeStruct((b, DK, V_W), F32),
        ),
        grid=(b, ns),
        in_specs=[
            pl.BlockSpec((1, rows, QK_W), lambda bi, c: (bi, c, 1)),
            pl.BlockSpec((1, rows, V_W), lambda bi, c: (bi, c, 0)),
            pl.BlockSpec((1, rows, QK_W), lambda bi, c: (bi, ns - 1 - c, 1)),
            pl.BlockSpec((1, rows, V_W), lambda bi, c: (bi, ns - 1 - c, 0)),
            pl.BlockSpec((1, DK, V_W), lambda bi, c: (bi, 0, 0)),
            pl.BlockSpec((1, DK, V_W), lambda bi, c: (bi, 0, 0)),
            const((CHUNK, QK_W)), const((CHUNK, QK_W)),
            const((1, V_W)), const((1, V_W)),
            const((QK_W, V_W)),
        ],
        out_specs=(
            pl.BlockSpec((1, cps, DK, V_W), lambda bi, c: (bi, c, 0, 0)),
            pl.BlockSpec((1, cps, DK, V_W), lambda bi, c: (bi, ns - 1 - c, 0, 0)),
            pl.BlockSpec((1, DK, V_W), lambda bi, c: (bi, 0, 0)),
            pl.BlockSpec((1, DK, V_W), lambda bi, c: (bi, 0, 0)),
        ),
        scratch_shapes=[pltpu.VMEM((QK_W, V_W), F32), pltpu.VMEM((QK_W, V_W), F32)],
        compiler_params=_cparams("parallel", "arbitrary"),
        name="state_scan",
    )(qk, v, qk, v, s0f, s0b, tabs["kdec_f"], tabs["kdec_b"], tabs["cdec_f"], tabs["cdec_b"],
      tabs["bd_f32"])


def _pool_kernel(*refs, gw, tile, halo, rows_total):
    if halo:
        prev_ref, cur_ref, next_ref, pw_ref, ps_ref, o_ref, buf = refs
    else:
        cur_ref, pw_ref, ps_ref, o_ref, buf = refs
    i = pl.program_id(1)
    last = pl.num_programs(1) - 1
    m = MAX_HALF_WINDOW
    span = tile + 2 * halo
    zeros_m = jnp.zeros((m, POOL_W), F32)
    buf[0:m] = zeros_m
    buf[m + span:2 * m + span] = zeros_m
    if halo:
        buf[m:m + halo] = jnp.where(i > 0, prev_ref[0], 0.0)
        buf[m + halo + tile:m + span] = jnp.where(i < last, next_ref[0], 0.0)
    buf[m + halo:m + halo + tile] = cur_ref[0]

    pos = lax.broadcasted_iota(jnp.int32, (span, 1), 0)
    col = pos % gw
    tpos = lax.broadcasted_iota(jnp.int32, (tile, 1), 0)
    tcol = tpos % gw
    trow = i * (tile // gw) + tpos // gw
    for gi, w in enumerate(POOL_WINDOWS):
        cs = slice(gi * POOL_GROUP_DIM, (gi + 1) * POOL_GROUP_DIM)
        hw = w // 2
        s = None
        for d in range(-hw, hw):
            valid = (col + d >= 0) & (col + d < gw)
            term = jnp.where(valid, buf[m + d:m + d + span, cs], 0.0)
            s = term if s is None else s + term
        if halo:
            acc = None
            for d in range(-hw, hw):
                start = halo + d * gw
                term = s[start:start + tile]
                acc = term if acc is None else acc + term
            cnt_r = jnp.minimum(trow + hw, rows_total) - jnp.maximum(trow - hw, 0)
        else:
            acc = s
            cnt_r = 1
        cnt_c = jnp.minimum(tcol + hw, gw) - jnp.maximum(tcol - hw, 0)
        cnt = (cnt_r * cnt_c).astype(F32)
        xg = buf[m + halo:m + halo + tile, cs]
        diff = (acc / cnt - xg).astype(BF16)
        y = _dot(diff, pw_ref[gi]) * ps_ref[:, cs]
        o_ref[0, :, cs] = y.astype(BF16)


def pool_mixer(p, pool_w_bf, pool_scale, gw, tile):
    b, n, pw = p.shape
    rows_total = n // gw
    halo = MAX_HALF_WINDOW * gw if rows_total > 1 else 0
    in_specs, args = [], []
    if halo:
        r = tile // halo
        nh = n // halo
        in_specs = [
            pl.BlockSpec((1, halo, pw), lambda bi, i: (bi, jnp.maximum(i * r - 1, 0), 0)),
            pl.BlockSpec((1, tile, pw), lambda bi, i: (bi, i, 0)),
            pl.BlockSpec((1, halo, pw), lambda bi, i: (bi, jnp.minimum((i + 1) * r, nh - 1), 0)),
        ]
        args = [p, p, p]
    else:
        in_specs = [pl.BlockSpec((1, tile, pw), lambda bi, i: (bi, i, 0))]
        args = [p]
    ng = len(POOL_WINDOWS)
    in_specs += [
        pl.BlockSpec((ng, POOL_GROUP_DIM, POOL_GROUP_DIM), lambda bi, i: (0, 0, 0)),
        pl.BlockSpec((1, pw), lambda bi, i: (0, 0)),
    ]
    args += [pool_w_bf, pool_scale.reshape(1, pw)]
    return pl.pallas_call(
        functools.partial(_pool_kernel, gw=gw, tile=tile, halo=halo, rows_total=rows_total),
        out_shape=jax.ShapeDtypeStruct((b, n, pw), BF16),
        grid=(b, n // tile),
        in_specs=in_specs,
        out_specs=pl.BlockSpec((1, tile, pw), lambda bi, i: (bi, i, 0)),
        scratch_shapes=[pltpu.VMEM((tile + 2 * halo + 2 * MAX_HALF_WINDOW, pw), F32)],
        compiler_params=_cparams("parallel", "parallel"),
        name="pool_mixer",
    )(*args)


def _ret_kernel(qk_ref, v_ref, gate_ref, py_ref, x_ref, sf_ref, sb_ref, mp_ref, qdf_ref, qdb_ref,
                hm_ref, bd_ref, avg_ref, gn_ref, wo_ref, nw_ref, g1_ref, o_ref, o_buf, *, nch):
    bd = bd_ref[...]
    avg = avg_ref[...]
    lane = lax.broadcasted_iota(jnp.int32, (1, 2 * DV), 1)
    lo_mask = (lane < DV).astype(BF16)
    hi_mask = (lane >= DV).astype(BF16)
    for ci in range(nch):
        rs = slice(ci * CHUNK, (ci + 1) * CHUNK)
        q = qk_ref[0, rs, 0:QK_W]
        k = qk_ref[0, rs, QK_W:2 * QK_W]
        vv = v_ref[0, rs, :]
        inner = []
        for j in range(N_HEADS // 2):
            kp = jnp.concatenate([k * hm_ref[2 * j], k * hm_ref[2 * j + 1]], axis=0)
            sc = _dot_nt(q, kp) * mp_ref[j]
            vpair = vv[:, 2 * DV * j:2 * DV * (j + 1)]
            vp = jnp.concatenate([vpair * lo_mask, vpair * hi_mask], axis=0)
            inner.append(_dot(sc.astype(BF16), vp))
        o = jnp.concatenate(inner, axis=1)
        sf_bd = _expand_state(sf_ref[0, ci]) * bd
        sb_bd = _expand_state(sb_ref[0, ci]) * bd
        o_buf[rs, :] = o + _dot(q, sf_bd) * qdf_ref[...] + _dot(q, sb_bd) * qdb_ref[...]

    def head_mean(a):
        hi = a.astype(BF16)
        lo = (a - hi.astype(F32)).astype(BF16)
        return _dot(hi, avg) + _dot(lo, avg)

    o = o_buf[...]
    dlt = o - head_mean(o)
    var = head_mean(dlt * dlt)
    yn = dlt * lax.rsqrt(var + EPS) * gn_ref[...]
    g = gate_ref[0]
    r = (g * jax.nn.sigmoid(g) * yn).astype(BF16)
    mix = _dot(r, wo_ref[0:V_W, :]) + _dot(py_ref[0], wo_ref[V_W:V_W + POOL_W, :])
    y = mix * lax.rsqrt(jnp.mean(mix * mix, axis=-1, keepdims=True) + EPS) * nw_ref[...]
    o_ref[0] = x_ref[0] + g1_ref[0] * y


def retention_mixer(qk, v, gate, pool_y, x, sf, sb, tabs, gn, w_out_bf, norm_w, g1, tm):
    b, n, d = x.shape
    nch = tm // CHUNK
    row = lambda bi, i: (bi, i, 0)
    const = lambda shape: pl.BlockSpec(shape, lambda bi, i: (0,) * len(shape))
    return pl.pallas_call(
        functools.partial(_ret_kernel, nch=nch),
        out_shape=jax.ShapeDtypeStruct((b, n, d), F32),
        grid=(b, n // tm),
        in_specs=[
            pl.BlockSpec((1, tm, 2 * QK_W), row),
            pl.BlockSpec((1, tm, V_W), row),
            pl.BlockSpec((1, tm, V_W), row),
            pl.BlockSpec((1, tm, POOL_W), row),
            pl.BlockSpec((1, tm, d), row),
            pl.BlockSpec((1, nch, DK, V_W), lambda bi, i: (bi, i, 0, 0)),
            pl.BlockSpec((1, nch, DK, V_W), lambda bi, i: (bi, i, 0, 0)),
            const((N_HEADS // 2, CHUNK, 2 * CHUNK)),
            const((CHUNK, V_W)), const((CHUNK, V_W)),
            const((N_HEADS, 1, QK_W)),
            const((QK_W, V_W)),
            const((V_W, V_W)),
            const((1, V_W)),
            const((V_W + POOL_W, d)),
            const((1, d)),
            pl.BlockSpec((1, 1, d), lambda bi, i: (bi, 0, 0)),
        ],
        out_specs=pl.BlockSpec((1, tm, d), row),
        scratch_shapes=[pltpu.VMEM((tm, V_W), F32)],
        compiler_params=_cparams("parallel", "parallel"),
        name="retention_mixer",
    )(qk, v, gate, pool_y, x, sf, sb, tabs["mpair"], tabs["qdec_f"], tabs["qdec_b"], tabs["hmask"],
      tabs["bd_bf16"], tabs["avg"], gn.reshape(1, V_W), w_out_bf, norm_w.reshape(1, d), g1)


def _packed_width(d, dtype):
    return d * jnp.dtype(dtype).itemsize // 4


def _pack_rows(h):
    bits = lax.bitcast_convert_type(h.astype(F32), jnp.uint32)
    if h.dtype.itemsize == 4:
        return bits
    half = h.shape[1] // 2
    return (bits[:, half:] & jnp.uint32(0xFFFF0000)) | (bits[:, :half] >> 16)


def _unpack_rows(w, dtype):
    if jnp.dtype(dtype).itemsize == 4:
        return lax.bitcast_convert_type(w, dtype)
    lo = lax.bitcast_convert_type(w << 16, F32)
    hi = lax.bitcast_convert_type(w & jnp.uint32(0xFFFF0000), F32)
    return jnp.concatenate([lo, hi], axis=1).astype(dtype)


def _router_kernel(x_ref, g_ref, sh_ref, sc_ref, wr_ref, h_ref, aff_ref):
    h = _norm_mod(x_ref[0], g_ref[...], sh_ref[0], sc_ref[0])
    h_hi = h.astype(BF16)
    h_ref[0] = _pack_rows(h_hi)
    ne = wr_ref.shape[0]
    h_lo = (h - h_hi.astype(F32)).astype(BF16)
    w = wr_ref[...]
    w_hi = w.astype(BF16)
    w_lo = (w - w_hi.astype(F32)).astype(BF16)
    both = _dot_nt(jnp.concatenate([w_hi, w_lo], axis=0), h_hi)
    logits = both[0:ne] + both[ne:2 * ne] + _dot_nt(w_hi, h_lo)
    mx = jnp.max(logits, axis=0, keepdims=True)
    e = jnp.exp(logits - mx)
    aff_ref[0] = e / jnp.sum(e, axis=0, keepdims=True)


def router(x, norm_w, shift, scale, w_router_t, tm):
    b, n, d = x.shape
    ne = w_router_t.shape[0]
    return pl.pallas_call(
        _router_kernel,
        out_shape=(jax.ShapeDtypeStruct((b, n, _packed_width(d, BF16)), jnp.uint32),
                   jax.ShapeDtypeStruct((b, ne, n), F32)),
        grid=(b, n // tm),
        in_specs=[
            pl.BlockSpec((1, tm, d), lambda bi, i: (bi, i, 0)),
            pl.BlockSpec((1, d), lambda bi, i: (0, 0)),
            pl.BlockSpec((1, 1, d), lambda bi, i: (bi, 0, 0)),
            pl.BlockSpec((1, 1, d), lambda bi, i: (bi, 0, 0)),
            pl.BlockSpec((ne, d), lambda bi, i: (0, 0)),
        ],
        out_specs=(
            pl.BlockSpec((1, tm, _packed_width(d, BF16)), lambda bi, i: (bi, i, 0)),
            pl.BlockSpec((1, ne, tm), lambda bi, i: (bi, 0, i)),
        ),
        compiler_params=_cparams("parallel", "parallel"),
        name="router",
    )(x, norm_w.reshape(1, d), shift, scale, w_router_t)


SC_CORES = 2
SC_SUBCORES = 16
GATHER_CHUNK = 64


def gather_rows(table, idx):
    rows, width = idx.shape[0], table.shape[1]
    workers = SC_CORES * SC_SUBCORES
    per_worker = rows // workers
    assert rows == per_worker * workers and per_worker % GATHER_CHUNK == 0
    mesh = plsc.VectorSubcoreMesh(core_axis_name="c", subcore_axis_name="s")

    def body(table_hbm, idx_hbm, out_hbm, idx_v, rows_v, sem):
        wid = lax.axis_index("s") * SC_CORES + lax.axis_index("c")
        base = wid * per_worker

        @pl.loop(0, per_worker // GATHER_CHUNK)
        def _(i):
            off = pl.multiple_of(base + i * GATHER_CHUNK, GATHER_CHUNK)
            pltpu.sync_copy(idx_hbm.at[pl.ds(off, GATHER_CHUNK)], idx_v)
            pltpu.async_copy(table_hbm.at[idx_v], rows_v, sem).wait()
            pltpu.sync_copy(rows_v, out_hbm.at[pl.ds(off, GATHER_CHUNK)])

    return pl.kernel(
        body,
        out_type=jax.ShapeDtypeStruct((rows, width), table.dtype),
        mesh=mesh,
        scratch_types=[
            pltpu.VMEM((GATHER_CHUNK,), jnp.int32),
            pltpu.VMEM((GATHER_CHUNK, width), table.dtype),
            pltpu.SemaphoreType.DMA,
        ],
        name="gather_rows",
    )(table, idx)


FFN_CHUNKS = 8
FFN_VMEM_LIMIT = 60 * 1024 * 1024


def _ffn_spans(f):
    chunk = f // FFN_CHUNKS
    assert chunk * FFN_CHUNKS == f
    span = max(-(-((c + 1) * chunk - c * chunk // 16 * 16) // 16) * 16 for c in range(FFN_CHUNKS))
    offs = [min(c * chunk // 16 * 16, f - span) for c in range(FFN_CHUNKS)]
    assert all(o % 16 == 0 and o <= c * chunk and o + span >= (c + 1) * chunk
               for c, o in enumerate(offs))
    return chunk, span


def _ffn_kernel(*refs, rt, layer, e_off, cps, extra_rows):
    if extra_rows:
        (x_ref, gate_ref, xc_ref, gc_ref, wg_hbm, wu_hbm, wd_hbm, o_ref, oc_ref, wbf, stage, sem) = refs
    else:
        x_ref, gate_ref, wg_hbm, wu_hbm, wd_hbm, o_ref, wbf, stage, sem = refs
    e = pl.program_id(0)
    g = pl.program_id(1)
    ne = pl.num_programs(0)
    slot = e % 2
    nxt = jnp.minimum(e + 1, ne - 1)
    mats = (wg_hbm, wu_hbm, wd_hbm)
    chunk, span = _ffn_spans(wbf.shape[2])

    def offset(c):
        last = wbf.shape[2] - span
        if isinstance(c, int):
            return min(c * chunk // 16 * 16, last)
        return pl.multiple_of(jnp.minimum(c * chunk // 16 * 16, last), 16)

    def copies(ee, c, st):
        return [pltpu.make_async_copy(m.at[layer, e_off + ee, pl.ds(offset(c), span), :],
                                      stage.at[st, i], sem.at[st, i]) for i, m in enumerate(mats)]

    def start(ee, c, st):
        for cp in copies(ee, c, st):
            cp.start()

    def wait(ee, c, st):
        for cp in copies(ee, c, st):
            cp.wait()

    def cast(c, st, i, sl):
        wbf[sl, i, pl.ds(offset(c), span), :] = stage[st, i].astype(wbf.dtype)

    @pl.when((e == 0) & (g == 0))
    def _():
        for c in range(FFN_CHUNKS):
            start(0, c, 0)
            wait(0, c, 0)
            for i in range(3):
                cast(c, 0, i, 0)

    def swiglu(words, gate_row):
        r = words.shape[0]
        xt = _unpack_rows(words, wbf.dtype)
        a = _dot_nt(xt, wbf[slot, 0])
        u = _dot_nt(xt, wbf[slot, 1])
        hh = (a * jax.nn.sigmoid(a) * u).astype(wbf.dtype)
        gate_col = jnp.broadcast_to(gate_row, (LANES, r)).T[:, 0:1]
        return (_dot(hh, wbf[slot, 2]) * gate_col).astype(o_ref.dtype)

    def compute_tile(t):
        rs = slice(t * rt, (t + 1) * rt)
        o_ref[0, 0, rs, :] = swiglu(x_ref[0, 0, rs, :], gate_ref[0, 0, :, rs])

    if extra_rows:
        @pl.when(g == pl.num_programs(1) - 1)
        def _():
            oc_ref[0, 0] = swiglu(xc_ref[0, 0], gc_ref[0, 0])

    c0 = g * cps
    ntiles = x_ref.shape[2] // rt
    start(nxt, c0, 0)
    for t in range(ntiles):
        compute_tile(t)
        if 1 <= t <= cps:
            for i in range(3):
                cast(c0 + t - 1, (t - 1) % 2, i, 1 - slot)
        if t < cps:
            wait(nxt, c0 + t, t % 2)
            if t + 1 < cps:
                start(nxt, c0 + t + 1, (t + 1) % 2)
    if cps == ntiles:
        for i in range(3):
            cast(c0 + cps - 1, (cps - 1) % 2, i, 1 - slot)


def expert_ffn(xs, gate, wg_t, wu_t, wd, layer, e_off=0, extra=None):
    g, ne, rows, dp = xs.shape
    f, d = wd.shape[2], wd.shape[3]
    assert FFN_CHUNKS % g == 0
    cps = FFN_CHUNKS // g
    rt = min(rows, 256, rows // cps)
    assert rows % rt == 0 and rt % 16 == 0
    span = _ffn_spans(f)[1]
    any_spec = pl.BlockSpec(memory_space=pl.ANY)
    in_specs = [pl.BlockSpec((1, 1, rows, dp), lambda e, gi: (gi, e, 0, 0)),
                pl.BlockSpec((1, 1, 1, rows), lambda e, gi: (gi, e, 0, 0))]
    out_specs = pl.BlockSpec((1, 1, rows, d), lambda e, gi: (gi, e, 0, 0))
    out_shape = jax.ShapeDtypeStruct((g, ne, rows, d), BF16)
    args = [xs, gate]
    if extra is not None:
        rc = extra[0].shape[2]
        in_specs += [pl.BlockSpec((1, 1, rc, dp), lambda e, gi: (0, e, 0, 0)),
                     pl.BlockSpec((1, 1, 1, rc), lambda e, gi: (0, e, 0, 0))]
        out_specs = (out_specs, pl.BlockSpec((1, 1, rc, d), lambda e, gi: (0, e, 0, 0)))
        out_shape = (out_shape, jax.ShapeDtypeStruct((1, ne, rc, d), BF16))
        args += list(extra)
    return pl.pallas_call(
        functools.partial(_ffn_kernel, rt=rt, layer=layer, e_off=e_off, cps=cps,
                          extra_rows=extra is not None),
        out_shape=out_shape,
        grid=(ne, g),
        in_specs=in_specs + [any_spec, any_spec, any_spec],
        out_specs=out_specs,
        scratch_shapes=[
            pltpu.VMEM((2, 3, f, d), BF16),
            pltpu.VMEM((min(cps, 2), 3, span, d), F32),
            pltpu.SemaphoreType.DMA((min(cps, 2), 3)),
        ],
        compiler_params=pltpu.CompilerParams(dimension_semantics=("arbitrary", "arbitrary"),
                                             vmem_limit_bytes=FFN_VMEM_LIMIT),
        name="expert_ffn",
    )(*args, wg_t, wu_t, wd)


ROUTE_WIN = 64
COMBINE_STRIDE = 2
LANES = 128
VAL_ROWS = 8


def _combine_kernel(st_ref, rank_ref, *refs, tt, ne, cap, merged, nparts, win, stride):
    y_parts = refs[:nparts]
    x_ref, nw_ref, g_ref, spread_ref, o_ref, buf, xbuf, sem, xsem, acc_ref = refs[nparts:]
    b = pl.program_id(0)
    k = pl.program_id(1)
    nk = pl.num_programs(1)
    step = b * nk + k
    slot = step % 2
    rows_total = y_parts[0].shape[2]
    per_part = ne // nparts
    base = b * cap if merged else 0
    lane = lax.broadcasted_iota(jnp.int32, (1, LANES), 1)

    def tile_bounds(bb, kk, e):
        off = bb * cap if merged else 0
        lo = off + st_ref[(bb * (nk * stride + 1) + kk * stride) * ne + e]
        hi = off + st_ref[(bb * (nk * stride + 1) + (kk + 1) * stride) * ne + e]
        return lo, hi, jnp.minimum((lo // 16) * 16, rows_total - win)

    def bounds(e):
        return tile_bounds(b, k, e)

    def rows_of(bb, e, a0, count):
        y_ref = y_parts[e // per_part]
        return y_ref.at[0 if merged else bb, e % per_part, pl.ds(pl.multiple_of(a0, 16), count), :]

    def win_copy(bb, e, a0, sl):
        return pltpu.make_async_copy(rows_of(bb, e, a0, win), buf.at[sl, pl.ds(e * win, win), :],
                                     sem.at[sl, e])

    def start_tile(bb, kk, sl):
        for e in range(ne):
            win_copy(bb, e, tile_bounds(bb, kk, e)[2], sl).start()

    @pl.when(step == 0)
    def _():
        start_tile(b, k, slot)

    nxt = step + 1

    @pl.when(nxt < pl.num_programs(0) * nk)
    def _():
        start_tile(nxt // nk, nxt % nk, 1 - slot)

    rk = rank_ref[0]

    def token_major(rows_f32):
        return jnp.concatenate([rows_f32, jnp.full((LANES - ne, tt), -1.0, F32)], axis=0).T

    sub_e = lax.broadcasted_iota(jnp.int32, (ne, 1), 0)
    origin = jnp.zeros((ne, 1), jnp.int32)
    for e in range(ne):
        origin = jnp.where(sub_e == e, bounds(e)[2], origin)
    pos = rk + (base - origin)
    tgt = jnp.where((rk >= 0) & (pos >= 0) & (pos < win), pos, -1).astype(F32)
    tgt_b = _dot(token_major(tgt).astype(BF16), spread_ref[...])
    lane_pos = lax.broadcasted_iota(jnp.int32, (1, ne * win), 1) % win
    p = (tgt_b == lane_pos.astype(F32)).astype(BF16)
    for e in range(ne):
        win_copy(b, e, bounds(e)[2], slot).wait()
    acc_ref[...] = _dot(p, buf[slot])

    for e in range(ne):
        lo, hi, a0 = bounds(e)

        def extra(w, carry, e=e, a0=a0):
            start = a0 + w * win
            aw = pl.multiple_of(jnp.minimum(start, rows_total - win), 16)

            cp = pltpu.make_async_copy(rows_of(b, e, aw, win), xbuf, xsem)
            cp.start()
            cp.wait()
            col = token_major(rk.astype(F32)).astype(jnp.int32)[:, e:e + 1]
            ok = (col >= 0) & (col + base >= start)
            px = (jnp.where(ok, col + (base - aw), -1) == lane[:, 0:win]).astype(BF16)
            acc_ref[...] += _dot(px, xbuf[...])
            return carry

        lax.fori_loop(1, (hi - a0 + win - 1) // win, extra, 0)

    y = acc_ref[...]
    yn = y * lax.rsqrt(jnp.mean(y * y, axis=-1, keepdims=True) + EPS) * nw_ref[...]
    o_ref[0] = x_ref[0] + g_ref[0] * yn


def combine_post(starts, rank_t, ys, x, norm_w, g, tt, cap, merged):
    ys = tuple(ys) if isinstance(ys, (tuple, list)) else (ys,)
    b, n, d = x.shape
    rows = ys[0].shape[2]
    ne = sum(y.shape[1] for y in ys)
    assert all(y.shape == ys[0].shape for y in ys)
    stride = COMBINE_STRIDE if n % (tt * COMBINE_STRIDE) == 0 and rows >= ROUTE_WIN * COMBINE_STRIDE * 2 else 1
    tt = tt * stride
    win = ROUTE_WIN * stride
    assert win <= LANES
    lanes = np.arange(ne * win) // win
    spread = jnp.asarray(np.arange(LANES)[:, None] == lanes[None, :], BF16)
    grid_spec = pltpu.PrefetchScalarGridSpec(
        num_scalar_prefetch=1,
        grid=(b, n // tt),
        in_specs=[
            pl.BlockSpec((1, ne, tt), lambda bi, k, st: (bi, 0, k)),
            *[pl.BlockSpec(memory_space=pl.ANY)] * len(ys),
            pl.BlockSpec((1, tt, d), lambda bi, k, st: (bi, k, 0)),
            pl.BlockSpec((1, d), lambda bi, k, st: (0, 0)),
            pl.BlockSpec((1, 1, d), lambda bi, k, st: (bi, 0, 0)),
            pl.BlockSpec((LANES, ne * win), lambda bi, k, st: (0, 0)),
        ],
        out_specs=pl.BlockSpec((1, tt, d), lambda bi, k, st: (bi, k, 0)),
        scratch_shapes=[
            pltpu.VMEM((2, ne * win, d), BF16),
            pltpu.VMEM((win, d), BF16),
            pltpu.SemaphoreType.DMA((2, ne)),
            pltpu.SemaphoreType.DMA(()),
            pltpu.VMEM((tt, d), F32),
        ],
    )
    return pl.pallas_call(
        functools.partial(_combine_kernel, tt=tt, ne=ne, cap=cap, merged=merged, nparts=len(ys),
                          win=win, stride=stride),
        out_shape=jax.ShapeDtypeStruct((b, n, d), F32),
        grid_spec=grid_spec,
        compiler_params=_cparams("arbitrary", "arbitrary"),
        name="combine_post",
    )(starts, rank_t, *ys, x, norm_w.reshape(1, d), g, spread)


def _select_kernel(aff_ref, tri_ref, rank_ref, st_ref, *, cap, tt):
    a = aff_ref[0]
    ne, n = a.shape
    bits = lax.bitcast_convert_type(a, jnp.int32)

    def search(_, c):
        lo, hi = c
        mid = lo + ((hi - lo) >> 1)
        cnt = jnp.sum((bits >= mid).astype(F32), axis=1, keepdims=True)
        ge = cnt >= cap
        return jnp.where(ge, mid, lo), jnp.where(ge, hi, mid)

    lo0 = jnp.zeros((ne, 1), jnp.int32)
    hi0 = jnp.full((ne, 1), 0x7F800000, jnp.int32)
    thr, _ = lax.fori_loop(0, 31, search, (lo0, hi0))
    gt = bits > thr
    eq = bits == thr
    need = cap - jnp.sum(gt.astype(F32), axis=1, keepdims=True)
    m = jnp.concatenate([gt, eq], axis=0).astype(BF16)
    tri = tri_ref[...]
    lane = lax.broadcasted_iota(jnp.int32, (1, LANES), 1)
    off = jnp.zeros((2 * ne, 1), F32)
    st = jnp.zeros((ne, LANES), jnp.int32)
    for j in range(n // LANES):
        cs = slice(j * LANES, (j + 1) * LANES)
        if (j * LANES) % tt == 0:
            off_sel = off[:ne] + jnp.minimum(off[ne:], need)
            st = jnp.where(lane == (j * LANES) // tt, off_sel.astype(jnp.int32), st)
        mj = m[:, cs]
        pj = _dot(mj, tri) + off
        pe = pj[ne:]
        sel = gt[:, cs] | (eq[:, cs] & (pe < need))
        rank_ref[0, :, cs] = jnp.where(sel, pj[:ne] + jnp.minimum(pe, need), -1.0).astype(jnp.int32)
        off = off + jnp.sum(mj.astype(F32), axis=1, keepdims=True)
    st_ref[0] = jnp.where(lane == n // tt, cap, st)


def route_select(aff_t, cap, tt):
    b, ne, n = aff_t.shape
    tri = jnp.asarray(np.triu(np.ones((LANES, LANES), np.float32), 1), BF16)
    return pl.pallas_call(
        functools.partial(_select_kernel, cap=cap, tt=tt),
        out_shape=(jax.ShapeDtypeStruct((b, ne, n), jnp.int32),
                   jax.ShapeDtypeStruct((b, ne, LANES), jnp.int32)),
        grid=(b,),
        in_specs=[pl.BlockSpec((1, ne, n), lambda bi: (bi, 0, 0)),
                  pl.BlockSpec((LANES, LANES), lambda bi: (0, 0))],
        out_specs=(pl.BlockSpec((1, ne, n), lambda bi: (bi, 0, 0)),
                   pl.BlockSpec((1, ne, LANES), lambda bi: (bi, 0, 0))),
        compiler_params=_cparams("parallel"),
        name="route_select",
    )(aff_t, tri)


def _compact_kernel(st_ref, rank_ref, aff_ref, idx_ref, gate_ref, out_ref, *, tt, ne, cap, n):
    b = pl.program_id(0)
    k = pl.program_id(1)
    nk = pl.num_programs(1)
    win = ROUTE_WIN

    @pl.when(k == 0)
    def _():
        out_ref[...] = jnp.zeros_like(out_ref)

    rk = rank_ref[0]
    a = aff_ref[0]
    g1 = a.astype(BF16).astype(F32)
    r1 = a - g1
    g2 = r1.astype(BF16).astype(F32)
    g3 = r1 - g2
    tok = k * tt + lax.broadcasted_iota(jnp.int32, (1, tt), 1)
    ids = jnp.concatenate([(tok >> 6).astype(F32), (tok & 63).astype(F32),
                           jnp.zeros((VAL_ROWS - 2, tt), F32)], axis=0)
    pad = jnp.zeros((LANES - VAL_ROWS - 3 * ne, tt), F32)
    payload = jnp.concatenate([ids, g1, g2, g3, pad], axis=0).astype(BF16)
    sub = lax.broadcasted_iota(jnp.int32, (win, 1), 0)

    def bounds(e):
        lo = st_ref[(b * (nk + 1) + k) * ne + e]
        hi = st_ref[(b * (nk + 1) + k + 1) * ne + e]
        return hi, (lo // 8) * 8

    def one_hot(e, aw):
        return ((rk[e:e + 1, :] - aw) == sub).astype(BF16)

    p_all = jnp.concatenate([one_hot(e, bounds(e)[1]) for e in range(ne)], axis=0)
    moved = _dot_nt(p_all, payload)
    for e in range(ne):
        a0 = pl.multiple_of(bounds(e)[1], 8)
        out_ref[e, pl.ds(a0, win), :] += moved[e * win:(e + 1) * win]

    for e in range(ne):
        hi, a0 = bounds(e)

        def window(w, carry, e=e, a0=a0):
            aw = pl.multiple_of(a0 + w * win, 8)
            out_ref[e, pl.ds(aw, win), :] += _dot_nt(one_hot(e, aw), payload)
            return carry

        lax.fori_loop(1, (hi - a0 + win - 1) // win, window, 0)

    @pl.when(k == nk - 1)
    def _():
        for e in range(ne):
            t = out_ref[e].T
            ids_e = (t[0:1] * 64.0 + t[1:2]).astype(jnp.int32) + b * n
            g_e = (t[VAL_ROWS + e:VAL_ROWS + e + 1] + t[VAL_ROWS + ne + e:VAL_ROWS + ne + e + 1]) \
                + t[VAL_ROWS + 2 * ne + e:VAL_ROWS + 2 * ne + e + 1]
            idx_ref[0, e:e + 1, :] = ids_e[:, 0:cap]
            gate_ref[0, e:e + 1, :] = g_e[:, 0:cap]


def route_compact(starts, rank_t, aff_t, cap, tt):
    b, ne, n = aff_t.shape
    rows = -(-(cap + ROUTE_WIN) // LANES) * LANES
    grid_spec = pltpu.PrefetchScalarGridSpec(
        num_scalar_prefetch=1,
        grid=(b, n // tt),
        in_specs=[pl.BlockSpec((1, ne, tt), lambda bi, k, st: (bi, 0, k)),
                  pl.BlockSpec((1, ne, tt), lambda bi, k, st: (bi, 0, k))],
        out_specs=(pl.BlockSpec((1, ne, cap), lambda bi, k, st: (bi, 0, 0)),
                   pl.BlockSpec((1, ne, cap), lambda bi, k, st: (bi, 0, 0))),
        scratch_shapes=[pltpu.VMEM((ne, rows, LANES), F32)],
    )
    return pl.pallas_call(
        functools.partial(_compact_kernel, tt=tt, ne=ne, cap=cap, n=n),
        out_shape=(jax.ShapeDtypeStruct((b, ne, cap), jnp.int32),
                   jax.ShapeDtypeStruct((b, ne, cap), F32)),
        grid_spec=grid_spec,
        compiler_params=_cparams("parallel", "arbitrary"),
        name="route_compact",
    )(starts, rank_t, aff_t)


def _qk_head_of_lane():
    half = QK_W // 2
    return (np.arange(QK_W) % half) // (DK // 2)


def _in_proj_perm():
    half = DK // 2
    first = [h * DK + i for h in range(N_HEADS) for i in range(half)]
    second = [h * DK + half + i for h in range(N_HEADS) for i in range(half)]
    qperm = np.array(first + second)
    rest = np.arange(2 * QK_W, 2 * QK_W + 2 * V_W + POOL_W)
    return np.concatenate([qperm, QK_W + qperm, rest])


def _static_tables():
    qk_head = _qk_head_of_lane()
    v_head = np.arange(V_W) // DV
    hmask = (qk_head[None, :] == np.arange(N_HEADS)[:, None])
    bd = (qk_head[:, None] == v_head[None, :])
    avg = (v_head[:, None] == v_head[None, :]).astype(np.float32) / DV
    return {
        "hmask": jnp.asarray(hmask[:, None, :], BF16),
        "bd_f32": jnp.asarray(bd, F32),
        "bd_bf16": jnp.asarray(bd, BF16),
        "avg": jnp.asarray(avg, BF16),
    }


def _decay_tables(lg_f, lg_b):
    pos = jnp.arange(CHUNK, dtype=F32)
    diff = pos[:, None] - pos[None, :]
    low = diff >= 0
    up = diff < 0
    m_f = jnp.where(low, jnp.exp(lg_f[:, None, None] * jnp.where(low, diff, 0.0)), 0.0)
    m_b = jnp.where(up, jnp.exp(lg_b[:, None, None] * jnp.where(up, -diff, 0.0)), 0.0)
    m = m_f + m_b
    mpair = m.reshape(N_HEADS // 2, 2, CHUNK, CHUNK).transpose(0, 2, 1, 3).reshape(
        N_HEADS // 2, CHUNK, 2 * CHUNK)
    qk_head = _qk_head_of_lane()
    v_head = np.arange(V_W) // DV
    qdec_f = jnp.exp(lg_f[None, :] * (pos[:, None] + 1.0))[:, v_head]
    qdec_b = jnp.exp(lg_b[None, :] * (CHUNK - pos[:, None]))[:, v_head]
    kdec_f = jnp.exp(lg_f[None, :] * (CHUNK - 1.0 - pos[:, None]))[:, qk_head]
    kdec_b = jnp.exp(lg_b[None, :] * pos[:, None])[:, qk_head]
    cdec_f = jnp.exp(lg_f * CHUNK)[None, v_head]
    cdec_b = jnp.exp(lg_b * CHUNK)[None, v_head]
    return {"mpair": mpair, "qdec_f": qdec_f, "qdec_b": qdec_b, "kdec_f": kdec_f,
            "kdec_b": kdec_b, "cdec_f": cdec_f, "cdec_b": cdec_b}


def _rope_tables(n):
    t = jnp.arange(n)
    row = (t // GRID_W).astype(F32)
    col = (t % GRID_W).astype(F32)
    n_freq = DK // 4
    inv = ROPE_BASE ** (-jnp.arange(n_freq, dtype=F32) / n_freq)
    ang = jnp.concatenate([row[:, None] * inv, col[:, None] * inv], axis=-1)
    return jnp.tile(jnp.cos(ang), (1, N_HEADS)), jnp.tile(jnp.sin(ang), (1, N_HEADS))


EXPERT_GROUPS = 2


def _route(h_pk, aff_t, merged):
    b, n, d = h_pk.shape
    cap = EC_FACTOR * n // N_EXPERTS
    ne = N_EXPERTS
    per = ne // EXPERT_GROUPS
    tt = min(n, 256)
    nk = n // tt
    rank_t, st = route_select(aff_t, cap, tt)
    starts = st[:, :, :nk + 1].transpose(0, 2, 1).reshape(-1)
    flat, gate = route_compact(starts, rank_t, aff_t, cap, tt)
    table = h_pk.reshape(b * n, d)
    xs_list, gate_list = [], []
    for i in range(EXPERT_GROUPS):
        fl, gt = flat[:, i * per:(i + 1) * per], gate[:, i * per:(i + 1) * per]
        if merged:
            fl = fl.transpose(1, 0, 2)
            gt = gt.transpose(1, 0, 2).reshape(1, per, b * cap)
        xs_list.append(gather_rows(table, fl.reshape(-1)).reshape(gt.shape + (d,)))
        gate_list.append(gt[:, :, None, :])
    return xs_list, gate_list, (starts, rank_t, tt, cap, merged)


def _combine(x, y, info, norm_w, g):
    starts, rank_t, tt, cap, merged = info
    return combine_post(starts, rank_t, y, x, norm_w, g, tt, cap, merged)


def kernel(x, c, ctx, c_ctx, w_ada, b_ada, norm_pre_mix, norm_post_mix, norm_pre_ffn, norm_post_ffn, w_in, ret_decay_fwd, ret_decay_bwd, ret_gn, pool_w, pool_scale, w_out, w_router, w_gate, w_up, w_down):
    b, n, d = x.shape
    lc = ctx.shape[1]
    depth = w_ada.shape[0]
    rope = _rope_tables(n)
    static = _static_tables()
    perm = _in_proj_perm()

    cc = jnp.concatenate([c, c_ctx[None, :], jnp.zeros((7, d), F32)], axis=0)
    mods = ada_modulation(cc, w_ada, b_ada)
    wg, wu, wd = jnp.swapaxes(w_gate, 2, 3), jnp.swapaxes(w_up, 2, 3), w_down

    tm = 1024
    for l in range(depth):
        last = l == depth - 1
        mx = mods[l, :b].reshape(b, 1, 6, d)
        sh1, sc1, g1, sh2, sc2, g2 = [mx[:, :, i] for i in range(6)]
        mc = jnp.broadcast_to(mods[l, b].reshape(1, 1, 6, d), (b, 1, 6, d))
        csh1, csc1, cg1, csh2, csc2, cg2 = [mc[:, :, i] for i in range(6)]
        lg_f = jax.nn.log_sigmoid(ret_decay_fwd[l].astype(F32))
        lg_b = jax.nn.log_sigmoid(ret_decay_bwd[l].astype(F32))
        tabs = dict(static, **_decay_tables(lg_f, lg_b))
        w_in_p = w_in[l][:, perm].astype(BF16)
        w_out_bf = w_out[l].astype(BF16)
        pool_w_bf = pool_w[l].astype(BF16)

        qk_c, v_c, gate_c, p_c = premix(ctx, norm_pre_mix[l], csh1, csc1, w_in_p, None, lc)
        zero = jnp.zeros((b, DK, V_W), F32)
        sf_c, sb_c, s_f, s_b = state_scan(qk_c, v_c, zero, zero, tabs)
        qk_x, v_x, gate_x, p_x = premix(x, norm_pre_mix[l], sh1, sc1, w_in_p, rope, tm)
        sf_x, sb_x, _, _ = state_scan(qk_x, v_x, s_f, s_b, tabs)
        pool_x = pool_mixer(p_x, pool_w_bf, pool_scale[l], GRID_W, min(n, 4096))
        x = retention_mixer(qk_x, v_x, gate_x, pool_x, x, sf_x, sb_x, tabs, ret_gn[l], w_out_bf,
                            norm_post_mix[l], g1, tm)
        if not last:
            pool_c = pool_mixer(p_c, pool_w_bf, pool_scale[l], lc, lc)
            ctx = retention_mixer(qk_c, v_c, gate_c, pool_c, ctx, sf_c, sb_c, tabs, ret_gn[l],
                                  w_out_bf, norm_post_mix[l], cg1, lc)

        wr_t = w_router[l].T
        h_x, aff_x = router(x, norm_pre_ffn[l], sh2, sc2, wr_t, tm)
        per = N_EXPERTS // EXPERT_GROUPS
        xs_x, gate_x2, info_x = _route(h_x, aff_x, False)
        if last:
            y_x = [expert_ffn(xs_x[i], gate_x2[i], wg, wu, wd, l, e_off=i * per)
                   for i in range(EXPERT_GROUPS)]
        else:
            h_c, aff_c = router(ctx, norm_pre_ffn[l], csh2, csc2, wr_t, lc)
            xs_c, gate_c2, info_c = _route(h_c, aff_c, True)
            pairs = [expert_ffn(xs_x[i], gate_x2[i], wg, wu, wd, l, e_off=i * per,
                                extra=(xs_c[i], gate_c2[i])) for i in range(EXPERT_GROUPS)]
            y_x = [p[0] for p in pairs]
            ctx = _combine(ctx, [p[1] for p in pairs], info_c, norm_post_ffn[l], cg2)
        x = _combine(x, y_x, info_x, norm_post_ffn[l], g2)
    return x
```

```python
import functools

import jax
import jax.numpy as jnp
import numpy as np
from jax import lax
from jax.experimental import pallas as pl
from jax.experimental.pallas import tpu as pltpu
from jax.experimental.pallas import tpu_sc as plsc

F32 = jnp.float32
BF16 = jnp.bfloat16

D_MODEL = 1024
GRID_W = 64
N_HEADS = 8
DV = 64
DK = 32
QK_W = N_HEADS * DK
V_W = N_HEADS * DV
POOL_W = 512
POOL_WINDOWS = (2, 4, 8, 16)
POOL_GROUP_DIM = 128
CHUNK = 128
ROPE_BASE = 10000.0
N_EXPERTS = 16
EC_FACTOR = 2
EPS = 1e-6
MAX_HALF_WINDOW = max(POOL_WINDOWS) // 2

VMEM_LIMIT = 56 * 1024 * 1024


def _cparams(*sem):
    return pltpu.CompilerParams(dimension_semantics=sem, vmem_limit_bytes=VMEM_LIMIT)


def _dot(a, b):
    return jnp.dot(a, b, preferred_element_type=F32)


def _dot_nt(a, b, precision=None):
    return lax.dot_general(a, b, (((1,), (1,)), ((), ())), precision=precision,
                           preferred_element_type=F32)


def _ada_kernel(cc_ref, w_ref, b_ref, o_ref):
    s = cc_ref[...]
    s = s * jax.nn.sigmoid(s)
    o_ref[0] = _dot(s.astype(BF16), w_ref[0].astype(BF16)) + b_ref[0]


def ada_modulation(cc, w_ada, b_ada):
    depth, d, d6 = w_ada.shape
    rows = cc.shape[0]
    tn = 1536
    return pl.pallas_call(
        _ada_kernel,
        out_shape=jax.ShapeDtypeStruct((depth, rows, d6), F32),
        grid=(depth, d6 // tn),
        in_specs=[
            pl.BlockSpec((rows, d), lambda l, j: (0, 0)),
            pl.BlockSpec((1, d, tn), lambda l, j: (l, 0, j)),
            pl.BlockSpec((1, 1, tn), lambda l, j: (l, 0, j)),
        ],
        out_specs=pl.BlockSpec((1, rows, tn), lambda l, j: (l, 0, j)),
        compiler_params=_cparams("parallel", "parallel"),
        name="ada_modulation",
    )(cc, w_ada, b_ada.reshape(depth, 1, d6))


def _norm_mod(xf, g, sh, sc):
    y = xf * lax.rsqrt(jnp.mean(xf * xf, axis=-1, keepdims=True) + EPS)
    return (y * g) * (1.0 + sc) + sh


def _premix_kernel(x_ref, g_ref, sh_ref, sc_ref, w_ref, *rest, rope):
    if rope:
        cos_ref, sin_ref, qk_ref, v_ref, gate_ref, p_ref = rest
    else:
        qk_ref, v_ref, gate_ref, p_ref = rest
    hb = _norm_mod(x_ref[0], g_ref[...], sh_ref[0], sc_ref[0]).astype(BF16)
    zqk = _dot(hb, w_ref[:, 0:2 * QK_W])
    half = QK_W // 2
    q1, q2 = zqk[:, 0:half], zqk[:, half:2 * half]
    k1 = zqk[:, 2 * half:3 * half] * (DK ** -0.5)
    k2 = zqk[:, 3 * half:4 * half] * (DK ** -0.5)
    if rope:
        cos, sin = cos_ref[...], sin_ref[...]
        q1, q2 = q1 * cos - q2 * sin, q1 * sin + q2 * cos
        k1, k2 = k1 * cos - k2 * sin, k1 * sin + k2 * cos
    qk_ref[0] = jnp.concatenate([q1, q2, k1, k2], axis=1).astype(BF16)
    o = 2 * QK_W
    v_ref[0] = _dot(hb, w_ref[:, o:o + V_W]).astype(BF16)
    gate_ref[0] = _dot(hb, w_ref[:, o + V_W:o + 2 * V_W])
    p_ref[0] = _dot(hb, w_ref[:, o + 2 * V_W:o + 2 * V_W + POOL_W])


def premix(x, norm_w, shift, scale, w_in_p, rope, tm):
    b, n, d = x.shape
    in_w = w_in_p.shape[1]
    row = lambda bi, i: (bi, i, 0)
    in_specs = [
        pl.BlockSpec((1, tm, d), row),
        pl.BlockSpec((1, d), lambda bi, i: (0, 0)),
        pl.BlockSpec((1, 1, d), lambda bi, i: (bi, 0, 0)),
        pl.BlockSpec((1, 1, d), lambda bi, i: (bi, 0, 0)),
        pl.BlockSpec((d, in_w), lambda bi, i: (0, 0)),
    ]
    args = [x, norm_w.reshape(1, d), shift, scale, w_in_p]
    if rope is not None:
        half = QK_W // 2
        in_specs += [pl.BlockSpec((tm, half), lambda bi, i: (i, 0))] * 2
        args += list(rope)
    return pl.pallas_call(
        functools.partial(_premix_kernel, rope=rope is not None),
        out_shape=(
            jax.ShapeDtypeStruct((b, n, 2 * QK_W), BF16),
            jax.ShapeDtypeStruct((b, n, V_W), BF16),
            jax.ShapeDtypeStruct((b, n, V_W), F32),
            jax.ShapeDtypeStruct((b, n, POOL_W), F32),
        ),
        grid=(b, n // tm),
        in_specs=in_specs,
        out_specs=(
            pl.BlockSpec((1, tm, 2 * QK_W), row),
            pl.BlockSpec((1, tm, V_W), row),
            pl.BlockSpec((1, tm, V_W), row),
            pl.BlockSpec((1, tm, POOL_W), row),
        ),
        compiler_params=_cparams("parallel", "parallel"),
        name="premix",
    )(*args)


def _fold_state(s):
    g = 2 * N_HEADS
    top = s[0:g]
    bot = s[QK_W // 2:QK_W // 2 + g]
    for h in range(1, N_HEADS):
        top = top + s[h * g:(h + 1) * g]
        bot = bot + s[QK_W // 2 + h * g:QK_W // 2 + (h + 1) * g]
    return jnp.concatenate([top, bot], axis=0)


def _expand_state(c):
    g = 2 * N_HEADS
    return jnp.concatenate([c[0:g]] * N_HEADS + [c[g:2 * g]] * N_HEADS, axis=0)


def _state_kernel(kf_ref, vf_ref, kb_ref, vb_ref, s0f_ref, s0b_ref, kdf_ref, kdb_ref,
                  cdf_ref, cdb_ref, bd_ref, sf_out, sb_out, ff_out, fb_out, sf_acc, sb_acc, *, cps):
    c = pl.program_id(1)
    nc = pl.num_programs(1)
    bd = bd_ref[...]

    @pl.when(c == 0)
    def _():
        sf_acc[...] = _expand_state(s0f_ref[0]) * bd
        sb_acc[...] = _expand_state(s0b_ref[0]) * bd

    def update(s, k, v, kd_ref, cd_ref):
        kd = k.astype(F32) * kd_ref[...]
        f = _dot(kd.T.astype(BF16), v)
        return s * cd_ref[...] + f * bd

    sf = sf_acc[...]
    sb = sb_acc[...]
    for i in range(cps):
        j = cps - 1 - i
        fs = slice(i * CHUNK, (i + 1) * CHUNK)
        bs = slice(j * CHUNK, (j + 1) * CHUNK)
        sf_out[0, i] = _fold_state(sf).astype(BF16)
        sb_out[0, j] = _fold_state(sb).astype(BF16)
        sf = update(sf, kf_ref[0, fs, :], vf_ref[0, fs, :], kdf_ref, cdf_ref)
        sb = update(sb, kb_ref[0, bs, :], vb_ref[0, bs, :], kdb_ref, cdb_ref)
    sf_acc[...] = sf
    sb_acc[...] = sb

    @pl.when(c == nc - 1)
    def _():
        ff_out[0] = _fold_state(sf)
        fb_out[0] = _fold_state(sb)


def state_scan(qk, v, s0f, s0b, tabs):
    b, n, _ = qk.shape
    nc = n // CHUNK
    cps = min(nc, 16)
    ns = nc // cps
    rows = cps * CHUNK
    const = lambda shape: pl.BlockSpec(shape, lambda bi, c: (0,) * len(shape))
    return pl.pallas_call(
        functools.partial(_state_kernel, cps=cps),
        out_shape=(
            jax.ShapeDtypeStruct((b, nc, DK, V_W), BF16),
            jax.ShapeDtypeStruct((b, nc, DK, V_W), BF16),
            jax.ShapeDtypeStruct((b, DK, V_W), F32),
            jax.ShapeDtypeStruct((b, DK, V_W), F32),
        ),
        grid=(b, ns),
        in_specs=[
            pl.BlockSpec((1, rows, QK_W), lambda bi, c: (bi, c, 1)),
            pl.BlockSpec((1, rows, V_W), lambda bi, c: (bi, c, 0)),
            pl.BlockSpec((1, rows, QK_W), lambda bi, c: (bi, ns - 1 - c, 1)),
            pl.BlockSpec((1, rows, V_W), lambda bi, c: (bi, ns - 1 - c, 0)),
            pl.BlockSpec((1, DK, V_W), lambda bi, c: (bi, 0, 0)),
            pl.BlockSpec((1, DK, V_W), lambda bi, c: (bi, 0, 0)),
            const((CHUNK, QK_W)), const((CHUNK, QK_W)),
            const((1, V_W)), const((1, V_W)),
            const((QK_W, V_W)),
        ],
        out_specs=(
            pl.BlockSpec((1, cps, DK, V_W), lambda bi, c: (bi, c, 0, 0)),
            pl.BlockSpec((1, cps, DK, V_W), lambda bi, c: (bi, ns - 1 - c, 0, 0)),
            pl.BlockSpec((1, DK, V_W), lambda bi, c: (bi, 0, 0)),
            pl.BlockSpec((1, DK, V_W), lambda bi, c: (bi, 0, 0)),
        ),
        scratch_shapes=[pltpu.VMEM((QK_W, V_W), F32), pltpu.VMEM((QK_W, V_W), F32)],
        compiler_params=_cparams("parallel", "arbitrary"),
        name="state_scan",
    )(qk, v, qk, v, s0f, s0b, tabs["kdec_f"], tabs["kdec_b"], tabs["cdec_f"], tabs["cdec_b"],
      tabs["bd_f32"])


def _pool_kernel(*refs, gw, tile, halo, rows_total):
    if halo:
        prev_ref, cur_ref, next_ref, pw_ref, ps_ref, o_ref, buf = refs
    else:
        cur_ref, pw_ref, ps_ref, o_ref, buf = refs
    i = pl.program_id(1)
    last = pl.num_programs(1) - 1
    m = MAX_HALF_WINDOW
    span = tile + 2 * halo
    zeros_m = jnp.zeros((m, POOL_W), F32)
    buf[0:m] = zeros_m
    buf[m + span:2 * m + span] = zeros_m
    if halo:
        buf[m:m + halo] = jnp.where(i > 0, prev_ref[0], 0.0)
        buf[m + halo + tile:m + span] = jnp.where(i < last, next_ref[0], 0.0)
    buf[m + halo:m + halo + tile] = cur_ref[0]

    pos = lax.broadcasted_iota(jnp.int32, (span, 1), 0)
    col = pos % gw
    tpos = lax.broadcasted_iota(jnp.int32, (tile, 1), 0)
    tcol = tpos % gw
    trow = i * (tile // gw) + tpos // gw
    for gi, w in enumerate(POOL_WINDOWS):
        cs = slice(gi * POOL_GROUP_DIM, (gi + 1) * POOL_GROUP_DIM)
        hw = w // 2
        s = None
        for d in range(-hw, hw):
            valid = (col + d >= 0) & (col + d < gw)
            term = jnp.where(valid, buf[m + d:m + d + span, cs], 0.0)
            s = term if s is None else s + term
        if halo:
            acc = None
            for d in range(-hw, hw):
                start = halo + d * gw
                term = s[start:start + tile]
                acc = term if acc is None else acc + term
            cnt_r = jnp.minimum(trow + hw, rows_total) - jnp.maximum(trow - hw, 0)
        else:
            acc = s
            cnt_r = 1
        cnt_c = jnp.minimum(tcol + hw, gw) - jnp.maximum(tcol - hw, 0)
        cnt = (cnt_r * cnt_c).astype(F32)
        xg = buf[m + halo:m + halo + tile, cs]
        diff = (acc / cnt - xg).astype(BF16)
        y = _dot(diff, pw_ref[gi]) * ps_ref[:, cs]
        o_ref[0, :, cs] = y.astype(BF16)


def pool_mixer(p, pool_w_bf, pool_scale, gw, tile):
    b, n, pw = p.shape
    rows_total = n // gw
    halo = MAX_HALF_WINDOW * gw if rows_total > 1 else 0
    in_specs, args = [], []
    if halo:
        r = tile // halo
        nh = n // halo
        in_specs = [
            pl.BlockSpec((1, halo, pw), lambda bi, i: (bi, jnp.maximum(i * r - 1, 0), 0)),
            pl.BlockSpec((1, tile, pw), lambda bi, i: (bi, i, 0)),
            pl.BlockSpec((1, halo, pw), lambda bi, i: (bi, jnp.minimum((i + 1) * r, nh - 1), 0)),
        ]
        args = [p, p, p]
    else:
        in_specs = [pl.BlockSpec((1, tile, pw), lambda bi, i: (bi, i, 0))]
        args = [p]
    ng = len(POOL_WINDOWS)
    in_specs += [
        pl.BlockSpec((ng, POOL_GROUP_DIM, POOL_GROUP_DIM), lambda bi, i: (0, 0, 0)),
        pl.BlockSpec((1, pw), lambda bi, i: (0, 0)),
    ]
    args += [pool_w_bf, pool_scale.reshape(1, pw)]
    return pl.pallas_call(
        functools.partial(_pool_kernel, gw=gw, tile=tile, halo=halo, rows_total=rows_total),
        out_shape=jax.ShapeDtypeStruct((b, n, pw), BF16),
        grid=(b, n // tile),
        in_specs=in_specs,
        out_specs=pl.BlockSpec((1, tile, pw), lambda bi, i: (bi, i, 0)),
        scratch_shapes=[pltpu.VMEM((tile + 2 * halo + 2 * MAX_HALF_WINDOW, pw), F32)],
        compiler_params=_cparams("parallel", "parallel"),
        name="pool_mixer",
    )(*args)


def _ret_kernel(qk_ref, v_ref, gate_ref, py_ref, x_ref, sf_ref, sb_ref, mp_ref, qdf_ref, qdb_ref,
                hm_ref, bd_ref, avg_ref, gn_ref, wo_ref, nw_ref, g1_ref, o_ref, o_buf, *, nch):
    bd = bd_ref[...]
    avg = avg_ref[...]
    lane = lax.broadcasted_iota(jnp.int32, (1, 2 * DV), 1)
    lo_mask = (lane < DV).astype(BF16)
    hi_mask = (lane >= DV).astype(BF16)
    for ci in range(nch):
        rs = slice(ci * CHUNK, (ci + 1) * CHUNK)
        q = qk_ref[0, rs, 0:QK_W]
        k = qk_ref[0, rs, QK_W:2 * QK_W]
        vv = v_ref[0, rs, :]
        inner = []
        for j in range(N_HEADS // 2):
            kp = jnp.concatenate([k * hm_ref[2 * j], k * hm_ref[2 * j + 1]], axis=0)
            sc = _dot_nt(q, kp) * mp_ref[j]
            vpair = vv[:, 2 * DV * j:2 * DV * (j + 1)]
            vp = jnp.concatenate([vpair * lo_mask, vpair * hi_mask], axis=0)
            inner.append(_dot(sc.astype(BF16), vp))
        o = jnp.concatenate(inner, axis=1)
        sf_bd = _expand_state(sf_ref[0, ci]) * bd
        sb_bd = _expand_state(sb_ref[0, ci]) * bd
        o_buf[rs, :] = o + _dot(q, sf_bd) * qdf_ref[...] + _dot(q, sb_bd) * qdb_ref[...]

    def head_mean(a):
        hi = a.astype(BF16)
        lo = (a - hi.astype(F32)).astype(BF16)
        return _dot(hi, avg) + _dot(lo, avg)

    o = o_buf[...]
    dlt = o - head_mean(o)
    var = head_mean(dlt * dlt)
    yn = dlt * lax.rsqrt(var + EPS) * gn_ref[...]
    g = gate_ref[0]
    r = (g * jax.nn.sigmoid(g) * yn).astype(BF16)
    mix = _dot(r, wo_ref[0:V_W, :]) + _dot(py_ref[0], wo_ref[V_W:V_W + POOL_W, :])
    y = mix * lax.rsqrt(jnp.mean(mix * mix, axis=-1, keepdims=True) + EPS) * nw_ref[...]
    o_ref[0] = x_ref[0] + g1_ref[0] * y


def retention_mixer(qk, v, gate, pool_y, x, sf, sb, tabs, gn, w_out_bf, norm_w, g1, tm):
    b, n, d = x.shape
    nch = tm // CHUNK
    row = lambda bi, i: (bi, i, 0)
    const = lambda shape: pl.BlockSpec(shape, lambda bi, i: (0,) * len(shape))
    return pl.pallas_call(
        functools.partial(_ret_kernel, nch=nch),
        out_shape=jax.ShapeDtypeStruct((b, n, d), F32),
        grid=(b, n // tm),
        in_specs=[
            pl.BlockSpec((1, tm, 2 * QK_W), row),
            pl.BlockSpec((1, tm, V_W), row),
            pl.BlockSpec((1, tm, V_W), row),
            pl.BlockSpec((1, tm, POOL_W), row),
            pl.BlockSpec((1, tm, d), row),
            pl.BlockSpec((1, nch, DK, V_W), lambda bi, i: (bi, i, 0, 0)),
            pl.BlockSpec((1, nch, DK, V_W), lambda bi, i: (bi, i, 0, 0)),
            const((N_HEADS // 2, CHUNK, 2 * CHUNK)),
            const((CHUNK, V_W)), const((CHUNK, V_W)),
            const((N_HEADS, 1, QK_W)),
            const((QK_W, V_W)),
            const((V_W, V_W)),
            const((1, V_W)),
            const((V_W + POOL_W, d)),
            const((1, d)),
            pl.BlockSpec((1, 1, d), lambda bi, i: (bi, 0, 0)),
        ],
        out_specs=pl.BlockSpec((1, tm, d), row),
        scratch_shapes=[pltpu.VMEM((tm, V_W), F32)],
        compiler_params=_cparams("parallel", "parallel"),
        name="retention_mixer",
    )(qk, v, gate, pool_y, x, sf, sb, tabs["mpair"], tabs["qdec_f"], tabs["qdec_b"], tabs["hmask"],
      tabs["bd_bf16"], tabs["avg"], gn.reshape(1, V_W), w_out_bf, norm_w.reshape(1, d), g1)


def _packed_width(d, dtype):
    return d * jnp.dtype(dtype).itemsize // 4


def _pack_rows(h):
    bits = lax.bitcast_convert_type(h.astype(F32), jnp.uint32)
    if h.dtype.itemsize == 4:
        return bits
    half = h.shape[1] // 2
    return (bits[:, half:] & jnp.uint32(0xFFFF0000)) | (bits[:, :half] >> 16)


def _unpack_rows(w, dtype):
    if jnp.dtype(dtype).itemsize == 4:
        return lax.bitcast_convert_type(w, dtype)
    lo = lax.bitcast_convert_type(w << 16, F32)
    hi = lax.bitcast_convert_type(w & jnp.uint32(0xFFFF0000), F32)
    return jnp.concatenate([lo, hi], axis=1).astype(dtype)


def _router_kernel(x_ref, g_ref, sh_ref, sc_ref, wr_ref, h_ref, aff_ref):
    h = _norm_mod(x_ref[0], g_ref[...], sh_ref[0], sc_ref[0])
    h_hi = h.astype(BF16)
    h_ref[0] = _pack_rows(h_hi)
    ne = wr_ref.shape[0]
    h_lo = (h - h_hi.astype(F32)).astype(BF16)
    w = wr_ref[...]
    w_hi = w.astype(BF16)
    w_lo = (w - w_hi.astype(F32)).astype(BF16)
    both = _dot_nt(jnp.concatenate([w_hi, w_lo], axis=0), h_hi)
    logits = both[0:ne] + both[ne:2 * ne] + _dot_nt(w_hi, h_lo)
    mx = jnp.max(logits, axis=0, keepdims=True)
    e = jnp.exp(logits - mx)
    aff_ref[0] = e / jnp.sum(e, axis=0, keepdims=True)


def router(x, norm_w, shift, scale, w_router_t, tm):
    b, n, d = x.shape
    ne = w_router_t.shape[0]
    return pl.pallas_call(
        _router_kernel,
        out_shape=(jax.ShapeDtypeStruct((b, n, _packed_width(d, BF16)), jnp.uint32),
                   jax.ShapeDtypeStruct((b, ne, n), F32)),
        grid=(b, n // tm),
        in_specs=[
            pl.BlockSpec((1, tm, d), lambda bi, i: (bi, i, 0)),
            pl.BlockSpec((1, d), lambda bi, i: (0, 0)),
            pl.BlockSpec((1, 1, d), lambda bi, i: (bi, 0, 0)),
            pl.BlockSpec((1, 1, d), lambda bi, i: (bi, 0, 0)),
            pl.BlockSpec((ne, d), lambda bi, i: (0, 0)),
        ],
        out_specs=(
            pl.BlockSpec((1, tm, _packed_width(d, BF16)), lambda bi, i: (bi, i, 0)),
            pl.BlockSpec((1, ne, tm), lambda bi, i: (bi, 0, i)),
        ),
        compiler_params=_cparams("parallel", "parallel"),
        name="router",
    )(x, norm_w.reshape(1, d), shift, scale, w_router_t)


SC_CORES = 2
SC_SUBCORES = 16
GATHER_CHUNK = 64


def gather_rows(table, idx):
    rows, width = idx.shape[0], table.shape[1]
    workers = SC_CORES * SC_SUBCORES
    per_worker = rows // workers
    assert rows == per_worker * workers and per_worker % GATHER_CHUNK == 0
    mesh = plsc.VectorSubcoreMesh(core_axis_name="c", subcore_axis_name="s")

    def body(table_hbm, idx_hbm, out_hbm, idx_v, rows_v, sem):
        wid = lax.axis_index("s") * SC_CORES + lax.axis_index("c")
        base = wid * per_worker

        @pl.loop(0, per_worker // GATHER_CHUNK)
        def _(i):
            off = pl.multiple_of(base + i * GATHER_CHUNK, GATHER_CHUNK)
            pltpu.sync_copy(idx_hbm.at[pl.ds(off, GATHER_CHUNK)], idx_v)
            pltpu.async_copy(table_hbm.at[idx_v], rows_v, sem).wait()
            pltpu.sync_copy(rows_v, out_hbm.at[pl.ds(off, GATHER_CHUNK)])

    return pl.kernel(
        body,
        out_type=jax.ShapeDtypeStruct((rows, width), table.dtype),
        mesh=mesh,
        scratch_types=[
            pltpu.VMEM((GATHER_CHUNK,), jnp.int32),
            pltpu.VMEM((GATHER_CHUNK, width), table.dtype),
            pltpu.SemaphoreType.DMA,
        ],
        name="gather_rows",
    )(table, idx)


FFN_CHUNKS = 8
FFN_VMEM_LIMIT = 60 * 1024 * 1024


def _ffn_spans(f):
    chunk = f // FFN_CHUNKS
    assert chunk * FFN_CHUNKS == f
    span = max(-(-((c + 1) * chunk - c * chunk // 16 * 16) // 16) * 16 for c in range(FFN_CHUNKS))
    offs = [min(c * chunk // 16 * 16, f - span) for c in range(FFN_CHUNKS)]
    assert all(o % 16 == 0 and o <= c * chunk and o + span >= (c + 1) * chunk
               for c, o in enumerate(offs))
    return chunk, span


def _ffn_kernel(*refs, rt, layer, e_off, cps, extra_rows):
    if extra_rows:
        (x_ref, gate_ref, xc_ref, gc_ref, wg_hbm, wu_hbm, wd_hbm, o_ref, oc_ref, wbf, stage, sem) = refs
    else:
        x_ref, gate_ref, wg_hbm, wu_hbm, wd_hbm, o_ref, wbf, stage, sem = refs
    e = pl.program_id(0)
    g = pl.program_id(1)
    ne = pl.num_programs(0)
    slot = e % 2
    nxt = jnp.minimum(e + 1, ne - 1)
    mats = (wg_hbm, wu_hbm, wd_hbm)
    chunk, span = _ffn_spans(wbf.shape[2])

    def offset(c):
        last = wbf.shape[2] - span
        if isinstance(c, int):
            return min(c * chunk // 16 * 16, last)
        return pl.multiple_of(jnp.minimum(c * chunk // 16 * 16, last), 16)

    def copies(ee, c, st):
        return [pltpu.make_async_copy(m.at[layer, e_off + ee, pl.ds(offset(c), span), :],
                                      stage.at[st, i], sem.at[st, i]) for i, m in enumerate(mats)]

    def start(ee, c, st):
        for cp in copies(ee, c, st):
            cp.start()

    def wait(ee, c, st):
        for cp in copies(ee, c, st):
            cp.wait()

    def cast(c, st, i, sl):
        wbf[sl, i, pl.ds(offset(c), span), :] = stage[st, i].astype(wbf.dtype)

    @pl.when((e == 0) & (g == 0))
    def _():
        for c in range(FFN_CHUNKS):
            start(0, c, 0)
            wait(0, c, 0)
            for i in range(3):
                cast(c, 0, i, 0)

    def swiglu(words, gate_row):
        r = words.shape[0]
        xt = _unpack_rows(words, wbf.dtype)
        a = _dot_nt(xt, wbf[slot, 0])
        u = _dot_nt(xt, wbf[slot, 1])
        hh = (a * jax.nn.sigmoid(a) * u).astype(wbf.dtype)
        gate_col = jnp.broadcast_to(gate_row, (LANES, r)).T[:, 0:1]
        return (_dot(hh, wbf[slot, 2]) * gate_col).astype(o_ref.dtype)

    def compute_tile(t):
        rs = slice(t * rt, (t + 1) * rt)
        o_ref[0, 0, rs, :] = swiglu(x_ref[0, 0, rs, :], gate_ref[0, 0, :, rs])

    if extra_rows:
        @pl.when(g == pl.num_programs(1) - 1)
        def _():
            oc_ref[0, 0] = swiglu(xc_ref[0, 0], gc_ref[0, 0])

    c0 = g * cps
    ntiles = x_ref.shape[2] // rt
    start(nxt, c0, 0)
    for t in range(ntiles):
        compute_tile(t)
        if 1 <= t <= cps:
            for i in range(3):
                cast(c0 + t - 1, (t - 1) % 2, i, 1 - slot)
        if t < cps:
            wait(nxt, c0 + t, t % 2)
            if t + 1 < cps:
                start(nxt, c0 + t + 1, (t + 1) % 2)
    if cps == ntiles:
        for i in range(3):
            cast(c0 + cps - 1, (cps - 1) % 2, i, 1 - slot)


def expert_ffn(xs, gate, wg_t, wu_t, wd, layer, e_off=0, extra=None):
    g, ne, rows, dp = xs.shape
    f, d = wd.shape[2], wd.shape[3]
    assert FFN_CHUNKS % g == 0
    cps = FFN_CHUNKS // g
    rt = min(rows, 256, rows // cps)
    assert rows % rt == 0 and rt % 16 == 0
    span = _ffn_spans(f)[1]
    any_spec = pl.BlockSpec(memory_space=pl.ANY)
    in_specs = [pl.BlockSpec((1, 1, rows, dp), lambda e, gi: (gi, e, 0, 0)),
                pl.BlockSpec((1, 1, 1, rows), lambda e, gi: (gi, e, 0, 0))]
    out_specs = pl.BlockSpec((1, 1, rows, d), lambda e, gi: (gi, e, 0, 0))
    out_shape = jax.ShapeDtypeStruct((g, ne, rows, d), BF16)
    args = [xs, gate]
    if extra is not None:
        rc = extra[0].shape[2]
        in_specs += [pl.BlockSpec((1, 1, rc, dp), lambda e, gi: (0, e, 0, 0)),
                     pl.BlockSpec((1, 1, 1, rc), lambda e, gi: (0, e, 0, 0))]
        out_specs = (out_specs, pl.BlockSpec((1, 1, rc, d), lambda e, gi: (0, e, 0, 0)))
        out_shape = (out_shape, jax.ShapeDtypeStruct((1, ne, rc, d), BF16))
        args += list(extra)
    return pl.pallas_call(
        functools.partial(_ffn_kernel, rt=rt, layer=layer, e_off=e_off, cps=cps,
                          extra_rows=extra is not None),
        out_shape=out_shape,
        grid=(ne, g),
        in_specs=in_specs + [any_spec, any_spec, any_spec],
        out_specs=out_specs,
        scratch_shapes=[
            pltpu.VMEM((2, 3, f, d), BF16),
            pltpu.VMEM((min(cps, 2), 3, span, d), F32),
            pltpu.SemaphoreType.DMA((min(cps, 2), 3)),
        ],
        compiler_params=pltpu.CompilerParams(dimension_semantics=("arbitrary", "arbitrary"),
                                             vmem_limit_bytes=FFN_VMEM_LIMIT),
        name="expert_ffn",
    )(*args, wg_t, wu_t, wd)


ROUTE_WIN = 64
COMBINE_STRIDE = 2
LANES = 128
VAL_ROWS = 8


def _combine_kernel(st_ref, rank_ref, *refs, tt, ne, cap, merged, nparts, win, stride):
    y_parts = refs[:nparts]
    x_ref, nw_ref, g_ref, spread_ref, o_ref, buf, xbuf, sem, xsem, acc_ref = refs[nparts:]
    b = pl.program_id(0)
    k = pl.program_id(1)
    nk = pl.num_programs(1)
    step = b * nk + k
    slot = step % 2
    rows_total = y_parts[0].shape[2]
    per_part = ne // nparts
    base = b * cap if merged else 0
    lane = lax.broadcasted_iota(jnp.int32, (1, LANES), 1)

    def tile_bounds(bb, kk, e):
        off = bb * cap if merged else 0
        lo = off + st_ref[(bb * (nk * stride + 1) + kk * stride) * ne + e]
        hi = off + st_ref[(bb * (nk * stride + 1) + (kk + 1) * stride) * ne + e]
        return lo, hi, jnp.minimum((lo // 16) * 16, rows_total - win)

    def bounds(e):
        return tile_bounds(b, k, e)

    def rows_of(bb, e, a0, count):
        y_ref = y_parts[e // per_part]
        return y_ref.at[0 if merged else bb, e % per_part, pl.ds(pl.multiple_of(a0, 16), count), :]

    def win_copy(bb, e, a0, sl):
        return pltpu.make_async_copy(rows_of(bb, e, a0, win), buf.at[sl, pl.ds(e * win, win), :],
                                     sem.at[sl, e])

    def start_tile(bb, kk, sl):
        for e in range(ne):
            win_copy(bb, e, tile_bounds(bb, kk, e)[2], sl).start()

    @pl.when(step == 0)
    def _():
        start_tile(b, k, slot)

    nxt = step + 1

    @pl.when(nxt < pl.num_programs(0) * nk)
    def _():
        start_tile(nxt // nk, nxt % nk, 1 - slot)

    rk = rank_ref[0]

    def token_major(rows_f32):
        return jnp.concatenate([rows_f32, jnp.full((LANES - ne, tt), -1.0, F32)], axis=0).T

    sub_e = lax.broadcasted_iota(jnp.int32, (ne, 1), 0)
    origin = jnp.zeros((ne, 1), jnp.int32)
    for e in range(ne):
        origin = jnp.where(sub_e == e, bounds(e)[2], origin)
    pos = rk + (base - origin)
    tgt = jnp.where((rk >= 0) & (pos >= 0) & (pos < win), pos, -1).astype(F32)
    tgt_b = _dot(token_major(tgt).astype(BF16), spread_ref[...])
    lane_pos = lax.broadcasted_iota(jnp.int32, (1, ne * win), 1) % win
    p = (tgt_b == lane_pos.astype(F32)).astype(BF16)
    for e in range(ne):
        win_copy(b, e, bounds(e)[2], slot).wait()
    acc_ref[...] = _dot(p, buf[slot])

    for e in range(ne):
        lo, hi, a0 = bounds(e)

        def extra(w, carry, e=e, a0=a0):
            start = a0 + w * win
            aw = pl.multiple_of(jnp.minimum(start, rows_total - win), 16)

            cp = pltpu.make_async_copy(rows_of(b, e, aw, win), xbuf, xsem)
            cp.start()
            cp.wait()
            col = token_major(rk.astype(F32)).astype(jnp.int32)[:, e:e + 1]
            ok = (col >= 0) & (col + base >= start)
            px = (jnp.where(ok, col + (base - aw), -1) == lane[:, 0:win]).astype(BF16)
            acc_ref[...] += _dot(px, xbuf[...])
            return carry

        lax.fori_loop(1, (hi - a0 + win - 1) // win, extra, 0)

    y = acc_ref[...]
    yn = y * lax.rsqrt(jnp.mean(y * y, axis=-1, keepdims=True) + EPS) * nw_ref[...]
    o_ref[0] = x_ref[0] + g_ref[0] * yn


def combine_post(starts, rank_t, ys, x, norm_w, g, tt, cap, merged):
    ys = tuple(ys) if isinstance(ys, (tuple, list)) else (ys,)
    b, n, d = x.shape
    rows = ys[0].shape[2]
    ne = sum(y.shape[1] for y in ys)
    assert all(y.shape == ys[0].shape for y in ys)
    stride = COMBINE_STRIDE if n % (tt * COMBINE_STRIDE) == 0 and rows >= ROUTE_WIN * COMBINE_STRIDE * 2 else 1
    tt = tt * stride
    win = ROUTE_WIN * stride
    assert win <= LANES
    lanes = np.arange(ne * win) // win
    spread = jnp.asarray(np.arange(LANES)[:, None] == lanes[None, :], BF16)
    grid_spec = pltpu.PrefetchScalarGridSpec(
        num_scalar_prefetch=1,
        grid=(b, n // tt),
        in_specs=[
            pl.BlockSpec((1, ne, tt), lambda bi, k, st: (bi, 0, k)),
            *[pl.BlockSpec(memory_space=pl.ANY)] * len(ys),
            pl.BlockSpec((1, tt, d), lambda bi, k, st: (bi, k, 0)),
            pl.BlockSpec((1, d), lambda bi, k, st: (0, 0)),
            pl.BlockSpec((1, 1, d), lambda bi, k, st: (bi, 0, 0)),
            pl.BlockSpec((LANES, ne * win), lambda bi, k, st: (0, 0)),
        ],
        out_specs=pl.BlockSpec((1, tt, d), lambda bi, k, st: (bi, k, 0)),
        scratch_shapes=[
            pltpu.VMEM((2, ne * win, d), BF16),
            pltpu.VMEM((win, d), BF16),
            pltpu.SemaphoreType.DMA((2, ne)),
            pltpu.SemaphoreType.DMA(()),
            pltpu.VMEM((tt, d), F32),
        ],
    )
    return pl.pallas_call(
        functools.partial(_combine_kernel, tt=tt, ne=ne, cap=cap, merged=merged, nparts=len(ys),
                          win=win, stride=stride),
        out_shape=jax.ShapeDtypeStruct((b, n, d), F32),
        grid_spec=grid_spec,
        compiler_params=_cparams("arbitrary", "arbitrary"),
        name="combine_post",
    )(starts, rank_t, *ys, x, norm_w.reshape(1, d), g, spread)


def _select_kernel(aff_ref, tri_ref, rank_ref, st_ref, *, cap, tt):
    a = aff_ref[0]
    ne, n = a.shape
    bits = lax.bitcast_convert_type(a, jnp.int32)

    def search(_, c):
        lo, hi = c
        mid = lo + ((hi - lo) >> 1)
        cnt = jnp.sum((bits >= mid).astype(F32), axis=1, keepdims=True)
        ge = cnt >= cap
        return jnp.where(ge, mid, lo), jnp.where(ge, hi, mid)

    lo0 = jnp.zeros((ne, 1), jnp.int32)
    hi0 = jnp.full((ne, 1), 0x7F800000, jnp.int32)
    thr, _ = lax.fori_loop(0, 31, search, (lo0, hi0))
    gt = bits > thr
    eq = bits == thr
    need = cap - jnp.sum(gt.astype(F32), axis=1, keepdims=True)
    m = jnp.concatenate([gt, eq], axis=0).astype(BF16)
    tri = tri_ref[...]
    lane = lax.broadcasted_iota(jnp.int32, (1, LANES), 1)
    off = jnp.zeros((2 * ne, 1), F32)
    st = jnp.zeros((ne, LANES), jnp.int32)
    for j in range(n // LANES):
        cs = slice(j * LANES, (j + 1) * LANES)
        if (j * LANES) % tt == 0:
            off_sel = off[:ne] + jnp.minimum(off[ne:], need)
            st = jnp.where(lane == (j * LANES) // tt, off_sel.astype(jnp.int32), st)
        mj = m[:, cs]
        pj = _dot(mj, tri) + off
        pe = pj[ne:]
        sel = gt[:, cs] | (eq[:, cs] & (pe < need))
        rank_ref[0, :, cs] = jnp.where(sel, pj[:ne] + jnp.minimum(pe, need), -1.0).astype(jnp.int32)
        off = off + jnp.sum(mj.astype(F32), axis=1, keepdims=True)
    st_ref[0] = jnp.where(lane == n // tt, cap, st)


def route_select(aff_t, cap, tt):
    b, ne, n = aff_t.shape
    tri = jnp.asarray(np.triu(np.ones((LANES, LANES), np.float32), 1), BF16)
    return pl.pallas_call(
        functools.partial(_select_kernel, cap=cap, tt=tt),
        out_shape=(jax.ShapeDtypeStruct((b, ne, n), jnp.int32),
                   jax.ShapeDtypeStruct((b, ne, LANES), jnp.int32)),
        grid=(b,),
        in_specs=[pl.BlockSpec((1, ne, n), lambda bi: (bi, 0, 0)),
                  pl.BlockSpec((LANES, LANES), lambda bi: (0, 0))],
        out_specs=(pl.BlockSpec((1, ne, n), lambda bi: (bi, 0, 0)),
                   pl.BlockSpec((1, ne, LANES), lambda bi: (bi, 0, 0))),
        compiler_params=_cparams("parallel"),
        name="route_select",
    )(aff_t, tri)


def _compact_kernel(st_ref, rank_ref, aff_ref, idx_ref, gate_ref, out_ref, *, tt, ne, cap, n):
    b = pl.program_id(0)
    k = pl.program_id(1)
    nk = pl.num_programs(1)
    win = ROUTE_WIN

    @pl.when(k == 0)
    def _():
        out_ref[...] = jnp.zeros_like(out_ref)

    rk = rank_ref[0]
    a = aff_ref[0]
    g1 = a.astype(BF16).astype(F32)
    r1 = a - g1
    g2 = r1.astype(BF16).astype(F32)
    g3 = r1 - g2
    tok = k * tt + lax.broadcasted_iota(jnp.int32, (1, tt), 1)
    ids = jnp.concatenate([(tok >> 6).astype(F32), (tok & 63).astype(F32),
                           jnp.zeros((VAL_ROWS - 2, tt), F32)], axis=0)
    pad = jnp.zeros((LANES - VAL_ROWS - 3 * ne, tt), F32)
    payload = jnp.concatenate([ids, g1, g2, g3, pad], axis=0).astype(BF16)
    sub = lax.broadcasted_iota(jnp.int32, (win, 1), 0)

    def bounds(e):
        lo = st_ref[(b * (nk + 1) + k) * ne + e]
        hi = st_ref[(b * (nk + 1) + k + 1) * ne + e]
        return hi, (lo // 8) * 8

    def one_hot(e, aw):
        return ((rk[e:e + 1, :] - aw) == sub).astype(BF16)

    p_all = jnp.concatenate([one_hot(e, bounds(e)[1]) for e in range(ne)], axis=0)
    moved = _dot_nt(p_all, payload)
    for e in range(ne):
        a0 = pl.multiple_of(bounds(e)[1], 8)
        out_ref[e, pl.ds(a0, win), :] += moved[e * win:(e + 1) * win]

    for e in range(ne):
        hi, a0 = bounds(e)

        def window(w, carry, e=e, a0=a0):
            aw = pl.multiple_of(a0 + w * win, 8)
            out_ref[e, pl.ds(aw, win), :] += _dot_nt(one_hot(e, aw), payload)
            return carry

        lax.fori_loop(1, (hi - a0 + win - 1) // win, window, 0)

    @pl.when(k == nk - 1)
    def _():
        for e in range(ne):
            t = out_ref[e].T
            ids_e = (t[0:1] * 64.0 + t[1:2]).astype(jnp.int32) + b * n
            g_e = (t[VAL_ROWS + e:VAL_ROWS + e + 1] + t[VAL_ROWS + ne + e:VAL_ROWS + ne + e + 1]) \
                + t[VAL_ROWS + 2 * ne + e:VAL_ROWS + 2 * ne + e + 1]
            idx_ref[0, e:e + 1, :] = ids_e[:, 0:cap]
            gate_ref[0, e:e + 1, :] = g_e[:, 0:cap]


def route_compact(starts, rank_t, aff_t, cap, tt):
    b, ne, n = aff_t.shape
    rows = -(-(cap + ROUTE_WIN) // LANES) * LANES
    grid_spec = pltpu.PrefetchScalarGridSpec(
        num_scalar_prefetch=1,
        grid=(b, n // tt),
        in_specs=[pl.BlockSpec((1, ne, tt), lambda bi, k, st: (bi, 0, k)),
                  pl.BlockSpec((1, ne, tt), lambda bi, k, st: (bi, 0, k))],
        out_specs=(pl.BlockSpec((1, ne, cap), lambda bi, k, st: (bi, 0, 0)),
                   pl.BlockSpec((1, ne, cap), lambda bi, k, st: (bi, 0, 0))),
        scratch_shapes=[pltpu.VMEM((ne, rows, LANES), F32)],
    )
    return pl.pallas_call(
        functools.partial(_compact_kernel, tt=tt, ne=ne, cap=cap, n=n),
        out_shape=(jax.ShapeDtypeStruct((b, ne, cap), jnp.int32),
                   jax.ShapeDtypeStruct((b, ne, cap), F32)),
        grid_spec=grid_spec,
        compiler_params=_cparams("parallel", "arbitrary"),
        name="route_compact",
    )(starts, rank_t, aff_t)


def _qk_head_of_lane():
    half = QK_W // 2
    return (np.arange(QK_W) % half) // (DK // 2)


def _in_proj_perm():
    half = DK // 2
    first = [h * DK + i for h in range(N_HEADS) for i in range(half)]
    second = [h * DK + half + i for h in range(N_HEADS) for i in range(half)]
    qperm = np.array(first + second)
    rest = np.arange(2 * QK_W, 2 * QK_W + 2 * V_W + POOL_W)
    return np.concatenate([qperm, QK_W + qperm, rest])


def _static_tables():
    qk_head = _qk_head_of_lane()
    v_head = np.arange(V_W) // DV
    hmask = (qk_head[None, :] == np.arange(N_HEADS)[:, None])
    bd = (qk_head[:, None] == v_head[None, :])
    avg = (v_head[:, None] == v_head[None, :]).astype(np.float32) / DV
    return {
        "hmask": jnp.asarray(hmask[:, None, :], BF16),
        "bd_f32": jnp.asarray(bd, F32),
        "bd_bf16": jnp.asarray(bd, BF16),
        "avg": jnp.asarray(avg, BF16),
    }


def _decay_tables(lg_f, lg_b):
    pos = jnp.arange(CHUNK, dtype=F32)
    diff = pos[:, None] - pos[None, :]
    low = diff >= 0
    up = diff < 0
    m_f = jnp.where(low, jnp.exp(lg_f[:, None, None] * jnp.where(low, diff, 0.0)), 0.0)
    m_b = jnp.where(up, jnp.exp(lg_b[:, None, None] * jnp.where(up, -diff, 0.0)), 0.0)
    m = m_f + m_b
    mpair = m.reshape(N_HEADS // 2, 2, CHUNK, CHUNK).transpose(0, 2, 1, 3).reshape(
        N_HEADS // 2, CHUNK, 2 * CHUNK)
    qk_head = _qk_head_of_lane()
    v_head = np.arange(V_W) // DV
    qdec_f = jnp.exp(lg_f[None, :] * (pos[:, None] + 1.0))[:, v_head]
    qdec_b = jnp.exp(lg_b[None, :] * (CHUNK - pos[:, None]))[:, v_head]
    kdec_f = jnp.exp(lg_f[None, :] * (CHUNK - 1.0 - pos[:, None]))[:, qk_head]
    kdec_b = jnp.exp(lg_b[None, :] * pos[:, None])[:, qk_head]
    cdec_f = jnp.exp(lg_f * CHUNK)[None, v_head]
    cdec_b = jnp.exp(lg_b * CHUNK)[None, v_head]
    return {"mpair": mpair, "qdec_f": qdec_f, "qdec_b": qdec_b, "kdec_f": kdec_f,
            "kdec_b": kdec_b, "cdec_f": cdec_f, "cdec_b": cdec_b}


def _rope_tables(n):
    t = jnp.arange(n)
    row = (t // GRID_W).astype(F32)
    col = (t % GRID_W).astype(F32)
    n_freq = DK // 4
    inv = ROPE_BASE ** (-jnp.arange(n_freq, dtype=F32) / n_freq)
    ang = jnp.concatenate([row[:, None] * inv, col[:, None] * inv], axis=-1)
    return jnp.tile(jnp.cos(ang), (1, N_HEADS)), jnp.tile(jnp.sin(ang), (1, N_HEADS))


EXPERT_GROUPS = 2


def _route(h_pk, aff_t, merged):
    b, n, d = h_pk.shape
    cap = EC_FACTOR * n // N_EXPERTS
    ne = N_EXPERTS
    per = ne // EXPERT_GROUPS
    tt = min(n, 256)
    nk = n // tt
    rank_t, st = route_select(aff_t, cap, tt)
    starts = st[:, :, :nk + 1].transpose(0, 2, 1).reshape(-1)
    flat, gate = route_compact(starts, rank_t, aff_t, cap, tt)
    table = h_pk.reshape(b * n, d)
    xs_list, gate_list = [], []
    for i in range(EXPERT_GROUPS):
        fl, gt = flat[:, i * per:(i + 1) * per], gate[:, i * per:(i + 1) * per]
        if merged:
            fl = fl.transpose(1, 0, 2)
            gt = gt.transpose(1, 0, 2).reshape(1, per, b * cap)
        xs_list.append(gather_rows(table, fl.reshape(-1)).reshape(gt.shape + (d,)))
        gate_list.append(gt[:, :, None, :])
    return xs_list, gate_list, (starts, rank_t, tt, cap, merged)


def _combine(x, y, info, norm_w, g):
    starts, rank_t, tt, cap, merged = info
    return combine_post(starts, rank_t, y, x, norm_w, g, tt, cap, merged)


def kernel(x, c, ctx, c_ctx, w_ada, b_ada, norm_pre_mix, norm_post_mix, norm_pre_ffn, norm_post_ffn, w_in, ret_decay_fwd, ret_decay_bwd, ret_gn, pool_w, pool_scale, w_out, w_router, w_gate, w_up, w_down):
    b, n, d = x.shape
    lc = ctx.shape[1]
    depth = w_ada.shape[0]
    rope = _rope_tables(n)
    static = _static_tables()
    perm = _in_proj_perm()

    cc = jnp.concatenate([c, c_ctx[None, :], jnp.zeros((7, d), F32)], axis=0)
    mods = ada_modulation(cc, w_ada, b_ada)
    wg, wu, wd = jnp.swapaxes(w_gate, 2, 3), jnp.swapaxes(w_up, 2, 3), w_down

    tm = 1024
    for l in range(depth):
        last = l == depth - 1
        mx = mods[l, :b].reshape(b, 1, 6, d)
        sh1, sc1, g1, sh2, sc2, g2 = [mx[:, :, i] for i in range(6)]
        mc = jnp.broadcast_to(mods[l, b].reshape(1, 1, 6, d), (b, 1, 6, d))
        csh1, csc1, cg1, csh2, csc2, cg2 = [mc[:, :, i] for i in range(6)]
        lg_f = jax.nn.log_sigmoid(ret_decay_fwd[l].astype(F32))
        lg_b = jax.nn.log_sigmoid(ret_decay_bwd[l].astype(F32))
        tabs = dict(static, **_decay_tables(lg_f, lg_b))
        w_in_p = w_in[l][:, perm].astype(BF16)
        w_out_bf = w_out[l].astype(BF16)
        pool_w_bf = pool_w[l].astype(BF16)

        qk_c, v_c, gate_c, p_c = premix(ctx, norm_pre_mix[l], csh1, csc1, w_in_p, None, lc)
        zero = jnp.zeros((b, DK, V_W), F32)
        sf_c, sb_c, s_f, s_b = state_scan(qk_c, v_c, zero, zero, tabs)
        qk_x, v_x, gate_x, p_x = premix(x, norm_pre_mix[l], sh1, sc1, w_in_p, rope, tm)
        sf_x, sb_x, _, _ = state_scan(qk_x, v_x, s_f, s_b, tabs)
        pool_x = pool_mixer(p_x, pool_w_bf, pool_scale[l], GRID_W, min(n, 4096))
        x = retention_mixer(qk_x, v_x, gate_x, pool_x, x, sf_x, sb_x, tabs, ret_gn[l], w_out_bf,
                            norm_post_mix[l], g1, tm)
        if not last:
            pool_c = pool_mixer(p_c, pool_w_bf, pool_scale[l], lc, lc)
            ctx = retention_mixer(qk_c, v_c, gate_c, pool_c, ctx, sf_c, sb_c, tabs, ret_gn[l],
                                  w_out_bf, norm_post_mix[l], cg1, lc)

        wr_t = w_router[l].T
        h_x, aff_x = router(x, norm_pre_ffn[l], sh2, sc2, wr_t, min(n, 2 * tm))
        per = N_EXPERTS // EXPERT_GROUPS
        xs_x, gate_x2, info_x = _route(h_x, aff_x, False)
        if last:
            y_x = [expert_ffn(xs_x[i], gate_x2[i], wg, wu, wd, l, e_off=i * per)
                   for i in range(EXPERT_GROUPS)]
        else:
            h_c, aff_c = router(ctx, norm_pre_ffn[l], csh2, csc2, wr_t, lc)
            xs_c, gate_c2, info_c = _route(h_c, aff_c, True)
            pairs = [expert_ffn(xs_x[i], gate_x2[i], wg, wu, wd, l, e_off=i * per,
                                extra=(xs_c[i], gate_c2[i])) for i in range(EXPERT_GROUPS)]
            y_x = [p[0] for p in pairs]
            ctx = _combine(ctx, [p[1] for p in pairs], info_c, norm_post_ffn[l], cg2)
        x = _combine(x, y_x, info_x, norm_post_ffn[l], g2)
    return x
```
